```python
import math
import jax, jax.numpy as jnp
from jax import lax
import numpy as np

D_MODEL = 1024
BATCH = 16
SEQ = 4096
DEPTH = 2
DEC_BATCH = 4
DEC_SEQ = 8192
PAST_LEN = 128

ATTN_HEADS = 8
ATTN_KV_HEADS = 2
ATTN_HEAD_DIM = 64
WINDOW = 128
BLOCK = 128
REL_BUCKETS = 32
REL_MAX_DIST = 128
RET_HEADS = 4
RET_KEY_DIM = 128
RET_VAL_DIM = 128
RET_CHUNK = 128
N_EXPERTS = 16
EC_CAPACITY_FACTOR = 2
EXPERT_FF = 1024
EPS = 1e-6

ATTN_WIDTH = ATTN_HEADS * ATTN_HEAD_DIM
KV_WIDTH = ATTN_KV_HEADS * ATTN_HEAD_DIM
RET_QK_WIDTH = RET_HEADS * RET_KEY_DIM
RET_V_WIDTH = RET_HEADS * RET_VAL_DIM
IN_SPLITS = (ATTN_WIDTH, KV_WIDTH, KV_WIDTH, RET_QK_WIDTH, RET_QK_WIDTH, RET_V_WIDTH, RET_V_WIDTH, D_MODEL, D_MODEL)
IN_WIDTH = ATTN_WIDTH + 2 * KV_WIDTH + 2 * RET_QK_WIDTH + 2 * RET_V_WIDTH + 2 * D_MODEL

kernel_name = "hybrid_bidir_window_gqa_retention_ec_moe"


def rms_norm(x, g):
    xf = x.astype(jnp.float32)
    y = xf * lax.rsqrt(jnp.mean(xf * xf, axis=-1, keepdims=True) + EPS)
    return (y * g.astype(jnp.float32)).astype(x.dtype)


def t5_bucket(rel):
    half = REL_BUCKETS // 2
    max_exact = half // 2
    base = np.where(rel > 0, half, 0)
    n = np.abs(rel)
    large = max_exact + (np.log(np.maximum(n, 1) / max_exact) / math.log(REL_MAX_DIST / max_exact)
                         * (half - max_exact)).astype(np.int32)
    large = np.minimum(large, half - 1)
    return (base + np.where(n < max_exact, n, large)).astype(np.int32)


def windowed_gqa(q, k, v, rel_bias, sink):
    b, s, _, hd = q.shape
    nb = s // BLOCK
    g = ATTN_HEADS // ATTN_KV_HEADS
    qb = q.reshape(b, nb, BLOCK, ATTN_KV_HEADS, g, hd)
    pad = ((0, 0), (BLOCK, BLOCK), (0, 0), (0, 0))

    def band(t):
        tp = jnp.pad(t, pad).reshape(b, nb + 2, BLOCK, ATTN_KV_HEADS, hd)
        return jnp.concatenate([tp[:, :-2], tp[:, 1:-1], tp[:, 2:]], axis=2)

    kb, vb = band(k), band(v)
    logits = jnp.einsum('bnqkgd,bnskd->bnkgqs', qb, kb).astype(jnp.float32) * (hd ** -0.5)
    q_pos = np.arange(BLOCK)[:, None]
    k_off = np.arange(3 * BLOCK)[None, :] - BLOCK
    rel = k_off - q_pos
    in_window = np.abs(rel) <= WINDOW
    abs_key = np.arange(nb)[:, None] * BLOCK + np.arange(3 * BLOCK)[None, :] - BLOCK
    valid = (abs_key >= 0) & (abs_key < s)
    mask = (in_window[None, :, :] & valid[:, None, :])[None, :, None, None]
    bias = rel_bias.astype(jnp.float32)[t5_bucket(rel)]
    bias = bias.transpose(2, 0, 1).reshape(ATTN_KV_HEADS, g, BLOCK, 3 * BLOCK)
    logits = jnp.where(mask, logits + bias, -1e30)
    sink_col = jnp.broadcast_to(sink.astype(jnp.float32).reshape(ATTN_KV_HEADS, g, 1, 1),
                                logits.shape[:-1] + (1,))
    probs = jax.nn.softmax(jnp.concatenate([logits, sink_col], axis=-1), axis=-1)[..., :-1]
    out = jnp.einsum('bnkgqs,bnskd->bnqkgd', probs.astype(v.dtype), vb)
    return out.reshape(b, s, ATTN_HEADS * hd)


def retention_one_direction(q, k, v, log_gamma, strict):
    b, s, h, dk = q.shape
    dv = v.shape[-1]
    nc = s // RET_CHUNK
    qc = q.reshape(b, nc, RET_CHUNK, h, dk).astype(jnp.float32)
    kc = k.reshape(b, nc, RET_CHUNK, h, dk).astype(jnp.float32)
    vc = v.reshape(b, nc, RET_CHUNK, h, dv).astype(jnp.float32)
    pos = np.arange(RET_CHUNK)
    dist = (pos[:, None] - pos[None, :]).astype(np.float32)
    keep = (dist > 0) if strict else (dist >= 0)
    intra_decay = jnp.where(keep[None], jnp.exp(log_gamma[:, None, None] * np.maximum(dist, 0.0)[None]), 0.0)
    scores = jnp.einsum('bnihd,bnjhd->bnhij', qc, kc) * intra_decay
    intra = jnp.einsum('bnhij,bnjhe->bnihe', scores, vc)
    posf = pos.astype(np.float32)
    k_decay = jnp.exp(log_gamma[:, None] * (RET_CHUNK - 1 - posf)[None])
    chunk_kv = jnp.einsum('bnjhd,bnjhe,hj->nbhde', kc, vc, k_decay)
    chunk_decay = jnp.exp(log_gamma * RET_CHUNK)[None, :, None, None]

    def step(state, kv_n):
        return chunk_decay * state + kv_n, state

    _, prev_states = lax.scan(step, jnp.zeros((b, h, dk, dv), jnp.float32), chunk_kv)
    q_decay = jnp.exp(log_gamma[:, None] * (posf + 1.0)[None])
    cross = jnp.einsum('bnihd,nbhde,hi->bnihe', qc, prev_states, q_decay)
    return (intra + cross).reshape(b, s, h, dv)


def bidirectional_retention(q, k, v, gate, decay_logit, norm_g):
    b, s = q.shape[:2]
    log_g = jax.nn.log_sigmoid(decay_logit.astype(jnp.float32))
    k = k * (RET_KEY_DIM ** -0.5)
    fwd = retention_one_direction(q, k, v, log_g[0], False)
    bwd = jnp.flip(retention_one_direction(jnp.flip(q, 1), jnp.flip(k, 1), jnp.flip(v, 1), log_g[1], True), 1)
    o = fwd + bwd
    mu = jnp.mean(o, axis=-1, keepdims=True)
    var = jnp.mean(jnp.square(o - mu), axis=-1, keepdims=True)
    o = (o - mu) * lax.rsqrt(var + EPS) * norm_g.astype(jnp.float32)
    return (jax.nn.silu(gate.astype(jnp.float32)) * o.reshape(b, s, -1)).astype(gate.dtype)


def expert_choice_ffn(x, w_router, w_gate, w_up, w_down):
    b, s, d = x.shape
    n = b * s
    cap = max(1, EC_CAPACITY_FACTOR * n // N_EXPERTS)
    xf = x.reshape(n, d)
    affinity = jax.nn.softmax((xf @ w_router).astype(jnp.float32), axis=-1)
    gate_vals, token_idx = lax.top_k(affinity.T, cap)

    def expert(args):
        idx, g, w1, w3, w2 = args
        xs = xf[idx]
        h = jax.nn.silu(xs @ w1) * (xs @ w3)
        return (h @ w2) * g[:, None].astype(xs.dtype)

    out = lax.map(expert, (token_idx, gate_vals, w_gate, w_up, w_down))
    y = jnp.zeros((n, d), out.dtype).at[token_idx.reshape(-1)].add(out.reshape(-1, d))
    return y.reshape(b, s, d).astype(x.dtype)


def encoder_layer(x, norm_mix_g, w_in, q_norm_g, k_norm_g, attn_sink, rel_bias, retn_decay_logit,
                  retn_norm_g, w_branch_attn, w_branch_retn, w_out, norm_ffn_g, w_router,
                  w_exp_gate, w_exp_up, w_exp_down):
    b, s, _ = x.shape
    h = rms_norm(x, norm_mix_g)
    proj = h @ w_in
    offsets = [int(o) for o in np.cumsum(IN_SPLITS)[:-1]]
    q_a, k_a, v_a, q_r, k_r, v_r, g_r, gate_a, gate_r = jnp.split(proj, offsets, axis=-1)
    q_a = rms_norm(q_a.reshape(b, s, ATTN_HEADS, ATTN_HEAD_DIM), q_norm_g)
    k_a = rms_norm(k_a.reshape(b, s, ATTN_KV_HEADS, ATTN_HEAD_DIM), k_norm_g)
    v_a = v_a.reshape(b, s, ATTN_KV_HEADS, ATTN_HEAD_DIM)
    attn = windowed_gqa(q_a, k_a, v_a, rel_bias, attn_sink)
    retn = bidirectional_retention(q_r.reshape(b, s, RET_HEADS, RET_KEY_DIM),
                                   k_r.reshape(b, s, RET_HEADS, RET_KEY_DIM),
                                   v_r.reshape(b, s, RET_HEADS, RET_VAL_DIM),
                                   g_r, retn_decay_logit, retn_norm_g)
    merged = jax.nn.sigmoid(gate_a) * (attn.astype(x.dtype) @ w_branch_attn) \
        + jax.nn.sigmoid(gate_r) * (retn.astype(x.dtype) @ w_branch_retn)
    x = x + (merged @ w_out).astype(x.dtype)
    x = x + expert_choice_ffn(rms_norm(x, norm_ffn_g), w_router, w_exp_gate, w_exp_up, w_exp_down)
    return x


def run_trunk(x, norm_mix_g, w_in, q_norm_g, k_norm_g, attn_sink, rel_bias, retn_decay_logit,
              retn_norm_g, w_branch_attn, w_branch_retn, w_out, norm_ffn_g, w_router,
              w_exp_gate, w_exp_up, w_exp_down):
    for l in range(DEPTH):
        x = encoder_layer(x, norm_mix_g[l], w_in[l], q_norm_g[l], k_norm_g[l], attn_sink[l], rel_bias,
                          retn_decay_logit[l], retn_norm_g[l], w_branch_attn[l], w_branch_retn[l],
                          w_out[l], norm_ffn_g[l], w_router[l], w_exp_gate[l], w_exp_up[l], w_exp_down[l])
    return x


def setup_inputs(seed: int = 0) -> dict:
    key = jax.random.key(seed)
    ks = jax.random.split(key, 20)
    f32 = jnp.float32
    nrm = lambda k, shape, scale: jax.random.normal(k, shape, f32) * scale
    base_logit = np.log(2.0 ** (5 + np.arange(RET_HEADS)) - 1.0).astype(np.float32)
    return {
        "x_prompt": nrm(ks[0], (BATCH, SEQ, D_MODEL), 1.0),
        "x_sample": nrm(ks[1], (DEC_BATCH, DEC_SEQ, D_MODEL), 1.0),
        "norm_mix_g": 1.0 + nrm(ks[2], (DEPTH, D_MODEL), 0.02),
        "w_in": nrm(ks[3], (DEPTH, D_MODEL, IN_WIDTH), D_MODEL ** -0.5),
        "q_norm_g": 1.0 + nrm(ks[4], (DEPTH, ATTN_HEAD_DIM), 0.02),
        "k_norm_g": 1.0 + nrm(ks[5], (DEPTH, ATTN_HEAD_DIM), 0.02),
        "attn_sink": nrm(ks[6], (DEPTH, ATTN_HEADS), 0.5),
        "rel_bias": nrm(ks[7], (REL_BUCKETS, ATTN_HEADS), 0.5),
        "retn_decay_logit": jnp.asarray(base_logit)[None, None, :] + nrm(ks[8], (DEPTH, 2, RET_HEADS), 0.05),
        "retn_norm_g": 1.0 + nrm(ks[9], (DEPTH, RET_HEADS, RET_VAL_DIM), 0.02),
        "w_branch_attn": nrm(ks[10], (DEPTH, ATTN_WIDTH, D_MODEL), ATTN_WIDTH ** -0.5),
        "w_branch_retn": nrm(ks[11], (DEPTH, RET_V_WIDTH, D_MODEL), RET_V_WIDTH ** -0.5),
        "w_out": nrm(ks[12], (DEPTH, D_MODEL, D_MODEL), D_MODEL ** -0.5),
        "norm_ffn_g": 1.0 + nrm(ks[13], (DEPTH, D_MODEL), 0.02),
        "w_router": nrm(ks[14], (DEPTH, D_MODEL, N_EXPERTS), D_MODEL ** -0.5),
        "w_exp_gate": nrm(ks[15], (DEPTH, N_EXPERTS, D_MODEL, EXPERT_FF), D_MODEL ** -0.5),
        "w_exp_up": nrm(ks[16], (DEPTH, N_EXPERTS, D_MODEL, EXPERT_FF), D_MODEL ** -0.5),
        "w_exp_down": nrm(ks[17], (DEPTH, N_EXPERTS, EXPERT_FF, D_MODEL), EXPERT_FF ** -0.5),
    }


def reference(x_prompt, x_sample, norm_mix_g, w_in, q_norm_g, k_norm_g, attn_sink, rel_bias,
              retn_decay_logit, retn_norm_g, w_branch_attn, w_branch_retn, w_out, norm_ffn_g,
              w_router, w_exp_gate, w_exp_up, w_exp_down):
    y_prompt = run_trunk(x_prompt, norm_mix_g, w_in, q_norm_g, k_norm_g, attn_sink, rel_bias,
                         retn_decay_logit, retn_norm_g, w_branch_attn, w_branch_retn, w_out,
                         norm_ffn_g, w_router, w_exp_gate, w_exp_up, w_exp_down)
    y_sample = run_trunk(x_sample, norm_mix_g, w_in, q_norm_g, k_norm_g, attn_sink, rel_bias,
                         retn_decay_logit, retn_norm_g, w_branch_attn, w_branch_retn, w_out,
                         norm_ffn_g, w_router, w_exp_gate, w_exp_up, w_exp_down)
    return (y_prompt, y_sample)
```

```python
import functools
import math

import numpy as np
import jax
import jax.numpy as jnp
from jax import lax
from jax.experimental import pallas as pl
from jax.experimental.pallas import tpu as pltpu

D_MODEL = 1024
ATTN_HEADS = 8
ATTN_KV_HEADS = 2
ATTN_HEAD_DIM = 64
WINDOW = 128
BLOCK = 128
REL_BUCKETS = 32
REL_MAX_DIST = 128
RET_HEADS = 4
RET_DIM = 128
N_EXPERTS = 16
EC_CAPACITY_FACTOR = 2
EXPERT_FF = 1024
EPS = 1e-6

ATTN_WIDTH = ATTN_HEADS * ATTN_HEAD_DIM
KV_WIDTH = ATTN_KV_HEADS * ATTN_HEAD_DIM
RET_WIDTH = RET_HEADS * RET_DIM
IN_SPLITS = (ATTN_WIDTH, KV_WIDTH, KV_WIDTH, RET_WIDTH, RET_WIDTH, RET_WIDTH, RET_WIDTH, D_MODEL, D_MODEL)
IN_OFFSETS = tuple(int(o) for o in np.cumsum((0,) + IN_SPLITS))

LANES = 128
VMEM_LIMIT_BYTES = 56 * 1024 * 1024

TOKEN_TILE = 512
RET_CHUNK = 256
FFN_ROWS = 256
SELECT_BLOCKS = 128
SELECT_SLOTS = 1024

F32 = jnp.float32
BF16 = jnp.bfloat16
NEG = -1e30


def _cparams(n_axes, vmem=VMEM_LIMIT_BYTES):
    return pltpu.CompilerParams(dimension_semantics=("arbitrary",) * n_axes, vmem_limit_bytes=vmem)


def _sigmoid(x):
    return 1.0 / (1.0 + jnp.exp(-x))


def _in_proj_kernel(x_ref, g_ref, w_ref, qg_ref, kg_ref, bdq_ref, bdk_ref,
                    qa_ref, ka_ref, va_ref, qr_ref, kr_ref, vr_ref, gr_ref, ga_ref, gt_ref):
    x = x_ref[...]
    ms = jnp.mean(x * x, axis=-1, keepdims=True)
    h = (x * lax.rsqrt(ms + EPS) * g_ref[...]).astype(BF16)

    def mm(k):
        return jnp.dot(h, w_ref[:, IN_OFFSETS[k]:IN_OFFSETS[k + 1]], preferred_element_type=F32)

    def head_norm(t, bd_ref, gain_ref):
        ss = jnp.dot((t * t).astype(BF16), bd_ref[...], preferred_element_type=F32)
        return t * lax.rsqrt(ss * (1.0 / ATTN_HEAD_DIM) + EPS) * gain_ref[...]

    qa_ref[...] = head_norm(mm(0), bdq_ref, qg_ref).astype(BF16)
    ka_ref[...] = head_norm(mm(1), bdk_ref, kg_ref).astype(BF16)
    for k, ref in ((2, va_ref), (3, qr_ref), (4, kr_ref), (5, vr_ref), (6, gr_ref), (7, ga_ref), (8, gt_ref)):
        ref[...] = mm(k).astype(BF16)


def _in_proj(x2, g, w, qg, kg, bdq, bdk):
    n = x2.shape[0]
    tm = TOKEN_TILE
    full = lambda a: pl.BlockSpec(a.shape, lambda i: (0,) * a.ndim)
    widths = IN_SPLITS
    return pl.pallas_call(
        _in_proj_kernel,
        grid=(n // tm,),
        in_specs=[pl.BlockSpec((tm, D_MODEL), lambda i: (i, 0)), full(g), full(w), full(qg), full(kg),
                  full(bdq), full(bdk)],
        out_specs=[pl.BlockSpec((tm, wd), lambda i: (i, 0)) for wd in widths],
        out_shape=[jax.ShapeDtypeStruct((n, wd), BF16) for wd in widths],
        compiler_params=_cparams(1),
        name="in_proj",
    )(x2, g, w, qg, kg, bdq, bdk)


def _attn_kernel(sink_ref, q_ref, kp_ref, kc_ref, kn_ref, vp_ref, vc_ref, vn_ref, bias_ref, o_ref):
    nq = ATTN_HEADS // 2
    q = q_ref[...]
    qs = jnp.concatenate([q[:, j * LANES:(j + 1) * LANES] for j in range(nq)], axis=0)
    k = jnp.concatenate([kp_ref[...], kc_ref[...], kn_ref[...]], axis=0)
    v = jnp.concatenate([vp_ref[...], vc_ref[...], vn_ref[...]], axis=0)
    low = lax.broadcasted_iota(jnp.int32, k.shape, 1) < ATTN_HEAD_DIM
    zero = jnp.zeros_like(k)
    kbd = jnp.concatenate([jnp.where(low, k, zero), jnp.where(low, zero, k)], axis=0)
    vbd = jnp.concatenate([jnp.where(low, v, zero), jnp.where(low, zero, v)], axis=0)
    s = lax.dot_general(qs, kbd, (((1,), (1,)), ((), ())), preferred_element_type=F32)
    s = s + bias_ref[0]
    nk = 3 * BLOCK
    probs, invs = [], []
    for j in range(nq):
        row_p, row_inv = [], []
        for half in range(2):
            sj = s[j * BLOCK:(j + 1) * BLOCK, half * nk:(half + 1) * nk]
            sk = sink_ref[j + nq * half]
            m = jnp.maximum(jnp.max(sj, axis=-1, keepdims=True), sk)
            p = jnp.exp(sj - m)
            den = jnp.sum(p, axis=-1, keepdims=True) + jnp.exp(sk - m)
            row_p.append(p.astype(BF16))
            row_inv.append(1.0 / den)
        probs.append(jnp.concatenate(row_p, axis=1))
        invs.append(row_inv)
    pm = jnp.concatenate(probs, axis=0)
    o = jnp.dot(pm, vbd, preferred_element_type=F32)
    low_o = lax.broadcasted_iota(jnp.int32, (BLOCK, LANES), 1) < ATTN_HEAD_DIM
    for j in range(nq):
        inv = jnp.where(low_o, invs[j][0], invs[j][1])
        o_ref[:, j * LANES:(j + 1) * LANES] = (o[j * BLOCK:(j + 1) * BLOCK] * inv).astype(BF16)


def _attention(qa, ka, va, bias3, sink, b, s):
    nb = s // BLOCK
    assert nb >= 2
    n = b * s

    def kv_spec(off):
        return pl.BlockSpec((BLOCK, KV_WIDTH), lambda bi, ni: (bi * nb + jnp.clip(ni + off, 0, nb - 1), 0))

    def bias_map(bi, ni):
        return (jnp.where(ni == 0, 0, jnp.where(ni == nb - 1, 2, 1)), 0, 0)

    return pl.pallas_call(
        _attn_kernel,
        grid=(b, nb),
        in_specs=[pl.BlockSpec(memory_space=pltpu.SMEM),
                  pl.BlockSpec((BLOCK, ATTN_WIDTH), lambda bi, ni: (bi * nb + ni, 0)),
                  kv_spec(-1), kv_spec(0), kv_spec(1), kv_spec(-1), kv_spec(0), kv_spec(1),
                  pl.BlockSpec((1,) + bias3.shape[1:], bias_map)],
        out_specs=pl.BlockSpec((BLOCK, ATTN_WIDTH), lambda bi, ni: (bi * nb + ni, 0)),
        out_shape=jax.ShapeDtypeStruct((n, ATTN_WIDTH), BF16),
        compiler_params=_cparams(2),
        name="attn",
    )(sink, qa, ka, ka, ka, va, va, va, bias3)


def _retn_kernel(dec_ref, q_ref, k_ref, v_ref, g_ref, dmask_ref, rowf_ref, rowb_ref, wkf_ref, wkb_ref, ng_ref,
                 o_ref, tstore, uf, tb, *, nc):
    p = pl.program_id(1)
    n = pl.program_id(2)
    tn = (((0,), (0,)), ((), ()))
    nt = (((1,), (1,)), ((), ()))
    hs = lambda h: slice(h * RET_DIM, (h + 1) * RET_DIM)

    @pl.when(p == 0)
    def _():
        @pl.when(n == 0)
        def _():
            tb[...] = jnp.zeros_like(tb)
        c = nc - 1 - n
        for h in range(RET_HEADS):
            tstore[c, h] = tb[h].astype(BF16)
            kw = (k_ref[:, hs(h)].astype(F32) * wkb_ref[h]).astype(BF16)
            upd = lax.dot_general(kw, v_ref[:, hs(h)], tn, preferred_element_type=F32)
            tb[h] = dec_ref[RET_HEADS + h] * tb[h] + upd

    @pl.when(p == 1)
    def _():
        @pl.when(n == 0)
        def _():
            uf[...] = jnp.zeros_like(uf)
        for h in range(RET_HEADS):
            qh = q_ref[:, hs(h)]
            kh = k_ref[:, hs(h)]
            vh = v_ref[:, hs(h)]
            sc = lax.dot_general(qh, kh, nt, preferred_element_type=F32) * dmask_ref[h]
            intra = jnp.dot(sc.astype(BF16), vh, preferred_element_type=F32)
            states = jnp.concatenate([uf[h].astype(BF16), tstore[n, h]], axis=1)
            cross = jnp.dot(qh, states, preferred_element_type=F32)
            o = intra + cross[:, :RET_DIM] * rowf_ref[h] + cross[:, RET_DIM:] * rowb_ref[h]
            mu = jnp.mean(o, axis=-1, keepdims=True)
            d = o - mu
            var = jnp.mean(d * d, axis=-1, keepdims=True)
            on = d * lax.rsqrt(var + EPS) * ng_ref[h:h + 1, :]
            gate = g_ref[:, hs(h)].astype(F32)
            o_ref[:, hs(h)] = (gate * _sigmoid(gate) * on).astype(BF16)
            kw = (kh.astype(F32) * wkf_ref[h]).astype(BF16)
            uf[h] = dec_ref[h] * uf[h] + lax.dot_general(kw, vh, tn, preferred_element_type=F32)


def _retention(qr, kr, vr, gr, tables, b, s):
    dec, dmask, rowf, rowb, wkf, wkb, ng = tables
    cr = RET_CHUNK
    nc = s // cr
    n = b * s
    full = lambda a: pl.BlockSpec(a.shape, lambda bi, pi, ni: (0,) * a.ndim)
    fwd_spec = pl.BlockSpec((cr, RET_WIDTH), lambda bi, pi, ni: (bi * nc + ni * pi, 0))
    kv_spec = pl.BlockSpec((cr, RET_WIDTH), lambda bi, pi, ni: (bi * nc + ni * pi + (1 - pi) * (nc - 1 - ni), 0))
    return pl.pallas_call(
        functools.partial(_retn_kernel, nc=nc),
        grid=(b, 2, nc),
        in_specs=[pl.BlockSpec(memory_space=pltpu.SMEM), fwd_spec, kv_spec, kv_spec, fwd_spec,
                  full(dmask), full(rowf), full(rowb), full(wkf), full(wkb), full(ng)],
        out_specs=fwd_spec,
        out_shape=jax.ShapeDtypeStruct((n, RET_WIDTH), BF16),
        scratch_shapes=[pltpu.VMEM((nc, RET_HEADS, RET_DIM, RET_DIM), BF16),
                        pltpu.VMEM((RET_HEADS, RET_DIM, RET_DIM), F32),
                        pltpu.VMEM((RET_HEADS, RET_DIM, RET_DIM), F32)],
        compiler_params=_cparams(3),
        name="retention",
    )(dec, qr, kr, vr, gr, dmask, rowf, rowb, wkf, wkb, ng)


def _merge_kernel(attn_ref, retn_ref, ga_ref, gr_ref, x_ref, wba_ref, wbr_ref, wo_ref, g2_ref, wr_ref,
                  xmid_ref, h2_ref, afft_ref):
    a = jnp.dot(attn_ref[...], wba_ref[...], preferred_element_type=F32)
    r = jnp.dot(retn_ref[...], wbr_ref[...], preferred_element_type=F32)
    merged = _sigmoid(ga_ref[...].astype(F32)) * a + _sigmoid(gr_ref[...].astype(F32)) * r
    xn = x_ref[...] + jnp.dot(merged.astype(BF16), wo_ref[...], preferred_element_type=F32)
    xmid_ref[...] = xn
    ms = jnp.mean(xn * xn, axis=-1, keepdims=True)
    h2 = xn * lax.rsqrt(ms + EPS) * g2_ref[...]
    h2_ref[...] = h2
    logits = jnp.dot(h2.astype(BF16), wr_ref[...], preferred_element_type=F32)
    real = lax.broadcasted_iota(jnp.int32, logits.shape, 1) < N_EXPERTS
    logits = jnp.where(real, logits, -jnp.inf)
    m = jnp.max(logits, axis=-1, keepdims=True)
    ex = jnp.exp(logits - m)
    aff = ex / jnp.sum(ex, axis=-1, keepdims=True)
    afft_ref[...] = aff.T[:N_EXPERTS, :]


def _merge(attn, retn, ga, gr, x2, wba, wbr, wo, g2, wr):
    n = x2.shape[0]
    tm = TOKEN_TILE
    full = lambda a: pl.BlockSpec(a.shape, lambda i: (0,) * a.ndim)
    row = lambda wd: pl.BlockSpec((tm, wd), lambda i: (i, 0))
    return pl.pallas_call(
        _merge_kernel,
        grid=(n // tm,),
        in_specs=[row(ATTN_WIDTH), row(RET_WIDTH), row(D_MODEL), row(D_MODEL), row(D_MODEL),
                  full(wba), full(wbr), full(wo), full(g2), full(wr)],
        out_specs=[row(D_MODEL), row(D_MODEL), pl.BlockSpec((N_EXPERTS, tm), lambda i: (0, i))],
        out_shape=[jax.ShapeDtypeStruct((n, D_MODEL), F32), jax.ShapeDtypeStruct((n, D_MODEL), F32),
                   jax.ShapeDtypeStruct((N_EXPERTS, n), F32)],
        compiler_params=_cparams(1),
        name="merge",
    )(attn, retn, ga, gr, x2, wba, wbr, wo, g2, wr)


def _select_kernel(aff_ref, u_ref, ls_ref, idx_ref, gate_ref, cgt, afft, offi, *, cap, tb):
    e = pl.program_id(0)
    j = pl.program_id(1)
    nblk = SELECT_BLOCKS
    pc = SELECT_SLOTS

    def cumsum(mask):
        inb = jnp.dot(mask.astype(BF16), u_ref[...], preferred_element_type=F32)
        tot = jnp.broadcast_to(inb[:, tb - 1:tb], (nblk, LANES))
        off = jnp.dot(ls_ref[...], tot, preferred_element_type=F32, precision=lax.Precision.HIGHEST)
        return inb, off[:, 0:1], tot[:, 0:1]

    @pl.when(j == 0)
    def _():
        a = aff_ref[e]
        bits = pltpu.bitcast(a, jnp.int32)

        def bit_step(t, cur):
            cand = cur | jnp.left_shift(jnp.int32(1), 30 - t)
            cnt = jnp.sum((bits >= cand).astype(jnp.int32), keepdims=True)
            return jnp.where(cnt >= cap, cand, cur)

        thr = lax.fori_loop(0, 31, bit_step, jnp.zeros((1, 1), jnp.int32))
        gt = bits > thr
        eq = bits == thr
        need = (cap - jnp.sum(gt.astype(jnp.int32), keepdims=True)).astype(F32)
        eqf = eq.astype(F32)
        eq_in, eq_off, _ = cumsum(eqf)
        eq_rank = eq_in + eq_off - eqf
        sel = jnp.logical_or(gt, jnp.logical_and(eq, eq_rank < need)).astype(F32)
        s_in, s_off, s_tot = cumsum(sel)
        cgt[...] = (s_in + s_off).T
        afft[...] = a.T
        offi[...] = jnp.broadcast_to(s_off + s_tot, (nblk, LANES))

    slot = (j * pc + lax.broadcasted_iota(jnp.int32, (1, pc), 1)).astype(F32)
    blk = jnp.sum((offi[:, 0:1] <= slot).astype(jnp.int32), axis=0, keepdims=True)
    onehot = (lax.broadcasted_iota(jnp.int32, (nblk, pc), 0) == blk).astype(F32)
    g = jnp.dot(cgt[...], onehot, preferred_element_type=F32, precision=lax.Precision.HIGHEST)
    inb = jnp.sum((g <= slot + 0.5).astype(jnp.int32), axis=0, keepdims=True)
    idx_ref[0] = blk * tb + inb
    ag = jnp.dot(afft[...], onehot, preferred_element_type=F32, precision=lax.Precision.HIGHEST)
    hit = lax.broadcasted_iota(jnp.int32, (tb, pc), 0) == inb
    gate_ref[0] = jnp.sum(jnp.where(hit, ag, 0.0), axis=0, keepdims=True)


def _select(afft, cap):
    n = afft.shape[1]
    nblk = SELECT_BLOCKS
    tb = n // nblk
    pc = min(SELECT_SLOTS, cap)
    assert n % nblk == 0 and tb % LANES == 0 and cap % pc == 0 and pc == SELECT_SLOTS
    nch = cap // pc
    aff3 = afft.reshape(N_EXPERTS, nblk, tb)
    upper = jnp.asarray(np.triu(np.ones((tb, tb), np.float32)), BF16)
    lstrict = jnp.asarray(np.tril(np.ones((nblk, nblk), np.float32), -1))
    full = lambda a: pl.BlockSpec(a.shape, lambda e, j: (0,) * a.ndim)
    out_spec = pl.BlockSpec((1, 1, pc), lambda e, j: (e * nch + j, 0, 0))
    idx, gate = pl.pallas_call(
        functools.partial(_select_kernel, cap=cap, tb=tb),
        grid=(N_EXPERTS, nch),
        in_specs=[full(aff3), full(upper), full(lstrict)],
        out_specs=[out_spec, out_spec],
        out_shape=[jax.ShapeDtypeStruct((N_EXPERTS * nch, 1, pc), jnp.int32),
                   jax.ShapeDtypeStruct((N_EXPERTS * nch, 1, pc), F32)],
        scratch_shapes=[pltpu.VMEM((tb, nblk), F32), pltpu.VMEM((tb, nblk), F32), pltpu.VMEM((nblk, LANES), F32)],
        compiler_params=_cparams(2),
        name="select",
    )(aff3, upper, lstrict)
    return idx.reshape(N_EXPERTS, cap), gate.reshape(N_EXPERTS, cap)


def _ffn_kernel(idx_ref, nxt_ref, gate_ref, h_hbm, w1_ref, w3_ref, w2_ref, yin_hbm, y_hbm, xbuf, ybuf, sems, *, nt):
    del yin_hbm
    rows = FFN_ROWS
    e = pl.program_id(0)
    i = pl.program_id(1)
    step = e * nt + i
    nsteps = N_EXPERTS * nt
    slot = lax.rem(step, 2)
    other = 1 - slot

    def row_copies(src_hbm, buf, ids_ref, b, sem_row, to_hbm):
        def body(r, carry):
            tok = ids_ref[0, 0, r]
            hbm_row = src_hbm.at[pl.ds(tok, 1)]
            vm_row = buf.at[b, pl.ds(r, 1)]
            if to_hbm:
                pltpu.make_async_copy(vm_row, hbm_row, sems.at[sem_row, b]).start()
            else:
                pltpu.make_async_copy(hbm_row, vm_row, sems.at[sem_row, b]).start()
            return carry
        lax.fori_loop(0, rows, body, 0, unroll=8)

    def wait_all(src_hbm, buf, b, sem_row):
        pltpu.make_async_copy(src_hbm.at[pl.ds(0, rows)], buf.at[b], sems.at[sem_row, b]).wait()

    @pl.when(step == 0)
    def _():
        row_copies(h_hbm, xbuf, idx_ref, 0, 0, False)
        row_copies(y_hbm, ybuf, idx_ref, 0, 1, False)

    @pl.when(step > 0)
    def _():
        wait_all(y_hbm, ybuf, other, 2)

    @pl.when(step + 1 < nsteps)
    def _():
        row_copies(h_hbm, xbuf, nxt_ref, other, 0, False)

        @pl.when(i + 1 < nt)
        def _():
            row_copies(y_hbm, ybuf, nxt_ref, other, 1, False)

    @pl.when(jnp.logical_and(i == 0, step > 0))
    def _():
        row_copies(y_hbm, ybuf, idx_ref, slot, 1, False)

    wait_all(h_hbm, xbuf, slot, 0)
    xs = xbuf[slot].astype(BF16)
    hg = jnp.dot(xs, w1_ref[0], preferred_element_type=F32)
    hu = jnp.dot(xs, w3_ref[0], preferred_element_type=F32)
    hid = (hg * _sigmoid(hg) * hu).astype(BF16)
    out = jnp.dot(hid, w2_ref[0], preferred_element_type=F32)
    eye = lax.broadcasted_iota(jnp.int32, (rows, rows), 0) == lax.broadcasted_iota(jnp.int32, (rows, rows), 1)
    gcol = jnp.sum(jnp.where(eye, gate_ref[0], 0.0), axis=1, keepdims=True)
    wait_all(y_hbm, ybuf, slot, 1)
    ybuf[slot] = ybuf[slot] + out * gcol
    row_copies(y_hbm, ybuf, idx_ref, slot, 2, True)

    @pl.when(step == nsteps - 1)
    def _():
        wait_all(y_hbm, ybuf, slot, 2)


def _expert_ffn(h2, xmid, idx, gate, w1, w3, w2):
    n = h2.shape[0]
    cap = idx.shape[1]
    rows = FFN_ROWS
    nt = cap // rows
    nsteps = N_EXPERTS * nt
    idx3 = idx.reshape(nsteps, 1, rows)
    gate3 = gate.reshape(nsteps, 1, rows)
    cur = lambda e, i: (e * nt + i, 0, 0)
    nxt = lambda e, i: (jnp.minimum(e * nt + i + 1, nsteps - 1), 0, 0)
    wspec = pl.BlockSpec((1, D_MODEL, EXPERT_FF), lambda e, i: (e, 0, 0))
    w2spec = pl.BlockSpec((1, EXPERT_FF, D_MODEL), lambda e, i: (e, 0, 0))
    any_spec = pl.BlockSpec(memory_space=pl.ANY)
    return pl.pallas_call(
        functools.partial(_ffn_kernel, nt=nt),
        grid=(N_EXPERTS, nt),
        in_specs=[pl.BlockSpec((1, 1, rows), cur, memory_space=pltpu.SMEM),
                  pl.BlockSpec((1, 1, rows), nxt, memory_space=pltpu.SMEM),
                  pl.BlockSpec((1, 1, rows), cur),
                  any_spec, wspec, wspec, w2spec, any_spec],
        out_specs=any_spec,
        out_shape=jax.ShapeDtypeStruct((n, D_MODEL), F32),
        scratch_shapes=[pltpu.VMEM((2, rows, D_MODEL), F32), pltpu.VMEM((2, rows, D_MODEL), F32),
                        pltpu.SemaphoreType.DMA((3, 2))],
        input_output_aliases={7: 0},
        compiler_params=_cparams(2),
        name="expert_ffn",
    )(idx3, idx3, gate3, h2, w1, w3, w2, xmid)


def _t5_bucket(rel):
    half = REL_BUCKETS // 2
    max_exact = half // 2
    base = np.where(rel > 0, half, 0)
    n = np.abs(rel)
    large = max_exact + (np.log(np.maximum(n, 1) / max_exact) / math.log(REL_MAX_DIST / max_exact)
                         * (half - max_exact)).astype(np.int32)
    large = np.minimum(large, half - 1)
    return (base + np.where(n < max_exact, n, large)).astype(np.int32)


def _head_perm():
    nq = ATTN_HEADS // 2
    cols = []
    for j in range(nq):
        for half in range(2):
            h = j + nq * half
            cols.extend(range(h * ATTN_HEAD_DIM, (h + 1) * ATTN_HEAD_DIM))
    return np.asarray(cols, np.int32)


def _attn_bias_tables(rel_bias):
    q_pos = np.arange(BLOCK)[:, None]
    k_off = np.arange(3 * BLOCK)[None, :] - BLOCK
    rel = k_off - q_pos
    in_window = np.abs(rel) <= WINDOW
    bias = rel_bias.astype(F32)[_t5_bucket(rel)]
    bias = jnp.transpose(bias, (2, 0, 1))
    col = np.arange(3 * BLOCK)[None, :]
    tables = []
    for valid in (col >= BLOCK, np.ones_like(col, bool), col < 2 * BLOCK):
        t = jnp.where(jnp.asarray(in_window & valid)[None], bias, NEG)
        nq = ATTN_HEADS // 2
        rows = [jnp.concatenate([t[j], t[j + nq]], axis=1) for j in range(nq)]
        tables.append(jnp.concatenate(rows, axis=0))
    return jnp.stack(tables)


def _retention_tables(decay_logit, norm_g):
    cr = RET_CHUNK
    lg = jax.nn.log_sigmoid(decay_logit.astype(F32))
    lgf, lgb = lg[0][:, None, None], lg[1][:, None, None]
    pos = np.arange(cr, dtype=np.float32)
    dist = pos[:, None] - pos[None, :]
    scale = RET_DIM ** -0.5
    dmask = jnp.where(jnp.asarray(dist >= 0)[None],
                      jnp.exp(lgf * np.maximum(dist, 0.0)[None]),
                      jnp.exp(lgb * np.maximum(-dist, 0.0)[None])) * scale
    col = lambda v: jnp.broadcast_to(v[:, :, None], (RET_HEADS, cr, RET_DIM))
    rowf = col(jnp.exp(lg[0][:, None] * pos[None]))
    rowb = col(jnp.exp(lg[1][:, None] * (cr - 1.0 - pos)[None]))
    wkf = col(jnp.exp(lg[0][:, None] * (cr - pos)[None]) * scale)
    wkb = col(jnp.exp(lg[1][:, None] * (pos + 1.0)[None]) * scale)
    dec = jnp.concatenate([jnp.exp(lg[0] * cr), jnp.exp(lg[1] * cr)])
    return dec, dmask, rowf, rowb, wkf, wkb, norm_g.astype(F32)


def _layer(x2, b, s, p):
    qa, ka, va, qr, kr, vr, gr, ga, gt = _in_proj(x2, p["g1"], p["w_in"], p["qg"], p["kg"], p["bdq"], p["bdk"])
    attn = _attention(qa, ka, va, p["bias3"], p["sink"], b, s)
    retn = _retention(qr, kr, vr, gr, p["retn"], b, s)
    xmid, h2, afft = _merge(attn, retn, ga, gt, x2, p["wba"], p["wbr"], p["wo"], p["g2"], p["wr"])
    n = b * s
    cap = max(1, EC_CAPACITY_FACTOR * n // N_EXPERTS)
    idx, gate = _select(afft, cap)
    return _expert_ffn(h2, xmid, idx, gate, p["w1"], p["w3"], p["w2"])


def kernel(x_prompt, x_sample, norm_mix_g, w_in, q_norm_g, k_norm_g, attn_sink, rel_bias, retn_decay_logit, retn_norm_g, w_branch_attn, w_branch_retn, w_out, norm_ffn_g, w_router, w_exp_gate, w_exp_up, w_exp_down):
    depth = w_in.shape[0]
    perm = _head_perm()
    bias3 = _attn_bias_tables(rel_bias)
    bdq = jnp.asarray(np.kron(np.eye(ATTN_HEADS), np.ones((ATTN_HEAD_DIM, ATTN_HEAD_DIM))), BF16)
    bdk = jnp.asarray(np.kron(np.eye(ATTN_KV_HEADS), np.ones((ATTN_HEAD_DIM, ATTN_HEAD_DIM))), BF16)
    layers = []
    for l in range(depth):
        w = w_in[l]
        w = jnp.concatenate([w[:, :ATTN_WIDTH][:, perm], w[:, ATTN_WIDTH:]], axis=1).astype(BF16)
        wr = jnp.pad(w_router[l], ((0, 0), (0, LANES - N_EXPERTS))).astype(BF16)
        layers.append(dict(
            g1=norm_mix_g[l].astype(F32)[None], w_in=w,
            qg=(jnp.tile(q_norm_g[l].astype(F32), ATTN_HEADS) * (ATTN_HEAD_DIM ** -0.5))[None],
            kg=jnp.tile(k_norm_g[l].astype(F32), ATTN_KV_HEADS)[None],
            bdq=bdq, bdk=bdk, bias3=bias3, sink=attn_sink[l].astype(F32),
            retn=_retention_tables(retn_decay_logit[l], retn_norm_g[l]),
            wba=w_branch_attn[l][perm, :].astype(BF16), wbr=w_branch_retn[l].astype(BF16),
            wo=w_out[l].astype(BF16), g2=norm_ffn_g[l].astype(F32)[None], wr=wr,
            w1=w_exp_gate[l].astype(BF16), w3=w_exp_up[l].astype(BF16), w2=w_exp_down[l].astype(BF16)))

    def trunk(x):
        b, s, d = x.shape
        x2 = x.reshape(b * s, d)
        for p in layers:
            x2 = _layer(x2, b, s, p)
        return x2.reshape(b, s, d)

    return (trunk(x_prompt), trunk(x_sample))
```

```python
import functools
import math

import numpy as np
import jax
import jax.numpy as jnp
from jax import lax
from jax.experimental import pallas as pl
from jax.experimental.pallas import tpu as pltpu
from jax.experimental.pallas import tpu_sc as plsc

D_MODEL = 1024
ATTN_HEADS = 8
ATTN_KV_HEADS = 2
ATTN_HEAD_DIM = 64
WINDOW = 128
BLOCK = 128
REL_BUCKETS = 32
REL_MAX_DIST = 128
RET_HEADS = 4
RET_DIM = 128
N_EXPERTS = 16
EC_CAPACITY_FACTOR = 2
EXPERT_FF = 1024
EPS = 1e-6

ATTN_WIDTH = ATTN_HEADS * ATTN_HEAD_DIM
KV_WIDTH = ATTN_KV_HEADS * ATTN_HEAD_DIM
RET_WIDTH = RET_HEADS * RET_DIM
IN_SPLITS = (ATTN_WIDTH, KV_WIDTH, KV_WIDTH, RET_WIDTH, RET_WIDTH, RET_WIDTH, RET_WIDTH, D_MODEL, D_MODEL)
IN_OFFSETS = tuple(int(o) for o in np.cumsum((0,) + IN_SPLITS))

LANES = 128
VMEM_LIMIT_BYTES = 56 * 1024 * 1024

TOKEN_TILE = 512
RET_CHUNK = 256
FFN_ROWS = 256
SELECT_BLOCKS = 128
SELECT_SLOTS = 1024
SC_WINDOW = 128
SC_ROW = 256
SC_PIECES = D_MODEL // SC_ROW

F32 = jnp.float32
BF16 = jnp.bfloat16
NEG = -1e30


def _cparams(n_axes, vmem=VMEM_LIMIT_BYTES):
    return pltpu.CompilerParams(dimension_semantics=("arbitrary",) * n_axes, vmem_limit_bytes=vmem)


def _sigmoid(x):
    return 1.0 / (1.0 + jnp.exp(-x))


def _in_proj_kernel(x_ref, g_ref, w_ref, qg_ref, kg_ref, bdq_ref, bdk_ref,
                    qa_ref, ka_ref, va_ref, qr_ref, kr_ref, vr_ref, gr_ref, ga_ref, gt_ref):
    x = x_ref[...]
    ms = jnp.mean(x * x, axis=-1, keepdims=True)
    h = (x * lax.rsqrt(ms + EPS) * g_ref[...]).astype(BF16)

    def mm(k):
        return jnp.dot(h, w_ref[:, IN_OFFSETS[k]:IN_OFFSETS[k + 1]], preferred_element_type=F32)

    def head_norm(t, bd_ref, gain_ref):
        ss = jnp.dot((t * t).astype(BF16), bd_ref[...], preferred_element_type=F32)
        return t * lax.rsqrt(ss * (1.0 / ATTN_HEAD_DIM) + EPS) * gain_ref[...]

    qa_ref[...] = head_norm(mm(0), bdq_ref, qg_ref).astype(BF16)
    ka_ref[...] = head_norm(mm(1), bdk_ref, kg_ref).astype(BF16)
    for k, ref in ((2, va_ref), (3, qr_ref), (4, kr_ref), (5, vr_ref), (6, gr_ref), (7, ga_ref), (8, gt_ref)):
        ref[...] = mm(k).astype(BF16)


def _in_proj(x2, g, w, qg, kg, bdq, bdk):
    n = x2.shape[0]
    tm = TOKEN_TILE
    full = lambda a: pl.BlockSpec(a.shape, lambda i: (0,) * a.ndim)
    widths = IN_SPLITS
    return pl.pallas_call(
        _in_proj_kernel,
        grid=(n // tm,),
        in_specs=[pl.BlockSpec((tm, D_MODEL), lambda i: (i, 0)), full(g), full(w), full(qg), full(kg),
                  full(bdq), full(bdk)],
        out_specs=[pl.BlockSpec((tm, wd), lambda i: (i, 0)) for wd in widths],
        out_shape=[jax.ShapeDtypeStruct((n, wd), BF16) for wd in widths],
        compiler_params=_cparams(1),
        name="in_proj",
    )(x2, g, w, qg, kg, bdq, bdk)


def _attn_kernel(sink_ref, q_ref, kp_ref, kc_ref, kn_ref, vp_ref, vc_ref, vn_ref, bias_ref, o_ref):
    nq = ATTN_HEADS // 2
    q = q_ref[...]
    qs = jnp.concatenate([q[:, j * LANES:(j + 1) * LANES] for j in range(nq)], axis=0)
    k = jnp.concatenate([kp_ref[...], kc_ref[...], kn_ref[...]], axis=0)
    v = jnp.concatenate([vp_ref[...], vc_ref[...], vn_ref[...]], axis=0)
    low = lax.broadcasted_iota(jnp.int32, k.shape, 1) < ATTN_HEAD_DIM
    zero = jnp.zeros_like(k)
    kbd = jnp.concatenate([jnp.where(low, k, zero), jnp.where(low, zero, k)], axis=0)
    vbd = jnp.concatenate([jnp.where(low, v, zero), jnp.where(low, zero, v)], axis=0)
    s = lax.dot_general(qs, kbd, (((1,), (1,)), ((), ())), preferred_element_type=F32)
    s = s + bias_ref[0]
    nk = 3 * BLOCK
    probs, invs = [], []
    for j in range(nq):
        row_p, row_inv = [], []
        for half in range(2):
            sj = s[j * BLOCK:(j + 1) * BLOCK, half * nk:(half + 1) * nk]
            sk = sink_ref[j + nq * half]
            m = jnp.maximum(jnp.max(sj, axis=-1, keepdims=True), sk)
            p = jnp.exp(sj - m)
            den = jnp.sum(p, axis=-1, keepdims=True) + jnp.exp(sk - m)
            row_p.append(p.astype(BF16))
            row_inv.append(1.0 / den)
        probs.append(jnp.concatenate(row_p, axis=1))
        invs.append(row_inv)
    pm = jnp.concatenate(probs, axis=0)
    o = jnp.dot(pm, vbd, preferred_element_type=F32)
    low_o = lax.broadcasted_iota(jnp.int32, (BLOCK, LANES), 1) < ATTN_HEAD_DIM
    for j in range(nq):
        inv = jnp.where(low_o, invs[j][0], invs[j][1])
        o_ref[:, j * LANES:(j + 1) * LANES] = (o[j * BLOCK:(j + 1) * BLOCK] * inv).astype(BF16)


def _attention(qa, ka, va, bias3, sink, b, s):
    nb = s // BLOCK
    assert nb >= 2
    n = b * s

    def kv_spec(off):
        return pl.BlockSpec((BLOCK, KV_WIDTH), lambda bi, ni: (bi * nb + jnp.clip(ni + off, 0, nb - 1), 0))

    def bias_map(bi, ni):
        return (jnp.where(ni == 0, 0, jnp.where(ni == nb - 1, 2, 1)), 0, 0)

    return pl.pallas_call(
        _attn_kernel,
        grid=(b, nb),
        in_specs=[pl.BlockSpec(memory_space=pltpu.SMEM),
                  pl.BlockSpec((BLOCK, ATTN_WIDTH), lambda bi, ni: (bi * nb + ni, 0)),
                  kv_spec(-1), kv_spec(0), kv_spec(1), kv_spec(-1), kv_spec(0), kv_spec(1),
                  pl.BlockSpec((1,) + bias3.shape[1:], bias_map)],
        out_specs=pl.BlockSpec((BLOCK, ATTN_WIDTH), lambda bi, ni: (bi * nb + ni, 0)),
        out_shape=jax.ShapeDtypeStruct((n, ATTN_WIDTH), BF16),
        compiler_params=_cparams(2),
        name="attn",
    )(sink, qa, ka, ka, ka, va, va, va, bias3)


def _retn_kernel(dec_ref, q_ref, k_ref, v_ref, g_ref, dmask_ref, rowf_ref, rowb_ref, wkf_ref, wkb_ref, ng_ref,
                 o_ref, tstore, uf, tb, *, nc):
    p = pl.program_id(1)
    n = pl.program_id(2)
    tn = (((0,), (0,)), ((), ()))
    nt = (((1,), (1,)), ((), ()))
    hs = lambda h: slice(h * RET_DIM, (h + 1) * RET_DIM)

    @pl.when(p == 0)
    def _():
        @pl.when(n == 0)
        def _():
            tb[...] = jnp.zeros_like(tb)
        c = nc - 1 - n
        for h in range(RET_HEADS):
            tstore[c, h] = tb[h].astype(BF16)
            kw = (k_ref[:, hs(h)].astype(F32) * wkb_ref[h]).astype(BF16)
            upd = lax.dot_general(kw, v_ref[:, hs(h)], tn, preferred_element_type=F32)
            tb[h] = dec_ref[RET_HEADS + h] * tb[h] + upd

    @pl.when(p == 1)
    def _():
        @pl.when(n == 0)
        def _():
            uf[...] = jnp.zeros_like(uf)
        for h in range(RET_HEADS):
            qh = q_ref[:, hs(h)]
            kh = k_ref[:, hs(h)]
            vh = v_ref[:, hs(h)]
            sc = lax.dot_general(qh, kh, nt, preferred_element_type=F32) * dmask_ref[h]
            intra = jnp.dot(sc.astype(BF16), vh, preferred_element_type=F32)
            states = jnp.concatenate([uf[h].astype(BF16), tstore[n, h]], axis=1)
            cross = jnp.dot(qh, states, preferred_element_type=F32)
            o = intra + cross[:, :RET_DIM] * rowf_ref[h] + cross[:, RET_DIM:] * rowb_ref[h]
            mu = jnp.mean(o, axis=-1, keepdims=True)
            d = o - mu
            var = jnp.mean(d * d, axis=-1, keepdims=True)
            on = d * lax.rsqrt(var + EPS) * ng_ref[h:h + 1, :]
            gate = g_ref[:, hs(h)].astype(F32)
            o_ref[:, hs(h)] = (gate * _sigmoid(gate) * on).astype(BF16)
            kw = (kh.astype(F32) * wkf_ref[h]).astype(BF16)
            uf[h] = dec_ref[h] * uf[h] + lax.dot_general(kw, vh, tn, preferred_element_type=F32)


def _retention(qr, kr, vr, gr, tables, b, s):
    dec, dmask, rowf, rowb, wkf, wkb, ng = tables
    cr = RET_CHUNK
    nc = s // cr
    n = b * s
    full = lambda a: pl.BlockSpec(a.shape, lambda bi, pi, ni: (0,) * a.ndim)
    fwd_spec = pl.BlockSpec((cr, RET_WIDTH), lambda bi, pi, ni: (bi * nc + ni * pi, 0))
    kv_spec = pl.BlockSpec((cr, RET_WIDTH), lambda bi, pi, ni: (bi * nc + ni * pi + (1 - pi) * (nc - 1 - ni), 0))
    return pl.pallas_call(
        functools.partial(_retn_kernel, nc=nc),
        grid=(b, 2, nc),
        in_specs=[pl.BlockSpec(memory_space=pltpu.SMEM), fwd_spec, kv_spec, kv_spec, fwd_spec,
                  full(dmask), full(rowf), full(rowb), full(wkf), full(wkb), full(ng)],
        out_specs=fwd_spec,
        out_shape=jax.ShapeDtypeStruct((n, RET_WIDTH), BF16),
        scratch_shapes=[pltpu.VMEM((nc, RET_HEADS, RET_DIM, RET_DIM), BF16),
                        pltpu.VMEM((RET_HEADS, RET_DIM, RET_DIM), F32),
                        pltpu.VMEM((RET_HEADS, RET_DIM, RET_DIM), F32)],
        compiler_params=_cparams(3),
        name="retention",
    )(dec, qr, kr, vr, gr, dmask, rowf, rowb, wkf, wkb, ng)


def _merge_kernel(attn_ref, retn_ref, ga_ref, gr_ref, x_ref, wba_ref, wbr_ref, wo_ref, g2_ref, wr_ref,
                  xmid_ref, afft_ref, *h2_refs):
    a = jnp.dot(attn_ref[...], wba_ref[...], preferred_element_type=F32)
    r = jnp.dot(retn_ref[...], wbr_ref[...], preferred_element_type=F32)
    merged = _sigmoid(ga_ref[...].astype(F32)) * a + _sigmoid(gr_ref[...].astype(F32)) * r
    xn = x_ref[...] + jnp.dot(merged.astype(BF16), wo_ref[...], preferred_element_type=F32)
    xmid_ref[...] = xn
    ms = jnp.mean(xn * xn, axis=-1, keepdims=True)
    h2 = xn * lax.rsqrt(ms + EPS) * g2_ref[...]
    for c, ref in enumerate(h2_refs):
        ref[...] = h2[:, c * SC_ROW:(c + 1) * SC_ROW]
    logits = jnp.dot(h2.astype(BF16), wr_ref[...], preferred_element_type=F32)
    real = lax.broadcasted_iota(jnp.int32, logits.shape, 1) < N_EXPERTS
    logits = jnp.where(real, logits, -jnp.inf)
    m = jnp.max(logits, axis=-1, keepdims=True)
    ex = jnp.exp(logits - m)
    aff = ex / jnp.sum(ex, axis=-1, keepdims=True)
    afft_ref[...] = aff.T[:N_EXPERTS, :]


def _merge(attn, retn, ga, gr, x2, wba, wbr, wo, g2, wr):
    n = x2.shape[0]
    tm = TOKEN_TILE
    full = lambda a: pl.BlockSpec(a.shape, lambda i: (0,) * a.ndim)
    row = lambda wd: pl.BlockSpec((tm, wd), lambda i: (i, 0))
    return pl.pallas_call(
        _merge_kernel,
        grid=(n // tm,),
        in_specs=[row(ATTN_WIDTH), row(RET_WIDTH), row(D_MODEL), row(D_MODEL), row(D_MODEL),
                  full(wba), full(wbr), full(wo), full(g2), full(wr)],
        out_specs=[row(D_MODEL), pl.BlockSpec((N_EXPERTS, tm), lambda i: (0, i))] + [row(SC_ROW)] * SC_PIECES,
        out_shape=[jax.ShapeDtypeStruct((n, D_MODEL), F32), jax.ShapeDtypeStruct((N_EXPERTS, n), F32)]
        + [jax.ShapeDtypeStruct((n, SC_ROW), F32)] * SC_PIECES,
        compiler_params=_cparams(1),
        name="merge",
    )(attn, retn, ga, gr, x2, wba, wbr, wo, g2, wr)


def _select_kernel(aff_ref, u_ref, ls_ref, idx_ref, gate_ref, cgt, afft, offi, *, cap, tb):
    e = pl.program_id(0)
    j = pl.program_id(1)
    nblk = SELECT_BLOCKS
    pc = SELECT_SLOTS

    def cumsum(mask):
        inb = jnp.dot(mask.astype(BF16), u_ref[...], preferred_element_type=F32)
        tot = jnp.broadcast_to(inb[:, tb - 1:tb], (nblk, LANES))
        off = jnp.dot(ls_ref[...], tot, preferred_element_type=F32, precision=lax.Precision.HIGHEST)
        return inb, off[:, 0:1], tot[:, 0:1]

    @pl.when(j == 0)
    def _():
        a = aff_ref[e]
        bits = pltpu.bitcast(a, jnp.int32)

        def bit_step(t, cur):
            cand = cur | jnp.left_shift(jnp.int32(1), 30 - t)
            cnt = jnp.sum((bits >= cand).astype(jnp.int32), keepdims=True)
            return jnp.where(cnt >= cap, cand, cur)

        thr = lax.fori_loop(0, 31, bit_step, jnp.zeros((1, 1), jnp.int32))
        gt = bits > thr
        eq = bits == thr
        need = (cap - jnp.sum(gt.astype(jnp.int32), keepdims=True)).astype(F32)
        eqf = eq.astype(F32)
        eq_in, eq_off, _ = cumsum(eqf)
        eq_rank = eq_in + eq_off - eqf
        sel = jnp.logical_or(gt, jnp.logical_and(eq, eq_rank < need)).astype(F32)
        s_in, s_off, s_tot = cumsum(sel)
        cgt[...] = (s_in + s_off).T
        afft[...] = a.T
        offi[...] = jnp.broadcast_to(s_off + s_tot, (nblk, LANES))

    slot = (j * pc + lax.broadcasted_iota(jnp.int32, (1, pc), 1)).astype(F32)
    blk = jnp.sum((offi[:, 0:1] <= slot).astype(jnp.int32), axis=0, keepdims=True)
    onehot = (lax.broadcasted_iota(jnp.int32, (nblk, pc), 0) == blk).astype(F32)
    g = jnp.dot(cgt[...], onehot, preferred_element_type=F32, precision=lax.Precision.HIGHEST)
    inb = jnp.sum((g <= slot + 0.5).astype(jnp.int32), axis=0, keepdims=True)
    idx_ref[0] = blk * tb + inb
    ag = jnp.dot(afft[...], onehot, preferred_element_type=F32, precision=lax.Precision.HIGHEST)
    hit = lax.broadcasted_iota(jnp.int32, (tb, pc), 0) == inb
    gate_ref[0] = jnp.sum(jnp.where(hit, ag, 0.0), axis=0, keepdims=True)


def _select(afft, cap):
    n = afft.shape[1]
    nblk = SELECT_BLOCKS
    tb = n // nblk
    pc = min(SELECT_SLOTS, cap)
    assert n % nblk == 0 and tb % LANES == 0 and cap % pc == 0 and pc == SELECT_SLOTS
    nch = cap // pc
    aff3 = afft.reshape(N_EXPERTS, nblk, tb)
    upper = jnp.asarray(np.triu(np.ones((tb, tb), np.float32)), BF16)
    lstrict = jnp.asarray(np.tril(np.ones((nblk, nblk), np.float32), -1))
    full = lambda a: pl.BlockSpec(a.shape, lambda e, j: (0,) * a.ndim)
    out_spec = pl.BlockSpec((1, 1, pc), lambda e, j: (e * nch + j, 0, 0))
    idx, gate = pl.pallas_call(
        functools.partial(_select_kernel, cap=cap, tb=tb),
        grid=(N_EXPERTS, nch),
        in_specs=[full(aff3), full(upper), full(lstrict)],
        out_specs=[out_spec, out_spec],
        out_shape=[jax.ShapeDtypeStruct((N_EXPERTS * nch, 1, pc), jnp.int32),
                   jax.ShapeDtypeStruct((N_EXPERTS * nch, 1, pc), F32)],
        scratch_shapes=[pltpu.VMEM((tb, nblk), F32), pltpu.VMEM((tb, nblk), F32), pltpu.VMEM((nblk, LANES), F32)],
        compiler_params=_cparams(2),
        name="select",
    )(aff3, upper, lstrict)
    return idx.reshape(N_EXPERTS, cap), gate.reshape(N_EXPERTS, cap)


def _sc_gather(table, idx):
    m = idx.shape[0]
    d = table.shape[1]
    assert m % SC_WINDOW == 0
    mesh = plsc.VectorSubcoreMesh(core_axis_name="c", subcore_axis_name="s")

    @functools.partial(pl.kernel, out_type=jax.ShapeDtypeStruct((m, d), table.dtype), mesh=mesh, name="sc_gather")
    def gather(x_hbm, i_hbm, o_hbm):
        def body(i_vmem, o_vmem):
            pltpu.sync_copy(x_hbm.at[i_vmem.at[0]], o_vmem)

        pltpu.emit_pipeline(
            body,
            grid=(m // SC_WINDOW,),
            in_specs=[pl.BlockSpec((1, SC_WINDOW), lambda i: (0, i))],
            out_specs=[pl.BlockSpec((SC_WINDOW, d), lambda i: (i, 0))],
            core_axis_name=("c", "s"),
            dimension_semantics=(pltpu.PARALLEL,),
        )(i_hbm, o_hbm)

    return gather(table, idx.reshape(1, m))


def _ffn_kernel(idx_ref, nxt_ref, gate_ref, x0_ref, x1_ref, x2_ref, x3_ref, w1_ref, w3_ref, w2_ref, yin_hbm, y_hbm,
                ybuf, sems, *, nt):
    del yin_hbm
    rows = FFN_ROWS
    e = pl.program_id(0)
    i = pl.program_id(1)
    step = e * nt + i
    nsteps = N_EXPERTS * nt
    slot = lax.rem(step, 2)
    other = 1 - slot

    def row_copies(ids_ref, b, sem_row, to_hbm):
        def body(r, carry):
            tok = ids_ref[0, 0, r]
            hbm_row = y_hbm.at[pl.ds(tok, 1)]
            vm_row = ybuf.at[b, pl.ds(r, 1)]
            if to_hbm:
                pltpu.make_async_copy(vm_row, hbm_row, sems.at[sem_row, b]).start()
            else:
                pltpu.make_async_copy(hbm_row, vm_row, sems.at[sem_row, b]).start()
            return carry
        lax.fori_loop(0, rows, body, 0, unroll=8)

    def wait_all(b, sem_row):
        pltpu.make_async_copy(y_hbm.at[pl.ds(0, rows)], ybuf.at[b], sems.at[sem_row, b]).wait()

    @pl.when(step == 0)
    def _():
        row_copies(idx_ref, 0, 0, False)

    @pl.when(step > 0)
    def _():
        wait_all(other, 1)

    @pl.when(jnp.logical_and(step + 1 < nsteps, i + 1 < nt))
    def _():
        row_copies(nxt_ref, other, 0, False)

    @pl.when(jnp.logical_and(i == 0, step > 0))
    def _():
        row_copies(idx_ref, slot, 0, False)

    xs = jnp.concatenate([r[...] for r in (x0_ref, x1_ref, x2_ref, x3_ref)], axis=1).astype(BF16)
    hg = jnp.dot(xs, w1_ref[0], preferred_element_type=F32)
    hu = jnp.dot(xs, w3_ref[0], preferred_element_type=F32)
    hid = (hg * _sigmoid(hg) * hu).astype(BF16)
    out = jnp.dot(hid, w2_ref[0], preferred_element_type=F32)
    eye = lax.broadcasted_iota(jnp.int32, (rows, rows), 0) == lax.broadcasted_iota(jnp.int32, (rows, rows), 1)
    gcol = jnp.sum(jnp.where(eye, gate_ref[0], 0.0), axis=1, keepdims=True)
    wait_all(slot, 0)
    ybuf[slot] = ybuf[slot] + out * gcol
    row_copies(idx_ref, slot, 1, True)

    @pl.when(step == nsteps - 1)
    def _():
        wait_all(slot, 1)


def _expert_ffn(xs, xmid, idx, gate, w1, w3, w2):
    n = xmid.shape[0]
    cap = idx.shape[1]
    rows = FFN_ROWS
    nt = cap // rows
    nsteps = N_EXPERTS * nt
    idx3 = idx.reshape(nsteps, 1, rows)
    gate3 = gate.reshape(nsteps, 1, rows)
    cur = lambda e, i: (e * nt + i, 0, 0)
    nxt = lambda e, i: (jnp.minimum(e * nt + i + 1, nsteps - 1), 0, 0)
    wspec = pl.BlockSpec((1, D_MODEL, EXPERT_FF), lambda e, i: (e, 0, 0))
    w2spec = pl.BlockSpec((1, EXPERT_FF, D_MODEL), lambda e, i: (e, 0, 0))
    any_spec = pl.BlockSpec(memory_space=pl.ANY)
    return pl.pallas_call(
        functools.partial(_ffn_kernel, nt=nt),
        grid=(N_EXPERTS, nt),
        in_specs=[pl.BlockSpec((1, 1, rows), cur, memory_space=pltpu.SMEM),
                  pl.BlockSpec((1, 1, rows), nxt, memory_space=pltpu.SMEM),
                  pl.BlockSpec((1, 1, rows), cur),
                  *[pl.BlockSpec((rows, SC_ROW), lambda e, i: (e * nt + i, 0))] * SC_PIECES,
                  wspec, wspec, w2spec, any_spec],
        out_specs=any_spec,
        out_shape=jax.ShapeDtypeStruct((n, D_MODEL), F32),
        scratch_shapes=[pltpu.VMEM((2, rows, D_MODEL), F32), pltpu.SemaphoreType.DMA((2, 2))],
        input_output_aliases={6 + SC_PIECES: 0},
        compiler_params=_cparams(2),
        name="expert_ffn",
    )(idx3, idx3, gate3, *xs, w1, w3, w2, xmid)


def _t5_bucket(rel):
    half = REL_BUCKETS // 2
    max_exact = half // 2
    base = np.where(rel > 0, half, 0)
    n = np.abs(rel)
    large = max_exact + (np.log(np.maximum(n, 1) / max_exact) / math.log(REL_MAX_DIST / max_exact)
                         * (half - max_exact)).astype(np.int32)
    large = np.minimum(large, half - 1)
    return (base + np.where(n < max_exact, n, large)).astype(np.int32)


def _head_perm():
    nq = ATTN_HEADS // 2
    cols = []
    for j in range(nq):
        for half in range(2):
            h = j + nq * half
            cols.extend(range(h * ATTN_HEAD_DIM, (h + 1) * ATTN_HEAD_DIM))
    return np.asarray(cols, np.int32)


def _attn_bias_tables(rel_bias):
    q_pos = np.arange(BLOCK)[:, None]
    k_off = np.arange(3 * BLOCK)[None, :] - BLOCK
    rel = k_off - q_pos
    in_window = np.abs(rel) <= WINDOW
    bias = rel_bias.astype(F32)[_t5_bucket(rel)]
    bias = jnp.transpose(bias, (2, 0, 1))
    col = np.arange(3 * BLOCK)[None, :]
    tables = []
    for valid in (col >= BLOCK, np.ones_like(col, bool), col < 2 * BLOCK):
        t = jnp.where(jnp.asarray(in_window & valid)[None], bias, NEG)
        nq = ATTN_HEADS // 2
        rows = [jnp.concatenate([t[j], t[j + nq]], axis=1) for j in range(nq)]
        tables.append(jnp.concatenate(rows, axis=0))
    return jnp.stack(tables)


def _retention_tables(decay_logit, norm_g):
    cr = RET_CHUNK
    lg = jax.nn.log_sigmoid(decay_logit.astype(F32))
    lgf, lgb = lg[0][:, None, None], lg[1][:, None, None]
    pos = np.arange(cr, dtype=np.float32)
    dist = pos[:, None] - pos[None, :]
    scale = RET_DIM ** -0.5
    dmask = jnp.where(jnp.asarray(dist >= 0)[None],
                      jnp.exp(lgf * np.maximum(dist, 0.0)[None]),
                      jnp.exp(lgb * np.maximum(-dist, 0.0)[None])) * scale
    col = lambda v: jnp.broadcast_to(v[:, :, None], (RET_HEADS, cr, RET_DIM))
    rowf = col(jnp.exp(lg[0][:, None] * pos[None]))
    rowb = col(jnp.exp(lg[1][:, None] * (cr - 1.0 - pos)[None]))
    wkf = col(jnp.exp(lg[0][:, None] * (cr - pos)[None]) * scale)
    wkb = col(jnp.exp(lg[1][:, None] * (pos + 1.0)[None]) * scale)
    dec = jnp.concatenate([jnp.exp(lg[0] * cr), jnp.exp(lg[1] * cr)])
    return dec, dmask, rowf, rowb, wkf, wkb, norm_g.astype(F32)


def _layer(x2, b, s, p):
    qa, ka, va, qr, kr, vr, gr, ga, gt = _in_proj(x2, p["g1"], p["w_in"], p["qg"], p["kg"], p["bdq"], p["bdk"])
    attn = _attention(qa, ka, va, p["bias3"], p["sink"], b, s)
    retn = _retention(qr, kr, vr, gr, p["retn"], b, s)
    xmid, afft, *h2 = _merge(attn, retn, ga, gt, x2, p["wba"], p["wbr"], p["wo"], p["g2"], p["wr"])
    n = b * s
    cap = max(1, EC_CAPACITY_FACTOR * n // N_EXPERTS)
    idx, gate = _select(afft, cap)
    flat = idx.reshape(-1)
    xs = [_sc_gather(piece, flat) for piece in h2]
    return _expert_ffn(xs, xmid, idx, gate, p["w1"], p["w3"], p["w2"])


def kernel(x_prompt, x_sample, norm_mix_g, w_in, q_norm_g, k_norm_g, attn_sink, rel_bias, retn_decay_logit, retn_norm_g, w_branch_attn, w_branch_retn, w_out, norm_ffn_g, w_router, w_exp_gate, w_exp_up, w_exp_down):
    depth = w_in.shape[0]
    perm = _head_perm()
    bias3 = _attn_bias_tables(rel_bias)
    bdq = jnp.asarray(np.kron(np.eye(ATTN_HEADS), np.ones((ATTN_HEAD_DIM, ATTN_HEAD_DIM))), BF16)
    bdk = jnp.asarray(np.kron(np.eye(ATTN_KV_HEADS), np.ones((ATTN_HEAD_DIM, ATTN_HEAD_DIM))), BF16)
    layers = []
    for l in range(depth):
        w = w_in[l]
        w = jnp.concatenate([w[:, :ATTN_WIDTH][:, perm], w[:, ATTN_WIDTH:]], axis=1).astype(BF16)
        wr = jnp.pad(w_router[l], ((0, 0), (0, LANES - N_EXPERTS))).astype(BF16)
        layers.append(dict(
            g1=norm_mix_g[l].astype(F32)[None], w_in=w,
            qg=(jnp.tile(q_norm_g[l].astype(F32), ATTN_HEADS) * (ATTN_HEAD_DIM ** -0.5))[None],
            kg=jnp.tile(k_norm_g[l].astype(F32), ATTN_KV_HEADS)[None],
            bdq=bdq, bdk=bdk, bias3=bias3, sink=attn_sink[l].astype(F32),
            retn=_retention_tables(retn_decay_logit[l], retn_norm_g[l]),
            wba=w_branch_attn[l][perm, :].astype(BF16), wbr=w_branch_retn[l].astype(BF16),
            wo=w_out[l].astype(BF16), g2=norm_ffn_g[l].astype(F32)[None], wr=wr,
            w1=w_exp_gate[l].astype(BF16), w3=w_exp_up[l].astype(BF16), w2=w_exp_down[l].astype(BF16)))

    def trunk(x):
        b, s, d = x.shape
        x2 = x.reshape(b * s, d)
        for p in layers:
            x2 = _layer(x2, b, s, p)
        return x2.reshape(b, s, d)

    return (trunk(x_prompt), trunk(x_sample))
```

```python
import functools
import math

import numpy as np
import jax
import jax.numpy as jnp
from jax import lax
from jax.experimental import pallas as pl
from jax.experimental.pallas import tpu as pltpu
from jax.experimental.pallas import tpu_sc as plsc

D_MODEL = 1024
ATTN_HEADS = 8
ATTN_KV_HEADS = 2
ATTN_HEAD_DIM = 64
WINDOW = 128
BLOCK = 128
REL_BUCKETS = 32
REL_MAX_DIST = 128
RET_HEADS = 4
RET_DIM = 128
N_EXPERTS = 16
EC_CAPACITY_FACTOR = 2
EXPERT_FF = 1024
EPS = 1e-6

ATTN_WIDTH = ATTN_HEADS * ATTN_HEAD_DIM
KV_WIDTH = ATTN_KV_HEADS * ATTN_HEAD_DIM
RET_WIDTH = RET_HEADS * RET_DIM
IN_SPLITS = (ATTN_WIDTH, KV_WIDTH, KV_WIDTH, RET_WIDTH, RET_WIDTH, RET_WIDTH, RET_WIDTH, D_MODEL, D_MODEL)
IN_OFFSETS = tuple(int(o) for o in np.cumsum((0,) + IN_SPLITS))

LANES = 128
VMEM_LIMIT_BYTES = 56 * 1024 * 1024

TOKEN_TILE = 512
RET_CHUNK = 256
FFN_ROWS = 256
SELECT_BLOCKS = 128
SELECT_SLOTS = 1024
SC_WINDOW = 128
SC_ROW = 256
SC_PIECES = D_MODEL // SC_ROW
COMBINE_TOKENS = 256
COMBINE_WINDOW = 768

F32 = jnp.float32
BF16 = jnp.bfloat16
NEG = -1e30


def _cparams(n_axes, vmem=VMEM_LIMIT_BYTES):
    return pltpu.CompilerParams(dimension_semantics=("arbitrary",) * n_axes, vmem_limit_bytes=vmem)


def _sigmoid(x):
    return 1.0 / (1.0 + jnp.exp(-x))


def _in_proj_kernel(x_ref, g_ref, w_ref, qg_ref, kg_ref, bdq_ref, bdk_ref,
                    qa_ref, ka_ref, va_ref, qr_ref, kr_ref, vr_ref, gr_ref, ga_ref, gt_ref):
    x = x_ref[...]
    ms = jnp.mean(x * x, axis=-1, keepdims=True)
    h = (x * lax.rsqrt(ms + EPS) * g_ref[...]).astype(BF16)

    def mm(k):
        return jnp.dot(h, w_ref[:, IN_OFFSETS[k]:IN_OFFSETS[k + 1]], preferred_element_type=F32)

    def head_norm(t, bd_ref, gain_ref):
        ss = jnp.dot((t * t).astype(BF16), bd_ref[...], preferred_element_type=F32)
        return t * lax.rsqrt(ss * (1.0 / ATTN_HEAD_DIM) + EPS) * gain_ref[...]

    qa_ref[...] = head_norm(mm(0), bdq_ref, qg_ref).astype(BF16)
    ka_ref[...] = head_norm(mm(1), bdk_ref, kg_ref).astype(BF16)
    for k, ref in ((2, va_ref), (3, qr_ref), (4, kr_ref), (5, vr_ref), (6, gr_ref), (7, ga_ref), (8, gt_ref)):
        ref[...] = mm(k).astype(BF16)


def _in_proj(x2, g, w, qg, kg, bdq, bdk):
    n = x2.shape[0]
    tm = TOKEN_TILE
    full = lambda a: pl.BlockSpec(a.shape, lambda i: (0,) * a.ndim)
    widths = IN_SPLITS
    return pl.pallas_call(
        _in_proj_kernel,
        grid=(n // tm,),
        in_specs=[pl.BlockSpec((tm, D_MODEL), lambda i: (i, 0)), full(g), full(w), full(qg), full(kg),
                  full(bdq), full(bdk)],
        out_specs=[pl.BlockSpec((tm, wd), lambda i: (i, 0)) for wd in widths],
        out_shape=[jax.ShapeDtypeStruct((n, wd), BF16) for wd in widths],
        compiler_params=_cparams(1),
        name="in_proj",
    )(x2, g, w, qg, kg, bdq, bdk)


def _attn_kernel(sink_ref, q_ref, kp_ref, kc_ref, kn_ref, vp_ref, vc_ref, vn_ref, bias_ref, o_ref):
    nq = ATTN_HEADS // 2
    q = q_ref[...]
    qs = jnp.concatenate([q[:, j * LANES:(j + 1) * LANES] for j in range(nq)], axis=0)
    k = jnp.concatenate([kp_ref[...], kc_ref[...], kn_ref[...]], axis=0)
    v = jnp.concatenate([vp_ref[...], vc_ref[...], vn_ref[...]], axis=0)
    low = lax.broadcasted_iota(jnp.int32, k.shape, 1) < ATTN_HEAD_DIM
    zero = jnp.zeros_like(k)
    kbd = jnp.concatenate([jnp.where(low, k, zero), jnp.where(low, zero, k)], axis=0)
    vbd = jnp.concatenate([jnp.where(low, v, zero), jnp.where(low, zero, v)], axis=0)
    s = lax.dot_general(qs, kbd, (((1,), (1,)), ((), ())), preferred_element_type=F32)
    s = s + bias_ref[0]
    nk = 3 * BLOCK
    probs, invs = [], []
    for j in range(nq):
        row_p, row_inv = [], []
        for half in range(2):
            sj = s[j * BLOCK:(j + 1) * BLOCK, half * nk:(half + 1) * nk]
            sk = sink_ref[j + nq * half]
            m = jnp.maximum(jnp.max(sj, axis=-1, keepdims=True), sk)
            p = jnp.exp(sj - m)
            den = jnp.sum(p, axis=-1, keepdims=True) + jnp.exp(sk - m)
            row_p.append(p.astype(BF16))
            row_inv.append(1.0 / den)
        probs.append(jnp.concatenate(row_p, axis=1))
        invs.append(row_inv)
    pm = jnp.concatenate(probs, axis=0)
    o = jnp.dot(pm, vbd, preferred_element_type=F32)
    low_o = lax.broadcasted_iota(jnp.int32, (BLOCK, LANES), 1) < ATTN_HEAD_DIM
    for j in range(nq):
        inv = jnp.where(low_o, invs[j][0], invs[j][1])
        o_ref[:, j * LANES:(j + 1) * LANES] = (o[j * BLOCK:(j + 1) * BLOCK] * inv).astype(BF16)


def _attention(qa, ka, va, bias3, sink, b, s):
    nb = s // BLOCK
    assert nb >= 2
    n = b * s

    def kv_spec(off):
        return pl.BlockSpec((BLOCK, KV_WIDTH), lambda bi, ni: (bi * nb + jnp.clip(ni + off, 0, nb - 1), 0))

    def bias_map(bi, ni):
        return (jnp.where(ni == 0, 0, jnp.where(ni == nb - 1, 2, 1)), 0, 0)

    return pl.pallas_call(
        _attn_kernel,
        grid=(b, nb),
        in_specs=[pl.BlockSpec(memory_space=pltpu.SMEM),
                  pl.BlockSpec((BLOCK, ATTN_WIDTH), lambda bi, ni: (bi * nb + ni, 0)),
                  kv_spec(-1), kv_spec(0), kv_spec(1), kv_spec(-1), kv_spec(0), kv_spec(1),
                  pl.BlockSpec((1,) + bias3.shape[1:], bias_map)],
        out_specs=pl.BlockSpec((BLOCK, ATTN_WIDTH), lambda bi, ni: (bi * nb + ni, 0)),
        out_shape=jax.ShapeDtypeStruct((n, ATTN_WIDTH), BF16),
        compiler_params=_cparams(2),
        name="attn",
    )(sink, qa, ka, ka, ka, va, va, va, bias3)


def _retn_kernel(dec_ref, q_ref, k_ref, v_ref, g_ref, dmask_ref, rowf_ref, rowb_ref, wkf_ref, wkb_ref, ng_ref,
                 o_ref, tstore, uf, tb, *, nc):
    p = pl.program_id(1)
    n = pl.program_id(2)
    tn = (((0,), (0,)), ((), ()))
    nt = (((1,), (1,)), ((), ()))
    hs = lambda h: slice(h * RET_DIM, (h + 1) * RET_DIM)

    @pl.when(p == 0)
    def _():
        @pl.when(n == 0)
        def _():
            tb[...] = jnp.zeros_like(tb)
        c = nc - 1 - n
        for h in range(RET_HEADS):
            tstore[c, h] = tb[h].astype(BF16)
            kw = (k_ref[:, hs(h)].astype(F32) * wkb_ref[h]).astype(BF16)
            upd = lax.dot_general(kw, v_ref[:, hs(h)], tn, preferred_element_type=F32)
            tb[h] = dec_ref[RET_HEADS + h] * tb[h] + upd

    @pl.when(p == 1)
    def _():
        @pl.when(n == 0)
        def _():
            uf[...] = jnp.zeros_like(uf)
        for h in range(RET_HEADS):
            qh = q_ref[:, hs(h)]
            kh = k_ref[:, hs(h)]
            vh = v_ref[:, hs(h)]
            sc = lax.dot_general(qh, kh, nt, preferred_element_type=F32) * dmask_ref[h]
            intra = jnp.dot(sc.astype(BF16), vh, preferred_element_type=F32)
            states = jnp.concatenate([uf[h].astype(BF16), tstore[n, h]], axis=1)
            cross = jnp.dot(qh, states, preferred_element_type=F32)
            o = intra + cross[:, :RET_DIM] * rowf_ref[h] + cross[:, RET_DIM:] * rowb_ref[h]
            mu = jnp.mean(o, axis=-1, keepdims=True)
            d = o - mu
            var = jnp.mean(d * d, axis=-1, keepdims=True)
            on = d * lax.rsqrt(var + EPS) * ng_ref[h:h + 1, :]
            gate = g_ref[:, hs(h)].astype(F32)
            o_ref[:, hs(h)] = (gate * _sigmoid(gate) * on).astype(BF16)
            kw = (kh.astype(F32) * wkf_ref[h]).astype(BF16)
            uf[h] = dec_ref[h] * uf[h] + lax.dot_general(kw, vh, tn, preferred_element_type=F32)


def _retention(qr, kr, vr, gr, tables, b, s):
    dec, dmask, rowf, rowb, wkf, wkb, ng = tables
    cr = RET_CHUNK
    nc = s // cr
    n = b * s
    full = lambda a: pl.BlockSpec(a.shape, lambda bi, pi, ni: (0,) * a.ndim)
    fwd_spec = pl.BlockSpec((cr, RET_WIDTH), lambda bi, pi, ni: (bi * nc + ni * pi, 0))
    kv_spec = pl.BlockSpec((cr, RET_WIDTH), lambda bi, pi, ni: (bi * nc + ni * pi + (1 - pi) * (nc - 1 - ni), 0))
    return pl.pallas_call(
        functools.partial(_retn_kernel, nc=nc),
        grid=(b, 2, nc),
        in_specs=[pl.BlockSpec(memory_space=pltpu.SMEM), fwd_spec, kv_spec, kv_spec, fwd_spec,
                  full(dmask), full(rowf), full(rowb), full(wkf), full(wkb), full(ng)],
        out_specs=fwd_spec,
        out_shape=jax.ShapeDtypeStruct((n, RET_WIDTH), BF16),
        scratch_shapes=[pltpu.VMEM((nc, RET_HEADS, RET_DIM, RET_DIM), BF16),
                        pltpu.VMEM((RET_HEADS, RET_DIM, RET_DIM), F32),
                        pltpu.VMEM((RET_HEADS, RET_DIM, RET_DIM), F32)],
        compiler_params=_cparams(3),
        name="retention",
    )(dec, qr, kr, vr, gr, dmask, rowf, rowb, wkf, wkb, ng)


def _merge_kernel(attn_ref, retn_ref, ga_ref, gr_ref, x_ref, wba_ref, wbr_ref, wo_ref, g2_ref, wr_ref,
                  xmid_ref, afft_ref, *h2_refs):
    a = jnp.dot(attn_ref[...], wba_ref[...], preferred_element_type=F32)
    r = jnp.dot(retn_ref[...], wbr_ref[...], preferred_element_type=F32)
    merged = _sigmoid(ga_ref[...].astype(F32)) * a + _sigmoid(gr_ref[...].astype(F32)) * r
    xn = x_ref[...] + jnp.dot(merged.astype(BF16), wo_ref[...], preferred_element_type=F32)
    xmid_ref[...] = xn
    ms = jnp.mean(xn * xn, axis=-1, keepdims=True)
    h2 = xn * lax.rsqrt(ms + EPS) * g2_ref[...]
    for c, ref in enumerate(h2_refs):
        ref[...] = h2[:, c * SC_ROW:(c + 1) * SC_ROW]
    logits = jnp.dot(h2.astype(BF16), wr_ref[...], preferred_element_type=F32)
    real = lax.broadcasted_iota(jnp.int32, logits.shape, 1) < N_EXPERTS
    logits = jnp.where(real, logits, -jnp.inf)
    m = jnp.max(logits, axis=-1, keepdims=True)
    ex = jnp.exp(logits - m)
    aff = ex / jnp.sum(ex, axis=-1, keepdims=True)
    afft_ref[...] = aff.T[:N_EXPERTS, :]


def _merge(attn, retn, ga, gr, x2, wba, wbr, wo, g2, wr):
    n = x2.shape[0]
    tm = TOKEN_TILE
    full = lambda a: pl.BlockSpec(a.shape, lambda i: (0,) * a.ndim)
    row = lambda wd: pl.BlockSpec((tm, wd), lambda i: (i, 0))
    return pl.pallas_call(
        _merge_kernel,
        grid=(n // tm,),
        in_specs=[row(ATTN_WIDTH), row(RET_WIDTH), row(D_MODEL), row(D_MODEL), row(D_MODEL),
                  full(wba), full(wbr), full(wo), full(g2), full(wr)],
        out_specs=[row(D_MODEL), pl.BlockSpec((N_EXPERTS, tm), lambda i: (0, i))] + [row(SC_ROW)] * SC_PIECES,
        out_shape=[jax.ShapeDtypeStruct((n, D_MODEL), F32), jax.ShapeDtypeStruct((N_EXPERTS, n), F32)]
        + [jax.ShapeDtypeStruct((n, SC_ROW), F32)] * SC_PIECES,
        compiler_params=_cparams(1),
        name="merge",
    )(attn, retn, ga, gr, x2, wba, wbr, wo, g2, wr)


def _split3(x):
    p0 = x.astype(BF16)
    r0 = x - p0.astype(F32)
    p1 = r0.astype(BF16)
    p2 = (r0 - p1.astype(F32)).astype(BF16)
    return p0, p1, p2


def _select_kernel(aff_ref, u_ref, ls_ref, idx_ref, gate_ref, dst_ref, cs_ref, ce_ref,
                   selbuf, cnt, csr, rank, pieces, offi, *, cap, tb):
    ps = pl.program_id(0)
    e = pl.program_id(1)
    j = pl.program_id(2)
    nblk = SELECT_BLOCKS
    pc = SELECT_SLOTS

    def cumsum(vals):
        inb = jnp.dot(vals.astype(BF16), u_ref[...], preferred_element_type=F32)
        tot = jnp.broadcast_to(inb[:, tb - 1:tb], (nblk, LANES))
        off = jnp.dot(ls_ref[...], tot, preferred_element_type=F32, precision=lax.Precision.HIGHEST)
        return inb, off[:, 0:1], tot[:, 0:1]

    @pl.when(jnp.logical_and(ps == 0, j == 0))
    def _():
        a = aff_ref[e]
        bits = pltpu.bitcast(a, jnp.int32)

        def bit_step(t, cur):
            cand = cur | jnp.left_shift(jnp.int32(1), 30 - t)
            n_ge = jnp.sum((bits >= cand).astype(jnp.int32), keepdims=True)
            return jnp.where(n_ge >= cap, cand, cur)

        thr = lax.fori_loop(0, 31, bit_step, jnp.zeros((1, 1), jnp.int32))
        gt = bits > thr
        eq = bits == thr
        need = (cap - jnp.sum(gt.astype(jnp.int32), keepdims=True)).astype(F32)
        eqf = eq.astype(F32)
        eq_in, eq_off, _ = cumsum(eqf)
        eq_rank = eq_in + eq_off - eqf
        sel = jnp.logical_or(gt, jnp.logical_and(eq, eq_rank < need)).astype(F32)
        selbuf[e] = sel.astype(BF16)

        @pl.when(e == 0)
        def _():
            cnt[...] = sel

        @pl.when(e > 0)
        def _():
            cnt[...] = cnt[...] + sel

    @pl.when(jnp.logical_and(ps == 1, j == 0))
    def _():
        @pl.when(e == 0)
        def _():
            c = cnt[...]
            c_in, c_off, _ = cumsum(c)
            start = c_in + c_off - c
            csr[...] = start
            cs_ref[...] = start.astype(jnp.int32)
            ce_ref[...] = (start + c).astype(jnp.int32)
            rank[...] = jnp.zeros_like(rank)

        sel = selbuf[e].astype(F32)
        s_in, s_off, s_tot = cumsum(sel)
        for k, piece in enumerate(_split3((s_in + s_off).T) + _split3(aff_ref[e].T) + _split3((csr[...] + rank[...]).T)):
            pieces[k] = piece
        rank[...] = rank[...] + sel
        offi[...] = jnp.broadcast_to(s_off + s_tot, (nblk, LANES))

    @pl.when(ps == 1)
    def _():
        slot = (j * pc + lax.broadcasted_iota(jnp.int32, (1, pc), 1)).astype(F32)
        blk = jnp.sum((offi[:, 0:1] <= slot).astype(jnp.int32), axis=0, keepdims=True)
        onehot = (lax.broadcasted_iota(jnp.int32, (nblk, pc), 0) == blk).astype(BF16)

        def pick_block(k):
            return sum(jnp.dot(pieces[k + q], onehot, preferred_element_type=F32) for q in range(3))

        inb = jnp.sum((pick_block(0) <= slot + 0.5).astype(jnp.int32), axis=0, keepdims=True)
        hit = lax.broadcasted_iota(jnp.int32, (tb, pc), 0) == inb
        idx_ref[0] = blk * tb + inb
        gate_ref[0] = jnp.sum(jnp.where(hit, pick_block(3), 0.0), axis=0, keepdims=True)
        dst_ref[0] = jnp.sum(jnp.where(hit, pick_block(6), 0.0), axis=0, keepdims=True).astype(jnp.int32)


def _select(afft, cap):
    n = afft.shape[1]
    nblk = SELECT_BLOCKS
    tb = n // nblk
    pc = SELECT_SLOTS
    assert n % nblk == 0 and tb % LANES == 0 and cap % pc == 0
    nch = cap // pc
    aff3 = afft.reshape(N_EXPERTS, nblk, tb)
    upper = jnp.asarray(np.triu(np.ones((tb, tb), np.float32)), BF16)
    lstrict = jnp.asarray(np.tril(np.ones((nblk, nblk), np.float32), -1))
    full = lambda a: pl.BlockSpec(a.shape, lambda ps, e, j: (0,) * a.ndim)
    slot_spec = pl.BlockSpec((1, 1, pc), lambda ps, e, j: (ps * (e * nch + j), 0, 0))
    tok_spec = pl.BlockSpec((nblk, tb), lambda ps, e, j: (0, 0))
    slots = lambda dt: jax.ShapeDtypeStruct((N_EXPERTS * nch, 1, pc), dt)
    idx, gate, dst, cs, ce = pl.pallas_call(
        functools.partial(_select_kernel, cap=cap, tb=tb),
        grid=(2, N_EXPERTS, nch),
        in_specs=[full(aff3), full(upper), full(lstrict)],
        out_specs=[slot_spec, slot_spec, slot_spec, tok_spec, tok_spec],
        out_shape=[slots(jnp.int32), slots(F32), slots(jnp.int32),
                   jax.ShapeDtypeStruct((nblk, tb), jnp.int32), jax.ShapeDtypeStruct((nblk, tb), jnp.int32)],
        scratch_shapes=[pltpu.VMEM((N_EXPERTS, nblk, tb), BF16), pltpu.VMEM((nblk, tb), F32),
                        pltpu.VMEM((nblk, tb), F32), pltpu.VMEM((nblk, tb), F32),
                        pltpu.VMEM((9, tb, nblk), BF16), pltpu.VMEM((nblk, LANES), F32)],
        compiler_params=_cparams(3),
        name="select",
    )(aff3, upper, lstrict)
    return idx.reshape(-1), gate.reshape(N_EXPERTS * nch, 1, pc), dst.reshape(-1), cs.reshape(-1), ce.reshape(-1)


def _sc_mesh():
    return plsc.VectorSubcoreMesh(core_axis_name="c", subcore_axis_name="s")


def _sc_scatter(rows, idx, m_out):
    m, d = rows.shape
    assert m % SC_WINDOW == 0

    @functools.partial(pl.kernel, out_type=jax.ShapeDtypeStruct((m_out, d), rows.dtype), mesh=_sc_mesh(),
                       name="sc_scatter")
    def scatter(x_hbm, i_hbm, o_hbm):
        def body(x_vmem, i_vmem):
            pltpu.sync_copy(x_vmem, o_hbm.at[i_vmem.at[0]])

        pltpu.emit_pipeline(
            body,
            grid=(m // SC_WINDOW,),
            in_specs=[pl.BlockSpec((SC_WINDOW, d), lambda i: (i, 0)),
                      pl.BlockSpec((1, SC_WINDOW), lambda i: (0, i))],
            out_specs=[],
            core_axis_name=("c", "s"),
            dimension_semantics=(pltpu.PARALLEL,),
        )(x_hbm, i_hbm)

    return scatter(rows, idx.reshape(1, m))


def _sc_gather(table, idx):
    m = idx.shape[0]
    d = table.shape[1]
    assert m % SC_WINDOW == 0

    @functools.partial(pl.kernel, out_type=jax.ShapeDtypeStruct((m, d), table.dtype), mesh=_sc_mesh(),
                       name="sc_gather")
    def gather(x_hbm, i_hbm, o_hbm):
        def body(i_vmem, o_vmem):
            pltpu.sync_copy(x_hbm.at[i_vmem.at[0]], o_vmem)

        pltpu.emit_pipeline(
            body,
            grid=(m // SC_WINDOW,),
            in_specs=[pl.BlockSpec((1, SC_WINDOW), lambda i: (0, i))],
            out_specs=[pl.BlockSpec((SC_WINDOW, d), lambda i: (i, 0))],
            core_axis_name=("c", "s"),
            dimension_semantics=(pltpu.PARALLEL,),
        )(i_hbm, o_hbm)

    return gather(table, idx.reshape(1, m))


def _row_to_col(row):
    n = row.shape[1]
    eye = lax.broadcasted_iota(jnp.int32, (n, n), 0) == lax.broadcasted_iota(jnp.int32, (n, n), 1)
    return jnp.sum(jnp.where(eye, row, jnp.zeros_like(row)), axis=1, keepdims=True)


def _ffn_kernel(gate_ref, x0_ref, x1_ref, x2_ref, x3_ref, w1_ref, w3_ref, w2_ref, *o_refs):
    xs = jnp.concatenate([r[...] for r in (x0_ref, x1_ref, x2_ref, x3_ref)], axis=1).astype(BF16)
    hg = jnp.dot(xs, w1_ref[0], preferred_element_type=F32)
    hu = jnp.dot(xs, w3_ref[0], preferred_element_type=F32)
    hid = (hg * _sigmoid(hg) * hu).astype(BF16)
    out = jnp.dot(hid, w2_ref[0], preferred_element_type=F32) * _row_to_col(gate_ref[0])
    for c, ref in enumerate(o_refs):
        ref[...] = out[:, c * SC_ROW:(c + 1) * SC_ROW]


def _expert_ffn(xs, gate3, w1, w3, w2):
    m = xs[0].shape[0]
    rows = FFN_ROWS
    nt = m // (N_EXPERTS * rows)
    wspec = pl.BlockSpec((1, D_MODEL, EXPERT_FF), lambda e, i: (e, 0, 0))
    w2spec = pl.BlockSpec((1, EXPERT_FF, D_MODEL), lambda e, i: (e, 0, 0))
    piece = pl.BlockSpec((rows, SC_ROW), lambda e, i: (e * nt + i, 0))
    per_step = gate3.shape[2] // rows
    gspec = pl.BlockSpec((1, 1, rows), lambda e, i: ((e * nt + i) // per_step, 0, (e * nt + i) % per_step))
    return pl.pallas_call(
        _ffn_kernel,
        grid=(N_EXPERTS, nt),
        in_specs=[gspec] + [piece] * SC_PIECES + [wspec, wspec, w2spec],
        out_specs=[piece] * SC_PIECES,
        out_shape=[jax.ShapeDtypeStruct((m, SC_ROW), F32)] * SC_PIECES,
        compiler_params=_cparams(2),
        name="expert_ffn",
    )(gate3, *xs, w1, w3, w2)


def _combine_kernel(tstart_ref, x_ref, cs_ref, ce_ref, r0_hbm, r1_hbm, r2_hbm, r3_hbm, o_ref, rbuf, obuf, sems, osem,
                    *, ntile, total):
    pieces_hbm = (r0_hbm, r1_hbm, r2_hbm, r3_hbm)
    win = COMBINE_WINDOW
    i = pl.program_id(0)
    slot = lax.rem(i, 2)

    def window_start(t):
        return pl.multiple_of((tstart_ref[t] // 8) * 8, 8)

    def copies(t, b):
        s = window_start(t)
        return [pltpu.make_async_copy(pieces_hbm[c].at[pl.ds(s, win)], rbuf.at[b, c], sems.at[b, c])
                for c in range(SC_PIECES)]

    @pl.when(i == 0)
    def _():
        for cp in copies(0, 0):
            cp.start()

    @pl.when(i + 1 < ntile)
    def _():
        for cp in copies(i + 1, 1 - slot):
            cp.start()

    for cp in copies(i, slot):
        cp.wait()

    first = _row_to_col(cs_ref[0])
    last = _row_to_col(ce_ref[0])

    def window_sum(read_piece, base):
        r = base + lax.broadcasted_iota(jnp.int32, (1, win), 1)
        q = jnp.logical_and(first <= r, r < last).astype(BF16)
        written = (base + lax.broadcasted_iota(jnp.int32, (win, 1), 0)) < total
        parts = [jnp.dot(q, jnp.where(written, read_piece(c), 0.0).astype(BF16), preferred_element_type=F32)
                 for c in range(SC_PIECES)]
        return jnp.concatenate(parts, axis=1)

    s0 = window_start(i)
    y = x_ref[...] + window_sum(lambda c: rbuf[slot, c], s0)

    n_extra = jnp.maximum(tstart_ref[i + 1] - (s0 + win) + win - 1, 0) // win

    def extra(k, acc):
        base = pl.multiple_of(s0 + (k + 1) * win, 8)
        cps = [pltpu.make_async_copy(pieces_hbm[c].at[pl.ds(base, win)], obuf.at[c], osem.at[c])
               for c in range(SC_PIECES)]
        for cp in cps:
            cp.start()
        for cp in cps:
            cp.wait()
        return acc + window_sum(lambda c: obuf[c], base)

    o_ref[...] = lax.fori_loop(0, n_extra, extra, y)


def _combine(xmid, cs, ce, pieces, total):
    n = xmid.shape[0]
    tt = COMBINE_TOKENS
    win = COMBINE_WINDOW
    ntile = n // tt
    tstart = jnp.concatenate([cs[::tt], jnp.full((1,), total, jnp.int32)])
    cs3 = cs.reshape(ntile, 1, tt)
    ce3 = ce.reshape(ntile, 1, tt)
    any_spec = pl.BlockSpec(memory_space=pl.ANY)
    tok = pl.BlockSpec((1, 1, tt), lambda i, ts: (i, 0, 0))
    grid_spec = pltpu.PrefetchScalarGridSpec(
        num_scalar_prefetch=1,
        grid=(ntile,),
        in_specs=[pl.BlockSpec((tt, D_MODEL), lambda i, ts: (i, 0)), tok, tok] + [any_spec] * SC_PIECES,
        out_specs=pl.BlockSpec((tt, D_MODEL), lambda i, ts: (i, 0)),
        scratch_shapes=[pltpu.VMEM((2, SC_PIECES, win, SC_ROW), F32), pltpu.VMEM((SC_PIECES, win, SC_ROW), F32),
                        pltpu.SemaphoreType.DMA((2, SC_PIECES)), pltpu.SemaphoreType.DMA((SC_PIECES,))],
    )
    return pl.pallas_call(
        functools.partial(_combine_kernel, ntile=ntile, total=total),
        grid_spec=grid_spec,
        out_shape=jax.ShapeDtypeStruct((n, D_MODEL), F32),
        compiler_params=_cparams(1),
        name="combine",
    )(tstart, xmid, cs3, ce3, *pieces)


def _t5_bucket(rel):
    half = REL_BUCKETS // 2
    max_exact = half // 2
    base = np.where(rel > 0, half, 0)
    n = np.abs(rel)
    large = max_exact + (np.log(np.maximum(n, 1) / max_exact) / math.log(REL_MAX_DIST / max_exact)
                         * (half - max_exact)).astype(np.int32)
    large = np.minimum(large, half - 1)
    return (base + np.where(n < max_exact, n, large)).astype(np.int32)


def _head_perm():
    nq = ATTN_HEADS // 2
    cols = []
    for j in range(nq):
        for half in range(2):
            h = j + nq * half
            cols.extend(range(h * ATTN_HEAD_DIM, (h + 1) * ATTN_HEAD_DIM))
    return np.asarray(cols, np.int32)


def _attn_bias_tables(rel_bias):
    q_pos = np.arange(BLOCK)[:, None]
    k_off = np.arange(3 * BLOCK)[None, :] - BLOCK
    rel = k_off - q_pos
    in_window = np.abs(rel) <= WINDOW
    bias = rel_bias.astype(F32)[_t5_bucket(rel)]
    bias = jnp.transpose(bias, (2, 0, 1))
    col = np.arange(3 * BLOCK)[None, :]
    tables = []
    for valid in (col >= BLOCK, np.ones_like(col, bool), col < 2 * BLOCK):
        t = jnp.where(jnp.asarray(in_window & valid)[None], bias, NEG)
        nq = ATTN_HEADS // 2
        rows = [jnp.concatenate([t[j], t[j + nq]], axis=1) for j in range(nq)]
        tables.append(jnp.concatenate(rows, axis=0))
    return jnp.stack(tables)


def _retention_tables(decay_logit, norm_g):
    cr = RET_CHUNK
    lg = jax.nn.log_sigmoid(decay_logit.astype(F32))
    lgf, lgb = lg[0][:, None, None], lg[1][:, None, None]
    pos = np.arange(cr, dtype=np.float32)
    dist = pos[:, None] - pos[None, :]
    scale = RET_DIM ** -0.5
    dmask = jnp.where(jnp.asarray(dist >= 0)[None],
                      jnp.exp(lgf * np.maximum(dist, 0.0)[None]),
                      jnp.exp(lgb * np.maximum(-dist, 0.0)[None])) * scale
    col = lambda v: jnp.broadcast_to(v[:, :, None], (RET_HEADS, cr, RET_DIM))
    rowf = col(jnp.exp(lg[0][:, None] * pos[None]))
    rowb = col(jnp.exp(lg[1][:, None] * (cr - 1.0 - pos)[None]))
    wkf = col(jnp.exp(lg[0][:, None] * (cr - pos)[None]) * scale)
    wkb = col(jnp.exp(lg[1][:, None] * (pos + 1.0)[None]) * scale)
    dec = jnp.concatenate([jnp.exp(lg[0] * cr), jnp.exp(lg[1] * cr)])
    return dec, dmask, rowf, rowb, wkf, wkb, norm_g.astype(F32)


def _layer(x2, b, s, p):
    qa, ka, va, qr, kr, vr, gr, ga, gt = _in_proj(x2, p["g1"], p["w_in"], p["qg"], p["kg"], p["bdq"], p["bdk"])
    attn = _attention(qa, ka, va, p["bias3"], p["sink"], b, s)
    retn = _retention(qr, kr, vr, gr, p["retn"], b, s)
    xmid, afft, *h2 = _merge(attn, retn, ga, gt, x2, p["wba"], p["wbr"], p["wo"], p["g2"], p["wr"])
    n = b * s
    cap = max(1, EC_CAPACITY_FACTOR * n // N_EXPERTS)
    total = N_EXPERTS * cap
    idx, gate3, dst, cs, ce = _select(afft, cap)
    xs = [_sc_gather(piece, idx) for piece in h2]
    outs = _expert_ffn(xs, gate3, p["w1"], p["w3"], p["w2"])
    by_token = [_sc_scatter(o, dst, total + COMBINE_WINDOW) for o in outs]
    return _combine(xmid, cs, ce, by_token, total)


def kernel(x_prompt, x_sample, norm_mix_g, w_in, q_norm_g, k_norm_g, attn_sink, rel_bias, retn_decay_logit, retn_norm_g, w_branch_attn, w_branch_retn, w_out, norm_ffn_g, w_router, w_exp_gate, w_exp_up, w_exp_down):
    depth = w_in.shape[0]
    perm = _head_perm()
    bias3 = _attn_bias_tables(rel_bias)
    bdq = jnp.asarray(np.kron(np.eye(ATTN_HEADS), np.ones((ATTN_HEAD_DIM, ATTN_HEAD_DIM))), BF16)
    bdk = jnp.asarray(np.kron(np.eye(ATTN_KV_HEADS), np.ones((ATTN_HEAD_DIM, ATTN_HEAD_DIM))), BF16)
    layers = []
    for l in range(depth):
        w = w_in[l]
        w = jnp.concatenate([w[:, :ATTN_WIDTH][:, perm], w[:, ATTN_WIDTH:]], axis=1).astype(BF16)
        wr = jnp.pad(w_router[l], ((0, 0), (0, LANES - N_EXPERTS))).astype(BF16)
        layers.append(dict(
            g1=norm_mix_g[l].astype(F32)[None], w_in=w,
            qg=(jnp.tile(q_norm_g[l].astype(F32), ATTN_HEADS) * (ATTN_HEAD_DIM ** -0.5))[None],
            kg=jnp.tile(k_norm_g[l].astype(F32), ATTN_KV_HEADS)[None],
            bdq=bdq, bdk=bdk, bias3=bias3, sink=attn_sink[l].astype(F32),
            retn=_retention_tables(retn_decay_logit[l], retn_norm_g[l]),
            wba=w_branch_attn[l][perm, :].astype(BF16), wbr=w_branch_retn[l].astype(BF16),
            wo=w_out[l].astype(BF16), g2=norm_ffn_g[l].astype(F32)[None], wr=wr,
            w1=w_exp_gate[l].astype(BF16), w3=w_exp_up[l].astype(BF16), w2=w_exp_down[l].astype(BF16)))

    def trunk(x):
        b, s, d = x.shape
        x2 = x.reshape(b * s, d)
        for p in layers:
            x2 = _layer(x2, b, s, p)
        return x2.reshape(b, s, d)

    return (trunk(x_prompt), trunk(x_sample))
```

```python
import functools
import math

import numpy as np
import jax
import jax.numpy as jnp
from jax import lax
from jax.experimental import pallas as pl
from jax.experimental.pallas import tpu as pltpu
from jax.experimental.pallas import tpu_sc as plsc

D_MODEL = 1024
ATTN_HEADS = 8
ATTN_KV_HEADS = 2
ATTN_HEAD_DIM = 64
WINDOW = 128
BLOCK = 128
REL_BUCKETS = 32
REL_MAX_DIST = 128
RET_HEADS = 4
RET_DIM = 128
N_EXPERTS = 16
EC_CAPACITY_FACTOR = 2
EXPERT_FF = 1024
EPS = 1e-6

ATTN_WIDTH = ATTN_HEADS * ATTN_HEAD_DIM
KV_WIDTH = ATTN_KV_HEADS * ATTN_HEAD_DIM
RET_WIDTH = RET_HEADS * RET_DIM
IN_SPLITS = (ATTN_WIDTH, KV_WIDTH, KV_WIDTH, RET_WIDTH, RET_WIDTH, RET_WIDTH, RET_WIDTH, D_MODEL, D_MODEL)
IN_OFFSETS = tuple(int(o) for o in np.cumsum((0,) + IN_SPLITS))

LANES = 128
VMEM_LIMIT_BYTES = 56 * 1024 * 1024

TOKEN_TILE = 512
IN_PROJ_TILE = 1024
ATTN_QUERIES = 512
RET_CHUNK = 256
FFN_ROWS = 1024
SELECT_BLOCKS = 128
SELECT_SLOTS = 1024
SC_WINDOW = 128
SC_ROW = 256
SC_PIECES = D_MODEL // SC_ROW
COMBINE_TOKENS = 256
COMBINE_WINDOW = 768

F32 = jnp.float32
BF16 = jnp.bfloat16
NEG = -1e30


def _cparams(n_axes, vmem=VMEM_LIMIT_BYTES):
    return pltpu.CompilerParams(dimension_semantics=("arbitrary",) * n_axes, vmem_limit_bytes=vmem)


def _sigmoid(x):
    return 1.0 / (1.0 + jnp.exp(-x))


def _in_proj_kernel(x_ref, g_ref, w_ref, qg_ref, kg_ref, bdq_ref, bdk_ref,
                    qa_ref, ka_ref, va_ref, qr_ref, kr_ref, vr_ref, gr_ref, ga_ref, gt_ref):
    x = x_ref[...]
    ms = jnp.mean(x * x, axis=-1, keepdims=True)
    h = (x * lax.rsqrt(ms + EPS) * g_ref[...]).astype(BF16)

    def mm(k):
        return jnp.dot(h, w_ref[:, IN_OFFSETS[k]:IN_OFFSETS[k + 1]], preferred_element_type=F32)

    def head_norm(t, bd_ref, gain_ref):
        ss = jnp.dot((t * t).astype(BF16), bd_ref[...], preferred_element_type=F32)
        return t * lax.rsqrt(ss * (1.0 / ATTN_HEAD_DIM) + EPS) * gain_ref[...]

    qa_ref[...] = head_norm(mm(0), bdq_ref, qg_ref).astype(BF16)
    ka_ref[...] = head_norm(mm(1), bdk_ref, kg_ref).astype(BF16)
    for k, ref in ((2, va_ref), (3, qr_ref), (4, kr_ref), (5, vr_ref), (6, gr_ref), (7, ga_ref), (8, gt_ref)):
        ref[...] = mm(k).astype(BF16)


def _in_proj(x2, g, w, qg, kg, bdq, bdk):
    n = x2.shape[0]
    tm = IN_PROJ_TILE
    full = lambda a: pl.BlockSpec(a.shape, lambda i: (0,) * a.ndim, pipeline_mode=pl.Buffered(1))
    widths = IN_SPLITS
    return pl.pallas_call(
        _in_proj_kernel,
        grid=(n // tm,),
        in_specs=[pl.BlockSpec((tm, D_MODEL), lambda i: (i, 0)), full(g), full(w), full(qg), full(kg),
                  full(bdq), full(bdk)],
        out_specs=[pl.BlockSpec((tm, wd), lambda i: (i, 0)) for wd in widths],
        out_shape=[jax.ShapeDtypeStruct((n, wd), BF16) for wd in widths],
        compiler_params=_cparams(1),
        name="in_proj",
    )(x2, g, w, qg, kg, bdq, bdk)


def _attn_kernel(sink_ref, q_ref, kp_ref, kc_ref, kn_ref, vp_ref, vc_ref, vn_ref, bias_ref, o_ref, *, nsteps):
    nq = ATTN_HEADS // 2
    ni = pl.program_id(1)
    k = jnp.concatenate([kp_ref[...], kc_ref[...], kn_ref[...]], axis=0)
    v = jnp.concatenate([vp_ref[...], vc_ref[...], vn_ref[...]], axis=0)
    low = lax.broadcasted_iota(jnp.int32, k.shape, 1) < ATTN_HEAD_DIM
    zero = jnp.zeros_like(k)
    k_lo, k_hi = jnp.where(low, k, zero), jnp.where(low, zero, k)
    v_lo, v_hi = jnp.where(low, v, zero), jnp.where(low, zero, v)
    nk = 3 * BLOCK
    low_o = lax.broadcasted_iota(jnp.int32, (BLOCK, LANES), 1) < ATTN_HEAD_DIM
    nsub = ATTN_QUERIES // BLOCK
    for sb in range(nsub):
        rows = slice(sb * BLOCK, (sb + 1) * BLOCK)
        keys = slice(sb * BLOCK, sb * BLOCK + nk)
        q = q_ref[rows, :]
        qs = jnp.concatenate([q[:, j * LANES:(j + 1) * LANES] for j in range(nq)], axis=0)
        kbd = jnp.concatenate([k_lo[keys], k_hi[keys]], axis=0)
        vbd = jnp.concatenate([v_lo[keys], v_hi[keys]], axis=0)
        s = lax.dot_general(qs, kbd, (((1,), (1,)), ((), ())), preferred_element_type=F32)
        if sb == 0:
            table = jnp.where(ni == 0, 0, 1)
        elif sb == nsub - 1:
            table = jnp.where(ni == nsteps - 1, 2, 1)
        else:
            table = 1
        s = s + bias_ref[table]
        probs, invs = [], []
        for j in range(nq):
            row_p, row_inv = [], []
            for half in range(2):
                sj = s[j * BLOCK:(j + 1) * BLOCK, half * nk:(half + 1) * nk]
                sk = sink_ref[j + nq * half]
                m = jnp.maximum(jnp.max(sj, axis=-1, keepdims=True), sk)
                p = jnp.exp(sj - m)
                den = jnp.sum(p, axis=-1, keepdims=True) + jnp.exp(sk - m)
                row_p.append(p.astype(BF16))
                row_inv.append(1.0 / den)
            probs.append(jnp.concatenate(row_p, axis=1))
            invs.append(row_inv)
        pm = jnp.concatenate(probs, axis=0)
        o = jnp.dot(pm, vbd, preferred_element_type=F32)
        for j in range(nq):
            inv = jnp.where(low_o, invs[j][0], invs[j][1])
            o_ref[rows, j * LANES:(j + 1) * LANES] = (o[j * BLOCK:(j + 1) * BLOCK] * inv).astype(BF16)


def _attention(qa, ka, va, bias3, sink, b, s):
    tq = ATTN_QUERIES
    per = tq // BLOCK
    nb = s // BLOCK
    nsteps = s // tq
    assert s % tq == 0 and nb >= 2
    n = b * s
    main = lambda wd: pl.BlockSpec((tq, wd), lambda bi, ni: (bi * nsteps + ni, 0))
    prev = pl.BlockSpec((BLOCK, KV_WIDTH), lambda bi, ni: (bi * nb + jnp.maximum(ni * per - 1, 0), 0))
    nxt = pl.BlockSpec((BLOCK, KV_WIDTH), lambda bi, ni: (bi * nb + jnp.minimum(ni * per + per, nb - 1), 0))
    return pl.pallas_call(
        functools.partial(_attn_kernel, nsteps=nsteps),
        grid=(b, nsteps),
        in_specs=[pl.BlockSpec(memory_space=pltpu.SMEM), main(ATTN_WIDTH),
                  prev, main(KV_WIDTH), nxt, prev, main(KV_WIDTH), nxt,
                  pl.BlockSpec(bias3.shape, lambda bi, ni: (0, 0, 0))],
        out_specs=main(ATTN_WIDTH),
        out_shape=jax.ShapeDtypeStruct((n, ATTN_WIDTH), BF16),
        compiler_params=_cparams(2),
        name="attn",
    )(sink, qa, ka, ka, ka, va, va, va, bias3)


def _retn_kernel(dec_ref, q_ref, k_ref, v_ref, g_ref, dmask_ref, rowf_ref, rowb_ref, wkf_ref, wkb_ref, ng_ref,
                 o_ref, tstore, uf, tb, *, nc):
    p = pl.program_id(1)
    n = pl.program_id(2)
    tn = (((0,), (0,)), ((), ()))
    nt = (((1,), (1,)), ((), ()))
    hs = lambda h: slice(h * RET_DIM, (h + 1) * RET_DIM)

    @pl.when(p == 0)
    def _():
        @pl.when(n == 0)
        def _():
            tb[...] = jnp.zeros_like(tb)
        c = nc - 1 - n
        for h in range(RET_HEADS):
            tstore[c, h] = tb[h].astype(BF16)
            kw = (k_ref[:, hs(h)].astype(F32) * wkb_ref[h]).astype(BF16)
            upd = lax.dot_general(kw, v_ref[:, hs(h)], tn, preferred_element_type=F32)
            tb[h] = dec_ref[RET_HEADS + h] * tb[h] + upd

    @pl.when(p == 1)
    def _():
        @pl.when(n == 0)
        def _():
            uf[...] = jnp.zeros_like(uf)
        for h in range(RET_HEADS):
            qh = q_ref[:, hs(h)]
            kh = k_ref[:, hs(h)]
            vh = v_ref[:, hs(h)]
            sc = lax.dot_general(qh, kh, nt, preferred_element_type=F32) * dmask_ref[h]
            intra = jnp.dot(sc.astype(BF16), vh, preferred_element_type=F32)
            states = jnp.concatenate([uf[h].astype(BF16), tstore[n, h]], axis=1)
            cross = jnp.dot(qh, states, preferred_element_type=F32)
            o = intra + cross[:, :RET_DIM] * rowf_ref[h] + cross[:, RET_DIM:] * rowb_ref[h]
            mu = jnp.mean(o, axis=-1, keepdims=True)
            d = o - mu
            var = jnp.mean(d * d, axis=-1, keepdims=True)
            on = d * lax.rsqrt(var + EPS) * ng_ref[h:h + 1, :]
            gate = g_ref[:, hs(h)].astype(F32)
            o_ref[:, hs(h)] = (gate * _sigmoid(gate) * on).astype(BF16)
            kw = (kh.astype(F32) * wkf_ref[h]).astype(BF16)
            uf[h] = dec_ref[h] * uf[h] + lax.dot_general(kw, vh, tn, preferred_element_type=F32)


def _retention(qr, kr, vr, gr, tables, b, s):
    dec, dmask, rowf, rowb, wkf, wkb, ng = tables
    cr = RET_CHUNK
    nc = s // cr
    n = b * s
    full = lambda a: pl.BlockSpec(a.shape, lambda bi, pi, ni: (0,) * a.ndim)
    fwd_spec = pl.BlockSpec((cr, RET_WIDTH), lambda bi, pi, ni: (bi * nc + ni * pi, 0))
    kv_spec = pl.BlockSpec((cr, RET_WIDTH), lambda bi, pi, ni: (bi * nc + ni * pi + (1 - pi) * (nc - 1 - ni), 0))
    return pl.pallas_call(
        functools.partial(_retn_kernel, nc=nc),
        grid=(b, 2, nc),
        in_specs=[pl.BlockSpec(memory_space=pltpu.SMEM), fwd_spec, kv_spec, kv_spec, fwd_spec,
                  full(dmask), full(rowf), full(rowb), full(wkf), full(wkb), full(ng)],
        out_specs=fwd_spec,
        out_shape=jax.ShapeDtypeStruct((n, RET_WIDTH), BF16),
        scratch_shapes=[pltpu.VMEM((nc, RET_HEADS, RET_DIM, RET_DIM), BF16),
                        pltpu.VMEM((RET_HEADS, RET_DIM, RET_DIM), F32),
                        pltpu.VMEM((RET_HEADS, RET_DIM, RET_DIM), F32)],
        compiler_params=_cparams(3),
        name="retention",
    )(dec, qr, kr, vr, gr, dmask, rowf, rowb, wkf, wkb, ng)


def _merge_kernel(attn_ref, retn_ref, ga_ref, gr_ref, x_ref, wba_ref, wbr_ref, wo_ref, g2_ref, wr_ref,
                  xmid_ref, afft_ref, *h2_refs):
    a = jnp.dot(attn_ref[...], wba_ref[...], preferred_element_type=F32)
    r = jnp.dot(retn_ref[...], wbr_ref[...], preferred_element_type=F32)
    merged = _sigmoid(ga_ref[...].astype(F32)) * a + _sigmoid(gr_ref[...].astype(F32)) * r
    xn = x_ref[...] + jnp.dot(merged.astype(BF16), wo_ref[...], preferred_element_type=F32)
    xmid_ref[...] = xn
    ms = jnp.mean(xn * xn, axis=-1, keepdims=True)
    h2 = xn * lax.rsqrt(ms + EPS) * g2_ref[...]
    for c, ref in enumerate(h2_refs):
        ref[...] = h2[:, c * SC_ROW:(c + 1) * SC_ROW]
    logits = jnp.dot(h2.astype(BF16), wr_ref[...], preferred_element_type=F32)
    real = lax.broadcasted_iota(jnp.int32, logits.shape, 1) < N_EXPERTS
    logits = jnp.where(real, logits, -jnp.inf)
    m = jnp.max(logits, axis=-1, keepdims=True)
    ex = jnp.exp(logits - m)
    aff = ex / jnp.sum(ex, axis=-1, keepdims=True)
    afft_ref[...] = aff.T[:N_EXPERTS, :]


def _merge(attn, retn, ga, gr, x2, wba, wbr, wo, g2, wr):
    n = x2.shape[0]
    tm = TOKEN_TILE
    full = lambda a: pl.BlockSpec(a.shape, lambda i: (0,) * a.ndim)
    row = lambda wd: pl.BlockSpec((tm, wd), lambda i: (i, 0))
    return pl.pallas_call(
        _merge_kernel,
        grid=(n // tm,),
        in_specs=[row(ATTN_WIDTH), row(RET_WIDTH), row(D_MODEL), row(D_MODEL), row(D_MODEL),
                  full(wba), full(wbr), full(wo), full(g2), full(wr)],
        out_specs=[row(D_MODEL), pl.BlockSpec((N_EXPERTS, tm), lambda i: (0, i))] + [row(SC_ROW)] * SC_PIECES,
        out_shape=[jax.ShapeDtypeStruct((n, D_MODEL), F32), jax.ShapeDtypeStruct((N_EXPERTS, n), F32)]
        + [jax.ShapeDtypeStruct((n, SC_ROW), F32)] * SC_PIECES,
        compiler_params=_cparams(1),
        name="merge",
    )(attn, retn, ga, gr, x2, wba, wbr, wo, g2, wr)


def _split3(x):
    p0 = x.astype(BF16)
    r0 = x - p0.astype(F32)
    p1 = r0.astype(BF16)
    p2 = (r0 - p1.astype(F32)).astype(BF16)
    return p0, p1, p2


def _select_kernel(aff_ref, u_ref, ls_ref, idx_ref, gate_ref, dst_ref, cs_ref, ce_ref,
                   selbuf, cnt, csr, rank, pieces, offi, *, cap, tb):
    ps = pl.program_id(0)
    e = pl.program_id(1)
    j = pl.program_id(2)
    nblk = SELECT_BLOCKS
    pc = SELECT_SLOTS

    def cumsum(vals):
        inb = jnp.dot(vals.astype(BF16), u_ref[...], preferred_element_type=F32)
        tot = jnp.broadcast_to(inb[:, tb - 1:tb], (nblk, LANES))
        off = jnp.dot(ls_ref[...], tot, preferred_element_type=F32, precision=lax.Precision.HIGHEST)
        return inb, off[:, 0:1], tot[:, 0:1]

    @pl.when(jnp.logical_and(ps == 0, j == 0))
    def _():
        a = aff_ref[e]
        bits = pltpu.bitcast(a, jnp.int32)

        def bit_step(t, cur):
            cand = cur | jnp.left_shift(jnp.int32(1), 30 - t)
            n_ge = jnp.sum((bits >= cand).astype(jnp.int32), keepdims=True)
            return jnp.where(n_ge >= cap, cand, cur)

        thr = lax.fori_loop(0, 31, bit_step, jnp.zeros((1, 1), jnp.int32))
        gt = bits > thr
        eq = bits == thr
        need = (cap - jnp.sum(gt.astype(jnp.int32), keepdims=True)).astype(F32)
        eqf = eq.astype(F32)
        eq_in, eq_off, _ = cumsum(eqf)
        eq_rank = eq_in + eq_off - eqf
        sel = jnp.logical_or(gt, jnp.logical_and(eq, eq_rank < need)).astype(F32)
        selbuf[e] = sel.astype(BF16)

        @pl.when(e == 0)
        def _():
            cnt[...] = sel

        @pl.when(e > 0)
        def _():
            cnt[...] = cnt[...] + sel

    @pl.when(jnp.logical_and(ps == 1, j == 0))
    def _():
        @pl.when(e == 0)
        def _():
            c = cnt[...]
            c_in, c_off, _ = cumsum(c)
            start = c_in + c_off - c
            csr[...] = start
            cs_ref[...] = start.astype(jnp.int32)
            ce_ref[...] = (start + c).astype(jnp.int32)
            rank[...] = jnp.zeros_like(rank)

        sel = selbuf[e].astype(F32)
        s_in, s_off, s_tot = cumsum(sel)
        for k, piece in enumerate(_split3((s_in + s_off).T) + _split3(aff_ref[e].T) + _split3((csr[...] + rank[...]).T)):
            pieces[k] = piece
        rank[...] = rank[...] + sel
        offi[...] = jnp.broadcast_to(s_off + s_tot, (nblk, LANES))

    @pl.when(ps == 1)
    def _():
        slot = (j * pc + lax.broadcasted_iota(jnp.int32, (1, pc), 1)).astype(F32)
        blk = jnp.sum((offi[:, 0:1] <= slot).astype(jnp.int32), axis=0, keepdims=True)
        onehot = (lax.broadcasted_iota(jnp.int32, (nblk, pc), 0) == blk).astype(BF16)

        def pick_block(k):
            return sum(jnp.dot(pieces[k + q], onehot, preferred_element_type=F32) for q in range(3))

        inb = jnp.sum((pick_block(0) <= slot + 0.5).astype(jnp.int32), axis=0, keepdims=True)
        hit = lax.broadcasted_iota(jnp.int32, (tb, pc), 0) == inb
        idx_ref[0] = blk * tb + inb
        gate_ref[0] = jnp.sum(jnp.where(hit, pick_block(3), 0.0), axis=0, keepdims=True)
        dst_ref[0] = jnp.sum(jnp.where(hit, pick_block(6), 0.0), axis=0, keepdims=True).astype(jnp.int32)


def _select(afft, cap):
    n = afft.shape[1]
    nblk = SELECT_BLOCKS
    tb = n // nblk
    pc = SELECT_SLOTS
    assert n % nblk == 0 and tb % LANES == 0 and cap % pc == 0
    nch = cap // pc
    aff3 = afft.reshape(N_EXPERTS, nblk, tb)
    upper = jnp.asarray(np.triu(np.ones((tb, tb), np.float32)), BF16)
    lstrict = jnp.asarray(np.tril(np.ones((nblk, nblk), np.float32), -1))
    full = lambda a: pl.BlockSpec(a.shape, lambda ps, e, j: (0,) * a.ndim)
    slot_spec = pl.BlockSpec((1, 1, pc), lambda ps, e, j: (ps * (e * nch + j), 0, 0))
    tok_spec = pl.BlockSpec((nblk, tb), lambda ps, e, j: (0, 0))
    slots = lambda dt: jax.ShapeDtypeStruct((N_EXPERTS * nch, 1, pc), dt)
    idx, gate, dst, cs, ce = pl.pallas_call(
        functools.partial(_select_kernel, cap=cap, tb=tb),
        grid=(2, N_EXPERTS, nch),
        in_specs=[full(aff3), full(upper), full(lstrict)],
        out_specs=[slot_spec, slot_spec, slot_spec, tok_spec, tok_spec],
        out_shape=[slots(jnp.int32), slots(F32), slots(jnp.int32),
                   jax.ShapeDtypeStruct((nblk, tb), jnp.int32), jax.ShapeDtypeStruct((nblk, tb), jnp.int32)],
        scratch_shapes=[pltpu.VMEM((N_EXPERTS, nblk, tb), BF16), pltpu.VMEM((nblk, tb), F32),
                        pltpu.VMEM((nblk, tb), F32), pltpu.VMEM((nblk, tb), F32),
                        pltpu.VMEM((9, tb, nblk), BF16), pltpu.VMEM((nblk, LANES), F32)],
        compiler_params=_cparams(3),
        name="select",
    )(aff3, upper, lstrict)
    return idx.reshape(-1), gate.reshape(N_EXPERTS * nch, 1, pc), dst.reshape(-1), cs.reshape(-1), ce.reshape(-1)


def _sc_mesh():
    return plsc.VectorSubcoreMesh(core_axis_name="c", subcore_axis_name="s")


def _sc_scatter(rows, idx, m_out):
    m, d = rows.shape
    assert m % SC_WINDOW == 0

    @functools.partial(pl.kernel, out_type=jax.ShapeDtypeStruct((m_out, d), rows.dtype), mesh=_sc_mesh(),
                       name="sc_scatter")
    def scatter(x_hbm, i_hbm, o_hbm):
        def body(x_vmem, i_vmem):
            pltpu.sync_copy(x_vmem, o_hbm.at[i_vmem.at[0]])

        pltpu.emit_pipeline(
            body,
            grid=(m // SC_WINDOW,),
            in_specs=[pl.BlockSpec((SC_WINDOW, d), lambda i: (i, 0)),
                      pl.BlockSpec((1, SC_WINDOW), lambda i: (0, i))],
            out_specs=[],
            core_axis_name=("c", "s"),
            dimension_semantics=(pltpu.PARALLEL,),
        )(x_hbm, i_hbm)

    return scatter(rows, idx.reshape(1, m))


def _sc_gather(table, idx):
    m = idx.shape[0]
    d = table.shape[1]
    assert m % SC_WINDOW == 0

    @functools.partial(pl.kernel, out_type=jax.ShapeDtypeStruct((m, d), table.dtype), mesh=_sc_mesh(),
                       name="sc_gather")
    def gather(x_hbm, i_hbm, o_hbm):
        def body(i_vmem, o_vmem):
            pltpu.sync_copy(x_hbm.at[i_vmem.at[0]], o_vmem)

        pltpu.emit_pipeline(
            body,
            grid=(m // SC_WINDOW,),
            in_specs=[pl.BlockSpec((1, SC_WINDOW), lambda i: (0, i))],
            out_specs=[pl.BlockSpec((SC_WINDOW, d), lambda i: (i, 0))],
            core_axis_name=("c", "s"),
            dimension_semantics=(pltpu.PARALLEL,),
        )(i_hbm, o_hbm)

    return gather(table, idx.reshape(1, m))


def _row_to_col(row):
    n = row.shape[1]
    eye = lax.broadcasted_iota(jnp.int32, (n, n), 0) == lax.broadcasted_iota(jnp.int32, (n, n), 1)
    return jnp.sum(jnp.where(eye, row, jnp.zeros_like(row)), axis=1, keepdims=True)


def _ffn_kernel(gate_ref, x0_ref, x1_ref, x2_ref, x3_ref, w1_ref, w3_ref, w2_ref, *o_refs):
    xs = jnp.concatenate([r[...] for r in (x0_ref, x1_ref, x2_ref, x3_ref)], axis=1).astype(BF16)
    hg = jnp.dot(xs, w1_ref[0], preferred_element_type=F32)
    hu = jnp.dot(xs, w3_ref[0], preferred_element_type=F32)
    hid = (hg * _sigmoid(hg) * hu).astype(BF16)
    out = jnp.dot(hid, w2_ref[0], preferred_element_type=F32) * _row_to_col(gate_ref[0])
    for c, ref in enumerate(o_refs):
        ref[...] = out[:, c * SC_ROW:(c + 1) * SC_ROW]


def _expert_ffn(xs, gate3, w1, w3, w2):
    m = xs[0].shape[0]
    rows = FFN_ROWS
    nt = m // (N_EXPERTS * rows)
    wspec = pl.BlockSpec((1, D_MODEL, EXPERT_FF), lambda e, i: (e, 0, 0))
    w2spec = pl.BlockSpec((1, EXPERT_FF, D_MODEL), lambda e, i: (e, 0, 0))
    piece = pl.BlockSpec((rows, SC_ROW), lambda e, i: (e * nt + i, 0))
    per_step = gate3.shape[2] // rows
    gspec = pl.BlockSpec((1, 1, rows), lambda e, i: ((e * nt + i) // per_step, 0, (e * nt + i) % per_step))
    return pl.pallas_call(
        _ffn_kernel,
        grid=(N_EXPERTS, nt),
        in_specs=[gspec] + [piece] * SC_PIECES + [wspec, wspec, w2spec],
        out_specs=[piece] * SC_PIECES,
        out_shape=[jax.ShapeDtypeStruct((m, SC_ROW), F32)] * SC_PIECES,
        compiler_params=_cparams(2),
        name="expert_ffn",
    )(gate3, *xs, w1, w3, w2)


def _combine_kernel(tstart_ref, x_ref, cs_ref, ce_ref, r0_hbm, r1_hbm, r2_hbm, r3_hbm, o_ref, rbuf, obuf, sems, osem,
                    *, ntile, total):
    pieces_hbm = (r0_hbm, r1_hbm, r2_hbm, r3_hbm)
    win = COMBINE_WINDOW
    i = pl.program_id(0)
    slot = lax.rem(i, 2)

    def window_start(t):
        return pl.multiple_of((tstart_ref[t] // 8) * 8, 8)

    def copies(t, b):
        s = window_start(t)
        return [pltpu.make_async_copy(pieces_hbm[c].at[pl.ds(s, win)], rbuf.at[b, c], sems.at[b, c])
                for c in range(SC_PIECES)]

    @pl.when(i == 0)
    def _():
        for cp in copies(0, 0):
            cp.start()

    @pl.when(i + 1 < ntile)
    def _():
        for cp in copies(i + 1, 1 - slot):
            cp.start()

    for cp in copies(i, slot):
        cp.wait()

    first = _row_to_col(cs_ref[0])
    last = _row_to_col(ce_ref[0])

    def window_sum(read_piece, base):
        r = base + lax.broadcasted_iota(jnp.int32, (1, win), 1)
        q = jnp.logical_and(first <= r, r < last).astype(BF16)
        written = (base + lax.broadcasted_iota(jnp.int32, (win, 1), 0)) < total
        parts = [jnp.dot(q, jnp.where(written, read_piece(c), 0.0).astype(BF16), preferred_element_type=F32)
                 for c in range(SC_PIECES)]
        return jnp.concatenate(parts, axis=1)

    s0 = window_start(i)
    y = x_ref[...] + window_sum(lambda c: rbuf[slot, c], s0)

    n_extra = jnp.maximum(tstart_ref[i + 1] - (s0 + win) + win - 1, 0) // win

    def extra(k, acc):
        base = pl.multiple_of(s0 + (k + 1) * win, 8)
        cps = [pltpu.make_async_copy(pieces_hbm[c].at[pl.ds(base, win)], obuf.at[c], osem.at[c])
               for c in range(SC_PIECES)]
        for cp in cps:
            cp.start()
        for cp in cps:
            cp.wait()
        return acc + window_sum(lambda c: obuf[c], base)

    o_ref[...] = lax.fori_loop(0, n_extra, extra, y)


def _combine(xmid, cs, ce, pieces, total):
    n = xmid.shape[0]
    tt = COMBINE_TOKENS
    win = COMBINE_WINDOW
    ntile = n // tt
    tstart = jnp.concatenate([cs[::tt], jnp.full((1,), total, jnp.int32)])
    cs3 = cs.reshape(ntile, 1, tt)
    ce3 = ce.reshape(ntile, 1, tt)
    any_spec = pl.BlockSpec(memory_space=pl.ANY)
    tok = pl.BlockSpec((1, 1, tt), lambda i, ts: (i, 0, 0))
    grid_spec = pltpu.PrefetchScalarGridSpec(
        num_scalar_prefetch=1,
        grid=(ntile,),
        in_specs=[pl.BlockSpec((tt, D_MODEL), lambda i, ts: (i, 0)), tok, tok] + [any_spec] * SC_PIECES,
        out_specs=pl.BlockSpec((tt, D_MODEL), lambda i, ts: (i, 0)),
        scratch_shapes=[pltpu.VMEM((2, SC_PIECES, win, SC_ROW), F32), pltpu.VMEM((SC_PIECES, win, SC_ROW), F32),
                        pltpu.SemaphoreType.DMA((2, SC_PIECES)), pltpu.SemaphoreType.DMA((SC_PIECES,))],
    )
    return pl.pallas_call(
        functools.partial(_combine_kernel, ntile=ntile, total=total),
        grid_spec=grid_spec,
        out_shape=jax.ShapeDtypeStruct((n, D_MODEL), F32),
        compiler_params=_cparams(1),
        name="combine",
    )(tstart, xmid, cs3, ce3, *pieces)


def _t5_bucket(rel):
    half = REL_BUCKETS // 2
    max_exact = half // 2
    base = np.where(rel > 0, half, 0)
    n = np.abs(rel)
    large = max_exact + (np.log(np.maximum(n, 1) / max_exact) / math.log(REL_MAX_DIST / max_exact)
                         * (half - max_exact)).astype(np.int32)
    large = np.minimum(large, half - 1)
    return (base + np.where(n < max_exact, n, large)).astype(np.int32)


def _head_perm():
    nq = ATTN_HEADS // 2
    cols = []
    for j in range(nq):
        for half in range(2):
            h = j + nq * half
            cols.extend(range(h * ATTN_HEAD_DIM, (h + 1) * ATTN_HEAD_DIM))
    return np.asarray(cols, np.int32)


def _attn_bias_tables(rel_bias):
    q_pos = np.arange(BLOCK)[:, None]
    k_off = np.arange(3 * BLOCK)[None, :] - BLOCK
    rel = k_off - q_pos
    in_window = np.abs(rel) <= WINDOW
    bias = rel_bias.astype(F32)[_t5_bucket(rel)]
    bias = jnp.transpose(bias, (2, 0, 1))
    col = np.arange(3 * BLOCK)[None, :]
    tables = []
    for valid in (col >= BLOCK, np.ones_like(col, bool), col < 2 * BLOCK):
        t = jnp.where(jnp.asarray(in_window & valid)[None], bias, NEG)
        nq = ATTN_HEADS // 2
        rows = [jnp.concatenate([t[j], t[j + nq]], axis=1) for j in range(nq)]
        tables.append(jnp.concatenate(rows, axis=0))
    return jnp.stack(tables)


def _retention_tables(decay_logit, norm_g):
    cr = RET_CHUNK
    lg = jax.nn.log_sigmoid(decay_logit.astype(F32))
    lgf, lgb = lg[0][:, None, None], lg[1][:, None, None]
    pos = np.arange(cr, dtype=np.float32)
    dist = pos[:, None] - pos[None, :]
    scale = RET_DIM ** -0.5
    dmask = jnp.where(jnp.asarray(dist >= 0)[None],
                      jnp.exp(lgf * np.maximum(dist, 0.0)[None]),
                      jnp.exp(lgb * np.maximum(-dist, 0.0)[None])) * scale
    col = lambda v: jnp.broadcast_to(v[:, :, None], (RET_HEADS, cr, RET_DIM))
    rowf = col(jnp.exp(lg[0][:, None] * pos[None]))
    rowb = col(jnp.exp(lg[1][:, None] * (cr - 1.0 - pos)[None]))
    wkf = col(jnp.exp(lg[0][:, None] * (cr - pos)[None]) * scale)
    wkb = col(jnp.exp(lg[1][:, None] * (pos + 1.0)[None]) * scale)
    dec = jnp.concatenate([jnp.exp(lg[0] * cr), jnp.exp(lg[1] * cr)])
    return dec, dmask, rowf, rowb, wkf, wkb, norm_g.astype(F32)


def _layer(x2, b, s, p):
    qa, ka, va, qr, kr, vr, gr, ga, gt = _in_proj(x2, p["g1"], p["w_in"], p["qg"], p["kg"], p["bdq"], p["bdk"])
    attn = _attention(qa, ka, va, p["bias3"], p["sink"], b, s)
    retn = _retention(qr, kr, vr, gr, p["retn"], b, s)
    xmid, afft, *h2 = _merge(attn, retn, ga, gt, x2, p["wba"], p["wbr"], p["wo"], p["g2"], p["wr"])
    n = b * s
    cap = max(1, EC_CAPACITY_FACTOR * n // N_EXPERTS)
    total = N_EXPERTS * cap
    idx, gate3, dst, cs, ce = _select(afft, cap)
    xs = [_sc_gather(piece, idx) for piece in h2]
    outs = _expert_ffn(xs, gate3, p["w1"], p["w3"], p["w2"])
    by_token = [_sc_scatter(o, dst, total + COMBINE_WINDOW) for o in outs]
    return _combine(xmid, cs, ce, by_token, total)


def kernel(x_prompt, x_sample, norm_mix_g, w_in, q_norm_g, k_norm_g, attn_sink, rel_bias, retn_decay_logit, retn_norm_g, w_branch_attn, w_branch_retn, w_out, norm_ffn_g, w_router, w_exp_gate, w_exp_up, w_exp_down):
    depth = w_in.shape[0]
    perm = _head_perm()
    bias3 = _attn_bias_tables(rel_bias)
    bdq = jnp.asarray(np.kron(np.eye(ATTN_HEADS), np.ones((ATTN_HEAD_DIM, ATTN_HEAD_DIM))), BF16)
    bdk = jnp.asarray(np.kron(np.eye(ATTN_KV_HEADS), np.ones((ATTN_HEAD_DIM, ATTN_HEAD_DIM))), BF16)
    layers = []
    for l in range(depth):
        w = w_in[l]
        w = jnp.concatenate([w[:, :ATTN_WIDTH][:, perm], w[:, ATTN_WIDTH:]], axis=1).astype(BF16)
        wr = jnp.pad(w_router[l], ((0, 0), (0, LANES - N_EXPERTS))).astype(BF16)
        layers.append(dict(
            g1=norm_mix_g[l].astype(F32)[None], w_in=w,
            qg=(jnp.tile(q_norm_g[l].astype(F32), ATTN_HEADS) * (ATTN_HEAD_DIM ** -0.5))[None],
            kg=jnp.tile(k_norm_g[l].astype(F32), ATTN_KV_HEADS)[None],
            bdq=bdq, bdk=bdk, bias3=bias3, sink=attn_sink[l].astype(F32),
            retn=_retention_tables(retn_decay_logit[l], retn_norm_g[l]),
            wba=w_branch_attn[l][perm, :].astype(BF16), wbr=w_branch_retn[l].astype(BF16),
            wo=w_out[l].astype(BF16), g2=norm_ffn_g[l].astype(F32)[None], wr=wr,
            w1=w_exp_gate[l].astype(BF16), w3=w_exp_up[l].astype(BF16), w2=w_exp_down[l].astype(BF16)))

    def trunk(x):
        b, s, d = x.shape
        x2 = x.reshape(b * s, d)
        for p in layers:
            x2 = _layer(x2, b, s, p)
        return x2.reshape(b, s, d)

    return (trunk(x_prompt), trunk(x_sample))
```

```python
import functools
import math

import numpy as np
import jax
import jax.numpy as jnp
from jax import lax
from jax.experimental import pallas as pl
from jax.experimental.pallas import tpu as pltpu
from jax.experimental.pallas import tpu_sc as plsc

D_MODEL = 1024
ATTN_HEADS = 8
ATTN_KV_HEADS = 2
ATTN_HEAD_DIM = 64
WINDOW = 128
BLOCK = 128
REL_BUCKETS = 32
REL_MAX_DIST = 128
RET_HEADS = 4
RET_DIM = 128
N_EXPERTS = 16
EC_CAPACITY_FACTOR = 2
EXPERT_FF = 1024
EPS = 1e-6

ATTN_WIDTH = ATTN_HEADS * ATTN_HEAD_DIM
KV_WIDTH = ATTN_KV_HEADS * ATTN_HEAD_DIM
RET_WIDTH = RET_HEADS * RET_DIM
IN_SPLITS = (ATTN_WIDTH, KV_WIDTH, KV_WIDTH, RET_WIDTH, RET_WIDTH, RET_WIDTH, RET_WIDTH, D_MODEL, D_MODEL)
IN_OFFSETS = tuple(int(o) for o in np.cumsum((0,) + IN_SPLITS))

LANES = 128
VMEM_LIMIT_BYTES = 56 * 1024 * 1024

TOKEN_TILE = 512
IN_PROJ_TILE = 1024
ATTN_QUERIES = 512
RET_CHUNK = 256
FFN_ROWS = 512
SELECT_BLOCKS = 128
SELECT_SLOTS = 1024
SC_WINDOW = 128
SC_ROW = 256
SC_PIECES = D_MODEL // SC_ROW
COMBINE_TOKENS = 512
COMBINE_WINDOW = 1280

F32 = jnp.float32
BF16 = jnp.bfloat16
NEG = -1e30


def _cparams(n_axes, vmem=VMEM_LIMIT_BYTES):
    return pltpu.CompilerParams(dimension_semantics=("arbitrary",) * n_axes, vmem_limit_bytes=vmem)


def _sigmoid(x):
    return 1.0 / (1.0 + jnp.exp(-x))


def _in_proj_kernel(x_ref, g_ref, w_ref, qg_ref, kg_ref, bdq_ref, bdk_ref,
                    qa_ref, ka_ref, va_ref, qr_ref, kr_ref, vr_ref, gr_ref, ga_ref, gt_ref):
    x = x_ref[...]
    ms = jnp.mean(x * x, axis=-1, keepdims=True)
    h = (x * lax.rsqrt(ms + EPS) * g_ref[...]).astype(BF16)

    def mm(k):
        return jnp.dot(h, w_ref[:, IN_OFFSETS[k]:IN_OFFSETS[k + 1]], preferred_element_type=F32)

    def head_norm(t, bd_ref, gain_ref):
        ss = jnp.dot((t * t).astype(BF16), bd_ref[...], preferred_element_type=F32)
        return t * lax.rsqrt(ss * (1.0 / ATTN_HEAD_DIM) + EPS) * gain_ref[...]

    qa_ref[...] = head_norm(mm(0), bdq_ref, qg_ref).astype(BF16)
    ka_ref[...] = head_norm(mm(1), bdk_ref, kg_ref).astype(BF16)
    for k, ref in ((2, va_ref), (3, qr_ref), (4, kr_ref), (5, vr_ref), (6, gr_ref), (7, ga_ref), (8, gt_ref)):
        ref[...] = mm(k).astype(BF16)


def _in_proj(x2, g, w, qg, kg, bdq, bdk):
    n = x2.shape[0]
    tm = IN_PROJ_TILE
    full = lambda a: pl.BlockSpec(a.shape, lambda i: (0,) * a.ndim, pipeline_mode=pl.Buffered(1))
    widths = IN_SPLITS
    return pl.pallas_call(
        _in_proj_kernel,
        grid=(n // tm,),
        in_specs=[pl.BlockSpec((tm, D_MODEL), lambda i: (i, 0)), full(g), full(w), full(qg), full(kg),
                  full(bdq), full(bdk)],
        out_specs=[pl.BlockSpec((tm, wd), lambda i: (i, 0)) for wd in widths],
        out_shape=[jax.ShapeDtypeStruct((n, wd), BF16) for wd in widths],
        compiler_params=_cparams(1),
        name="in_proj",
    )(x2, g, w, qg, kg, bdq, bdk)


def _attn_kernel(sink_ref, q_ref, kp_ref, kc_ref, kn_ref, vp_ref, vc_ref, vn_ref, bias_ref, o_ref, *, nsteps):
    nq = ATTN_HEADS // 2
    ni = pl.program_id(1)
    k = jnp.concatenate([kp_ref[...], kc_ref[...], kn_ref[...]], axis=0)
    v = jnp.concatenate([vp_ref[...], vc_ref[...], vn_ref[...]], axis=0)
    low = lax.broadcasted_iota(jnp.int32, k.shape, 1) < ATTN_HEAD_DIM
    zero = jnp.zeros_like(k)
    k_lo, k_hi = jnp.where(low, k, zero), jnp.where(low, zero, k)
    v_lo, v_hi = jnp.where(low, v, zero), jnp.where(low, zero, v)
    nk = 3 * BLOCK
    low_o = lax.broadcasted_iota(jnp.int32, (BLOCK, LANES), 1) < ATTN_HEAD_DIM
    nsub = ATTN_QUERIES // BLOCK
    for sb in range(nsub):
        rows = slice(sb * BLOCK, (sb + 1) * BLOCK)
        keys = slice(sb * BLOCK, sb * BLOCK + nk)
        q = q_ref[rows, :]
        qs = jnp.concatenate([q[:, j * LANES:(j + 1) * LANES] for j in range(nq)], axis=0)
        kbd = jnp.concatenate([k_lo[keys], k_hi[keys]], axis=0)
        vbd = jnp.concatenate([v_lo[keys], v_hi[keys]], axis=0)
        s = lax.dot_general(qs, kbd, (((1,), (1,)), ((), ())), preferred_element_type=F32)
        if sb == 0:
            table = jnp.where(ni == 0, 0, 1)
        elif sb == nsub - 1:
            table = jnp.where(ni == nsteps - 1, 2, 1)
        else:
            table = 1
        s = s + bias_ref[table]
        probs, invs = [], []
        for j in range(nq):
            row_p, row_inv = [], []
            for half in range(2):
                sj = s[j * BLOCK:(j + 1) * BLOCK, half * nk:(half + 1) * nk]
                sk = sink_ref[j + nq * half]
                m = jnp.maximum(jnp.max(sj, axis=-1, keepdims=True), sk)
                p = jnp.exp(sj - m)
                den = jnp.sum(p, axis=-1, keepdims=True) + jnp.exp(sk - m)
                row_p.append(p.astype(BF16))
                row_inv.append(1.0 / den)
            probs.append(jnp.concatenate(row_p, axis=1))
            invs.append(row_inv)
        pm = jnp.concatenate(probs, axis=0)
        o = jnp.dot(pm, vbd, preferred_element_type=F32)
        for j in range(nq):
            inv = jnp.where(low_o, invs[j][0], invs[j][1])
            o_ref[rows, j * LANES:(j + 1) * LANES] = (o[j * BLOCK:(j + 1) * BLOCK] * inv).astype(BF16)


def _attention(qa, ka, va, bias3, sink, b, s):
    tq = ATTN_QUERIES
    per = tq // BLOCK
    nb = s // BLOCK
    nsteps = s // tq
    assert s % tq == 0 and nb >= 2
    n = b * s
    main = lambda wd: pl.BlockSpec((tq, wd), lambda bi, ni: (bi * nsteps + ni, 0))
    prev = pl.BlockSpec((BLOCK, KV_WIDTH), lambda bi, ni: (bi * nb + jnp.maximum(ni * per - 1, 0), 0))
    nxt = pl.BlockSpec((BLOCK, KV_WIDTH), lambda bi, ni: (bi * nb + jnp.minimum(ni * per + per, nb - 1), 0))
    return pl.pallas_call(
        functools.partial(_attn_kernel, nsteps=nsteps),
        grid=(b, nsteps),
        in_specs=[pl.BlockSpec(memory_space=pltpu.SMEM), main(ATTN_WIDTH),
                  prev, main(KV_WIDTH), nxt, prev, main(KV_WIDTH), nxt,
                  pl.BlockSpec(bias3.shape, lambda bi, ni: (0, 0, 0))],
        out_specs=main(ATTN_WIDTH),
        out_shape=jax.ShapeDtypeStruct((n, ATTN_WIDTH), BF16),
        compiler_params=_cparams(2),
        name="attn",
    )(sink, qa, ka, ka, ka, va, va, va, bias3)


def _retn_kernel(dec_ref, q_ref, k_ref, v_ref, g_ref, dmask_ref, rowf_ref, rowb_ref, wkf_ref, wkb_ref, ng_ref,
                 o_ref, tstore, uf, tb, *, nc):
    p = pl.program_id(1)
    n = pl.program_id(2)
    tn = (((0,), (0,)), ((), ()))
    nt = (((1,), (1,)), ((), ()))
    hs = lambda h: slice(h * RET_DIM, (h + 1) * RET_DIM)

    @pl.when(p == 0)
    def _():
        @pl.when(n == 0)
        def _():
            tb[...] = jnp.zeros_like(tb)
        c = nc - 1 - n
        for h in range(RET_HEADS):
            tstore[c, h] = tb[h].astype(BF16)
            kw = (k_ref[:, hs(h)].astype(F32) * wkb_ref[h]).astype(BF16)
            upd = lax.dot_general(kw, v_ref[:, hs(h)], tn, preferred_element_type=F32)
            tb[h] = dec_ref[RET_HEADS + h] * tb[h] + upd

    @pl.when(p == 1)
    def _():
        @pl.when(n == 0)
        def _():
            uf[...] = jnp.zeros_like(uf)
        for h in range(RET_HEADS):
            qh = q_ref[:, hs(h)]
            kh = k_ref[:, hs(h)]
            vh = v_ref[:, hs(h)]
            sc = lax.dot_general(qh, kh, nt, preferred_element_type=F32) * dmask_ref[h]
            intra = jnp.dot(sc.astype(BF16), vh, preferred_element_type=F32)
            states = jnp.concatenate([uf[h].astype(BF16), tstore[n, h]], axis=1)
            cross = jnp.dot(qh, states, preferred_element_type=F32)
            o = intra + cross[:, :RET_DIM] * rowf_ref[h] + cross[:, RET_DIM:] * rowb_ref[h]
            mu = jnp.mean(o, axis=-1, keepdims=True)
            d = o - mu
            var = jnp.mean(d * d, axis=-1, keepdims=True)
            on = d * lax.rsqrt(var + EPS) * ng_ref[h:h + 1, :]
            gate = g_ref[:, hs(h)].astype(F32)
            o_ref[:, hs(h)] = (gate * _sigmoid(gate) * on).astype(BF16)
            kw = (kh.astype(F32) * wkf_ref[h]).astype(BF16)
            uf[h] = dec_ref[h] * uf[h] + lax.dot_general(kw, vh, tn, preferred_element_type=F32)


def _retention(qr, kr, vr, gr, tables, b, s):
    dec, dmask, rowf, rowb, wkf, wkb, ng = tables
    cr = RET_CHUNK
    nc = s // cr
    n = b * s
    full = lambda a: pl.BlockSpec(a.shape, lambda bi, pi, ni: (0,) * a.ndim)
    fwd_spec = pl.BlockSpec((cr, RET_WIDTH), lambda bi, pi, ni: (bi * nc + ni * pi, 0))
    kv_spec = pl.BlockSpec((cr, RET_WIDTH), lambda bi, pi, ni: (bi * nc + ni * pi + (1 - pi) * (nc - 1 - ni), 0))
    return pl.pallas_call(
        functools.partial(_retn_kernel, nc=nc),
        grid=(b, 2, nc),
        in_specs=[pl.BlockSpec(memory_space=pltpu.SMEM), fwd_spec, kv_spec, kv_spec, fwd_spec,
                  full(dmask), full(rowf), full(rowb), full(wkf), full(wkb), full(ng)],
        out_specs=fwd_spec,
        out_shape=jax.ShapeDtypeStruct((n, RET_WIDTH), BF16),
        scratch_shapes=[pltpu.VMEM((nc, RET_HEADS, RET_DIM, RET_DIM), BF16),
                        pltpu.VMEM((RET_HEADS, RET_DIM, RET_DIM), F32),
                        pltpu.VMEM((RET_HEADS, RET_DIM, RET_DIM), F32)],
        compiler_params=_cparams(3),
        name="retention",
    )(dec, qr, kr, vr, gr, dmask, rowf, rowb, wkf, wkb, ng)


def _merge_kernel(attn_ref, retn_ref, ga_ref, gr_ref, x_ref, wba_ref, wbr_ref, wo_ref, g2_ref, wr_ref,
                  xmid_ref, afft_ref, *h2_refs):
    a = jnp.dot(attn_ref[...], wba_ref[...], preferred_element_type=F32)
    r = jnp.dot(retn_ref[...], wbr_ref[...], preferred_element_type=F32)
    merged = _sigmoid(ga_ref[...].astype(F32)) * a + _sigmoid(gr_ref[...].astype(F32)) * r
    xn = x_ref[...] + jnp.dot(merged.astype(BF16), wo_ref[...], preferred_element_type=F32)
    xmid_ref[...] = xn
    ms = jnp.mean(xn * xn, axis=-1, keepdims=True)
    h2 = xn * lax.rsqrt(ms + EPS) * g2_ref[...]
    for c, ref in enumerate(h2_refs):
        ref[...] = h2[:, c * SC_ROW:(c + 1) * SC_ROW]
    logits = jnp.dot(h2.astype(BF16), wr_ref[...], preferred_element_type=F32)
    real = lax.broadcasted_iota(jnp.int32, logits.shape, 1) < N_EXPERTS
    logits = jnp.where(real, logits, -jnp.inf)
    m = jnp.max(logits, axis=-1, keepdims=True)
    ex = jnp.exp(logits - m)
    aff = ex / jnp.sum(ex, axis=-1, keepdims=True)
    afft_ref[...] = aff.T[:N_EXPERTS, :]


def _merge(attn, retn, ga, gr, x2, wba, wbr, wo, g2, wr):
    n = x2.shape[0]
    tm = TOKEN_TILE
    full = lambda a: pl.BlockSpec(a.shape, lambda i: (0,) * a.ndim)
    row = lambda wd: pl.BlockSpec((tm, wd), lambda i: (i, 0))
    return pl.pallas_call(
        _merge_kernel,
        grid=(n // tm,),
        in_specs=[row(ATTN_WIDTH), row(RET_WIDTH), row(D_MODEL), row(D_MODEL), row(D_MODEL),
                  full(wba), full(wbr), full(wo), full(g2), full(wr)],
        out_specs=[row(D_MODEL), pl.BlockSpec((N_EXPERTS, tm), lambda i: (0, i))] + [row(SC_ROW)] * SC_PIECES,
        out_shape=[jax.ShapeDtypeStruct((n, D_MODEL), F32), jax.ShapeDtypeStruct((N_EXPERTS, n), F32)]
        + [jax.ShapeDtypeStruct((n, SC_ROW), F32)] * SC_PIECES,
        compiler_params=_cparams(1),
        name="merge",
    )(attn, retn, ga, gr, x2, wba, wbr, wo, g2, wr)


def _split3(x):
    p0 = x.astype(BF16)
    r0 = x - p0.astype(F32)
    p1 = r0.astype(BF16)
    p2 = (r0 - p1.astype(F32)).astype(BF16)
    return p0, p1, p2


def _select_kernel(aff_ref, u_ref, ls_ref, idx_ref, gate_ref, dst_ref, cs_ref, ce_ref,
                   selbuf, cnt, csr, rank, pieces, offi, *, cap, tb):
    ps = pl.program_id(0)
    e = pl.program_id(1)
    j = pl.program_id(2)
    nblk = SELECT_BLOCKS
    pc = SELECT_SLOTS

    def cumsum(vals):
        inb = jnp.dot(vals.astype(BF16), u_ref[...], preferred_element_type=F32)
        tot = jnp.broadcast_to(inb[:, tb - 1:tb], (nblk, LANES))
        off = jnp.dot(ls_ref[...], tot, preferred_element_type=F32, precision=lax.Precision.HIGHEST)
        return inb, off[:, 0:1], tot[:, 0:1]

    @pl.when(jnp.logical_and(ps == 0, j == 0))
    def _():
        a = aff_ref[e]
        bits = pltpu.bitcast(a, jnp.int32)

        def bit_step(t, cur):
            cand = cur | jnp.left_shift(jnp.int32(1), 30 - t)
            n_ge = jnp.sum((bits >= cand).astype(jnp.int32), keepdims=True)
            return jnp.where(n_ge >= cap, cand, cur)

        thr = lax.fori_loop(0, 31, bit_step, jnp.zeros((1, 1), jnp.int32))
        gt = bits > thr
        eq = bits == thr
        need = (cap - jnp.sum(gt.astype(jnp.int32), keepdims=True)).astype(F32)
        eqf = eq.astype(F32)
        eq_in, eq_off, _ = cumsum(eqf)
        eq_rank = eq_in + eq_off - eqf
        sel = jnp.logical_or(gt, jnp.logical_and(eq, eq_rank < need)).astype(F32)
        selbuf[e] = sel.astype(BF16)

        @pl.when(e == 0)
        def _():
            cnt[...] = sel

        @pl.when(e > 0)
        def _():
            cnt[...] = cnt[...] + sel

    @pl.when(jnp.logical_and(ps == 1, j == 0))
    def _():
        @pl.when(e == 0)
        def _():
            c = cnt[...]
            c_in, c_off, _ = cumsum(c)
            start = c_in + c_off - c
            csr[...] = start
            cs_ref[...] = start.astype(jnp.int32)
            ce_ref[...] = (start + c).astype(jnp.int32)
            rank[...] = jnp.zeros_like(rank)

        sel = selbuf[e].astype(F32)
        s_in, s_off, s_tot = cumsum(sel)
        for k, piece in enumerate(_split3((s_in + s_off).T) + _split3(aff_ref[e].T) + _split3((csr[...] + rank[...]).T)):
            pieces[k] = piece
        rank[...] = rank[...] + sel
        offi[...] = jnp.broadcast_to(s_off + s_tot, (nblk, LANES))

    @pl.when(ps == 1)
    def _():
        slot = (j * pc + lax.broadcasted_iota(jnp.int32, (1, pc), 1)).astype(F32)
        blk = jnp.sum((offi[:, 0:1] <= slot).astype(jnp.int32), axis=0, keepdims=True)
        onehot = (lax.broadcasted_iota(jnp.int32, (nblk, pc), 0) == blk).astype(BF16)

        def pick_block(k):
            return sum(jnp.dot(pieces[k + q], onehot, preferred_element_type=F32) for q in range(3))

        inb = jnp.sum((pick_block(0) <= slot + 0.5).astype(jnp.int32), axis=0, keepdims=True)
        hit = lax.broadcasted_iota(jnp.int32, (tb, pc), 0) == inb
        idx_ref[0] = blk * tb + inb
        gate_ref[0] = jnp.sum(jnp.where(hit, pick_block(3), 0.0), axis=0, keepdims=True)
        dst_ref[0] = jnp.sum(jnp.where(hit, pick_block(6), 0.0), axis=0, keepdims=True).astype(jnp.int32)


def _select(afft, cap):
    n = afft.shape[1]
    nblk = SELECT_BLOCKS
    tb = n // nblk
    pc = SELECT_SLOTS
    assert n % nblk == 0 and tb % LANES == 0 and cap % pc == 0
    nch = cap // pc
    aff3 = afft.reshape(N_EXPERTS, nblk, tb)
    upper = jnp.asarray(np.triu(np.ones((tb, tb), np.float32)), BF16)
    lstrict = jnp.asarray(np.tril(np.ones((nblk, nblk), np.float32), -1))
    full = lambda a: pl.BlockSpec(a.shape, lambda ps, e, j: (0,) * a.ndim)
    slot_spec = pl.BlockSpec((1, 1, pc), lambda ps, e, j: (ps * (e * nch + j), 0, 0))
    tok_spec = pl.BlockSpec((nblk, tb), lambda ps, e, j: (0, 0))
    slots = lambda dt: jax.ShapeDtypeStruct((N_EXPERTS * nch, 1, pc), dt)
    idx, gate, dst, cs, ce = pl.pallas_call(
        functools.partial(_select_kernel, cap=cap, tb=tb),
        grid=(2, N_EXPERTS, nch),
        in_specs=[full(aff3), full(upper), full(lstrict)],
        out_specs=[slot_spec, slot_spec, slot_spec, tok_spec, tok_spec],
        out_shape=[slots(jnp.int32), slots(F32), slots(jnp.int32),
                   jax.ShapeDtypeStruct((nblk, tb), jnp.int32), jax.ShapeDtypeStruct((nblk, tb), jnp.int32)],
        scratch_shapes=[pltpu.VMEM((N_EXPERTS, nblk, tb), BF16), pltpu.VMEM((nblk, tb), F32),
                        pltpu.VMEM((nblk, tb), F32), pltpu.VMEM((nblk, tb), F32),
                        pltpu.VMEM((9, tb, nblk), BF16), pltpu.VMEM((nblk, LANES), F32)],
        compiler_params=_cparams(3),
        name="select",
    )(aff3, upper, lstrict)
    return idx.reshape(-1), gate.reshape(N_EXPERTS * nch, 1, pc), dst.reshape(-1), cs.reshape(-1), ce.reshape(-1)


def _sc_mesh():
    return plsc.VectorSubcoreMesh(core_axis_name="c", subcore_axis_name="s")


def _sc_scatter(rows, idx, m_out):
    m, d = rows.shape
    assert m % SC_WINDOW == 0

    @functools.partial(pl.kernel, out_type=jax.ShapeDtypeStruct((m_out, d), rows.dtype), mesh=_sc_mesh(),
                       name="sc_scatter")
    def scatter(x_hbm, i_hbm, o_hbm):
        def body(x_vmem, i_vmem):
            pltpu.sync_copy(x_vmem, o_hbm.at[i_vmem.at[0]])

        pltpu.emit_pipeline(
            body,
            grid=(m // SC_WINDOW,),
            in_specs=[pl.BlockSpec((SC_WINDOW, d), lambda i: (i, 0)),
                      pl.BlockSpec((1, SC_WINDOW), lambda i: (0, i))],
            out_specs=[],
            core_axis_name=("c", "s"),
            dimension_semantics=(pltpu.PARALLEL,),
        )(x_hbm, i_hbm)

    return scatter(rows, idx.reshape(1, m))


def _sc_gather(table, idx):
    m = idx.shape[0]
    d = table.shape[1]
    assert m % SC_WINDOW == 0

    @functools.partial(pl.kernel, out_type=jax.ShapeDtypeStruct((m, d), table.dtype), mesh=_sc_mesh(),
                       name="sc_gather")
    def gather(x_hbm, i_hbm, o_hbm):
        def body(i_vmem, o_vmem):
            pltpu.sync_copy(x_hbm.at[i_vmem.at[0]], o_vmem)

        pltpu.emit_pipeline(
            body,
            grid=(m // SC_WINDOW,),
            in_specs=[pl.BlockSpec((1, SC_WINDOW), lambda i: (0, i))],
            out_specs=[pl.BlockSpec((SC_WINDOW, d), lambda i: (i, 0))],
            core_axis_name=("c", "s"),
            dimension_semantics=(pltpu.PARALLEL,),
        )(i_hbm, o_hbm)

    return gather(table, idx.reshape(1, m))


def _row_to_col(row):
    n = row.shape[1]
    eye = lax.broadcasted_iota(jnp.int32, (n, n), 0) == lax.broadcasted_iota(jnp.int32, (n, n), 1)
    return jnp.sum(jnp.where(eye, row, jnp.zeros_like(row)), axis=1, keepdims=True)


def _ffn_kernel(gate_ref, x0_ref, x1_ref, x2_ref, x3_ref, w1_ref, w3_ref, w2_ref, o0_ref, o1_ref, o2_ref, o3_ref,
                w1b, w3b, w2b):
    @pl.when(pl.program_id(1) == 0)
    def _():
        w1b[...] = w1_ref[0].astype(BF16)
        w3b[...] = w3_ref[0].astype(BF16)
        w2b[...] = w2_ref[0].astype(BF16)

    xs = jnp.concatenate([r[...] for r in (x0_ref, x1_ref, x2_ref, x3_ref)], axis=1).astype(BF16)
    hg = jnp.dot(xs, w1b[...], preferred_element_type=F32)
    hu = jnp.dot(xs, w3b[...], preferred_element_type=F32)
    hid = (hg * _sigmoid(hg) * hu).astype(BF16)
    out = jnp.dot(hid, w2b[...], preferred_element_type=F32) * _row_to_col(gate_ref[0])
    for c, ref in enumerate((o0_ref, o1_ref, o2_ref, o3_ref)):
        ref[...] = out[:, c * SC_ROW:(c + 1) * SC_ROW]


def _expert_ffn(xs, gate3, w1, w3, w2):
    m = xs[0].shape[0]
    rows = FFN_ROWS
    nt = m // (N_EXPERTS * rows)
    wspec = pl.BlockSpec((1, D_MODEL, EXPERT_FF), lambda e, i: (e, 0, 0))
    w2spec = pl.BlockSpec((1, EXPERT_FF, D_MODEL), lambda e, i: (e, 0, 0))
    piece = pl.BlockSpec((rows, SC_ROW), lambda e, i: (e * nt + i, 0))
    per_step = gate3.shape[2] // rows
    gspec = pl.BlockSpec((1, 1, rows), lambda e, i: ((e * nt + i) // per_step, 0, (e * nt + i) % per_step))
    return pl.pallas_call(
        _ffn_kernel,
        grid=(N_EXPERTS, nt),
        in_specs=[gspec] + [piece] * SC_PIECES + [wspec, wspec, w2spec],
        out_specs=[piece] * SC_PIECES,
        out_shape=[jax.ShapeDtypeStruct((m, SC_ROW), F32)] * SC_PIECES,
        scratch_shapes=[pltpu.VMEM((D_MODEL, EXPERT_FF), BF16), pltpu.VMEM((D_MODEL, EXPERT_FF), BF16),
                        pltpu.VMEM((EXPERT_FF, D_MODEL), BF16)],
        compiler_params=_cparams(2),
        name="expert_ffn",
    )(gate3, *xs, w1, w3, w2)


def _combine_kernel(tstart_ref, x_ref, cs_ref, ce_ref, r0_hbm, r1_hbm, r2_hbm, r3_hbm, o_ref, rbuf, obuf, sems, osem,
                    *, ntile, total):
    pieces_hbm = (r0_hbm, r1_hbm, r2_hbm, r3_hbm)
    win = COMBINE_WINDOW
    i = pl.program_id(0)
    slot = lax.rem(i, 2)

    def window_start(t):
        return pl.multiple_of((tstart_ref[t] // 8) * 8, 8)

    def copies(t, b):
        s = window_start(t)
        return [pltpu.make_async_copy(pieces_hbm[c].at[pl.ds(s, win)], rbuf.at[b, c], sems.at[b, c])
                for c in range(SC_PIECES)]

    @pl.when(i == 0)
    def _():
        for cp in copies(0, 0):
            cp.start()

    @pl.when(i + 1 < ntile)
    def _():
        for cp in copies(i + 1, 1 - slot):
            cp.start()

    for cp in copies(i, slot):
        cp.wait()

    first = _row_to_col(cs_ref[0])
    last = _row_to_col(ce_ref[0])

    def window_sum(read_piece, base):
        r = base + lax.broadcasted_iota(jnp.int32, (1, win), 1)
        q = jnp.logical_and(first <= r, r < last).astype(BF16)

        def summed(clean):
            parts = [jnp.dot(q, clean(read_piece(c)).astype(BF16), preferred_element_type=F32)
                     for c in range(SC_PIECES)]
            return jnp.concatenate(parts, axis=1)

        def zero_unwritten(rows):
            written = (base + lax.broadcasted_iota(jnp.int32, (win, 1), 0)) < total
            return jnp.where(written, rows, 0.0)

        return lax.cond(base + win > total, lambda: summed(zero_unwritten), lambda: summed(lambda rows: rows))

    s0 = window_start(i)
    y = x_ref[...] + window_sum(lambda c: rbuf[slot, c], s0)

    n_extra = jnp.maximum(tstart_ref[i + 1] - (s0 + win) + win - 1, 0) // win

    def extra(k, acc):
        base = pl.multiple_of(s0 + (k + 1) * win, 8)
        cps = [pltpu.make_async_copy(pieces_hbm[c].at[pl.ds(base, win)], obuf.at[c], osem.at[c])
               for c in range(SC_PIECES)]
        for cp in cps:
            cp.start()
        for cp in cps:
            cp.wait()
        return acc + window_sum(lambda c: obuf[c], base)

    o_ref[...] = lax.fori_loop(0, n_extra, extra, y)


def _combine(xmid, cs, ce, pieces, total):
    n = xmid.shape[0]
    tt = COMBINE_TOKENS
    win = COMBINE_WINDOW
    ntile = n // tt
    tstart = jnp.concatenate([cs[::tt], jnp.full((1,), total, jnp.int32)])
    cs3 = cs.reshape(ntile, 1, tt)
    ce3 = ce.reshape(ntile, 1, tt)
    any_spec = pl.BlockSpec(memory_space=pl.ANY)
    tok = pl.BlockSpec((1, 1, tt), lambda i, ts: (i, 0, 0))
    grid_spec = pltpu.PrefetchScalarGridSpec(
        num_scalar_prefetch=1,
        grid=(ntile,),
        in_specs=[pl.BlockSpec((tt, D_MODEL), lambda i, ts: (i, 0)), tok, tok] + [any_spec] * SC_PIECES,
        out_specs=pl.BlockSpec((tt, D_MODEL), lambda i, ts: (i, 0)),
        scratch_shapes=[pltpu.VMEM((2, SC_PIECES, win, SC_ROW), F32), pltpu.VMEM((SC_PIECES, win, SC_ROW), F32),
                        pltpu.SemaphoreType.DMA((2, SC_PIECES)), pltpu.SemaphoreType.DMA((SC_PIECES,))],
    )
    return pl.pallas_call(
        functools.partial(_combine_kernel, ntile=ntile, total=total),
        grid_spec=grid_spec,
        out_shape=jax.ShapeDtypeStruct((n, D_MODEL), F32),
        compiler_params=_cparams(1),
        name="combine",
    )(tstart, xmid, cs3, ce3, *pieces)


def _t5_bucket(rel):
    half = REL_BUCKETS // 2
    max_exact = half // 2
    base = np.where(rel > 0, half, 0)
    n = np.abs(rel)
    large = max_exact + (np.log(np.maximum(n, 1) / max_exact) / math.log(REL_MAX_DIST / max_exact)
                         * (half - max_exact)).astype(np.int32)
    large = np.minimum(large, half - 1)
    return (base + np.where(n < max_exact, n, large)).astype(np.int32)


def _head_perm():
    nq = ATTN_HEADS // 2
    cols = []
    for j in range(nq):
        for half in range(2):
            h = j + nq * half
            cols.extend(range(h * ATTN_HEAD_DIM, (h + 1) * ATTN_HEAD_DIM))
    return np.asarray(cols, np.int32)


def _attn_bias_tables(rel_bias):
    q_pos = np.arange(BLOCK)[:, None]
    k_off = np.arange(3 * BLOCK)[None, :] - BLOCK
    rel = k_off - q_pos
    in_window = np.abs(rel) <= WINDOW
    onehot = jnp.asarray(_t5_bucket(rel)[:, :, None] == np.arange(REL_BUCKETS)[None, None, :], F32)
    bias = jnp.einsum("qkb,bh->hqk", onehot, rel_bias.astype(F32), precision=lax.Precision.HIGHEST)
    col = np.arange(3 * BLOCK)[None, :]
    tables = []
    for valid in (col >= BLOCK, np.ones_like(col, bool), col < 2 * BLOCK):
        t = jnp.where(jnp.asarray(in_window & valid)[None], bias, NEG)
        nq = ATTN_HEADS // 2
        rows = [jnp.concatenate([t[j], t[j + nq]], axis=1) for j in range(nq)]
        tables.append(jnp.concatenate(rows, axis=0))
    return jnp.stack(tables)


def _retention_tables(decay_logit, norm_g):
    cr = RET_CHUNK
    lg = jax.nn.log_sigmoid(decay_logit.astype(F32))
    lgf, lgb = lg[0][:, None, None], lg[1][:, None, None]
    pos = np.arange(cr, dtype=np.float32)
    dist = pos[:, None] - pos[None, :]
    scale = RET_DIM ** -0.5
    dmask = jnp.where(jnp.asarray(dist >= 0)[None],
                      jnp.exp(lgf * np.maximum(dist, 0.0)[None]),
                      jnp.exp(lgb * np.maximum(-dist, 0.0)[None])) * scale
    col = lambda v: jnp.broadcast_to(v[:, :, None], (RET_HEADS, cr, RET_DIM))
    rowf = col(jnp.exp(lg[0][:, None] * pos[None]))
    rowb = col(jnp.exp(lg[1][:, None] * (cr - 1.0 - pos)[None]))
    wkf = col(jnp.exp(lg[0][:, None] * (cr - pos)[None]) * scale)
    wkb = col(jnp.exp(lg[1][:, None] * (pos + 1.0)[None]) * scale)
    dec = jnp.concatenate([jnp.exp(lg[0] * cr), jnp.exp(lg[1] * cr)])
    return dec, dmask, rowf, rowb, wkf, wkb, norm_g.astype(F32)


def _layer(x2, b, s, p):
    qa, ka, va, qr, kr, vr, gr, ga, gt = _in_proj(x2, p["g1"], p["w_in"], p["qg"], p["kg"], p["bdq"], p["bdk"])
    attn = _attention(qa, ka, va, p["bias3"], p["sink"], b, s)
    retn = _retention(qr, kr, vr, gr, p["retn"], b, s)
    xmid, afft, *h2 = _merge(attn, retn, ga, gt, x2, p["wba"], p["wbr"], p["wo"], p["g2"], p["wr"])
    n = b * s
    cap = max(1, EC_CAPACITY_FACTOR * n // N_EXPERTS)
    total = N_EXPERTS * cap
    idx, gate3, dst, cs, ce = _select(afft, cap)
    xs = [_sc_gather(piece, idx) for piece in h2]
    outs = _expert_ffn(xs, gate3, p["w1"], p["w3"], p["w2"])
    by_token = [_sc_scatter(o, dst, total + COMBINE_WINDOW) for o in outs]
    return _combine(xmid, cs, ce, by_token, total)


def kernel(x_prompt, x_sample, norm_mix_g, w_in, q_norm_g, k_norm_g, attn_sink, rel_bias, retn_decay_logit, retn_norm_g, w_branch_attn, w_branch_retn, w_out, norm_ffn_g, w_router, w_exp_gate, w_exp_up, w_exp_down):
    depth = w_in.shape[0]
    perm = _head_perm()
    bias3 = _attn_bias_tables(rel_bias)
    bdq = jnp.asarray(np.kron(np.eye(ATTN_HEADS), np.ones((ATTN_HEAD_DIM, ATTN_HEAD_DIM))), BF16)
    bdk = jnp.asarray(np.kron(np.eye(ATTN_KV_HEADS), np.ones((ATTN_HEAD_DIM, ATTN_HEAD_DIM))), BF16)
    layers = []
    for l in range(depth):
        w = w_in[l]
        w = jnp.concatenate([w[:, :ATTN_WIDTH][:, perm], w[:, ATTN_WIDTH:]], axis=1).astype(BF16)
        wr = jnp.pad(w_router[l], ((0, 0), (0, LANES - N_EXPERTS))).astype(BF16)
        layers.append(dict(
            g1=norm_mix_g[l].astype(F32)[None], w_in=w,
            qg=(jnp.tile(q_norm_g[l].astype(F32), ATTN_HEADS) * (ATTN_HEAD_DIM ** -0.5))[None],
            kg=jnp.tile(k_norm_g[l].astype(F32), ATTN_KV_HEADS)[None],
            bdq=bdq, bdk=bdk, bias3=bias3, sink=attn_sink[l].astype(F32),
            retn=_retention_tables(retn_decay_logit[l], retn_norm_g[l]),
            wba=w_branch_attn[l][perm, :].astype(BF16), wbr=w_branch_retn[l].astype(BF16),
            wo=w_out[l].astype(BF16), g2=norm_ffn_g[l].astype(F32)[None], wr=wr,
            w1=w_exp_gate[l], w3=w_exp_up[l], w2=w_exp_down[l]))

    def trunk(x):
        b, s, d = x.shape
        x2 = x.reshape(b * s, d)
        for p in layers:
            x2 = _layer(x2, b, s, p)
        return x2.reshape(b, s, d)

    return (trunk(x_prompt), trunk(x_sample))
```

```python
import functools
import math

import numpy as np
import jax
import jax.numpy as jnp
from jax import lax
from jax.experimental import pallas as pl
from jax.experimental.pallas import tpu as pltpu
from jax.experimental.pallas import tpu_sc as plsc

D_MODEL = 1024
ATTN_HEADS = 8
ATTN_KV_HEADS = 2
ATTN_HEAD_DIM = 64
WINDOW = 128
BLOCK = 128
REL_BUCKETS = 32
REL_MAX_DIST = 128
RET_HEADS = 4
RET_DIM = 128
N_EXPERTS = 16
EC_CAPACITY_FACTOR = 2
EXPERT_FF = 1024
EPS = 1e-6

ATTN_WIDTH = ATTN_HEADS * ATTN_HEAD_DIM
KV_WIDTH = ATTN_KV_HEADS * ATTN_HEAD_DIM
RET_WIDTH = RET_HEADS * RET_DIM
IN_SPLITS = (ATTN_WIDTH, KV_WIDTH, KV_WIDTH, RET_WIDTH, RET_WIDTH, RET_WIDTH, RET_WIDTH, D_MODEL, D_MODEL)
IN_OFFSETS = tuple(int(o) for o in np.cumsum((0,) + IN_SPLITS))

LANES = 128
VMEM_LIMIT_BYTES = 56 * 1024 * 1024

TOKEN_TILE = 512
IN_PROJ_TILE = 1024
ATTN_QUERIES = 512
RET_CHUNK = 256
FFN_ROWS = 1024
SELECT_BLOCKS = 128
SELECT_SLOTS = 1024
SC_WINDOW = 128
SC_ROW = 256
PACKED_WIDTH = D_MODEL // 2
SC_PIECES = PACKED_WIDTH // SC_ROW
COMBINE_TOKENS = 512
COMBINE_WINDOW = 1280

F32 = jnp.float32
BF16 = jnp.bfloat16
NEG = -1e30


def _cparams(n_axes, vmem=VMEM_LIMIT_BYTES):
    return pltpu.CompilerParams(dimension_semantics=("arbitrary",) * n_axes, vmem_limit_bytes=vmem)


def _sigmoid(x):
    return 1.0 / (1.0 + jnp.exp(-x))


HIGH_HALF = -65536


def _pack_rows(x):
    bits = pltpu.bitcast(x.astype(BF16).astype(F32), jnp.int32)
    words = lax.shift_right_logical(bits[:, :PACKED_WIDTH], 16) | (bits[:, PACKED_WIDTH:] & HIGH_HALF)
    return [words[:, c * SC_ROW:(c + 1) * SC_ROW] for c in range(SC_PIECES)]


def _unpack_rows(pieces):
    low = [pltpu.bitcast(lax.shift_left(w, 16), F32) for w in pieces]
    high = [pltpu.bitcast(w & HIGH_HALF, F32) for w in pieces]
    return jnp.concatenate(low + high, axis=1).astype(BF16)


def _in_proj_kernel(x_ref, g_ref, w_ref, qg_ref, kg_ref, bdq_ref, bdk_ref,
                    qa_ref, ka_ref, va_ref, qr_ref, kr_ref, vr_ref, gr_ref, ga_ref, gt_ref):
    x = x_ref[...]
    ms = jnp.mean(x * x, axis=-1, keepdims=True)
    h = (x * lax.rsqrt(ms + EPS) * g_ref[...]).astype(BF16)

    def mm(k):
        return jnp.dot(h, w_ref[:, IN_OFFSETS[k]:IN_OFFSETS[k + 1]], preferred_element_type=F32)

    def head_norm(t, bd_ref, gain_ref):
        ss = jnp.dot((t * t).astype(BF16), bd_ref[...], preferred_element_type=F32)
        return t * lax.rsqrt(ss * (1.0 / ATTN_HEAD_DIM) + EPS) * gain_ref[...]

    qa_ref[...] = head_norm(mm(0), bdq_ref, qg_ref).astype(BF16)
    ka_ref[...] = head_norm(mm(1), bdk_ref, kg_ref).astype(BF16)
    for k, ref in ((2, va_ref), (3, qr_ref), (4, kr_ref), (5, vr_ref), (6, gr_ref), (7, ga_ref), (8, gt_ref)):
        ref[...] = mm(k).astype(BF16)


def _in_proj(x2, g, w, qg, kg, bdq, bdk):
    n = x2.shape[0]
    tm = IN_PROJ_TILE
    full = lambda a: pl.BlockSpec(a.shape, lambda i: (0,) * a.ndim, pipeline_mode=pl.Buffered(1))
    widths = IN_SPLITS
    return pl.pallas_call(
        _in_proj_kernel,
        grid=(n // tm,),
        in_specs=[pl.BlockSpec((tm, D_MODEL), lambda i: (i, 0)), full(g), full(w), full(qg), full(kg),
                  full(bdq), full(bdk)],
        out_specs=[pl.BlockSpec((tm, wd), lambda i: (i, 0)) for wd in widths],
        out_shape=[jax.ShapeDtypeStruct((n, wd), BF16) for wd in widths],
        compiler_params=_cparams(1),
        name="in_proj",
    )(x2, g, w, qg, kg, bdq, bdk)


def _attn_kernel(sink_ref, q_ref, kp_ref, kc_ref, kn_ref, vp_ref, vc_ref, vn_ref, bias_ref, o_ref, *, nsteps):
    nq = ATTN_HEADS // 2
    ni = pl.program_id(1)
    k = jnp.concatenate([kp_ref[...], kc_ref[...], kn_ref[...]], axis=0)
    v = jnp.concatenate([vp_ref[...], vc_ref[...], vn_ref[...]], axis=0)
    low = lax.broadcasted_iota(jnp.int32, k.shape, 1) < ATTN_HEAD_DIM
    zero = jnp.zeros_like(k)
    k_lo, k_hi = jnp.where(low, k, zero), jnp.where(low, zero, k)
    v_lo, v_hi = jnp.where(low, v, zero), jnp.where(low, zero, v)
    nk = 3 * BLOCK
    low_o = lax.broadcasted_iota(jnp.int32, (BLOCK, LANES), 1) < ATTN_HEAD_DIM
    nsub = ATTN_QUERIES // BLOCK
    for sb in range(nsub):
        rows = slice(sb * BLOCK, (sb + 1) * BLOCK)
        keys = slice(sb * BLOCK, sb * BLOCK + nk)
        q = q_ref[rows, :]
        qs = jnp.concatenate([q[:, j * LANES:(j + 1) * LANES] for j in range(nq)], axis=0)
        kbd = jnp.concatenate([k_lo[keys], k_hi[keys]], axis=0)
        vbd = jnp.concatenate([v_lo[keys], v_hi[keys]], axis=0)
        s = lax.dot_general(qs, kbd, (((1,), (1,)), ((), ())), preferred_element_type=F32)
        if sb == 0:
            table = jnp.where(ni == 0, 0, 1)
        elif sb == nsub - 1:
            table = jnp.where(ni == nsteps - 1, 2, 1)
        else:
            table = 1
        s = s + bias_ref[table]
        probs, invs = [], []
        for j in range(nq):
            row_p, row_inv = [], []
            for half in range(2):
                sj = s[j * BLOCK:(j + 1) * BLOCK, half * nk:(half + 1) * nk]
                sk = sink_ref[j + nq * half]
                m = jnp.maximum(jnp.max(sj, axis=-1, keepdims=True), sk)
                p = jnp.exp(sj - m)
                den = jnp.sum(p, axis=-1, keepdims=True) + jnp.exp(sk - m)
                row_p.append(p.astype(BF16))
                row_inv.append(1.0 / den)
            probs.append(jnp.concatenate(row_p, axis=1))
            invs.append(row_inv)
        pm = jnp.concatenate(probs, axis=0)
        o = jnp.dot(pm, vbd, preferred_element_type=F32)
        for j in range(nq):
            inv = jnp.where(low_o, invs[j][0], invs[j][1])
            o_ref[rows, j * LANES:(j + 1) * LANES] = (o[j * BLOCK:(j + 1) * BLOCK] * inv).astype(BF16)


def _attention(qa, ka, va, bias3, sink, b, s):
    tq = ATTN_QUERIES
    per = tq // BLOCK
    nb = s // BLOCK
    nsteps = s // tq
    assert s % tq == 0 and nb >= 2
    n = b * s
    main = lambda wd: pl.BlockSpec((tq, wd), lambda bi, ni: (bi * nsteps + ni, 0))
    prev = pl.BlockSpec((BLOCK, KV_WIDTH), lambda bi, ni: (bi * nb + jnp.maximum(ni * per - 1, 0), 0))
    nxt = pl.BlockSpec((BLOCK, KV_WIDTH), lambda bi, ni: (bi * nb + jnp.minimum(ni * per + per, nb - 1), 0))
    return pl.pallas_call(
        functools.partial(_attn_kernel, nsteps=nsteps),
        grid=(b, nsteps),
        in_specs=[pl.BlockSpec(memory_space=pltpu.SMEM), main(ATTN_WIDTH),
                  prev, main(KV_WIDTH), nxt, prev, main(KV_WIDTH), nxt,
                  pl.BlockSpec(bias3.shape, lambda bi, ni: (0, 0, 0))],
        out_specs=main(ATTN_WIDTH),
        out_shape=jax.ShapeDtypeStruct((n, ATTN_WIDTH), BF16),
        compiler_params=_cparams(2),
        name="attn",
    )(sink, qa, ka, ka, ka, va, va, va, bias3)


def _retn_kernel(dec_ref, q_ref, k_ref, v_ref, g_ref, dmask_ref, rowf_ref, rowb_ref, wkf_ref, wkb_ref, ng_ref,
                 o_ref, tstore, uf, tb, *, nc):
    p = pl.program_id(1)
    n = pl.program_id(2)
    tn = (((0,), (0,)), ((), ()))
    nt = (((1,), (1,)), ((), ()))
    hs = lambda h: slice(h * RET_DIM, (h + 1) * RET_DIM)

    @pl.when(p == 0)
    def _():
        @pl.when(n == 0)
        def _():
            tb[...] = jnp.zeros_like(tb)
        c = nc - 1 - n
        for h in range(RET_HEADS):
            tstore[c, h] = tb[h].astype(BF16)
            kw = (k_ref[:, hs(h)].astype(F32) * wkb_ref[h]).astype(BF16)
            upd = lax.dot_general(kw, v_ref[:, hs(h)], tn, preferred_element_type=F32)
            tb[h] = dec_ref[RET_HEADS + h] * tb[h] + upd

    @pl.when(p == 1)
    def _():
        @pl.when(n == 0)
        def _():
            uf[...] = jnp.zeros_like(uf)
        for h in range(RET_HEADS):
            qh = q_ref[:, hs(h)]
            kh = k_ref[:, hs(h)]
            vh = v_ref[:, hs(h)]
            sc = lax.dot_general(qh, kh, nt, preferred_element_type=F32) * dmask_ref[h]
            intra = jnp.dot(sc.astype(BF16), vh, preferred_element_type=F32)
            states = jnp.concatenate([uf[h].astype(BF16), tstore[n, h]], axis=1)
            cross = jnp.dot(qh, states, preferred_element_type=F32)
            o = intra + cross[:, :RET_DIM] * rowf_ref[h] + cross[:, RET_DIM:] * rowb_ref[h]
            mu = jnp.mean(o, axis=-1, keepdims=True)
            d = o - mu
            var = jnp.mean(d * d, axis=-1, keepdims=True)
            on = d * lax.rsqrt(var + EPS) * ng_ref[h:h + 1, :]
            gate = g_ref[:, hs(h)].astype(F32)
            o_ref[:, hs(h)] = (gate * _sigmoid(gate) * on).astype(BF16)
            kw = (kh.astype(F32) * wkf_ref[h]).astype(BF16)
            uf[h] = dec_ref[h] * uf[h] + lax.dot_general(kw, vh, tn, preferred_element_type=F32)


def _retention(qr, kr, vr, gr, tables, b, s):
    dec, dmask, rowf, rowb, wkf, wkb, ng = tables
    cr = RET_CHUNK
    nc = s // cr
    n = b * s
    full = lambda a: pl.BlockSpec(a.shape, lambda bi, pi, ni: (0,) * a.ndim)
    fwd_spec = pl.BlockSpec((cr, RET_WIDTH), lambda bi, pi, ni: (bi * nc + ni * pi, 0))
    kv_spec = pl.BlockSpec((cr, RET_WIDTH), lambda bi, pi, ni: (bi * nc + ni * pi + (1 - pi) * (nc - 1 - ni), 0))
    return pl.pallas_call(
        functools.partial(_retn_kernel, nc=nc),
        grid=(b, 2, nc),
        in_specs=[pl.BlockSpec(memory_space=pltpu.SMEM), fwd_spec, kv_spec, kv_spec, fwd_spec,
                  full(dmask), full(rowf), full(rowb), full(wkf), full(wkb), full(ng)],
        out_specs=fwd_spec,
        out_shape=jax.ShapeDtypeStruct((n, RET_WIDTH), BF16),
        scratch_shapes=[pltpu.VMEM((nc, RET_HEADS, RET_DIM, RET_DIM), BF16),
                        pltpu.VMEM((RET_HEADS, RET_DIM, RET_DIM), F32),
                        pltpu.VMEM((RET_HEADS, RET_DIM, RET_DIM), F32)],
        compiler_params=_cparams(3),
        name="retention",
    )(dec, qr, kr, vr, gr, dmask, rowf, rowb, wkf, wkb, ng)


def _merge_kernel(attn_ref, retn_ref, ga_ref, gr_ref, x_ref, wba_ref, wbr_ref, wo_ref, g2_ref, wr_ref,
                  xmid_ref, afft_ref, *h2_refs):
    a = jnp.dot(attn_ref[...], wba_ref[...], preferred_element_type=F32)
    r = jnp.dot(retn_ref[...], wbr_ref[...], preferred_element_type=F32)
    merged = _sigmoid(ga_ref[...].astype(F32)) * a + _sigmoid(gr_ref[...].astype(F32)) * r
    xn = x_ref[...] + jnp.dot(merged.astype(BF16), wo_ref[...], preferred_element_type=F32)
    xmid_ref[...] = xn
    ms = jnp.mean(xn * xn, axis=-1, keepdims=True)
    h2 = xn * lax.rsqrt(ms + EPS) * g2_ref[...]
    for ref, words in zip(h2_refs, _pack_rows(h2)):
        ref[...] = words
    logits = jnp.dot(h2.astype(BF16), wr_ref[...], preferred_element_type=F32)
    real = lax.broadcasted_iota(jnp.int32, logits.shape, 1) < N_EXPERTS
    logits = jnp.where(real, logits, -jnp.inf)
    m = jnp.max(logits, axis=-1, keepdims=True)
    ex = jnp.exp(logits - m)
    aff = ex / jnp.sum(ex, axis=-1, keepdims=True)
    afft_ref[...] = aff.T[:N_EXPERTS, :]


def _merge(attn, retn, ga, gr, x2, wba, wbr, wo, g2, wr):
    n = x2.shape[0]
    tm = TOKEN_TILE
    full = lambda a: pl.BlockSpec(a.shape, lambda i: (0,) * a.ndim)
    row = lambda wd: pl.BlockSpec((tm, wd), lambda i: (i, 0))
    return pl.pallas_call(
        _merge_kernel,
        grid=(n // tm,),
        in_specs=[row(ATTN_WIDTH), row(RET_WIDTH), row(D_MODEL), row(D_MODEL), row(D_MODEL),
                  full(wba), full(wbr), full(wo), full(g2), full(wr)],
        out_specs=[row(D_MODEL), pl.BlockSpec((N_EXPERTS, tm), lambda i: (0, i))] + [row(SC_ROW)] * SC_PIECES,
        out_shape=[jax.ShapeDtypeStruct((n, D_MODEL), F32), jax.ShapeDtypeStruct((N_EXPERTS, n), F32)]
        + [jax.ShapeDtypeStruct((n, SC_ROW), jnp.int32)] * SC_PIECES,
        compiler_params=_cparams(1),
        name="merge",
    )(attn, retn, ga, gr, x2, wba, wbr, wo, g2, wr)


def _split3(x):
    p0 = x.astype(BF16)
    r0 = x - p0.astype(F32)
    p1 = r0.astype(BF16)
    p2 = (r0 - p1.astype(F32)).astype(BF16)
    return p0, p1, p2


def _select_kernel(aff_ref, u_ref, ls_ref, idx_ref, gate_ref, dst_ref, cs_ref, ce_ref,
                   selbuf, cnt, csr, rank, pieces, offi, *, cap, tb):
    ps = pl.program_id(0)
    e = pl.program_id(1)
    j = pl.program_id(2)
    nblk = SELECT_BLOCKS
    pc = SELECT_SLOTS

    def cumsum(vals):
        inb = jnp.dot(vals.astype(BF16), u_ref[...], preferred_element_type=F32)
        tot = jnp.broadcast_to(inb[:, tb - 1:tb], (nblk, LANES))
        off = jnp.dot(ls_ref[...], tot, preferred_element_type=F32, precision=lax.Precision.HIGHEST)
        return inb, off[:, 0:1], tot[:, 0:1]

    @pl.when(jnp.logical_and(ps == 0, j == 0))
    def _():
        a = aff_ref[e]
        bits = pltpu.bitcast(a, jnp.int32)

        def bit_step(t, cur):
            cand = cur | jnp.left_shift(jnp.int32(1), 30 - t)
            n_ge = jnp.sum((bits >= cand).astype(jnp.int32), keepdims=True)
            return jnp.where(n_ge >= cap, cand, cur)

        thr = lax.fori_loop(0, 31, bit_step, jnp.zeros((1, 1), jnp.int32))
        gt = bits > thr
        eq = bits == thr
        need = (cap - jnp.sum(gt.astype(jnp.int32), keepdims=True)).astype(F32)
        eqf = eq.astype(F32)
        eq_in, eq_off, _ = cumsum(eqf)
        eq_rank = eq_in + eq_off - eqf
        sel = jnp.logical_or(gt, jnp.logical_and(eq, eq_rank < need)).astype(F32)
        selbuf[e] = sel.astype(BF16)

        @pl.when(e == 0)
        def _():
            cnt[...] = sel

        @pl.when(e > 0)
        def _():
            cnt[...] = cnt[...] + sel

    @pl.when(jnp.logical_and(ps == 1, j == 0))
    def _():
        @pl.when(e == 0)
        def _():
            c = cnt[...]
            c_in, c_off, _ = cumsum(c)
            start = c_in + c_off - c
            csr[...] = start
            cs_ref[...] = start.astype(jnp.int32)
            ce_ref[...] = (start + c).astype(jnp.int32)
            rank[...] = jnp.zeros_like(rank)

        sel = selbuf[e].astype(F32)
        s_in, s_off, s_tot = cumsum(sel)
        for k, piece in enumerate(_split3((s_in + s_off).T) + _split3(aff_ref[e].T) + _split3((csr[...] + rank[...]).T)):
            pieces[k] = piece
        rank[...] = rank[...] + sel
        offi[...] = jnp.broadcast_to(s_off + s_tot, (nblk, LANES))

    @pl.when(ps == 1)
    def _():
        slot = (j * pc + lax.broadcasted_iota(jnp.int32, (1, pc), 1)).astype(F32)
        blk = jnp.sum((offi[:, 0:1] <= slot).astype(jnp.int32), axis=0, keepdims=True)
        onehot = (lax.broadcasted_iota(jnp.int32, (nblk, pc), 0) == blk).astype(BF16)

        def pick_block(k):
            return sum(jnp.dot(pieces[k + q], onehot, preferred_element_type=F32) for q in range(3))

        inb = jnp.sum((pick_block(0) <= slot + 0.5).astype(jnp.int32), axis=0, keepdims=True)
        hit = lax.broadcasted_iota(jnp.int32, (tb, pc), 0) == inb
        idx_ref[0] = blk * tb + inb
        gate_ref[0] = jnp.sum(jnp.where(hit, pick_block(3), 0.0), axis=0, keepdims=True)
        dst_ref[0] = jnp.sum(jnp.where(hit, pick_block(6), 0.0), axis=0, keepdims=True).astype(jnp.int32)


def _select(afft, cap):
    n = afft.shape[1]
    nblk = SELECT_BLOCKS
    tb = n // nblk
    pc = SELECT_SLOTS
    assert n % nblk == 0 and tb % LANES == 0 and cap % pc == 0
    nch = cap // pc
    aff3 = afft.reshape(N_EXPERTS, nblk, tb)
    upper = jnp.asarray(np.triu(np.ones((tb, tb), np.float32)), BF16)
    lstrict = jnp.asarray(np.tril(np.ones((nblk, nblk), np.float32), -1))
    full = lambda a: pl.BlockSpec(a.shape, lambda ps, e, j: (0,) * a.ndim)
    slot_spec = pl.BlockSpec((1, 1, pc), lambda ps, e, j: (ps * (e * nch + j), 0, 0))
    tok_spec = pl.BlockSpec((nblk, tb), lambda ps, e, j: (0, 0))
    slots = lambda dt: jax.ShapeDtypeStruct((N_EXPERTS * nch, 1, pc), dt)
    idx, gate, dst, cs, ce = pl.pallas_call(
        functools.partial(_select_kernel, cap=cap, tb=tb),
        grid=(2, N_EXPERTS, nch),
        in_specs=[full(aff3), full(upper), full(lstrict)],
        out_specs=[slot_spec, slot_spec, slot_spec, tok_spec, tok_spec],
        out_shape=[slots(jnp.int32), slots(F32), slots(jnp.int32),
                   jax.ShapeDtypeStruct((nblk, tb), jnp.int32), jax.ShapeDtypeStruct((nblk, tb), jnp.int32)],
        scratch_shapes=[pltpu.VMEM((N_EXPERTS, nblk, tb), BF16), pltpu.VMEM((nblk, tb), F32),
                        pltpu.VMEM((nblk, tb), F32), pltpu.VMEM((nblk, tb), F32),
                        pltpu.VMEM((9, tb, nblk), BF16), pltpu.VMEM((nblk, LANES), F32)],
        compiler_params=_cparams(3),
        name="select",
    )(aff3, upper, lstrict)
    return idx.reshape(-1), gate.reshape(N_EXPERTS * nch, 1, pc), dst.reshape(-1), cs.reshape(-1), ce.reshape(-1)


def _sc_mesh():
    return plsc.VectorSubcoreMesh(core_axis_name="c", subcore_axis_name="s")


def _sc_scatter(rows, idx, m_out):
    m, d = rows.shape
    assert m % SC_WINDOW == 0

    @functools.partial(pl.kernel, out_type=jax.ShapeDtypeStruct((m_out, d), rows.dtype), mesh=_sc_mesh(),
                       name="sc_scatter")
    def scatter(x_hbm, i_hbm, o_hbm):
        def body(x_vmem, i_vmem):
            pltpu.sync_copy(x_vmem, o_hbm.at[i_vmem.at[0]])

        pltpu.emit_pipeline(
            body,
            grid=(m // SC_WINDOW,),
            in_specs=[pl.BlockSpec((SC_WINDOW, d), lambda i: (i, 0)),
                      pl.BlockSpec((1, SC_WINDOW), lambda i: (0, i))],
            out_specs=[],
            core_axis_name=("c", "s"),
            dimension_semantics=(pltpu.PARALLEL,),
        )(x_hbm, i_hbm)

    return scatter(rows, idx.reshape(1, m))


def _sc_gather(table, idx):
    m = idx.shape[0]
    d = table.shape[1]
    assert m % SC_WINDOW == 0

    @functools.partial(pl.kernel, out_type=jax.ShapeDtypeStruct((m, d), table.dtype), mesh=_sc_mesh(),
                       name="sc_gather")
    def gather(x_hbm, i_hbm, o_hbm):
        def body(i_vmem, o_vmem):
            pltpu.sync_copy(x_hbm.at[i_vmem.at[0]], o_vmem)

        pltpu.emit_pipeline(
            body,
            grid=(m // SC_WINDOW,),
            in_specs=[pl.BlockSpec((1, SC_WINDOW), lambda i: (0, i))],
            out_specs=[pl.BlockSpec((SC_WINDOW, d), lambda i: (i, 0))],
            core_axis_name=("c", "s"),
            dimension_semantics=(pltpu.PARALLEL,),
        )(i_hbm, o_hbm)

    return gather(table, idx.reshape(1, m))


def _row_to_col(row):
    n = row.shape[1]
    eye = lax.broadcasted_iota(jnp.int32, (n, n), 0) == lax.broadcasted_iota(jnp.int32, (n, n), 1)
    return jnp.sum(jnp.where(eye, row, jnp.zeros_like(row)), axis=1, keepdims=True)


def _ffn_kernel(gate_ref, x0_ref, x1_ref, w1_hbm, w3_hbm, w2_hbm, o0_ref, o1_ref, wstage, w1b, w3b, w2b, wsem,
                *, layer):
    e = pl.program_id(0)
    i = pl.program_id(1)

    def weight_copies(expert):
        return [pltpu.make_async_copy(w_hbm.at[layer, expert], wstage.at[k], wsem.at[k])
                for k, w_hbm in enumerate((w1_hbm, w3_hbm, w2_hbm))]

    @pl.when(i == 0)
    def _():
        @pl.when(e == 0)
        def _():
            for cp in weight_copies(0):
                cp.start()
        for cp in weight_copies(e):
            cp.wait()
        w1b[...] = wstage[0].astype(BF16)
        w3b[...] = wstage[1].astype(BF16)
        w2b[...] = wstage[2].astype(BF16)

    @pl.when(jnp.logical_and(i == 1, e + 1 < N_EXPERTS))
    def _():
        for cp in weight_copies(e + 1):
            cp.start()

    xs = _unpack_rows([x0_ref[...], x1_ref[...]])
    hg = jnp.dot(xs, w1b[...], preferred_element_type=F32)
    hu = jnp.dot(xs, w3b[...], preferred_element_type=F32)
    hid = (hg * _sigmoid(hg) * hu).astype(BF16)
    out = jnp.dot(hid, w2b[...], preferred_element_type=F32) * _row_to_col(gate_ref[0])
    for ref, words in zip((o0_ref, o1_ref), _pack_rows(out)):
        ref[...] = words


def _expert_ffn(xs, gate3, w1, w3, w2, layer):
    m = xs[0].shape[0]
    rows = FFN_ROWS
    nt = m // (N_EXPERTS * rows)
    assert SC_PIECES == 2 and nt >= 2
    piece = pl.BlockSpec((rows, SC_ROW), lambda e, i: (e * nt + i, 0))
    per_step = gate3.shape[2] // rows
    gspec = pl.BlockSpec((1, 1, rows), lambda e, i: ((e * nt + i) // per_step, 0, (e * nt + i) % per_step))
    any_spec = pl.BlockSpec(memory_space=pl.ANY)
    return pl.pallas_call(
        functools.partial(_ffn_kernel, layer=layer),
        grid=(N_EXPERTS, nt),
        in_specs=[gspec] + [piece] * SC_PIECES + [any_spec] * 3,
        out_specs=[piece] * SC_PIECES,
        out_shape=[jax.ShapeDtypeStruct((m, SC_ROW), jnp.int32)] * SC_PIECES,
        scratch_shapes=[pltpu.VMEM((3, D_MODEL, EXPERT_FF), F32), pltpu.VMEM((D_MODEL, EXPERT_FF), BF16),
                        pltpu.VMEM((D_MODEL, EXPERT_FF), BF16), pltpu.VMEM((EXPERT_FF, D_MODEL), BF16),
                        pltpu.SemaphoreType.DMA((3,))],
        compiler_params=_cparams(2),
        name="expert_ffn",
    )(gate3, *xs, w1, w3, w2)


def _combine_kernel(tstart_ref, x_ref, cs_ref, ce_ref, r0_hbm, r1_hbm, o_ref, rbuf, obuf, sems, osem, *, ntile, total):
    pieces_hbm = (r0_hbm, r1_hbm)
    win = COMBINE_WINDOW
    i = pl.program_id(0)
    slot = lax.rem(i, 2)

    def window_start(t):
        return pl.multiple_of((tstart_ref[t] // 8) * 8, 8)

    def copies(t, b):
        s = window_start(t)
        return [pltpu.make_async_copy(pieces_hbm[c].at[pl.ds(s, win)], rbuf.at[b, c], sems.at[b, c])
                for c in range(SC_PIECES)]

    @pl.when(i == 0)
    def _():
        for cp in copies(0, 0):
            cp.start()

    @pl.when(i + 1 < ntile)
    def _():
        for cp in copies(i + 1, 1 - slot):
            cp.start()

    for cp in copies(i, slot):
        cp.wait()

    first = _row_to_col(cs_ref[0])
    last = _row_to_col(ce_ref[0])

    def window_sum(read_piece, base):
        r = base + lax.broadcasted_iota(jnp.int32, (1, win), 1)
        q = jnp.logical_and(first <= r, r < last).astype(BF16)

        def summed(clean):
            rows = _unpack_rows([clean(read_piece(c)) for c in range(SC_PIECES)])
            return jnp.dot(q, rows, preferred_element_type=F32)

        def zero_unwritten(words):
            written = (base + lax.broadcasted_iota(jnp.int32, (win, 1), 0)) < total
            return jnp.where(written, words, 0)

        return lax.cond(base + win > total, lambda: summed(zero_unwritten), lambda: summed(lambda words: words))

    s0 = window_start(i)
    y = x_ref[...] + window_sum(lambda c: rbuf[slot, c], s0)

    n_extra = jnp.maximum(tstart_ref[i + 1] - (s0 + win) + win - 1, 0) // win

    def extra(k, acc):
        base = pl.multiple_of(s0 + (k + 1) * win, 8)
        cps = [pltpu.make_async_copy(pieces_hbm[c].at[pl.ds(base, win)], obuf.at[c], osem.at[c])
               for c in range(SC_PIECES)]
        for cp in cps:
            cp.start()
        for cp in cps:
            cp.wait()
        return acc + window_sum(lambda c: obuf[c], base)

    o_ref[...] = lax.fori_loop(0, n_extra, extra, y)


def _combine(xmid, cs, ce, pieces, total):
    n = xmid.shape[0]
    tt = COMBINE_TOKENS
    win = COMBINE_WINDOW
    ntile = n // tt
    tstart = jnp.concatenate([cs[::tt], jnp.full((1,), total, jnp.int32)])
    cs3 = cs.reshape(ntile, 1, tt)
    ce3 = ce.reshape(ntile, 1, tt)
    any_spec = pl.BlockSpec(memory_space=pl.ANY)
    tok = pl.BlockSpec((1, 1, tt), lambda i, ts: (i, 0, 0))
    grid_spec = pltpu.PrefetchScalarGridSpec(
        num_scalar_prefetch=1,
        grid=(ntile,),
        in_specs=[pl.BlockSpec((tt, D_MODEL), lambda i, ts: (i, 0)), tok, tok] + [any_spec] * SC_PIECES,
        out_specs=pl.BlockSpec((tt, D_MODEL), lambda i, ts: (i, 0)),
        scratch_shapes=[pltpu.VMEM((2, SC_PIECES, win, SC_ROW), jnp.int32), pltpu.VMEM((SC_PIECES, win, SC_ROW), jnp.int32),
                        pltpu.SemaphoreType.DMA((2, SC_PIECES)), pltpu.SemaphoreType.DMA((SC_PIECES,))],
    )
    return pl.pallas_call(
        functools.partial(_combine_kernel, ntile=ntile, total=total),
        grid_spec=grid_spec,
        out_shape=jax.ShapeDtypeStruct((n, D_MODEL), F32),
        compiler_params=_cparams(1),
        name="combine",
    )(tstart, xmid, cs3, ce3, *pieces)


def _t5_bucket(rel):
    half = REL_BUCKETS // 2
    max_exact = half // 2
    base = np.where(rel > 0, half, 0)
    n = np.abs(rel)
    large = max_exact + (np.log(np.maximum(n, 1) / max_exact) / math.log(REL_MAX_DIST / max_exact)
                         * (half - max_exact)).astype(np.int32)
    large = np.minimum(large, half - 1)
    return (base + np.where(n < max_exact, n, large)).astype(np.int32)


def _head_perm():
    nq = ATTN_HEADS // 2
    cols = []
    for j in range(nq):
        for half in range(2):
            h = j + nq * half
            cols.extend(range(h * ATTN_HEAD_DIM, (h + 1) * ATTN_HEAD_DIM))
    return np.asarray(cols, np.int32)


def _attn_bias_tables(rel_bias):
    q_pos = np.arange(BLOCK)[:, None]
    k_off = np.arange(3 * BLOCK)[None, :] - BLOCK
    rel = k_off - q_pos
    in_window = np.abs(rel) <= WINDOW
    onehot = jnp.asarray(_t5_bucket(rel)[:, :, None] == np.arange(REL_BUCKETS)[None, None, :], F32)
    bias = jnp.einsum("qkb,bh->hqk", onehot, rel_bias.astype(F32), precision=lax.Precision.HIGHEST)
    col = np.arange(3 * BLOCK)[None, :]
    tables = []
    for valid in (col >= BLOCK, np.ones_like(col, bool), col < 2 * BLOCK):
        t = jnp.where(jnp.asarray(in_window & valid)[None], bias, NEG)
        nq = ATTN_HEADS // 2
        rows = [jnp.concatenate([t[j], t[j + nq]], axis=1) for j in range(nq)]
        tables.append(jnp.concatenate(rows, axis=0))
    return jnp.stack(tables)


def _retention_tables(decay_logit, norm_g):
    cr = RET_CHUNK
    lg = jax.nn.log_sigmoid(decay_logit.astype(F32))
    lgf, lgb = lg[0][:, None, None], lg[1][:, None, None]
    pos = np.arange(cr, dtype=np.float32)
    dist = pos[:, None] - pos[None, :]
    scale = RET_DIM ** -0.5
    dmask = jnp.where(jnp.asarray(dist >= 0)[None],
                      jnp.exp(lgf * np.maximum(dist, 0.0)[None]),
                      jnp.exp(lgb * np.maximum(-dist, 0.0)[None])) * scale
    col = lambda v: jnp.broadcast_to(v[:, :, None], (RET_HEADS, cr, RET_DIM))
    rowf = col(jnp.exp(lg[0][:, None] * pos[None]))
    rowb = col(jnp.exp(lg[1][:, None] * (cr - 1.0 - pos)[None]))
    wkf = col(jnp.exp(lg[0][:, None] * (cr - pos)[None]) * scale)
    wkb = col(jnp.exp(lg[1][:, None] * (pos + 1.0)[None]) * scale)
    dec = jnp.concatenate([jnp.exp(lg[0] * cr), jnp.exp(lg[1] * cr)])
    return dec, dmask, rowf, rowb, wkf, wkb, norm_g.astype(F32)


def _layer(x2, b, s, p):
    qa, ka, va, qr, kr, vr, gr, ga, gt = _in_proj(x2, p["g1"], p["w_in"], p["qg"], p["kg"], p["bdq"], p["bdk"])
    attn = _attention(qa, ka, va, p["bias3"], p["sink"], b, s)
    retn = _retention(qr, kr, vr, gr, p["retn"], b, s)
    xmid, afft, *h2 = _merge(attn, retn, ga, gt, x2, p["wba"], p["wbr"], p["wo"], p["g2"], p["wr"])
    n = b * s
    cap = max(1, EC_CAPACITY_FACTOR * n // N_EXPERTS)
    total = N_EXPERTS * cap
    idx, gate3, dst, cs, ce = _select(afft, cap)
    xs = [_sc_gather(piece, idx) for piece in h2]
    outs = _expert_ffn(xs, gate3, p["w1"], p["w3"], p["w2"], p["layer"])
    by_token = [_sc_scatter(o, dst, total + COMBINE_WINDOW) for o in outs]
    return _combine(xmid, cs, ce, by_token, total)


def kernel(x_prompt, x_sample, norm_mix_g, w_in, q_norm_g, k_norm_g, attn_sink, rel_bias, retn_decay_logit, retn_norm_g, w_branch_attn, w_branch_retn, w_out, norm_ffn_g, w_router, w_exp_gate, w_exp_up, w_exp_down):
    depth = w_in.shape[0]
    perm = _head_perm()
    bias3 = _attn_bias_tables(rel_bias)
    bdq = jnp.asarray(np.kron(np.eye(ATTN_HEADS), np.ones((ATTN_HEAD_DIM, ATTN_HEAD_DIM))), BF16)
    bdk = jnp.asarray(np.kron(np.eye(ATTN_KV_HEADS), np.ones((ATTN_HEAD_DIM, ATTN_HEAD_DIM))), BF16)
    layers = []
    for l in range(depth):
        w = w_in[l]
        w = jnp.concatenate([w[:, :ATTN_WIDTH][:, perm], w[:, ATTN_WIDTH:]], axis=1).astype(BF16)
        wr = jnp.pad(w_router[l], ((0, 0), (0, LANES - N_EXPERTS))).astype(BF16)
        layers.append(dict(
            g1=norm_mix_g[l].astype(F32)[None], w_in=w,
            qg=(jnp.tile(q_norm_g[l].astype(F32), ATTN_HEADS) * (ATTN_HEAD_DIM ** -0.5))[None],
            kg=jnp.tile(k_norm_g[l].astype(F32), ATTN_KV_HEADS)[None],
            bdq=bdq, bdk=bdk, bias3=bias3, sink=attn_sink[l].astype(F32),
            retn=_retention_tables(retn_decay_logit[l], retn_norm_g[l]),
            wba=w_branch_attn[l][perm, :].astype(BF16), wbr=w_branch_retn[l].astype(BF16),
            wo=w_out[l].astype(BF16), g2=norm_ffn_g[l].astype(F32)[None], wr=wr,
            w1=w_exp_gate, w3=w_exp_up, w2=w_exp_down, layer=l))

    def trunk(x):
        b, s, d = x.shape
        x2 = x.reshape(b * s, d)
        for p in layers:
            x2 = _layer(x2, b, s, p)
        return x2.reshape(b, s, d)

    return (trunk(x_prompt), trunk(x_sample))
```

```python
import functools
import math

import numpy as np
import jax
import jax.numpy as jnp
from jax import lax
from jax.experimental import pallas as pl
from jax.experimental.pallas import tpu as pltpu
from jax.experimental.pallas import tpu_sc as plsc

D_MODEL = 1024
ATTN_HEADS = 8
ATTN_KV_HEADS = 2
ATTN_HEAD_DIM = 64
WINDOW = 128
BLOCK = 128
REL_BUCKETS = 32
REL_MAX_DIST = 128
RET_HEADS = 4
RET_DIM = 128
N_EXPERTS = 16
EC_CAPACITY_FACTOR = 2
EXPERT_FF = 1024
EPS = 1e-6

ATTN_WIDTH = ATTN_HEADS * ATTN_HEAD_DIM
KV_WIDTH = ATTN_KV_HEADS * ATTN_HEAD_DIM
RET_WIDTH = RET_HEADS * RET_DIM
IN_SPLITS = (ATTN_WIDTH, KV_WIDTH, KV_WIDTH, RET_WIDTH, RET_WIDTH, RET_WIDTH, RET_WIDTH, D_MODEL, D_MODEL)
IN_OFFSETS = tuple(int(o) for o in np.cumsum((0,) + IN_SPLITS))

LANES = 128
VMEM_LIMIT_BYTES = 56 * 1024 * 1024

TOKEN_TILE = 512
IN_PROJ_TILE = 1024
ATTN_QUERIES = 512
RET_CHUNK = 256
RET_STEP = 1024
FFN_ROWS = 1024
SELECT_BLOCKS = 128
SELECT_SLOTS = 1024
SC_WINDOW = 128
SC_ROW = 256
PACKED_WIDTH = D_MODEL // 2
SC_PIECES = PACKED_WIDTH // SC_ROW
COMBINE_TOKENS = 512
COMBINE_WINDOW = 1280

F32 = jnp.float32
BF16 = jnp.bfloat16
NEG = -1e30


def _cparams(n_axes, vmem=VMEM_LIMIT_BYTES):
    return pltpu.CompilerParams(dimension_semantics=("arbitrary",) * n_axes, vmem_limit_bytes=vmem)


def _sigmoid(x):
    return 1.0 / (1.0 + jnp.exp(-x))


HIGH_HALF = -65536


def _pack_rows(x):
    bits = pltpu.bitcast(x.astype(BF16).astype(F32), jnp.int32)
    words = lax.shift_right_logical(bits[:, :PACKED_WIDTH], 16) | (bits[:, PACKED_WIDTH:] & HIGH_HALF)
    return [words[:, c * SC_ROW:(c + 1) * SC_ROW] for c in range(SC_PIECES)]


def _unpack_rows(pieces):
    low = [pltpu.bitcast(lax.shift_left(w, 16), F32) for w in pieces]
    high = [pltpu.bitcast(w & HIGH_HALF, F32) for w in pieces]
    return jnp.concatenate(low + high, axis=1).astype(BF16)


def _in_proj_kernel(x_ref, g_ref, w_ref, qg_ref, kg_ref, bdq_ref, bdk_ref,
                    qa_ref, ka_ref, va_ref, qr_ref, kr_ref, vr_ref, gr_ref, ga_ref, gt_ref):
    x = x_ref[...]
    ms = jnp.mean(x * x, axis=-1, keepdims=True)
    h = (x * lax.rsqrt(ms + EPS) * g_ref[...]).astype(BF16)

    def mm(k):
        return jnp.dot(h, w_ref[:, IN_OFFSETS[k]:IN_OFFSETS[k + 1]], preferred_element_type=F32)

    def head_norm(t, bd_ref, gain_ref):
        ss = jnp.dot((t * t).astype(BF16), bd_ref[...], preferred_element_type=F32)
        return t * lax.rsqrt(ss * (1.0 / ATTN_HEAD_DIM) + EPS) * gain_ref[...]

    qa_ref[...] = head_norm(mm(0), bdq_ref, qg_ref).astype(BF16)
    ka_ref[...] = head_norm(mm(1), bdk_ref, kg_ref).astype(BF16)
    for k, ref in ((2, va_ref), (3, qr_ref), (4, kr_ref), (5, vr_ref), (6, gr_ref), (7, ga_ref), (8, gt_ref)):
        ref[...] = mm(k).astype(BF16)


def _in_proj(x2, g, w, qg, kg, bdq, bdk):
    n = x2.shape[0]
    tm = IN_PROJ_TILE
    full = lambda a: pl.BlockSpec(a.shape, lambda i: (0,) * a.ndim, pipeline_mode=pl.Buffered(1))
    widths = IN_SPLITS
    return pl.pallas_call(
        _in_proj_kernel,
        grid=(n // tm,),
        in_specs=[pl.BlockSpec((tm, D_MODEL), lambda i: (i, 0)), full(g), full(w), full(qg), full(kg),
                  full(bdq), full(bdk)],
        out_specs=[pl.BlockSpec((tm, wd), lambda i: (i, 0)) for wd in widths],
        out_shape=[jax.ShapeDtypeStruct((n, wd), BF16) for wd in widths],
        compiler_params=_cparams(1),
        name="in_proj",
    )(x2, g, w, qg, kg, bdq, bdk)


def _attn_kernel(sink_ref, q_ref, kp_ref, kc_ref, kn_ref, vp_ref, vc_ref, vn_ref, bias_ref, o_ref, *, nsteps):
    nq = ATTN_HEADS // 2
    ni = pl.program_id(1)
    k = jnp.concatenate([kp_ref[...], kc_ref[...], kn_ref[...]], axis=0)
    v = jnp.concatenate([vp_ref[...], vc_ref[...], vn_ref[...]], axis=0)
    low = lax.broadcasted_iota(jnp.int32, k.shape, 1) < ATTN_HEAD_DIM
    zero = jnp.zeros_like(k)
    k_lo, k_hi = jnp.where(low, k, zero), jnp.where(low, zero, k)
    v_lo, v_hi = jnp.where(low, v, zero), jnp.where(low, zero, v)
    nk = 3 * BLOCK
    low_o = lax.broadcasted_iota(jnp.int32, (BLOCK, LANES), 1) < ATTN_HEAD_DIM
    nsub = ATTN_QUERIES // BLOCK
    for sb in range(nsub):
        rows = slice(sb * BLOCK, (sb + 1) * BLOCK)
        keys = slice(sb * BLOCK, sb * BLOCK + nk)
        q = q_ref[rows, :]
        qs = jnp.concatenate([q[:, j * LANES:(j + 1) * LANES] for j in range(nq)], axis=0)
        kbd = jnp.concatenate([k_lo[keys], k_hi[keys]], axis=0)
        vbd = jnp.concatenate([v_lo[keys], v_hi[keys]], axis=0)
        s = lax.dot_general(qs, kbd, (((1,), (1,)), ((), ())), preferred_element_type=F32)
        if sb == 0:
            table = jnp.where(ni == 0, 0, 1)
        elif sb == nsub - 1:
            table = jnp.where(ni == nsteps - 1, 2, 1)
        else:
            table = 1
        s = s + bias_ref[table]
        probs, invs = [], []
        for j in range(nq):
            row_p, row_inv = [], []
            for half in range(2):
                sj = s[j * BLOCK:(j + 1) * BLOCK, half * nk:(half + 1) * nk]
                sk = sink_ref[j + nq * half]
                m = jnp.maximum(jnp.max(sj, axis=-1, keepdims=True), sk)
                p = jnp.exp(sj - m)
                den = jnp.sum(p, axis=-1, keepdims=True) + jnp.exp(sk - m)
                row_p.append(p.astype(BF16))
                row_inv.append(1.0 / den)
            probs.append(jnp.concatenate(row_p, axis=1))
            invs.append(row_inv)
        pm = jnp.concatenate(probs, axis=0)
        o = jnp.dot(pm, vbd, preferred_element_type=F32)
        for j in range(nq):
            inv = jnp.where(low_o, invs[j][0], invs[j][1])
            o_ref[rows, j * LANES:(j + 1) * LANES] = (o[j * BLOCK:(j + 1) * BLOCK] * inv).astype(BF16)


def _attention(qa, ka, va, bias3, sink, b, s):
    tq = ATTN_QUERIES
    per = tq // BLOCK
    nb = s // BLOCK
    nsteps = s // tq
    assert s % tq == 0 and nb >= 2
    n = b * s
    main = lambda wd: pl.BlockSpec((tq, wd), lambda bi, ni: (bi * nsteps + ni, 0))
    prev = pl.BlockSpec((BLOCK, KV_WIDTH), lambda bi, ni: (bi * nb + jnp.maximum(ni * per - 1, 0), 0))
    nxt = pl.BlockSpec((BLOCK, KV_WIDTH), lambda bi, ni: (bi * nb + jnp.minimum(ni * per + per, nb - 1), 0))
    return pl.pallas_call(
        functools.partial(_attn_kernel, nsteps=nsteps),
        grid=(b, nsteps),
        in_specs=[pl.BlockSpec(memory_space=pltpu.SMEM), main(ATTN_WIDTH),
                  prev, main(KV_WIDTH), nxt, prev, main(KV_WIDTH), nxt,
                  pl.BlockSpec(bias3.shape, lambda bi, ni: (0, 0, 0))],
        out_specs=main(ATTN_WIDTH),
        out_shape=jax.ShapeDtypeStruct((n, ATTN_WIDTH), BF16),
        compiler_params=_cparams(2),
        name="attn",
    )(sink, qa, ka, ka, ka, va, va, va, bias3)


def _retn_kernel(dec_ref, q_ref, k_ref, v_ref, g_ref, dmask_ref, rowf_ref, rowb_ref, wkf_ref, wkb_ref, ng_ref,
                 o_ref, tstore, uf, tb, *, nsteps):
    p = pl.program_id(1)
    n = pl.program_id(2)
    cr = RET_CHUNK
    per = RET_STEP // cr
    tn = (((0,), (0,)), ((), ()))
    nt = (((1,), (1,)), ((), ()))
    hs = lambda h: slice(h * RET_DIM, (h + 1) * RET_DIM)

    @pl.when(p == 0)
    def _():
        @pl.when(n == 0)
        def _():
            tb[...] = jnp.zeros_like(tb)
        first_chunk = (nsteps - 1 - n) * per
        for sub in reversed(range(per)):
            rows = slice(sub * cr, (sub + 1) * cr)
            for h in range(RET_HEADS):
                tstore[first_chunk + sub, h] = tb[h].astype(BF16)
                kw = (k_ref[rows, hs(h)].astype(F32) * wkb_ref[h]).astype(BF16)
                upd = lax.dot_general(kw, v_ref[rows, hs(h)], tn, preferred_element_type=F32)
                tb[h] = dec_ref[RET_HEADS + h] * tb[h] + upd

    @pl.when(p == 1)
    def _():
        @pl.when(n == 0)
        def _():
            uf[...] = jnp.zeros_like(uf)
        for sub in range(per):
            rows = slice(sub * cr, (sub + 1) * cr)
            for h in range(RET_HEADS):
                qh = q_ref[rows, hs(h)]
                kh = k_ref[rows, hs(h)]
                vh = v_ref[rows, hs(h)]
                sc = lax.dot_general(qh, kh, nt, preferred_element_type=F32) * dmask_ref[h]
                intra = jnp.dot(sc.astype(BF16), vh, preferred_element_type=F32)
                states = jnp.concatenate([uf[h].astype(BF16), tstore[n * per + sub, h]], axis=1)
                cross = jnp.dot(qh, states, preferred_element_type=F32)
                o = intra + cross[:, :RET_DIM] * rowf_ref[h] + cross[:, RET_DIM:] * rowb_ref[h]
                mu = jnp.mean(o, axis=-1, keepdims=True)
                d = o - mu
                var = jnp.mean(d * d, axis=-1, keepdims=True)
                on = d * lax.rsqrt(var + EPS) * ng_ref[h:h + 1, :]
                gate = g_ref[rows, hs(h)].astype(F32)
                o_ref[rows, hs(h)] = (gate * _sigmoid(gate) * on).astype(BF16)
                kw = (kh.astype(F32) * wkf_ref[h]).astype(BF16)
                uf[h] = dec_ref[h] * uf[h] + lax.dot_general(kw, vh, tn, preferred_element_type=F32)


def _retention(qr, kr, vr, gr, tables, b, s):
    dec, dmask, rowf, rowb, wkf, wkb, ng = tables
    rs = RET_STEP
    nsteps = s // rs
    nc = s // RET_CHUNK
    assert s % rs == 0
    n = b * s
    full = lambda a: pl.BlockSpec(a.shape, lambda bi, pi, ni: (0,) * a.ndim)
    fwd_spec = pl.BlockSpec((rs, RET_WIDTH), lambda bi, pi, ni: (bi * nsteps + ni * pi, 0))
    kv_spec = pl.BlockSpec((rs, RET_WIDTH),
                           lambda bi, pi, ni: (bi * nsteps + ni * pi + (1 - pi) * (nsteps - 1 - ni), 0))
    return pl.pallas_call(
        functools.partial(_retn_kernel, nsteps=nsteps),
        grid=(b, 2, nsteps),
        in_specs=[pl.BlockSpec(memory_space=pltpu.SMEM), fwd_spec, kv_spec, kv_spec, fwd_spec,
                  full(dmask), full(rowf), full(rowb), full(wkf), full(wkb), full(ng)],
        out_specs=fwd_spec,
        out_shape=jax.ShapeDtypeStruct((n, RET_WIDTH), BF16),
        scratch_shapes=[pltpu.VMEM((nc, RET_HEADS, RET_DIM, RET_DIM), BF16),
                        pltpu.VMEM((RET_HEADS, RET_DIM, RET_DIM), F32),
                        pltpu.VMEM((RET_HEADS, RET_DIM, RET_DIM), F32)],
        compiler_params=_cparams(3),
        name="retention",
    )(dec, qr, kr, vr, gr, dmask, rowf, rowb, wkf, wkb, ng)


def _merge_kernel(attn_ref, retn_ref, ga_ref, gr_ref, x_ref, wba_ref, wbr_ref, wo_ref, g2_ref, wr_ref,
                  xmid_ref, afft_ref, *h2_refs):
    a = jnp.dot(attn_ref[...], wba_ref[...], preferred_element_type=F32)
    r = jnp.dot(retn_ref[...], wbr_ref[...], preferred_element_type=F32)
    merged = _sigmoid(ga_ref[...].astype(F32)) * a + _sigmoid(gr_ref[...].astype(F32)) * r
    xn = x_ref[...] + jnp.dot(merged.astype(BF16), wo_ref[...], preferred_element_type=F32)
    xmid_ref[...] = xn
    ms = jnp.mean(xn * xn, axis=-1, keepdims=True)
    h2 = xn * lax.rsqrt(ms + EPS) * g2_ref[...]
    for ref, words in zip(h2_refs, _pack_rows(h2)):
        ref[...] = words
    logits = jnp.dot(h2.astype(BF16), wr_ref[...], preferred_element_type=F32)
    real = lax.broadcasted_iota(jnp.int32, logits.shape, 1) < N_EXPERTS
    logits = jnp.where(real, logits, -jnp.inf)
    m = jnp.max(logits, axis=-1, keepdims=True)
    ex = jnp.exp(logits - m)
    aff = ex / jnp.sum(ex, axis=-1, keepdims=True)
    afft_ref[...] = aff.T[:N_EXPERTS, :]


def _merge(attn, retn, ga, gr, x2, wba, wbr, wo, g2, wr):
    n = x2.shape[0]
    tm = TOKEN_TILE
    full = lambda a: pl.BlockSpec(a.shape, lambda i: (0,) * a.ndim)
    row = lambda wd: pl.BlockSpec((tm, wd), lambda i: (i, 0))
    return pl.pallas_call(
        _merge_kernel,
        grid=(n // tm,),
        in_specs=[row(ATTN_WIDTH), row(RET_WIDTH), row(D_MODEL), row(D_MODEL), row(D_MODEL),
                  full(wba), full(wbr), full(wo), full(g2), full(wr)],
        out_specs=[row(D_MODEL), pl.BlockSpec((N_EXPERTS, tm), lambda i: (0, i))] + [row(SC_ROW)] * SC_PIECES,
        out_shape=[jax.ShapeDtypeStruct((n, D_MODEL), F32), jax.ShapeDtypeStruct((N_EXPERTS, n), F32)]
        + [jax.ShapeDtypeStruct((n, SC_ROW), jnp.int32)] * SC_PIECES,
        compiler_params=_cparams(1),
        name="merge",
    )(attn, retn, ga, gr, x2, wba, wbr, wo, g2, wr)


def _select_kernel(aff_ref, u_ref, ls_ref, idx_ref, slot_ref, cs_ref, ce_ref,
                   thr, selbuf, cnt, csr, rank, pieces, offi, *, cap, tb):
    ps = pl.program_id(0)
    e = pl.program_id(1)
    j = pl.program_id(2)
    nblk = SELECT_BLOCKS
    pc = SELECT_SLOTS

    def cumsum(vals):
        inb = jnp.dot(vals.astype(BF16), u_ref[...], preferred_element_type=F32)
        tot = jnp.broadcast_to(inb[:, tb - 1:tb], (nblk, LANES))
        off = jnp.dot(ls_ref[...], tot, preferred_element_type=F32, precision=lax.Precision.HIGHEST)
        return inb, off[:, 0:1], tot[:, 0:1]

    @pl.when(jnp.logical_and(ps == 0, jnp.logical_and(e == 0, j == 0)))
    def _():
        def bit_step(t, curs):
            bit = jnp.left_shift(jnp.int32(1), 30 - t)
            out = []
            for x in range(N_EXPERTS):
                cand = curs[x] | bit
                n_ge = jnp.sum((pltpu.bitcast(aff_ref[x], jnp.int32) >= cand).astype(jnp.int32), keepdims=True)
                out.append(jnp.where(n_ge >= cap, cand, curs[x]))
            return tuple(out)

        found = lax.fori_loop(0, 31, bit_step, tuple(jnp.zeros((1, 1), jnp.int32) for _ in range(N_EXPERTS)))
        for x in range(N_EXPERTS):
            thr[x] = jnp.broadcast_to(found[x], thr.shape[1:])

    @pl.when(jnp.logical_and(ps == 0, j == 0))
    def _():
        bits = pltpu.bitcast(aff_ref[e], jnp.int32)
        limit = thr[e][0:1, 0:1]
        gt = bits > limit
        eq = bits == limit
        need = (cap - jnp.sum(gt.astype(jnp.int32), keepdims=True)).astype(F32)
        eqf = eq.astype(F32)
        eq_in, eq_off, _ = cumsum(eqf)
        eq_rank = eq_in + eq_off - eqf
        sel = jnp.logical_or(gt, jnp.logical_and(eq, eq_rank < need)).astype(F32)
        selbuf[e] = sel.astype(BF16)

        @pl.when(e == 0)
        def _():
            cnt[...] = sel

        @pl.when(e > 0)
        def _():
            cnt[...] = cnt[...] + sel

    @pl.when(jnp.logical_and(ps == 1, j == 0))
    def _():
        @pl.when(e == 0)
        def _():
            c = cnt[...]
            c_in, c_off, _ = cumsum(c)
            start = c_in + c_off - c
            csr[...] = start
            cs_ref[...] = start.astype(jnp.int32)
            ce_ref[...] = (start + c).astype(jnp.int32)
            rank[...] = jnp.zeros_like(rank)

        sel = selbuf[e].astype(F32)
        s_in, s_off, s_tot = cumsum(sel)
        count_t = (s_in + s_off).T
        high = jnp.floor(count_t * (1.0 / 256.0))
        pieces[0] = high.astype(BF16)
        pieces[1] = (count_t - 256.0 * high).astype(BF16)
        slot_ref[0] = (csr[...] + rank[...]).astype(jnp.int32)
        rank[...] = rank[...] + sel
        offi[...] = jnp.broadcast_to(s_off + s_tot, (nblk, LANES))

    @pl.when(ps == 1)
    def _():
        slot = (j * pc + lax.broadcasted_iota(jnp.int32, (1, pc), 1)).astype(F32)
        blk = jnp.sum((offi[:, 0:1] <= slot).astype(jnp.int32), axis=0, keepdims=True)
        onehot = (lax.broadcasted_iota(jnp.int32, (nblk, pc), 0) == blk).astype(BF16)
        counts = (256.0 * jnp.dot(pieces[0], onehot, preferred_element_type=F32)
                  + jnp.dot(pieces[1], onehot, preferred_element_type=F32))
        inb = jnp.sum((counts <= slot + 0.5).astype(jnp.int32), axis=0, keepdims=True)
        idx_ref[0] = blk * tb + inb


def _select(afft, cap):
    n = afft.shape[1]
    nblk = SELECT_BLOCKS
    tb = n // nblk
    pc = SELECT_SLOTS
    assert n % nblk == 0 and tb % LANES == 0 and cap % pc == 0 and cap < 65536
    nch = cap // pc
    aff3 = afft.reshape(N_EXPERTS, nblk, tb)
    upper = jnp.asarray(np.triu(np.ones((tb, tb), np.float32)), BF16)
    lstrict = jnp.asarray(np.tril(np.ones((nblk, nblk), np.float32), -1))
    full = lambda a: pl.BlockSpec(a.shape, lambda ps, e, j: (0,) * a.ndim)
    idx_spec = pl.BlockSpec((1, 1, pc), lambda ps, e, j: (ps * (e * nch + j), 0, 0))
    slot_spec = pl.BlockSpec((1, nblk, tb), lambda ps, e, j: (ps * e, 0, 0))
    tok_spec = pl.BlockSpec((nblk, tb), lambda ps, e, j: (0, 0))
    idx, slots, cs, ce = pl.pallas_call(
        functools.partial(_select_kernel, cap=cap, tb=tb),
        grid=(2, N_EXPERTS, nch),
        in_specs=[full(aff3), full(upper), full(lstrict)],
        out_specs=[idx_spec, slot_spec, tok_spec, tok_spec],
        out_shape=[jax.ShapeDtypeStruct((N_EXPERTS * nch, 1, pc), jnp.int32),
                   jax.ShapeDtypeStruct((N_EXPERTS, nblk, tb), jnp.int32),
                   jax.ShapeDtypeStruct((nblk, tb), jnp.int32), jax.ShapeDtypeStruct((nblk, tb), jnp.int32)],
        scratch_shapes=[pltpu.VMEM((N_EXPERTS, 8, LANES), jnp.int32), pltpu.VMEM((N_EXPERTS, nblk, tb), BF16),
                        pltpu.VMEM((nblk, tb), F32), pltpu.VMEM((nblk, tb), F32), pltpu.VMEM((nblk, tb), F32),
                        pltpu.VMEM((2, tb, nblk), BF16), pltpu.VMEM((nblk, LANES), F32)],
        compiler_params=_cparams(3),
        name="select",
    )(aff3, upper, lstrict)
    return idx.reshape(-1), slots.reshape(N_EXPERTS, n), cs.reshape(-1), ce.reshape(-1)


def _slot_values(afft, slots, idx, cap):
    n = afft.shape[1]
    table = jnp.concatenate([lax.bitcast_convert_type(afft, jnp.int32).T, slots.T,
                             jnp.zeros((n, LANES - 2 * N_EXPERTS), jnp.int32)], axis=1)
    rows = _sc_gather(table, idx).reshape(N_EXPERTS, cap, LANES)
    own = jnp.eye(N_EXPERTS, dtype=jnp.int32)[:, None, :]
    gate = lax.bitcast_convert_type(jnp.sum(rows[:, :, :N_EXPERTS] * own, axis=-1), F32)
    dst = jnp.sum(rows[:, :, N_EXPERTS:2 * N_EXPERTS] * own, axis=-1)
    return gate, dst


def _sc_mesh():
    return plsc.VectorSubcoreMesh(core_axis_name="c", subcore_axis_name="s")


def _sc_scatter(rows, idx, m_out):
    m, d = rows.shape
    assert m % SC_WINDOW == 0

    @functools.partial(pl.kernel, out_type=jax.ShapeDtypeStruct((m_out, d), rows.dtype), mesh=_sc_mesh(),
                       name="sc_scatter")
    def scatter(x_hbm, i_hbm, o_hbm):
        def body(x_vmem, i_vmem):
            pltpu.sync_copy(x_vmem, o_hbm.at[i_vmem.at[0]])

        pltpu.emit_pipeline(
            body,
            grid=(m // SC_WINDOW,),
            in_specs=[pl.BlockSpec((SC_WINDOW, d), lambda i: (i, 0)),
                      pl.BlockSpec((1, SC_WINDOW), lambda i: (0, i))],
            out_specs=[],
            core_axis_name=("c", "s"),
            dimension_semantics=(pltpu.PARALLEL,),
        )(x_hbm, i_hbm)

    return scatter(rows, idx.reshape(1, m))


def _sc_gather(table, idx):
    m = idx.shape[0]
    d = table.shape[1]
    assert m % SC_WINDOW == 0

    @functools.partial(pl.kernel, out_type=jax.ShapeDtypeStruct((m, d), table.dtype), mesh=_sc_mesh(),
                       name="sc_gather")
    def gather(x_hbm, i_hbm, o_hbm):
        def body(i_vmem, o_vmem):
            pltpu.sync_copy(x_hbm.at[i_vmem.at[0]], o_vmem)

        pltpu.emit_pipeline(
            body,
            grid=(m // SC_WINDOW,),
            in_specs=[pl.BlockSpec((1, SC_WINDOW), lambda i: (0, i))],
            out_specs=[pl.BlockSpec((SC_WINDOW, d), lambda i: (i, 0))],
            core_axis_name=("c", "s"),
            dimension_semantics=(pltpu.PARALLEL,),
        )(i_hbm, o_hbm)

    return gather(table, idx.reshape(1, m))


def _row_to_col(row):
    n = row.shape[1]
    eye = lax.broadcasted_iota(jnp.int32, (n, n), 0) == lax.broadcasted_iota(jnp.int32, (n, n), 1)
    return jnp.sum(jnp.where(eye, row, jnp.zeros_like(row)), axis=1, keepdims=True)


def _ffn_kernel(gate_ref, x0_ref, x1_ref, w1_hbm, w3_hbm, w2_hbm, o0_ref, o1_ref, wstage, w1b, w3b, w2b, wsem,
                *, layer):
    e = pl.program_id(0)
    i = pl.program_id(1)

    def weight_copies(expert):
        return [pltpu.make_async_copy(w_hbm.at[layer, expert], wstage.at[k], wsem.at[k])
                for k, w_hbm in enumerate((w1_hbm, w3_hbm, w2_hbm))]

    @pl.when(i == 0)
    def _():
        @pl.when(e == 0)
        def _():
            for cp in weight_copies(0):
                cp.start()
        for cp in weight_copies(e):
            cp.wait()
        w1b[...] = wstage[0].astype(BF16)
        w3b[...] = wstage[1].astype(BF16)
        w2b[...] = wstage[2].astype(BF16)

    @pl.when(jnp.logical_and(i == 1, e + 1 < N_EXPERTS))
    def _():
        for cp in weight_copies(e + 1):
            cp.start()

    xs = _unpack_rows([x0_ref[...], x1_ref[...]])
    hg = jnp.dot(xs, w1b[...], preferred_element_type=F32)
    hu = jnp.dot(xs, w3b[...], preferred_element_type=F32)
    hid = (hg * _sigmoid(hg) * hu).astype(BF16)
    out = jnp.dot(hid, w2b[...], preferred_element_type=F32) * _row_to_col(gate_ref[0])
    for ref, words in zip((o0_ref, o1_ref), _pack_rows(out)):
        ref[...] = words


def _expert_ffn(xs, gate3, w1, w3, w2, layer):
    m = xs[0].shape[0]
    rows = FFN_ROWS
    nt = m // (N_EXPERTS * rows)
    assert SC_PIECES == 2 and nt >= 2
    piece = pl.BlockSpec((rows, SC_ROW), lambda e, i: (e * nt + i, 0))
    per_step = gate3.shape[2] // rows
    gspec = pl.BlockSpec((1, 1, rows), lambda e, i: ((e * nt + i) // per_step, 0, (e * nt + i) % per_step))
    any_spec = pl.BlockSpec(memory_space=pl.ANY)
    return pl.pallas_call(
        functools.partial(_ffn_kernel, layer=layer),
        grid=(N_EXPERTS, nt),
        in_specs=[gspec] + [piece] * SC_PIECES + [any_spec] * 3,
        out_specs=[piece] * SC_PIECES,
        out_shape=[jax.ShapeDtypeStruct((m, SC_ROW), jnp.int32)] * SC_PIECES,
        scratch_shapes=[pltpu.VMEM((3, D_MODEL, EXPERT_FF), F32), pltpu.VMEM((D_MODEL, EXPERT_FF), BF16),
                        pltpu.VMEM((D_MODEL, EXPERT_FF), BF16), pltpu.VMEM((EXPERT_FF, D_MODEL), BF16),
                        pltpu.SemaphoreType.DMA((3,))],
        compiler_params=_cparams(2),
        name="expert_ffn",
    )(gate3, *xs, w1, w3, w2)


def _combine_kernel(tstart_ref, x_ref, cs_ref, ce_ref, r0_hbm, r1_hbm, o_ref, rbuf, obuf, sems, osem, *, ntile, total):
    pieces_hbm = (r0_hbm, r1_hbm)
    win = COMBINE_WINDOW
    i = pl.program_id(0)
    slot = lax.rem(i, 2)

    def window_start(t):
        return pl.multiple_of((tstart_ref[t] // 8) * 8, 8)

    def copies(t, b):
        s = window_start(t)
        return [pltpu.make_async_copy(pieces_hbm[c].at[pl.ds(s, win)], rbuf.at[b, c], sems.at[b, c])
                for c in range(SC_PIECES)]

    @pl.when(i == 0)
    def _():
        for cp in copies(0, 0):
            cp.start()

    @pl.when(i + 1 < ntile)
    def _():
        for cp in copies(i + 1, 1 - slot):
            cp.start()

    for cp in copies(i, slot):
        cp.wait()

    first = _row_to_col(cs_ref[0])
    last = _row_to_col(ce_ref[0])

    def window_sum(read_piece, base):
        r = base + lax.broadcasted_iota(jnp.int32, (1, win), 1)
        q = jnp.logical_and(first <= r, r < last).astype(BF16)

        def summed(clean):
            rows = _unpack_rows([clean(read_piece(c)) for c in range(SC_PIECES)])
            return jnp.dot(q, rows, preferred_element_type=F32)

        def zero_unwritten(words):
            written = (base + lax.broadcasted_iota(jnp.int32, (win, 1), 0)) < total
            return jnp.where(written, words, 0)

        return lax.cond(base + win > total, lambda: summed(zero_unwritten), lambda: summed(lambda words: words))

    s0 = window_start(i)
    y = x_ref[...] + window_sum(lambda c: rbuf[slot, c], s0)

    n_extra = jnp.maximum(tstart_ref[i + 1] - (s0 + win) + win - 1, 0) // win

    def extra(k, acc):
        base = pl.multiple_of(s0 + (k + 1) * win, 8)
        cps = [pltpu.make_async_copy(pieces_hbm[c].at[pl.ds(base, win)], obuf.at[c], osem.at[c])
               for c in range(SC_PIECES)]
        for cp in cps:
            cp.start()
        for cp in cps:
            cp.wait()
        return acc + window_sum(lambda c: obuf[c], base)

    o_ref[...] = lax.fori_loop(0, n_extra, extra, y)


def _combine(xmid, cs, ce, pieces, total):
    n = xmid.shape[0]
    tt = COMBINE_TOKENS
    win = COMBINE_WINDOW
    ntile = n // tt
    tstart = jnp.concatenate([cs[::tt], jnp.full((1,), total, jnp.int32)])
    cs3 = cs.reshape(ntile, 1, tt)
    ce3 = ce.reshape(ntile, 1, tt)
    any_spec = pl.BlockSpec(memory_space=pl.ANY)
    tok = pl.BlockSpec((1, 1, tt), lambda i, ts: (i, 0, 0))
    grid_spec = pltpu.PrefetchScalarGridSpec(
        num_scalar_prefetch=1,
        grid=(ntile,),
        in_specs=[pl.BlockSpec((tt, D_MODEL), lambda i, ts: (i, 0)), tok, tok] + [any_spec] * SC_PIECES,
        out_specs=pl.BlockSpec((tt, D_MODEL), lambda i, ts: (i, 0)),
        scratch_shapes=[pltpu.VMEM((2, SC_PIECES, win, SC_ROW), jnp.int32), pltpu.VMEM((SC_PIECES, win, SC_ROW), jnp.int32),
                        pltpu.SemaphoreType.DMA((2, SC_PIECES)), pltpu.SemaphoreType.DMA((SC_PIECES,))],
    )
    return pl.pallas_call(
        functools.partial(_combine_kernel, ntile=ntile, total=total),
        grid_spec=grid_spec,
        out_shape=jax.ShapeDtypeStruct((n, D_MODEL), F32),
        compiler_params=_cparams(1),
        name="combine",
    )(tstart, xmid, cs3, ce3, *pieces)


def _t5_bucket(rel):
    half = REL_BUCKETS // 2
    max_exact = half // 2
    base = np.where(rel > 0, half, 0)
    n = np.abs(rel)
    large = max_exact + (np.log(np.maximum(n, 1) / max_exact) / math.log(REL_MAX_DIST / max_exact)
                         * (half - max_exact)).astype(np.int32)
    large = np.minimum(large, half - 1)
    return (base + np.where(n < max_exact, n, large)).astype(np.int32)


def _head_perm():
    nq = ATTN_HEADS // 2
    cols = []
    for j in range(nq):
        for half in range(2):
            h = j + nq * half
            cols.extend(range(h * ATTN_HEAD_DIM, (h + 1) * ATTN_HEAD_DIM))
    return np.asarray(cols, np.int32)


def _attn_bias_tables(rel_bias):
    q_pos = np.arange(BLOCK)[:, None]
    k_off = np.arange(3 * BLOCK)[None, :] - BLOCK
    rel = k_off - q_pos
    in_window = np.abs(rel) <= WINDOW
    onehot = jnp.asarray(_t5_bucket(rel)[:, :, None] == np.arange(REL_BUCKETS)[None, None, :], F32)
    bias = jnp.einsum("qkb,bh->hqk", onehot, rel_bias.astype(F32), precision=lax.Precision.HIGHEST)
    col = np.arange(3 * BLOCK)[None, :]
    tables = []
    for valid in (col >= BLOCK, np.ones_like(col, bool), col < 2 * BLOCK):
        t = jnp.where(jnp.asarray(in_window & valid)[None], bias, NEG)
        nq = ATTN_HEADS // 2
        rows = [jnp.concatenate([t[j], t[j + nq]], axis=1) for j in range(nq)]
        tables.append(jnp.concatenate(rows, axis=0))
    return jnp.stack(tables)


def _retention_tables(decay_logit, norm_g):
    cr = RET_CHUNK
    lg = jax.nn.log_sigmoid(decay_logit.astype(F32))
    lgf, lgb = lg[0][:, None, None], lg[1][:, None, None]
    pos = np.arange(cr, dtype=np.float32)
    dist = pos[:, None] - pos[None, :]
    scale = RET_DIM ** -0.5
    dmask = jnp.where(jnp.asarray(dist >= 0)[None],
                      jnp.exp(lgf * np.maximum(dist, 0.0)[None]),
                      jnp.exp(lgb * np.maximum(-dist, 0.0)[None])) * scale
    col = lambda v: jnp.broadcast_to(v[:, :, None], (RET_HEADS, cr, RET_DIM))
    rowf = col(jnp.exp(lg[0][:, None] * pos[None]))
    rowb = col(jnp.exp(lg[1][:, None] * (cr - 1.0 - pos)[None]))
    wkf = col(jnp.exp(lg[0][:, None] * (cr - pos)[None]) * scale)
    wkb = col(jnp.exp(lg[1][:, None] * (pos + 1.0)[None]) * scale)
    dec = jnp.concatenate([jnp.exp(lg[0] * cr), jnp.exp(lg[1] * cr)])
    return dec, dmask, rowf, rowb, wkf, wkb, norm_g.astype(F32)


def _layer(x2, b, s, p):
    qa, ka, va, qr, kr, vr, gr, ga, gt = _in_proj(x2, p["g1"], p["w_in"], p["qg"], p["kg"], p["bdq"], p["bdk"])
    attn = _attention(qa, ka, va, p["bias3"], p["sink"], b, s)
    retn = _retention(qr, kr, vr, gr, p["retn"], b, s)
    xmid, afft, *h2 = _merge(attn, retn, ga, gt, x2, p["wba"], p["wbr"], p["wo"], p["g2"], p["wr"])
    n = b * s
    cap = max(1, EC_CAPACITY_FACTOR * n // N_EXPERTS)
    total = N_EXPERTS * cap
    idx, slots, cs, ce = _select(afft, cap)
    gate, dst = _slot_values(afft, slots, idx, cap)
    xs = [_sc_gather(piece, idx) for piece in h2]
    outs = _expert_ffn(xs, gate.reshape(total // FFN_ROWS, 1, FFN_ROWS), p["w1"], p["w3"], p["w2"], p["layer"])
    by_token = [_sc_scatter(o, dst.reshape(-1), total + COMBINE_WINDOW) for o in outs]
    return _combine(xmid, cs, ce, by_token, total)


def kernel(x_prompt, x_sample, norm_mix_g, w_in, q_norm_g, k_norm_g, attn_sink, rel_bias, retn_decay_logit, retn_norm_g, w_branch_attn, w_branch_retn, w_out, norm_ffn_g, w_router, w_exp_gate, w_exp_up, w_exp_down):
    depth = w_in.shape[0]
    perm = _head_perm()
    bias3 = _attn_bias_tables(rel_bias)
    bdq = jnp.asarray(np.kron(np.eye(ATTN_HEADS), np.ones((ATTN_HEAD_DIM, ATTN_HEAD_DIM))), BF16)
    bdk = jnp.asarray(np.kron(np.eye(ATTN_KV_HEADS), np.ones((ATTN_HEAD_DIM, ATTN_HEAD_DIM))), BF16)
    layers = []
    for l in range(depth):
        w = w_in[l]
        w = jnp.concatenate([w[:, :ATTN_WIDTH][:, perm], w[:, ATTN_WIDTH:]], axis=1).astype(BF16)
        wr = jnp.pad(w_router[l], ((0, 0), (0, LANES - N_EXPERTS))).astype(BF16)
        layers.append(dict(
            g1=norm_mix_g[l].astype(F32)[None], w_in=w,
            qg=(jnp.tile(q_norm_g[l].astype(F32), ATTN_HEADS) * (ATTN_HEAD_DIM ** -0.5))[None],
            kg=jnp.tile(k_norm_g[l].astype(F32), ATTN_KV_HEADS)[None],
            bdq=bdq, bdk=bdk, bias3=bias3, sink=attn_sink[l].astype(F32),
            retn=_retention_tables(retn_decay_logit[l], retn_norm_g[l]),
            wba=w_branch_attn[l][perm, :].astype(BF16), wbr=w_branch_retn[l].astype(BF16),
            wo=w_out[l].astype(BF16), g2=norm_ffn_g[l].astype(F32)[None], wr=wr,
            w1=w_exp_gate, w3=w_exp_up, w2=w_exp_down, layer=l))

    def trunk(x):
        b, s, d = x.shape
        x2 = x.reshape(b * s, d)
        for p in layers:
            x2 = _layer(x2, b, s, p)
        return x2.reshape(b, s, d)

    return (trunk(x_prompt), trunk(x_sample))
```

```python
import functools
import math

import numpy as np
import jax
import jax.numpy as jnp
from jax import lax
from jax.experimental import pallas as pl
from jax.experimental.pallas import tpu as pltpu
from jax.experimental.pallas import tpu_sc as plsc

D_MODEL = 1024
ATTN_HEADS = 8
ATTN_KV_HEADS = 2
ATTN_HEAD_DIM = 64
WINDOW = 128
BLOCK = 128
REL_BUCKETS = 32
REL_MAX_DIST = 128
RET_HEADS = 4
RET_DIM = 128
N_EXPERTS = 16
EC_CAPACITY_FACTOR = 2
EXPERT_FF = 1024
EPS = 1e-6

ATTN_WIDTH = ATTN_HEADS * ATTN_HEAD_DIM
KV_WIDTH = ATTN_KV_HEADS * ATTN_HEAD_DIM
RET_WIDTH = RET_HEADS * RET_DIM
IN_SPLITS = (ATTN_WIDTH, KV_WIDTH, KV_WIDTH, RET_WIDTH, RET_WIDTH, RET_WIDTH, RET_WIDTH, D_MODEL, D_MODEL)
IN_OFFSETS = tuple(int(o) for o in np.cumsum((0,) + IN_SPLITS))

LANES = 128
VMEM_LIMIT_BYTES = 56 * 1024 * 1024

TOKEN_TILE = 1024
MERGE_ROWS = 256
IN_PROJ_TILE = 1024
ATTN_QUERIES = 512
RET_CHUNK = 256
RET_STEP = 1024
FFN_ROWS = 1024
SELECT_BLOCKS = 128
SELECT_SLOTS = 1024
SC_WINDOW = 128
SC_ROW = 256
PACKED_WIDTH = D_MODEL // 2
SC_PIECES = PACKED_WIDTH // SC_ROW
COMBINE_TOKENS = 512
COMBINE_WINDOW = 1280
COMBINE_SUB = 128
COMBINE_SUBWIN = 384

F32 = jnp.float32
BF16 = jnp.bfloat16
NEG = -1e30


def _cparams(n_axes, vmem=VMEM_LIMIT_BYTES):
    return pltpu.CompilerParams(dimension_semantics=("arbitrary",) * n_axes, vmem_limit_bytes=vmem)


def _sigmoid(x):
    return 1.0 / (1.0 + jnp.exp(-x))


HIGH_HALF = -65536


def _pack_rows(x):
    bits = pltpu.bitcast(x.astype(BF16).astype(F32), jnp.int32)
    words = lax.shift_right_logical(bits[:, :PACKED_WIDTH], 16) | (bits[:, PACKED_WIDTH:] & HIGH_HALF)
    return [words[:, c * SC_ROW:(c + 1) * SC_ROW] for c in range(SC_PIECES)]


def _unpack_rows(pieces):
    low = [pltpu.bitcast(lax.shift_left(w, 16), F32) for w in pieces]
    high = [pltpu.bitcast(w & HIGH_HALF, F32) for w in pieces]
    return jnp.concatenate(low + high, axis=1).astype(BF16)


def _in_proj_kernel(x_ref, g_ref, w_ref, qg_ref, kg_ref, bdq_ref, bdk_ref,
                    qa_ref, ka_ref, va_ref, qr_ref, kr_ref, vr_ref, gr_ref, ga_ref, gt_ref):
    x = x_ref[...]
    ms = jnp.mean(x * x, axis=-1, keepdims=True)
    h = (x * lax.rsqrt(ms + EPS) * g_ref[...]).astype(BF16)

    def mm(k):
        return jnp.dot(h, w_ref[:, IN_OFFSETS[k]:IN_OFFSETS[k + 1]], preferred_element_type=F32)

    def head_norm(t, bd_ref, gain_ref):
        ss = jnp.dot((t * t).astype(BF16), bd_ref[...], preferred_element_type=F32)
        return t * lax.rsqrt(ss * (1.0 / ATTN_HEAD_DIM) + EPS) * gain_ref[...]

    qa_ref[...] = head_norm(mm(0), bdq_ref, qg_ref).astype(BF16)
    ka_ref[...] = head_norm(mm(1), bdk_ref, kg_ref).astype(BF16)
    for k, ref in ((2, va_ref), (3, qr_ref), (4, kr_ref), (5, vr_ref), (6, gr_ref), (7, ga_ref), (8, gt_ref)):
        ref[...] = mm(k).astype(BF16)


def _in_proj(x2, g, w, qg, kg, bdq, bdk):
    n = x2.shape[0]
    tm = IN_PROJ_TILE
    full = lambda a: pl.BlockSpec(a.shape, lambda i: (0,) * a.ndim, pipeline_mode=pl.Buffered(1))
    widths = IN_SPLITS
    return pl.pallas_call(
        _in_proj_kernel,
        grid=(n // tm,),
        in_specs=[pl.BlockSpec((tm, D_MODEL), lambda i: (i, 0)), full(g), full(w), full(qg), full(kg),
                  full(bdq), full(bdk)],
        out_specs=[pl.BlockSpec((tm, wd), lambda i: (i, 0)) for wd in widths],
        out_shape=[jax.ShapeDtypeStruct((n, wd), BF16) for wd in widths],
        compiler_params=_cparams(1),
        name="in_proj",
    )(x2, g, w, qg, kg, bdq, bdk)


def _attn_kernel(sink_ref, q_ref, kp_ref, kc_ref, kn_ref, vp_ref, vc_ref, vn_ref, bias_ref, o_ref, *, nsteps):
    nq = ATTN_HEADS // 2
    ni = pl.program_id(1)
    k = jnp.concatenate([kp_ref[...], kc_ref[...], kn_ref[...]], axis=0)
    v = jnp.concatenate([vp_ref[...], vc_ref[...], vn_ref[...]], axis=0)
    low = lax.broadcasted_iota(jnp.int32, k.shape, 1) < ATTN_HEAD_DIM
    zero = jnp.zeros_like(k)
    k_lo, k_hi = jnp.where(low, k, zero), jnp.where(low, zero, k)
    v_lo, v_hi = jnp.where(low, v, zero), jnp.where(low, zero, v)
    nk = 3 * BLOCK
    low_o = lax.broadcasted_iota(jnp.int32, (BLOCK, LANES), 1) < ATTN_HEAD_DIM
    key_low = lax.broadcasted_iota(jnp.int32, (2 * nk, LANES), 0) < nk
    lane_low = lax.broadcasted_iota(jnp.int32, (2 * nk, LANES), 1) < ATTN_HEAD_DIM
    ones_bd = (key_low == lane_low).astype(BF16)
    nsub = ATTN_QUERIES // BLOCK
    for sb in range(nsub):
        rows = slice(sb * BLOCK, (sb + 1) * BLOCK)
        keys = slice(sb * BLOCK, sb * BLOCK + nk)
        q = q_ref[rows, :]
        qs = jnp.concatenate([q[:, j * LANES:(j + 1) * LANES] for j in range(nq)], axis=0)
        kbd = jnp.concatenate([k_lo[keys], k_hi[keys]], axis=0)
        vbd = jnp.concatenate([v_lo[keys], v_hi[keys]], axis=0)
        s = lax.dot_general(qs, kbd, (((1,), (1,)), ((), ())), preferred_element_type=F32)
        if sb == 0:
            table = jnp.where(ni == 0, 0, 1)
        elif sb == nsub - 1:
            table = jnp.where(ni == nsteps - 1, 2, 1)
        else:
            table = 1
        s = s + bias_ref[table]
        probs, sink_terms = [], []
        for j in range(nq):
            row_p, row_sink = [], []
            for half in range(2):
                sj = s[j * BLOCK:(j + 1) * BLOCK, half * nk:(half + 1) * nk]
                sk = sink_ref[j + nq * half]
                m = jnp.maximum(jnp.max(sj, axis=-1, keepdims=True), sk)
                row_p.append(jnp.exp(sj - m).astype(BF16))
                row_sink.append(jnp.exp(sk - m))
            probs.append(jnp.concatenate(row_p, axis=1))
            sink_terms.append(jnp.where(low_o, row_sink[0], row_sink[1]))
        pm = jnp.concatenate(probs, axis=0)
        od = jnp.dot(pm, jnp.concatenate([vbd, ones_bd], axis=1), preferred_element_type=F32)
        o = od[:, :LANES] / (od[:, LANES:] + jnp.concatenate(sink_terms, axis=0))
        for j in range(nq):
            o_ref[rows, j * LANES:(j + 1) * LANES] = o[j * BLOCK:(j + 1) * BLOCK].astype(BF16)


def _attention(qa, ka, va, bias3, sink, b, s):
    tq = ATTN_QUERIES
    per = tq // BLOCK
    nb = s // BLOCK
    nsteps = s // tq
    assert s % tq == 0 and nb >= 2
    n = b * s
    main = lambda wd: pl.BlockSpec((tq, wd), lambda bi, ni: (bi * nsteps + ni, 0))
    prev = pl.BlockSpec((BLOCK, KV_WIDTH), lambda bi, ni: (bi * nb + jnp.maximum(ni * per - 1, 0), 0))
    nxt = pl.BlockSpec((BLOCK, KV_WIDTH), lambda bi, ni: (bi * nb + jnp.minimum(ni * per + per, nb - 1), 0))
    return pl.pallas_call(
        functools.partial(_attn_kernel, nsteps=nsteps),
        grid=(b, nsteps),
        in_specs=[pl.BlockSpec(memory_space=pltpu.SMEM), main(ATTN_WIDTH),
                  prev, main(KV_WIDTH), nxt, prev, main(KV_WIDTH), nxt,
                  pl.BlockSpec(bias3.shape, lambda bi, ni: (0, 0, 0))],
        out_specs=main(ATTN_WIDTH),
        out_shape=jax.ShapeDtypeStruct((n, ATTN_WIDTH), BF16),
        compiler_params=_cparams(2),
        name="attn",
    )(sink, qa, ka, ka, ka, va, va, va, bias3)


def _retn_kernel(dec_ref, q_ref, k_ref, v_ref, g_ref, dmask_ref, rowf_ref, rowb_ref, wkf_ref, wkb_ref, ng_ref,
                 o_ref, tstore, uf, tb, *, nsteps):
    p = pl.program_id(1)
    n = pl.program_id(2)
    cr = RET_CHUNK
    per = RET_STEP // cr
    tn = (((0,), (0,)), ((), ()))
    nt = (((1,), (1,)), ((), ()))
    hs = lambda h: slice(h * RET_DIM, (h + 1) * RET_DIM)

    @pl.when(p == 0)
    def _():
        @pl.when(n == 0)
        def _():
            tb[...] = jnp.zeros_like(tb)
        first_chunk = (nsteps - 1 - n) * per
        for sub in reversed(range(per)):
            rows = slice(sub * cr, (sub + 1) * cr)
            for h in range(RET_HEADS):
                tstore[first_chunk + sub, h] = tb[h].astype(BF16)
                kw = (k_ref[rows, hs(h)].astype(F32) * wkb_ref[h]).astype(BF16)
                upd = lax.dot_general(kw, v_ref[rows, hs(h)], tn, preferred_element_type=F32)
                tb[h] = dec_ref[RET_HEADS + h] * tb[h] + upd

    @pl.when(p == 1)
    def _():
        @pl.when(n == 0)
        def _():
            uf[...] = jnp.zeros_like(uf)
        for sub in range(per):
            rows = slice(sub * cr, (sub + 1) * cr)
            for h in range(RET_HEADS):
                qh = q_ref[rows, hs(h)]
                kh = k_ref[rows, hs(h)]
                vh = v_ref[rows, hs(h)]
                sc = lax.dot_general(qh, kh, nt, preferred_element_type=F32) * dmask_ref[h]
                intra = jnp.dot(sc.astype(BF16), vh, preferred_element_type=F32)
                states = jnp.concatenate([uf[h].astype(BF16), tstore[n * per + sub, h]], axis=1)
                cross = jnp.dot(qh, states, preferred_element_type=F32)
                o = intra + cross[:, :RET_DIM] * rowf_ref[h] + cross[:, RET_DIM:] * rowb_ref[h]
                mu = jnp.mean(o, axis=-1, keepdims=True)
                d = o - mu
                var = jnp.mean(d * d, axis=-1, keepdims=True)
                on = d * lax.rsqrt(var + EPS) * ng_ref[h:h + 1, :]
                gate = g_ref[rows, hs(h)].astype(F32)
                o_ref[rows, hs(h)] = (gate * _sigmoid(gate) * on).astype(BF16)
                kw = (kh.astype(F32) * wkf_ref[h]).astype(BF16)
                uf[h] = dec_ref[h] * uf[h] + lax.dot_general(kw, vh, tn, preferred_element_type=F32)


def _retention(qr, kr, vr, gr, tables, b, s):
    dec, dmask, rowf, rowb, wkf, wkb, ng = tables
    rs = RET_STEP
    nsteps = s // rs
    nc = s // RET_CHUNK
    assert s % rs == 0
    n = b * s
    full = lambda a: pl.BlockSpec(a.shape, lambda bi, pi, ni: (0,) * a.ndim)
    fwd_spec = pl.BlockSpec((rs, RET_WIDTH), lambda bi, pi, ni: (bi * nsteps + ni * pi, 0))
    kv_spec = pl.BlockSpec((rs, RET_WIDTH),
                           lambda bi, pi, ni: (bi * nsteps + ni * pi + (1 - pi) * (nsteps - 1 - ni), 0))
    return pl.pallas_call(
        functools.partial(_retn_kernel, nsteps=nsteps),
        grid=(b, 2, nsteps),
        in_specs=[pl.BlockSpec(memory_space=pltpu.SMEM), fwd_spec, kv_spec, kv_spec, fwd_spec,
                  full(dmask), full(rowf), full(rowb), full(wkf), full(wkb), full(ng)],
        out_specs=fwd_spec,
        out_shape=jax.ShapeDtypeStruct((n, RET_WIDTH), BF16),
        scratch_shapes=[pltpu.VMEM((nc, RET_HEADS, RET_DIM, RET_DIM), BF16),
                        pltpu.VMEM((RET_HEADS, RET_DIM, RET_DIM), F32),
                        pltpu.VMEM((RET_HEADS, RET_DIM, RET_DIM), F32)],
        compiler_params=_cparams(3),
        name="retention",
    )(dec, qr, kr, vr, gr, dmask, rowf, rowb, wkf, wkb, ng)


def _merge_kernel(attn_ref, retn_ref, ga_ref, gr_ref, x_ref, wba_ref, wbr_ref, wo_ref, g2_ref, wr_ref,
                  xmid_ref, afft_ref, *h2_refs):
    for sub in range(TOKEN_TILE // MERGE_ROWS):
        rows = slice(sub * MERGE_ROWS, (sub + 1) * MERGE_ROWS)
        a = jnp.dot(attn_ref[rows, :], wba_ref[...], preferred_element_type=F32)
        r = jnp.dot(retn_ref[rows, :], wbr_ref[...], preferred_element_type=F32)
        merged = _sigmoid(ga_ref[rows, :].astype(F32)) * a + _sigmoid(gr_ref[rows, :].astype(F32)) * r
        xn = x_ref[rows, :] + jnp.dot(merged.astype(BF16), wo_ref[...], preferred_element_type=F32)
        xmid_ref[rows, :] = xn
        ms = jnp.mean(xn * xn, axis=-1, keepdims=True)
        h2 = xn * lax.rsqrt(ms + EPS) * g2_ref[...]
        for ref, words in zip(h2_refs, _pack_rows(h2)):
            ref[rows, :] = words
        logits = jnp.dot(h2.astype(BF16), wr_ref[...], preferred_element_type=F32)
        real = lax.broadcasted_iota(jnp.int32, logits.shape, 1) < N_EXPERTS
        logits = jnp.where(real, logits, -jnp.inf)
        m = jnp.max(logits, axis=-1, keepdims=True)
        ex = jnp.exp(logits - m)
        aff = ex / jnp.sum(ex, axis=-1, keepdims=True)
        afft_ref[:, rows] = aff.T[:N_EXPERTS, :]


def _merge(attn, retn, ga, gr, x2, wba, wbr, wo, g2, wr):
    n = x2.shape[0]
    tm = TOKEN_TILE
    full = lambda a: pl.BlockSpec(a.shape, lambda i: (0,) * a.ndim)
    row = lambda wd: pl.BlockSpec((tm, wd), lambda i: (i, 0))
    return pl.pallas_call(
        _merge_kernel,
        grid=(n // tm,),
        in_specs=[row(ATTN_WIDTH), row(RET_WIDTH), row(D_MODEL), row(D_MODEL), row(D_MODEL),
                  full(wba), full(wbr), full(wo), full(g2), full(wr)],
        out_specs=[row(D_MODEL), pl.BlockSpec((N_EXPERTS, tm), lambda i: (0, i))] + [row(SC_ROW)] * SC_PIECES,
        out_shape=[jax.ShapeDtypeStruct((n, D_MODEL), F32), jax.ShapeDtypeStruct((N_EXPERTS, n), F32)]
        + [jax.ShapeDtypeStruct((n, SC_ROW), jnp.int32)] * SC_PIECES,
        compiler_params=_cparams(1),
        name="merge",
    )(attn, retn, ga, gr, x2, wba, wbr, wo, g2, wr)


def _select_kernel(aff_ref, u_ref, ls_ref, idx_ref, slot_ref, cs_ref, ce_ref,
                   thr, selbuf, cnt, csr, rank, pieces, offi, *, cap, tb):
    ps = pl.program_id(0)
    e = pl.program_id(1)
    j = pl.program_id(2)
    nblk = SELECT_BLOCKS
    pc = SELECT_SLOTS

    def cumsum(vals):
        inb = jnp.dot(vals.astype(BF16), u_ref[...], preferred_element_type=F32)
        tot = jnp.broadcast_to(inb[:, tb - 1:tb], (nblk, LANES))
        off = jnp.dot(ls_ref[...], tot, preferred_element_type=F32, precision=lax.Precision.HIGHEST)
        return inb, off[:, 0:1], tot[:, 0:1]

    @pl.when(jnp.logical_and(ps == 0, jnp.logical_and(e == 0, j == 0)))
    def _():
        def bit_step(t, curs):
            bit = jnp.left_shift(jnp.int32(1), 30 - t)
            out = []
            for x in range(N_EXPERTS):
                cand = curs[x] | bit
                n_ge = jnp.sum((pltpu.bitcast(aff_ref[x], jnp.int32) >= cand).astype(jnp.int32), keepdims=True)
                out.append(jnp.where(n_ge >= cap, cand, curs[x]))
            return tuple(out)

        found = lax.fori_loop(0, 31, bit_step, tuple(jnp.zeros((1, 1), jnp.int32) for _ in range(N_EXPERTS)))
        for x in range(N_EXPERTS):
            thr[x] = jnp.broadcast_to(found[x], thr.shape[1:])

    @pl.when(jnp.logical_and(ps == 0, j == 0))
    def _():
        bits = pltpu.bitcast(aff_ref[e], jnp.int32)
        limit = thr[e][0:1, 0:1]
        gt = bits > limit
        eq = bits == limit
        need = (cap - jnp.sum(gt.astype(jnp.int32), keepdims=True)).astype(F32)
        eqf = eq.astype(F32)
        eq_in, eq_off, _ = cumsum(eqf)
        eq_rank = eq_in + eq_off - eqf
        sel = jnp.logical_or(gt, jnp.logical_and(eq, eq_rank < need)).astype(F32)
        selbuf[e] = sel.astype(BF16)

        @pl.when(e == 0)
        def _():
            cnt[...] = sel

        @pl.when(e > 0)
        def _():
            cnt[...] = cnt[...] + sel

    @pl.when(jnp.logical_and(ps == 1, j == 0))
    def _():
        @pl.when(e == 0)
        def _():
            c = cnt[...]
            c_in, c_off, _ = cumsum(c)
            start = c_in + c_off - c
            csr[...] = start
            cs_ref[...] = start.astype(jnp.int32)
            ce_ref[...] = (start + c).astype(jnp.int32)
            rank[...] = jnp.zeros_like(rank)

        sel = selbuf[e].astype(F32)
        s_in, s_off, s_tot = cumsum(sel)
        count_t = (s_in + s_off).T
        high = jnp.floor(count_t * (1.0 / 256.0))
        pieces[0] = high.astype(BF16)
        pieces[1] = (count_t - 256.0 * high).astype(BF16)
        slot_ref[0] = (csr[...] + rank[...]).astype(jnp.int32)
        rank[...] = rank[...] + sel
        offi[...] = jnp.broadcast_to(s_off + s_tot, (nblk, LANES))

    @pl.when(ps == 1)
    def _():
        slot = (j * pc + lax.broadcasted_iota(jnp.int32, (1, pc), 1)).astype(F32)
        blk = jnp.sum((offi[:, 0:1] <= slot).astype(jnp.int32), axis=0, keepdims=True)
        onehot = (lax.broadcasted_iota(jnp.int32, (nblk, pc), 0) == blk).astype(BF16)
        counts = (256.0 * jnp.dot(pieces[0], onehot, preferred_element_type=F32)
                  + jnp.dot(pieces[1], onehot, preferred_element_type=F32))
        inb = jnp.sum((counts <= slot + 0.5).astype(jnp.int32), axis=0, keepdims=True)
        idx_ref[0] = blk * tb + inb


def _select(afft, cap):
    n = afft.shape[1]
    nblk = SELECT_BLOCKS
    tb = n // nblk
    pc = SELECT_SLOTS
    assert n % nblk == 0 and tb % LANES == 0 and cap % pc == 0 and cap < 65536
    nch = cap // pc
    aff3 = afft.reshape(N_EXPERTS, nblk, tb)
    upper = jnp.asarray(np.triu(np.ones((tb, tb), np.float32)), BF16)
    lstrict = jnp.asarray(np.tril(np.ones((nblk, nblk), np.float32), -1))
    full = lambda a: pl.BlockSpec(a.shape, lambda ps, e, j: (0,) * a.ndim)
    idx_spec = pl.BlockSpec((1, 1, pc), lambda ps, e, j: (ps * (e * nch + j), 0, 0))
    slot_spec = pl.BlockSpec((1, nblk, tb), lambda ps, e, j: (ps * e, 0, 0))
    tok_spec = pl.BlockSpec((nblk, tb), lambda ps, e, j: (0, 0))
    idx, slots, cs, ce = pl.pallas_call(
        functools.partial(_select_kernel, cap=cap, tb=tb),
        grid=(2, N_EXPERTS, nch),
        in_specs=[full(aff3), full(upper), full(lstrict)],
        out_specs=[idx_spec, slot_spec, tok_spec, tok_spec],
        out_shape=[jax.ShapeDtypeStruct((N_EXPERTS * nch, 1, pc), jnp.int32),
                   jax.ShapeDtypeStruct((N_EXPERTS, nblk, tb), jnp.int32),
                   jax.ShapeDtypeStruct((nblk, tb), jnp.int32), jax.ShapeDtypeStruct((nblk, tb), jnp.int32)],
        scratch_shapes=[pltpu.VMEM((N_EXPERTS, 8, LANES), jnp.int32), pltpu.VMEM((N_EXPERTS, nblk, tb), BF16),
                        pltpu.VMEM((nblk, tb), F32), pltpu.VMEM((nblk, tb), F32), pltpu.VMEM((nblk, tb), F32),
                        pltpu.VMEM((2, tb, nblk), BF16), pltpu.VMEM((nblk, LANES), F32)],
        compiler_params=_cparams(3),
        name="select",
    )(aff3, upper, lstrict)
    return idx.reshape(-1), slots.reshape(N_EXPERTS, n), cs.reshape(-1), ce.reshape(-1)


def _slot_values(afft, slots, idx, cap):
    n = afft.shape[1]
    table = jnp.concatenate([lax.bitcast_convert_type(afft, jnp.int32).T, slots.T,
                             jnp.zeros((n, LANES - 2 * N_EXPERTS), jnp.int32)], axis=1)
    rows = _sc_gather(table, idx).reshape(N_EXPERTS, cap, LANES)
    own = jnp.eye(N_EXPERTS, dtype=jnp.int32)[:, None, :]
    gate = lax.bitcast_convert_type(jnp.sum(rows[:, :, :N_EXPERTS] * own, axis=-1), F32)
    dst = jnp.sum(rows[:, :, N_EXPERTS:2 * N_EXPERTS] * own, axis=-1)
    return gate, dst


def _sc_mesh():
    return plsc.VectorSubcoreMesh(core_axis_name="c", subcore_axis_name="s")


def _sc_scatter(rows, idx, m_out):
    m, d = rows.shape
    assert m % SC_WINDOW == 0

    @functools.partial(pl.kernel, out_type=jax.ShapeDtypeStruct((m_out, d), rows.dtype), mesh=_sc_mesh(),
                       name="sc_scatter")
    def scatter(x_hbm, i_hbm, o_hbm):
        def body(x_vmem, i_vmem):
            pltpu.sync_copy(x_vmem, o_hbm.at[i_vmem.at[0]])

        pltpu.emit_pipeline(
            body,
            grid=(m // SC_WINDOW,),
            in_specs=[pl.BlockSpec((SC_WINDOW, d), lambda i: (i, 0)),
                      pl.BlockSpec((1, SC_WINDOW), lambda i: (0, i))],
            out_specs=[],
            core_axis_name=("c", "s"),
            dimension_semantics=(pltpu.PARALLEL,),
        )(x_hbm, i_hbm)

    return scatter(rows, idx.reshape(1, m))


def _sc_gather(table, idx):
    m = idx.shape[0]
    d = table.shape[1]
    assert m % SC_WINDOW == 0

    @functools.partial(pl.kernel, out_type=jax.ShapeDtypeStruct((m, d), table.dtype), mesh=_sc_mesh(),
                       name="sc_gather")
    def gather(x_hbm, i_hbm, o_hbm):
        def body(i_vmem, o_vmem):
            pltpu.sync_copy(x_hbm.at[i_vmem.at[0]], o_vmem)

        pltpu.emit_pipeline(
            body,
            grid=(m // SC_WINDOW,),
            in_specs=[pl.BlockSpec((1, SC_WINDOW), lambda i: (0, i))],
            out_specs=[pl.BlockSpec((SC_WINDOW, d), lambda i: (i, 0))],
            core_axis_name=("c", "s"),
            dimension_semantics=(pltpu.PARALLEL,),
        )(i_hbm, o_hbm)

    return gather(table, idx.reshape(1, m))


def _row_to_col(row):
    n = row.shape[1]
    eye = lax.broadcasted_iota(jnp.int32, (n, n), 0) == lax.broadcasted_iota(jnp.int32, (n, n), 1)
    return jnp.sum(jnp.where(eye, row, jnp.zeros_like(row)), axis=1, keepdims=True)


def _ffn_kernel(gate_ref, x0_ref, x1_ref, w1_hbm, w3_hbm, w2_hbm, o0_ref, o1_ref, wstage, w1b, w3b, w2b, wsem,
                *, layer):
    e = pl.program_id(0)
    i = pl.program_id(1)

    def weight_copies(expert):
        return [pltpu.make_async_copy(w_hbm.at[layer, expert], wstage.at[k], wsem.at[k])
                for k, w_hbm in enumerate((w1_hbm, w3_hbm, w2_hbm))]

    @pl.when(i == 0)
    def _():
        @pl.when(e == 0)
        def _():
            for cp in weight_copies(0):
                cp.start()
        for cp in weight_copies(e):
            cp.wait()
        w1b[...] = wstage[0].astype(BF16)
        w3b[...] = wstage[1].astype(BF16)
        w2b[...] = wstage[2].astype(BF16)

    @pl.when(jnp.logical_and(i == 1, e + 1 < N_EXPERTS))
    def _():
        for cp in weight_copies(e + 1):
            cp.start()

    xs = _unpack_rows([x0_ref[...], x1_ref[...]])
    hg = jnp.dot(xs, w1b[...], preferred_element_type=F32)
    hu = jnp.dot(xs, w3b[...], preferred_element_type=F32)
    hid = (hg * _sigmoid(hg) * hu).astype(BF16)
    out = jnp.dot(hid, w2b[...], preferred_element_type=F32) * _row_to_col(gate_ref[0])
    for ref, words in zip((o0_ref, o1_ref), _pack_rows(out)):
        ref[...] = words


def _expert_ffn(xs, gate3, w1, w3, w2, layer):
    m = xs[0].shape[0]
    rows = FFN_ROWS
    nt = m // (N_EXPERTS * rows)
    assert SC_PIECES == 2 and nt >= 2
    piece = pl.BlockSpec((rows, SC_ROW), lambda e, i: (e * nt + i, 0))
    per_step = gate3.shape[2] // rows
    gspec = pl.BlockSpec((1, 1, rows), lambda e, i: ((e * nt + i) // per_step, 0, (e * nt + i) % per_step))
    any_spec = pl.BlockSpec(memory_space=pl.ANY)
    return pl.pallas_call(
        functools.partial(_ffn_kernel, layer=layer),
        grid=(N_EXPERTS, nt),
        in_specs=[gspec] + [piece] * SC_PIECES + [any_spec] * 3,
        out_specs=[piece] * SC_PIECES,
        out_shape=[jax.ShapeDtypeStruct((m, SC_ROW), jnp.int32)] * SC_PIECES,
        scratch_shapes=[pltpu.VMEM((3, D_MODEL, EXPERT_FF), F32), pltpu.VMEM((D_MODEL, EXPERT_FF), BF16),
                        pltpu.VMEM((D_MODEL, EXPERT_FF), BF16), pltpu.VMEM((EXPERT_FF, D_MODEL), BF16),
                        pltpu.SemaphoreType.DMA((3,))],
        compiler_params=_cparams(2),
        name="expert_ffn",
    )(gate3, *xs, w1, w3, w2)


def _combine_kernel(tsub_ref, x_ref, cs_ref, ce_ref, r0_hbm, r1_hbm, o_ref, rbuf, obuf, rows16, sems, osem,
                    *, ntile, total):
    pieces_hbm = (r0_hbm, r1_hbm)
    win = COMBINE_WINDOW
    sub = COMBINE_SUB
    subwin = COMBINE_SUBWIN
    per = COMBINE_TOKENS // sub
    i = pl.program_id(0)
    slot = lax.rem(i, 2)

    def window_start(t):
        return pl.multiple_of((tsub_ref[t * per] // 16) * 16, 16)

    def copies(t, b):
        s = window_start(t)
        return [pltpu.make_async_copy(pieces_hbm[c].at[pl.ds(s, win)], rbuf.at[b, c], sems.at[b, c])
                for c in range(SC_PIECES)]

    @pl.when(i == 0)
    def _():
        for cp in copies(0, 0):
            cp.start()

    @pl.when(i + 1 < ntile)
    def _():
        for cp in copies(i + 1, 1 - slot):
            cp.start()

    for cp in copies(i, slot):
        cp.wait()

    first = cs_ref[...]
    last = ce_ref[...]

    def zero_unwritten(words, base):
        written = (base + lax.broadcasted_iota(jnp.int32, (win, 1), 0)) < total
        return jnp.where(written, words, 0)

    def owner_matrix(tokens, base, width):
        r = base + lax.broadcasted_iota(jnp.int32, (1, width), 1)
        return jnp.logical_and(first[tokens] <= r, r < last[tokens]).astype(BF16)

    s0 = window_start(i)
    tail = s0 + win > total

    @pl.when(tail)
    def _():
        rows16[...] = _unpack_rows([zero_unwritten(rbuf[slot, c], s0) for c in range(SC_PIECES)])

    @pl.when(jnp.logical_not(tail))
    def _():
        rows16[...] = _unpack_rows([rbuf[slot, c] for c in range(SC_PIECES)])

    offsets = []
    fits = None
    for g in range(per):
        off = (tsub_ref[i * per + g] // 16) * 16 - s0
        ok = jnp.logical_and(tsub_ref[i * per + g + 1] - s0 <= off + subwin, off + subwin <= win)
        fits = ok if fits is None else jnp.logical_and(fits, ok)
        offsets.append(off)

    @pl.when(fits)
    def _():
        for g in range(per):
            tokens = slice(g * sub, (g + 1) * sub)
            off = pl.multiple_of(offsets[g], 16)
            q = owner_matrix(tokens, s0 + off, subwin)
            o_ref[tokens, :] = x_ref[tokens, :] + jnp.dot(q, rows16[pl.ds(off, subwin), :],
                                                         preferred_element_type=F32)

    @pl.when(jnp.logical_not(fits))
    def _():
        everyone = slice(0, COMBINE_TOKENS)
        y = x_ref[...] + jnp.dot(owner_matrix(everyone, s0, win), rows16[...], preferred_element_type=F32)
        n_extra = jnp.maximum(tsub_ref[(i + 1) * per] - (s0 + win) + win - 1, 0) // win

        def extra(k, acc):
            base = pl.multiple_of(s0 + (k + 1) * win, 16)
            cps = [pltpu.make_async_copy(pieces_hbm[c].at[pl.ds(base, win)], obuf.at[c], osem.at[c])
                   for c in range(SC_PIECES)]
            for cp in cps:
                cp.start()
            for cp in cps:
                cp.wait()
            rows = _unpack_rows([zero_unwritten(obuf[c], base) for c in range(SC_PIECES)])
            return acc + jnp.dot(owner_matrix(everyone, base, win), rows, preferred_element_type=F32)

        o_ref[...] = lax.fori_loop(0, n_extra, extra, y)


def _combine(xmid, cs, ce, pieces, total):
    n = xmid.shape[0]
    tt = COMBINE_TOKENS
    win = COMBINE_WINDOW
    ntile = n // tt
    tsub = jnp.concatenate([cs[::COMBINE_SUB], jnp.full((1,), total, jnp.int32)])
    cs2 = cs.reshape(n, 1)
    ce2 = ce.reshape(n, 1)
    any_spec = pl.BlockSpec(memory_space=pl.ANY)
    tok = pl.BlockSpec((tt, 1), lambda i, ts: (i, 0))
    grid_spec = pltpu.PrefetchScalarGridSpec(
        num_scalar_prefetch=1,
        grid=(ntile,),
        in_specs=[pl.BlockSpec((tt, D_MODEL), lambda i, ts: (i, 0)), tok, tok] + [any_spec] * SC_PIECES,
        out_specs=pl.BlockSpec((tt, D_MODEL), lambda i, ts: (i, 0)),
        scratch_shapes=[pltpu.VMEM((2, SC_PIECES, win, SC_ROW), jnp.int32), pltpu.VMEM((SC_PIECES, win, SC_ROW), jnp.int32),
                        pltpu.VMEM((win, D_MODEL), BF16),
                        pltpu.SemaphoreType.DMA((2, SC_PIECES)), pltpu.SemaphoreType.DMA((SC_PIECES,))],
    )
    return pl.pallas_call(
        functools.partial(_combine_kernel, ntile=ntile, total=total),
        grid_spec=grid_spec,
        out_shape=jax.ShapeDtypeStruct((n, D_MODEL), F32),
        compiler_params=_cparams(1),
        name="combine",
    )(tsub, xmid, cs2, ce2, *pieces)


def _t5_bucket(rel):
    half = REL_BUCKETS // 2
    max_exact = half // 2
    base = np.where(rel > 0, half, 0)
    n = np.abs(rel)
    large = max_exact + (np.log(np.maximum(n, 1) / max_exact) / math.log(REL_MAX_DIST / max_exact)
                         * (half - max_exact)).astype(np.int32)
    large = np.minimum(large, half - 1)
    return (base + np.where(n < max_exact, n, large)).astype(np.int32)


def _head_perm():
    nq = ATTN_HEADS // 2
    cols = []
    for j in range(nq):
        for half in range(2):
            h = j + nq * half
            cols.extend(range(h * ATTN_HEAD_DIM, (h + 1) * ATTN_HEAD_DIM))
    return np.asarray(cols, np.int32)


def _attn_bias_tables(rel_bias):
    q_pos = np.arange(BLOCK)[:, None]
    k_off = np.arange(3 * BLOCK)[None, :] - BLOCK
    rel = k_off - q_pos
    in_window = np.abs(rel) <= WINDOW
    onehot = jnp.asarray(_t5_bucket(rel)[:, :, None] == np.arange(REL_BUCKETS)[None, None, :], F32)
    bias = jnp.einsum("qkb,bh->hqk", onehot, rel_bias.astype(F32), precision=lax.Precision.HIGHEST)
    col = np.arange(3 * BLOCK)[None, :]
    tables = []
    for valid in (col >= BLOCK, np.ones_like(col, bool), col < 2 * BLOCK):
        t = jnp.where(jnp.asarray(in_window & valid)[None], bias, NEG)
        nq = ATTN_HEADS // 2
        rows = [jnp.concatenate([t[j], t[j + nq]], axis=1) for j in range(nq)]
        tables.append(jnp.concatenate(rows, axis=0))
    return jnp.stack(tables)


def _retention_tables(decay_logit, norm_g):
    cr = RET_CHUNK
    lg = jax.nn.log_sigmoid(decay_logit.astype(F32))
    lgf, lgb = lg[0][:, None, None], lg[1][:, None, None]
    pos = np.arange(cr, dtype=np.float32)
    dist = pos[:, None] - pos[None, :]
    scale = RET_DIM ** -0.5
    dmask = jnp.where(jnp.asarray(dist >= 0)[None],
                      jnp.exp(lgf * np.maximum(dist, 0.0)[None]),
                      jnp.exp(lgb * np.maximum(-dist, 0.0)[None])) * scale
    col = lambda v: jnp.broadcast_to(v[:, :, None], (RET_HEADS, cr, RET_DIM))
    rowf = col(jnp.exp(lg[0][:, None] * pos[None]))
    rowb = col(jnp.exp(lg[1][:, None] * (cr - 1.0 - pos)[None]))
    wkf = col(jnp.exp(lg[0][:, None] * (cr - pos)[None]) * scale)
    wkb = col(jnp.exp(lg[1][:, None] * (pos + 1.0)[None]) * scale)
    dec = jnp.concatenate([jnp.exp(lg[0] * cr), jnp.exp(lg[1] * cr)])
    return dec, dmask, rowf, rowb, wkf, wkb, norm_g.astype(F32)


def _layer(x2, b, s, p):
    qa, ka, va, qr, kr, vr, gr, ga, gt = _in_proj(x2, p["g1"], p["w_in"], p["qg"], p["kg"], p["bdq"], p["bdk"])
    attn = _attention(qa, ka, va, p["bias3"], p["sink"], b, s)
    retn = _retention(qr, kr, vr, gr, p["retn"], b, s)
    xmid, afft, *h2 = _merge(attn, retn, ga, gt, x2, p["wba"], p["wbr"], p["wo"], p["g2"], p["wr"])
    n = b * s
    cap = max(1, EC_CAPACITY_FACTOR * n // N_EXPERTS)
    total = N_EXPERTS * cap
    idx, slots, cs, ce = _select(afft, cap)
    gate, dst = _slot_values(afft, slots, idx, cap)
    xs = [_sc_gather(piece, idx) for piece in h2]
    outs = _expert_ffn(xs, gate.reshape(total // FFN_ROWS, 1, FFN_ROWS), p["w1"], p["w3"], p["w2"], p["layer"])
    by_token = [_sc_scatter(o, dst.reshape(-1), total + COMBINE_WINDOW) for o in outs]
    return _combine(xmid, cs, ce, by_token, total)


def kernel(x_prompt, x_sample, norm_mix_g, w_in, q_norm_g, k_norm_g, attn_sink, rel_bias, retn_decay_logit, retn_norm_g, w_branch_attn, w_branch_retn, w_out, norm_ffn_g, w_router, w_exp_gate, w_exp_up, w_exp_down):
    depth = w_in.shape[0]
    perm = _head_perm()
    bias3 = _attn_bias_tables(rel_bias)
    bdq = jnp.asarray(np.kron(np.eye(ATTN_HEADS), np.ones((ATTN_HEAD_DIM, ATTN_HEAD_DIM))), BF16)
    bdk = jnp.asarray(np.kron(np.eye(ATTN_KV_HEADS), np.ones((ATTN_HEAD_DIM, ATTN_HEAD_DIM))), BF16)
    layers = []
    for l in range(depth):
        w = w_in[l]
        w = jnp.concatenate([w[:, :ATTN_WIDTH][:, perm], w[:, ATTN_WIDTH:]], axis=1).astype(BF16)
        wr = jnp.pad(w_router[l], ((0, 0), (0, LANES - N_EXPERTS))).astype(BF16)
        layers.append(dict(
            g1=norm_mix_g[l].astype(F32)[None], w_in=w,
            qg=(jnp.tile(q_norm_g[l].astype(F32), ATTN_HEADS) * (ATTN_HEAD_DIM ** -0.5))[None],
            kg=jnp.tile(k_norm_g[l].astype(F32), ATTN_KV_HEADS)[None],
            bdq=bdq, bdk=bdk, bias3=bias3, sink=attn_sink[l].astype(F32),
            retn=_retention_tables(retn_decay_logit[l], retn_norm_g[l]),
            wba=w_branch_attn[l][perm, :].astype(BF16), wbr=w_branch_retn[l].astype(BF16),
            wo=w_out[l].astype(BF16), g2=norm_ffn_g[l].astype(F32)[None], wr=wr,
            w1=w_exp_gate, w3=w_exp_up, w2=w_exp_down, layer=l))

    def trunk(x):
        b, s, d = x.shape
        x2 = x.reshape(b * s, d)
        for p in layers:
            x2 = _layer(x2, b, s, p)
        return x2.reshape(b, s, d)

    return (trunk(x_prompt), trunk(x_sample))
```

```python
import functools
import math

import numpy as np
import jax
import jax.numpy as jnp
from jax import lax
from jax.experimental import pallas as pl
from jax.experimental.pallas import tpu as pltpu
from jax.experimental.pallas import tpu_sc as plsc

D_MODEL = 1024
ATTN_HEADS = 8
ATTN_KV_HEADS = 2
ATTN_HEAD_DIM = 64
WINDOW = 128
BLOCK = 128
REL_BUCKETS = 32
REL_MAX_DIST = 128
RET_HEADS = 4
RET_DIM = 128
N_EXPERTS = 16
EC_CAPACITY_FACTOR = 2
EXPERT_FF = 1024
EPS = 1e-6

ATTN_WIDTH = ATTN_HEADS * ATTN_HEAD_DIM
KV_WIDTH = ATTN_KV_HEADS * ATTN_HEAD_DIM
RET_WIDTH = RET_HEADS * RET_DIM
IN_SPLITS = (ATTN_WIDTH, KV_WIDTH, KV_WIDTH, RET_WIDTH, RET_WIDTH, RET_WIDTH, RET_WIDTH, D_MODEL, D_MODEL)
IN_OFFSETS = tuple(int(o) for o in np.cumsum((0,) + IN_SPLITS))

LANES = 128
VMEM_LIMIT_BYTES = 56 * 1024 * 1024

TOKEN_TILE = 512
MERGE_ROWS = 256
IN_PROJ_TILE = 1024
ATTN_QUERIES = 512
RET_CHUNK = 256
RET_STEP = 1024
FFN_ROWS = 1024
SELECT_BLOCKS = 128
SELECT_SLOTS = 1024
SC_WINDOW = 128
SC_ROW = 256
PACKED_WIDTH = D_MODEL // 2
SC_PIECES = PACKED_WIDTH // SC_ROW
COMBINE_TOKENS = 512
COMBINE_WINDOW = 1280
COMBINE_SUB = 128
COMBINE_SUBWIN = 384
COMBINE_BUFFERS = 3

F32 = jnp.float32
BF16 = jnp.bfloat16
NEG = -1e30


def _cparams(n_axes, vmem=VMEM_LIMIT_BYTES):
    return pltpu.CompilerParams(dimension_semantics=("arbitrary",) * n_axes, vmem_limit_bytes=vmem)


def _sigmoid(x):
    return 0.5 * jnp.tanh(0.5 * x) + 0.5


HIGH_HALF = -65536


def _pack_rows(x):
    bits = pltpu.bitcast(x.astype(BF16).astype(F32), jnp.int32)
    words = lax.shift_right_logical(bits[:, :PACKED_WIDTH], 16) | (bits[:, PACKED_WIDTH:] & HIGH_HALF)
    return [words[:, c * SC_ROW:(c + 1) * SC_ROW] for c in range(SC_PIECES)]


def _unpack_rows(pieces):
    low = [pltpu.bitcast(lax.shift_left(w, 16), F32) for w in pieces]
    high = [pltpu.bitcast(w & HIGH_HALF, F32) for w in pieces]
    return jnp.concatenate(low + high, axis=1).astype(BF16)


def _in_proj_kernel(x_ref, g_ref, w_ref, qg_ref, kg_ref, bdq_ref, bdk_ref,
                    qa_ref, ka_ref, va_ref, qr_ref, kr_ref, vr_ref, gr_ref, ga_ref, gt_ref):
    x = x_ref[...]
    ms = jnp.mean(x * x, axis=-1, keepdims=True)
    h = (x * lax.rsqrt(ms + EPS) * g_ref[...]).astype(BF16)

    def mm(k):
        return jnp.dot(h, w_ref[:, IN_OFFSETS[k]:IN_OFFSETS[k + 1]], preferred_element_type=F32)

    def head_norm(t, bd_ref, gain_ref):
        ss = jnp.dot((t * t).astype(BF16), bd_ref[...], preferred_element_type=F32)
        return t * lax.rsqrt(ss * (1.0 / ATTN_HEAD_DIM) + EPS) * gain_ref[...]

    qa_ref[...] = head_norm(mm(0), bdq_ref, qg_ref).astype(BF16)
    ka_ref[...] = head_norm(mm(1), bdk_ref, kg_ref).astype(BF16)
    for k, ref in ((2, va_ref), (3, qr_ref), (4, kr_ref), (5, vr_ref), (6, gr_ref), (7, ga_ref), (8, gt_ref)):
        ref[...] = mm(k).astype(BF16)


def _in_proj(x2, g, w, qg, kg, bdq, bdk):
    n = x2.shape[0]
    tm = IN_PROJ_TILE
    full = lambda a: pl.BlockSpec(a.shape, lambda i: (0,) * a.ndim, pipeline_mode=pl.Buffered(1))
    widths = IN_SPLITS
    return pl.pallas_call(
        _in_proj_kernel,
        grid=(n // tm,),
        in_specs=[pl.BlockSpec((tm, D_MODEL), lambda i: (i, 0)), full(g), full(w), full(qg), full(kg),
                  full(bdq), full(bdk)],
        out_specs=[pl.BlockSpec((tm, wd), lambda i: (i, 0)) for wd in widths],
        out_shape=[jax.ShapeDtypeStruct((n, wd), BF16) for wd in widths],
        compiler_params=_cparams(1),
        name="in_proj",
    )(x2, g, w, qg, kg, bdq, bdk)


def _attn_kernel(sink_ref, q_ref, kp_ref, kc_ref, kn_ref, vp_ref, vc_ref, vn_ref, bias_ref, o_ref, *, nsteps):
    nq = ATTN_HEADS // 2
    ni = pl.program_id(1)
    k = jnp.concatenate([kp_ref[...], kc_ref[...], kn_ref[...]], axis=0)
    v = jnp.concatenate([vp_ref[...], vc_ref[...], vn_ref[...]], axis=0)
    low = lax.broadcasted_iota(jnp.int32, k.shape, 1) < ATTN_HEAD_DIM
    zero = jnp.zeros_like(k)
    k_lo, k_hi = jnp.where(low, k, zero), jnp.where(low, zero, k)
    v_lo, v_hi = jnp.where(low, v, zero), jnp.where(low, zero, v)
    nk = 3 * BLOCK
    low_o = lax.broadcasted_iota(jnp.int32, (BLOCK, LANES), 1) < ATTN_HEAD_DIM
    key_low = lax.broadcasted_iota(jnp.int32, (2 * nk, LANES), 0) < nk
    lane_low = lax.broadcasted_iota(jnp.int32, (2 * nk, LANES), 1) < ATTN_HEAD_DIM
    ones_bd = (key_low == lane_low).astype(BF16)
    nsub = ATTN_QUERIES // BLOCK
    for sb in range(nsub):
        rows = slice(sb * BLOCK, (sb + 1) * BLOCK)
        keys = slice(sb * BLOCK, sb * BLOCK + nk)
        q = q_ref[rows, :]
        qs = jnp.concatenate([q[:, j * LANES:(j + 1) * LANES] for j in range(nq)], axis=0)
        kbd = jnp.concatenate([k_lo[keys], k_hi[keys]], axis=0)
        vbd = jnp.concatenate([v_lo[keys], v_hi[keys]], axis=0)
        s = lax.dot_general(qs, kbd, (((1,), (1,)), ((), ())), preferred_element_type=F32)
        if sb == 0:
            table = jnp.where(ni == 0, 0, 1)
        elif sb == nsub - 1:
            table = jnp.where(ni == nsteps - 1, 2, 1)
        else:
            table = 1
        s = s + bias_ref[table]
        probs, sink_terms = [], []
        for j in range(nq):
            row_p, row_sink = [], []
            for half in range(2):
                sj = s[j * BLOCK:(j + 1) * BLOCK, half * nk:(half + 1) * nk]
                sk = sink_ref[j + nq * half]
                m = jnp.maximum(jnp.max(sj, axis=-1, keepdims=True), sk)
                row_p.append(jnp.exp(sj - m).astype(BF16))
                row_sink.append(jnp.exp(sk - m))
            probs.append(jnp.concatenate(row_p, axis=1))
            sink_terms.append(jnp.where(low_o, row_sink[0], row_sink[1]))
        pm = jnp.concatenate(probs, axis=0)
        od = jnp.dot(pm, jnp.concatenate([vbd, ones_bd], axis=1), preferred_element_type=F32)
        o = od[:, :LANES] / (od[:, LANES:] + jnp.concatenate(sink_terms, axis=0))
        for j in range(nq):
            o_ref[rows, j * LANES:(j + 1) * LANES] = o[j * BLOCK:(j + 1) * BLOCK].astype(BF16)


def _attention(qa, ka, va, bias3, sink, b, s):
    tq = ATTN_QUERIES
    per = tq // BLOCK
    nb = s // BLOCK
    nsteps = s // tq
    assert s % tq == 0 and nb >= 2
    n = b * s
    main = lambda wd: pl.BlockSpec((tq, wd), lambda bi, ni: (bi * nsteps + ni, 0))
    prev = pl.BlockSpec((BLOCK, KV_WIDTH), lambda bi, ni: (bi * nb + jnp.maximum(ni * per - 1, 0), 0))
    nxt = pl.BlockSpec((BLOCK, KV_WIDTH), lambda bi, ni: (bi * nb + jnp.minimum(ni * per + per, nb - 1), 0))
    return pl.pallas_call(
        functools.partial(_attn_kernel, nsteps=nsteps),
        grid=(b, nsteps),
        in_specs=[pl.BlockSpec(memory_space=pltpu.SMEM), main(ATTN_WIDTH),
                  prev, main(KV_WIDTH), nxt, prev, main(KV_WIDTH), nxt,
                  pl.BlockSpec(bias3.shape, lambda bi, ni: (0, 0, 0))],
        out_specs=main(ATTN_WIDTH),
        out_shape=jax.ShapeDtypeStruct((n, ATTN_WIDTH), BF16),
        compiler_params=_cparams(2),
        name="attn",
    )(sink, qa, ka, ka, ka, va, va, va, bias3)


def _retn_kernel(dec_ref, q_ref, k_ref, v_ref, g_ref, dmask_ref, rowf_ref, rowb_ref, wkf_ref, wkb_ref, ng_ref,
                 o_ref, tstore, uf, tb, *, nsteps):
    p = pl.program_id(1)
    n = pl.program_id(2)
    cr = RET_CHUNK
    per = RET_STEP // cr
    tn = (((0,), (0,)), ((), ()))
    nt = (((1,), (1,)), ((), ()))
    hs = lambda h: slice(h * RET_DIM, (h + 1) * RET_DIM)

    @pl.when(p == 0)
    def _():
        @pl.when(n == 0)
        def _():
            tb[...] = jnp.zeros_like(tb)
        first_chunk = (nsteps - 1 - n) * per
        for sub in reversed(range(per)):
            rows = slice(sub * cr, (sub + 1) * cr)
            for h in range(RET_HEADS):
                tstore[first_chunk + sub, h] = tb[h].astype(BF16)
                kw = (k_ref[rows, hs(h)].astype(F32) * wkb_ref[h]).astype(BF16)
                upd = lax.dot_general(kw, v_ref[rows, hs(h)], tn, preferred_element_type=F32)
                tb[h] = dec_ref[RET_HEADS + h] * tb[h] + upd

    @pl.when(p == 1)
    def _():
        @pl.when(n == 0)
        def _():
            uf[...] = jnp.zeros_like(uf)
        for sub in range(per):
            rows = slice(sub * cr, (sub + 1) * cr)
            for h in range(RET_HEADS):
                qh = q_ref[rows, hs(h)]
                kh = k_ref[rows, hs(h)]
                vh = v_ref[rows, hs(h)]
                sc = lax.dot_general(qh, kh, nt, preferred_element_type=F32) * dmask_ref[h]
                intra = jnp.dot(sc.astype(BF16), vh, preferred_element_type=F32)
                states = jnp.concatenate([uf[h].astype(BF16), tstore[n * per + sub, h]], axis=1)
                cross = jnp.dot(qh, states, preferred_element_type=F32)
                o = intra + cross[:, :RET_DIM] * rowf_ref[h] + cross[:, RET_DIM:] * rowb_ref[h]
                mu = jnp.mean(o, axis=-1, keepdims=True)
                d = o - mu
                var = jnp.mean(d * d, axis=-1, keepdims=True)
                on = d * lax.rsqrt(var + EPS) * ng_ref[h:h + 1, :]
                gate = g_ref[rows, hs(h)].astype(F32)
                o_ref[rows, hs(h)] = (gate * _sigmoid(gate) * on).astype(BF16)
                kw = (kh.astype(F32) * wkf_ref[h]).astype(BF16)
                uf[h] = dec_ref[h] * uf[h] + lax.dot_general(kw, vh, tn, preferred_element_type=F32)


def _retention(qr, kr, vr, gr, tables, b, s):
    dec, dmask, rowf, rowb, wkf, wkb, ng = tables
    rs = RET_STEP
    nsteps = s // rs
    nc = s // RET_CHUNK
    assert s % rs == 0
    n = b * s
    full = lambda a: pl.BlockSpec(a.shape, lambda bi, pi, ni: (0,) * a.ndim)
    fwd_spec = pl.BlockSpec((rs, RET_WIDTH), lambda bi, pi, ni: (bi * nsteps + ni * pi, 0))
    kv_spec = pl.BlockSpec((rs, RET_WIDTH),
                           lambda bi, pi, ni: (bi * nsteps + ni * pi + (1 - pi) * (nsteps - 1 - ni), 0))
    return pl.pallas_call(
        functools.partial(_retn_kernel, nsteps=nsteps),
        grid=(b, 2, nsteps),
        in_specs=[pl.BlockSpec(memory_space=pltpu.SMEM), fwd_spec, kv_spec, kv_spec, fwd_spec,
                  full(dmask), full(rowf), full(rowb), full(wkf), full(wkb), full(ng)],
        out_specs=fwd_spec,
        out_shape=jax.ShapeDtypeStruct((n, RET_WIDTH), BF16),
        scratch_shapes=[pltpu.VMEM((nc, RET_HEADS, RET_DIM, RET_DIM), BF16),
                        pltpu.VMEM((RET_HEADS, RET_DIM, RET_DIM), F32),
                        pltpu.VMEM((RET_HEADS, RET_DIM, RET_DIM), F32)],
        compiler_params=_cparams(3),
        name="retention",
    )(dec, qr, kr, vr, gr, dmask, rowf, rowb, wkf, wkb, ng)


def _merge_kernel(attn_ref, retn_ref, ga_ref, gr_ref, x_ref, wba_ref, wbr_ref, wo_ref, g2_ref, wr_ref,
                  xmid_ref, afft_ref, *h2_refs):
    for sub in range(TOKEN_TILE // MERGE_ROWS):
        rows = slice(sub * MERGE_ROWS, (sub + 1) * MERGE_ROWS)
        a = jnp.dot(attn_ref[rows, :], wba_ref[...], preferred_element_type=F32)
        r = jnp.dot(retn_ref[rows, :], wbr_ref[...], preferred_element_type=F32)
        merged = _sigmoid(ga_ref[rows, :].astype(F32)) * a + _sigmoid(gr_ref[rows, :].astype(F32)) * r
        xn = x_ref[rows, :] + jnp.dot(merged.astype(BF16), wo_ref[...], preferred_element_type=F32)
        xmid_ref[rows, :] = xn
        ms = jnp.mean(xn * xn, axis=-1, keepdims=True)
        h2 = xn * lax.rsqrt(ms + EPS) * g2_ref[...]
        for ref, words in zip(h2_refs, _pack_rows(h2)):
            ref[rows, :] = words
        logits = jnp.dot(h2.astype(BF16), wr_ref[...], preferred_element_type=F32)
        real = lax.broadcasted_iota(jnp.int32, logits.shape, 1) < N_EXPERTS
        logits = jnp.where(real, logits, -jnp.inf)
        m = jnp.max(logits, axis=-1, keepdims=True)
        ex = jnp.exp(logits - m)
        aff = ex / jnp.sum(ex, axis=-1, keepdims=True)
        afft_ref[:, rows] = aff.T[:N_EXPERTS, :]


def _merge(attn, retn, ga, gr, x2, wba, wbr, wo, g2, wr):
    n = x2.shape[0]
    tm = TOKEN_TILE
    full = lambda a: pl.BlockSpec(a.shape, lambda i: (0,) * a.ndim)
    row = lambda wd: pl.BlockSpec((tm, wd), lambda i: (i, 0))
    return pl.pallas_call(
        _merge_kernel,
        grid=(n // tm,),
        in_specs=[row(ATTN_WIDTH), row(RET_WIDTH), row(D_MODEL), row(D_MODEL), row(D_MODEL),
                  full(wba), full(wbr), full(wo), full(g2), full(wr)],
        out_specs=[row(D_MODEL), pl.BlockSpec((N_EXPERTS, tm), lambda i: (0, i))] + [row(SC_ROW)] * SC_PIECES,
        out_shape=[jax.ShapeDtypeStruct((n, D_MODEL), F32), jax.ShapeDtypeStruct((N_EXPERTS, n), F32)]
        + [jax.ShapeDtypeStruct((n, SC_ROW), jnp.int32)] * SC_PIECES,
        compiler_params=_cparams(1),
        name="merge",
    )(attn, retn, ga, gr, x2, wba, wbr, wo, g2, wr)


def _select_kernel(aff_ref, u_ref, ls_ref, idx_ref, slot_ref, cs_ref, ce_ref,
                   thr, selbuf, cnt, csr, rank, pieces, offi, *, cap, tb):
    ps = pl.program_id(0)
    e = pl.program_id(1)
    j = pl.program_id(2)
    nblk = SELECT_BLOCKS
    pc = SELECT_SLOTS

    def cumsum(vals):
        inb = jnp.dot(vals.astype(BF16), u_ref[...], preferred_element_type=F32)
        tot = jnp.broadcast_to(inb[:, tb - 1:tb], (nblk, LANES))
        off = jnp.dot(ls_ref[...], tot, preferred_element_type=F32, precision=lax.Precision.HIGHEST)
        return inb, off[:, 0:1], tot[:, 0:1]

    @pl.when(jnp.logical_and(ps == 0, jnp.logical_and(e == 0, j == 0)))
    def _():
        def bit_step(t, curs):
            bit = jnp.left_shift(jnp.int32(1), 30 - t)
            out = []
            for x in range(N_EXPERTS):
                cand = curs[x] | bit
                n_ge = jnp.sum((pltpu.bitcast(aff_ref[x], jnp.int32) >= cand).astype(jnp.int32), keepdims=True)
                out.append(jnp.where(n_ge >= cap, cand, curs[x]))
            return tuple(out)

        found = lax.fori_loop(0, 31, bit_step, tuple(jnp.zeros((1, 1), jnp.int32) for _ in range(N_EXPERTS)))
        for x in range(N_EXPERTS):
            thr[x] = jnp.broadcast_to(found[x], thr.shape[1:])

    @pl.when(jnp.logical_and(ps == 0, j == 0))
    def _():
        bits = pltpu.bitcast(aff_ref[e], jnp.int32)
        limit = thr[e][0:1, 0:1]
        gt = bits > limit
        eq = bits == limit
        need = (cap - jnp.sum(gt.astype(jnp.int32), keepdims=True)).astype(F32)
        eqf = eq.astype(F32)
        eq_in, eq_off, _ = cumsum(eqf)
        eq_rank = eq_in + eq_off - eqf
        sel = jnp.logical_or(gt, jnp.logical_and(eq, eq_rank < need)).astype(F32)
        selbuf[e] = sel.astype(BF16)

        @pl.when(e == 0)
        def _():
            cnt[...] = sel

        @pl.when(e > 0)
        def _():
            cnt[...] = cnt[...] + sel

    @pl.when(jnp.logical_and(ps == 1, j == 0))
    def _():
        @pl.when(e == 0)
        def _():
            c = cnt[...]
            c_in, c_off, _ = cumsum(c)
            start = c_in + c_off - c
            csr[...] = start
            cs_ref[...] = start.astype(jnp.int32)
            ce_ref[...] = (start + c).astype(jnp.int32)
            rank[...] = jnp.zeros_like(rank)

        sel = selbuf[e].astype(F32)
        s_in, s_off, s_tot = cumsum(sel)
        count_t = (s_in + s_off).T
        high = jnp.floor(count_t * (1.0 / 256.0))
        pieces[0] = high.astype(BF16)
        pieces[1] = (count_t - 256.0 * high).astype(BF16)
        slot_ref[0] = (csr[...] + rank[...]).astype(jnp.int32)
        rank[...] = rank[...] + sel
        offi[...] = jnp.broadcast_to(s_off + s_tot, (nblk, LANES))

    @pl.when(ps == 1)
    def _():
        slot = (j * pc + lax.broadcasted_iota(jnp.int32, (1, pc), 1)).astype(F32)
        blk = jnp.sum((offi[:, 0:1] <= slot).astype(jnp.int32), axis=0, keepdims=True)
        onehot = (lax.broadcasted_iota(jnp.int32, (nblk, pc), 0) == blk).astype(BF16)
        counts = (256.0 * jnp.dot(pieces[0], onehot, preferred_element_type=F32)
                  + jnp.dot(pieces[1], onehot, preferred_element_type=F32))
        inb = jnp.sum((counts <= slot + 0.5).astype(jnp.int32), axis=0, keepdims=True)
        idx_ref[0] = blk * tb + inb


def _select(afft, cap):
    n = afft.shape[1]
    nblk = SELECT_BLOCKS
    tb = n // nblk
    pc = SELECT_SLOTS
    assert n % nblk == 0 and tb % LANES == 0 and cap % pc == 0 and cap < 65536
    nch = cap // pc
    aff3 = afft.reshape(N_EXPERTS, nblk, tb)
    upper = jnp.asarray(np.triu(np.ones((tb, tb), np.float32)), BF16)
    lstrict = jnp.asarray(np.tril(np.ones((nblk, nblk), np.float32), -1))
    full = lambda a: pl.BlockSpec(a.shape, lambda ps, e, j: (0,) * a.ndim)
    idx_spec = pl.BlockSpec((1, 1, pc), lambda ps, e, j: (ps * (e * nch + j), 0, 0))
    slot_spec = pl.BlockSpec((1, nblk, tb), lambda ps, e, j: (ps * e, 0, 0))
    tok_spec = pl.BlockSpec((nblk, tb), lambda ps, e, j: (0, 0))
    idx, slots, cs, ce = pl.pallas_call(
        functools.partial(_select_kernel, cap=cap, tb=tb),
        grid=(2, N_EXPERTS, nch),
        in_specs=[full(aff3), full(upper), full(lstrict)],
        out_specs=[idx_spec, slot_spec, tok_spec, tok_spec],
        out_shape=[jax.ShapeDtypeStruct((N_EXPERTS * nch, 1, pc), jnp.int32),
                   jax.ShapeDtypeStruct((N_EXPERTS, nblk, tb), jnp.int32),
                   jax.ShapeDtypeStruct((nblk, tb), jnp.int32), jax.ShapeDtypeStruct((nblk, tb), jnp.int32)],
        scratch_shapes=[pltpu.VMEM((N_EXPERTS, 8, LANES), jnp.int32), pltpu.VMEM((N_EXPERTS, nblk, tb), BF16),
                        pltpu.VMEM((nblk, tb), F32), pltpu.VMEM((nblk, tb), F32), pltpu.VMEM((nblk, tb), F32),
                        pltpu.VMEM((2, tb, nblk), BF16), pltpu.VMEM((nblk, LANES), F32)],
        compiler_params=_cparams(3),
        name="select",
    )(aff3, upper, lstrict)
    return idx.reshape(-1), slots.reshape(N_EXPERTS, n), cs.reshape(-1), ce.reshape(-1)


def _slot_values(afft, slots, idx, cap):
    n = afft.shape[1]
    table = jnp.concatenate([lax.bitcast_convert_type(afft, jnp.int32), slots,
                             jnp.zeros((LANES - 2 * N_EXPERTS, n), jnp.int32)], axis=0).T
    rows = _sc_gather(table, idx).reshape(N_EXPERTS, cap, LANES)
    own = jnp.eye(N_EXPERTS, dtype=jnp.int32)[:, None, :]
    gate = lax.bitcast_convert_type(jnp.sum(rows[:, :, :N_EXPERTS] * own, axis=-1), F32)
    dst = jnp.sum(rows[:, :, N_EXPERTS:2 * N_EXPERTS] * own, axis=-1)
    return gate, dst


def _sc_mesh():
    return plsc.VectorSubcoreMesh(core_axis_name="c", subcore_axis_name="s")


def _sc_scatter(rows, idx, m_out):
    m, d = rows.shape
    assert m % SC_WINDOW == 0

    @functools.partial(pl.kernel, out_type=jax.ShapeDtypeStruct((m_out, d), rows.dtype), mesh=_sc_mesh(),
                       name="sc_scatter")
    def scatter(x_hbm, i_hbm, o_hbm):
        def body(x_vmem, i_vmem):
            pltpu.sync_copy(x_vmem, o_hbm.at[i_vmem.at[0]])

        pltpu.emit_pipeline(
            body,
            grid=(m // SC_WINDOW,),
            in_specs=[pl.BlockSpec((SC_WINDOW, d), lambda i: (i, 0)),
                      pl.BlockSpec((1, SC_WINDOW), lambda i: (0, i))],
            out_specs=[],
            core_axis_name=("c", "s"),
            dimension_semantics=(pltpu.PARALLEL,),
        )(x_hbm, i_hbm)

    return scatter(rows, idx.reshape(1, m))


def _sc_gather(table, idx):
    m = idx.shape[0]
    d = table.shape[1]
    assert m % SC_WINDOW == 0

    @functools.partial(pl.kernel, out_type=jax.ShapeDtypeStruct((m, d), table.dtype), mesh=_sc_mesh(),
                       name="sc_gather")
    def gather(x_hbm, i_hbm, o_hbm):
        def body(i_vmem, o_vmem):
            pltpu.sync_copy(x_hbm.at[i_vmem.at[0]], o_vmem)

        pltpu.emit_pipeline(
            body,
            grid=(m // SC_WINDOW,),
            in_specs=[pl.BlockSpec((1, SC_WINDOW), lambda i: (0, i))],
            out_specs=[pl.BlockSpec((SC_WINDOW, d), lambda i: (i, 0))],
            core_axis_name=("c", "s"),
            dimension_semantics=(pltpu.PARALLEL,),
        )(i_hbm, o_hbm)

    return gather(table, idx.reshape(1, m))


def _row_to_col(row):
    n = row.shape[1]
    eye = lax.broadcasted_iota(jnp.int32, (n, n), 0) == lax.broadcasted_iota(jnp.int32, (n, n), 1)
    return jnp.sum(jnp.where(eye, row, jnp.zeros_like(row)), axis=1, keepdims=True)


def _ffn_kernel(gate_ref, x0_ref, x1_ref, w1_hbm, w3_hbm, w2_hbm, o0_ref, o1_ref, wstage, w1b, w3b, w2b, wsem,
                *, layer):
    e = pl.program_id(0)
    i = pl.program_id(1)

    def weight_copies(expert):
        return [pltpu.make_async_copy(w_hbm.at[layer, expert], wstage.at[k], wsem.at[k])
                for k, w_hbm in enumerate((w1_hbm, w3_hbm, w2_hbm))]

    @pl.when(i == 0)
    def _():
        @pl.when(e == 0)
        def _():
            for cp in weight_copies(0):
                cp.start()
        for cp in weight_copies(e):
            cp.wait()
        w1b[...] = wstage[0].astype(BF16)
        w3b[...] = wstage[1].astype(BF16)
        w2b[...] = wstage[2].astype(BF16)

    @pl.when(jnp.logical_and(i == 1, e + 1 < N_EXPERTS))
    def _():
        for cp in weight_copies(e + 1):
            cp.start()

    xs = _unpack_rows([x0_ref[...], x1_ref[...]])
    hg = jnp.dot(xs, w1b[...], preferred_element_type=F32)
    hu = jnp.dot(xs, w3b[...], preferred_element_type=F32)
    hid = (hg * _sigmoid(hg) * hu).astype(BF16)
    out = jnp.dot(hid, w2b[...], preferred_element_type=F32) * _row_to_col(gate_ref[0])
    for ref, words in zip((o0_ref, o1_ref), _pack_rows(out)):
        ref[...] = words


def _expert_ffn(xs, gate3, w1, w3, w2, layer):
    m = xs[0].shape[0]
    rows = FFN_ROWS
    nt = m // (N_EXPERTS * rows)
    assert SC_PIECES == 2 and nt >= 2
    piece = pl.BlockSpec((rows, SC_ROW), lambda e, i: (e * nt + i, 0))
    per_step = gate3.shape[2] // rows
    gspec = pl.BlockSpec((1, 1, rows), lambda e, i: ((e * nt + i) // per_step, 0, (e * nt + i) % per_step))
    any_spec = pl.BlockSpec(memory_space=pl.ANY)
    return pl.pallas_call(
        functools.partial(_ffn_kernel, layer=layer),
        grid=(N_EXPERTS, nt),
        in_specs=[gspec] + [piece] * SC_PIECES + [any_spec] * 3,
        out_specs=[piece] * SC_PIECES,
        out_shape=[jax.ShapeDtypeStruct((m, SC_ROW), jnp.int32)] * SC_PIECES,
        scratch_shapes=[pltpu.VMEM((3, D_MODEL, EXPERT_FF), F32), pltpu.VMEM((D_MODEL, EXPERT_FF), BF16),
                        pltpu.VMEM((D_MODEL, EXPERT_FF), BF16), pltpu.VMEM((EXPERT_FF, D_MODEL), BF16),
                        pltpu.SemaphoreType.DMA((3,))],
        compiler_params=_cparams(2),
        name="expert_ffn",
    )(gate3, *xs, w1, w3, w2)


def _combine_kernel(tsub_ref, x_ref, cs_ref, ce_ref, r0_hbm, r1_hbm, o_ref, rbuf, obuf, rows16, sems, osem,
                    *, ntile, total):
    pieces_hbm = (r0_hbm, r1_hbm)
    win = COMBINE_WINDOW
    sub = COMBINE_SUB
    subwin = COMBINE_SUBWIN
    per = COMBINE_TOKENS // sub
    i = pl.program_id(0)
    slot = lax.rem(i, COMBINE_BUFFERS)

    def window_start(t):
        return pl.multiple_of((tsub_ref[t * per] // 16) * 16, 16)

    def copies(t, b):
        s = window_start(t)
        return [pltpu.make_async_copy(pieces_hbm[c].at[pl.ds(s, win)], rbuf.at[b, c], sems.at[b, c])
                for c in range(SC_PIECES)]

    ahead = COMBINE_BUFFERS - 1

    @pl.when(i == 0)
    def _():
        for t in range(min(ahead, ntile)):
            for cp in copies(t, t):
                cp.start()

    @pl.when(i + ahead < ntile)
    def _():
        for cp in copies(i + ahead, lax.rem(i + ahead, COMBINE_BUFFERS)):
            cp.start()

    for cp in copies(i, slot):
        cp.wait()

    first = [_row_to_col(cs_ref[0][:, g * sub:(g + 1) * sub]) for g in range(per)]
    last = [_row_to_col(ce_ref[0][:, g * sub:(g + 1) * sub]) for g in range(per)]

    def zero_unwritten(words, base):
        written = (base + lax.broadcasted_iota(jnp.int32, (win, 1), 0)) < total
        return jnp.where(written, words, 0)

    def owner_matrix(g, base, width):
        r = base + lax.broadcasted_iota(jnp.int32, (1, width), 1)
        return jnp.logical_and(first[g] <= r, r < last[g]).astype(BF16)

    s0 = window_start(i)
    tail = s0 + win > total

    @pl.when(tail)
    def _():
        rows16[...] = _unpack_rows([zero_unwritten(rbuf[slot, c], s0) for c in range(SC_PIECES)])

    @pl.when(jnp.logical_not(tail))
    def _():
        rows16[...] = _unpack_rows([rbuf[slot, c] for c in range(SC_PIECES)])

    offsets = []
    fits = None
    for g in range(per):
        off = (tsub_ref[i * per + g] // 16) * 16 - s0
        ok = jnp.logical_and(tsub_ref[i * per + g + 1] - s0 <= off + subwin, off + subwin <= win)
        fits = ok if fits is None else jnp.logical_and(fits, ok)
        offsets.append(off)

    @pl.when(fits)
    def _():
        for g in range(per):
            tokens = slice(g * sub, (g + 1) * sub)
            off = pl.multiple_of(offsets[g], 16)
            q = owner_matrix(g, s0 + off, subwin)
            o_ref[tokens, :] = x_ref[tokens, :] + jnp.dot(q, rows16[pl.ds(off, subwin), :],
                                                         preferred_element_type=F32)

    @pl.when(jnp.logical_not(fits))
    def _():
        def everyone(base):
            return jnp.concatenate([owner_matrix(g, base, win) for g in range(per)], axis=0)

        y = x_ref[...] + jnp.dot(everyone(s0), rows16[...], preferred_element_type=F32)
        n_extra = jnp.maximum(tsub_ref[(i + 1) * per] - (s0 + win) + win - 1, 0) // win

        def extra(k, acc):
            base = pl.multiple_of(s0 + (k + 1) * win, 16)
            cps = [pltpu.make_async_copy(pieces_hbm[c].at[pl.ds(base, win)], obuf.at[c], osem.at[c])
                   for c in range(SC_PIECES)]
            for cp in cps:
                cp.start()
            for cp in cps:
                cp.wait()
            rows = _unpack_rows([zero_unwritten(obuf[c], base) for c in range(SC_PIECES)])
            return acc + jnp.dot(everyone(base), rows, preferred_element_type=F32)

        o_ref[...] = lax.fori_loop(0, n_extra, extra, y)


def _combine(xmid, cs, ce, pieces, total):
    n = xmid.shape[0]
    tt = COMBINE_TOKENS
    win = COMBINE_WINDOW
    ntile = n // tt
    tsub = jnp.concatenate([cs[::COMBINE_SUB], jnp.full((1,), total, jnp.int32)])
    cs3 = cs.reshape(ntile, 1, tt)
    ce3 = ce.reshape(ntile, 1, tt)
    any_spec = pl.BlockSpec(memory_space=pl.ANY)
    tok = pl.BlockSpec((1, 1, tt), lambda i, ts: (i, 0, 0))
    grid_spec = pltpu.PrefetchScalarGridSpec(
        num_scalar_prefetch=1,
        grid=(ntile,),
        in_specs=[pl.BlockSpec((tt, D_MODEL), lambda i, ts: (i, 0)), tok, tok] + [any_spec] * SC_PIECES,
        out_specs=pl.BlockSpec((tt, D_MODEL), lambda i, ts: (i, 0)),
        scratch_shapes=[pltpu.VMEM((COMBINE_BUFFERS, SC_PIECES, win, SC_ROW), jnp.int32),
                        pltpu.VMEM((SC_PIECES, win, SC_ROW), jnp.int32), pltpu.VMEM((win, D_MODEL), BF16),
                        pltpu.SemaphoreType.DMA((COMBINE_BUFFERS, SC_PIECES)), pltpu.SemaphoreType.DMA((SC_PIECES,))],
    )
    return pl.pallas_call(
        functools.partial(_combine_kernel, ntile=ntile, total=total),
        grid_spec=grid_spec,
        out_shape=jax.ShapeDtypeStruct((n, D_MODEL), F32),
        compiler_params=_cparams(1),
        name="combine",
    )(tsub, xmid, cs3, ce3, *pieces)


def _t5_bucket(rel):
    half = REL_BUCKETS // 2
    max_exact = half // 2
    base = np.where(rel > 0, half, 0)
    n = np.abs(rel)
    large = max_exact + (np.log(np.maximum(n, 1) / max_exact) / math.log(REL_MAX_DIST / max_exact)
                         * (half - max_exact)).astype(np.int32)
    large = np.minimum(large, half - 1)
    return (base + np.where(n < max_exact, n, large)).astype(np.int32)


def _head_perm():
    nq = ATTN_HEADS // 2
    cols = []
    for j in range(nq):
        for half in range(2):
            h = j + nq * half
            cols.extend(range(h * ATTN_HEAD_DIM, (h + 1) * ATTN_HEAD_DIM))
    return np.asarray(cols, np.int32)


def _attn_bias_tables(rel_bias):
    q_pos = np.arange(BLOCK)[:, None]
    k_off = np.arange(3 * BLOCK)[None, :] - BLOCK
    rel = k_off - q_pos
    in_window = np.abs(rel) <= WINDOW
    onehot = jnp.asarray(_t5_bucket(rel)[:, :, None] == np.arange(REL_BUCKETS)[None, None, :], F32)
    bias = jnp.einsum("qkb,bh->hqk", onehot, rel_bias.astype(F32), precision=lax.Precision.HIGHEST)
    col = np.arange(3 * BLOCK)[None, :]
    tables = []
    for valid in (col >= BLOCK, np.ones_like(col, bool), col < 2 * BLOCK):
        t = jnp.where(jnp.asarray(in_window & valid)[None], bias, NEG)
        nq = ATTN_HEADS // 2
        rows = [jnp.concatenate([t[j], t[j + nq]], axis=1) for j in range(nq)]
        tables.append(jnp.concatenate(rows, axis=0))
    return jnp.stack(tables)


def _retention_tables(decay_logit, norm_g):
    cr = RET_CHUNK
    lg = jax.nn.log_sigmoid(decay_logit.astype(F32))
    lgf, lgb = lg[0][:, None, None], lg[1][:, None, None]
    pos = np.arange(cr, dtype=np.float32)
    dist = pos[:, None] - pos[None, :]
    scale = RET_DIM ** -0.5
    dmask = jnp.where(jnp.asarray(dist >= 0)[None],
                      jnp.exp(lgf * np.maximum(dist, 0.0)[None]),
                      jnp.exp(lgb * np.maximum(-dist, 0.0)[None])) * scale
    col = lambda v: jnp.broadcast_to(v[:, :, None], (RET_HEADS, cr, RET_DIM))
    rowf = col(jnp.exp(lg[0][:, None] * pos[None]))
    rowb = col(jnp.exp(lg[1][:, None] * (cr - 1.0 - pos)[None]))
    wkf = col(jnp.exp(lg[0][:, None] * (cr - pos)[None]) * scale)
    wkb = col(jnp.exp(lg[1][:, None] * (pos + 1.0)[None]) * scale)
    dec = jnp.concatenate([jnp.exp(lg[0] * cr), jnp.exp(lg[1] * cr)])
    return dec, dmask, rowf, rowb, wkf, wkb, norm_g.astype(F32)


def _layer(x2, b, s, p):
    qa, ka, va, qr, kr, vr, gr, ga, gt = _in_proj(x2, p["g1"], p["w_in"], p["qg"], p["kg"], p["bdq"], p["bdk"])
    attn = _attention(qa, ka, va, p["bias3"], p["sink"], b, s)
    retn = _retention(qr, kr, vr, gr, p["retn"], b, s)
    xmid, afft, *h2 = _merge(attn, retn, ga, gt, x2, p["wba"], p["wbr"], p["wo"], p["g2"], p["wr"])
    n = b * s
    cap = max(1, EC_CAPACITY_FACTOR * n // N_EXPERTS)
    total = N_EXPERTS * cap
    idx, slots, cs, ce = _select(afft, cap)
    gate, dst = _slot_values(afft, slots, idx, cap)
    xs = [_sc_gather(piece, idx) for piece in h2]
    outs = _expert_ffn(xs, gate.reshape(total // FFN_ROWS, 1, FFN_ROWS), p["w1"], p["w3"], p["w2"], p["layer"])
    by_token = [_sc_scatter(o, dst.reshape(-1), total + COMBINE_WINDOW) for o in outs]
    return _combine(xmid, cs, ce, by_token, total)


def kernel(x_prompt, x_sample, norm_mix_g, w_in, q_norm_g, k_norm_g, attn_sink, rel_bias, retn_decay_logit, retn_norm_g, w_branch_attn, w_branch_retn, w_out, norm_ffn_g, w_router, w_exp_gate, w_exp_up, w_exp_down):
    depth = w_in.shape[0]
    perm = _head_perm()
    bias3 = _attn_bias_tables(rel_bias)
    bdq = jnp.asarray(np.kron(np.eye(ATTN_HEADS), np.ones((ATTN_HEAD_DIM, ATTN_HEAD_DIM))), BF16)
    bdk = jnp.asarray(np.kron(np.eye(ATTN_KV_HEADS), np.ones((ATTN_HEAD_DIM, ATTN_HEAD_DIM))), BF16)
    layers = []
    for l in range(depth):
        w = w_in[l]
        w = jnp.concatenate([w[:, :ATTN_WIDTH][:, perm], w[:, ATTN_WIDTH:]], axis=1).astype(BF16)
        wr = jnp.pad(w_router[l], ((0, 0), (0, LANES - N_EXPERTS))).astype(BF16)
        layers.append(dict(
            g1=norm_mix_g[l].astype(F32)[None], w_in=w,
            qg=(jnp.tile(q_norm_g[l].astype(F32), ATTN_HEADS) * (ATTN_HEAD_DIM ** -0.5))[None],
            kg=jnp.tile(k_norm_g[l].astype(F32), ATTN_KV_HEADS)[None],
            bdq=bdq, bdk=bdk, bias3=bias3, sink=attn_sink[l].astype(F32),
            retn=_retention_tables(retn_decay_logit[l], retn_norm_g[l]),
            wba=w_branch_attn[l][perm, :].astype(BF16), wbr=w_branch_retn[l].astype(BF16),
            wo=w_out[l].astype(BF16), g2=norm_ffn_g[l].astype(F32)[None], wr=wr,
            w1=w_exp_gate, w3=w_exp_up, w2=w_exp_down, layer=l))

    def trunk(x):
        b, s, d = x.shape
        x2 = x.reshape(b * s, d)
        for p in layers:
            x2 = _layer(x2, b, s, p)
        return x2.reshape(b, s, d)

    return (trunk(x_prompt), trunk(x_sample))
```

```python
import functools
import math

import numpy as np
import jax
import jax.numpy as jnp
from jax import lax
from jax.experimental import pallas as pl
from jax.experimental.pallas import tpu as pltpu
from jax.experimental.pallas import tpu_sc as plsc

D_MODEL = 1024
ATTN_HEADS = 8
ATTN_KV_HEADS = 2
ATTN_HEAD_DIM = 64
WINDOW = 128
BLOCK = 128
REL_BUCKETS = 32
REL_MAX_DIST = 128
RET_HEADS = 4
RET_DIM = 128
N_EXPERTS = 16
EC_CAPACITY_FACTOR = 2
EXPERT_FF = 1024
EPS = 1e-6

ATTN_WIDTH = ATTN_HEADS * ATTN_HEAD_DIM
KV_WIDTH = ATTN_KV_HEADS * ATTN_HEAD_DIM
RET_WIDTH = RET_HEADS * RET_DIM
IN_SPLITS = (ATTN_WIDTH, KV_WIDTH, KV_WIDTH, RET_WIDTH, RET_WIDTH, RET_WIDTH, RET_WIDTH, D_MODEL, D_MODEL)
IN_OFFSETS = tuple(int(o) for o in np.cumsum((0,) + IN_SPLITS))

LANES = 128
VMEM_LIMIT_BYTES = 56 * 1024 * 1024

TOKEN_TILE = 512
MERGE_ROWS = 512
IN_PROJ_TILE = 1024
ATTN_QUERIES = 512
RET_CHUNK = 256
RET_STEP = 1024
FFN_ROWS = 1024
SELECT_BLOCKS = 128
SELECT_SLOTS = 1024
SC_WINDOW = 128
SC_ROW = 256
PACKED_WIDTH = D_MODEL // 2
SC_PIECES = PACKED_WIDTH // SC_ROW
COMBINE_TOKENS = 512
COMBINE_WINDOW = 1280
COMBINE_SUB = 128
COMBINE_SUBWIN = 384
COMBINE_BUFFERS = 3

F32 = jnp.float32
BF16 = jnp.bfloat16
NEG = -1e30


def _cparams(n_axes, vmem=VMEM_LIMIT_BYTES):
    return pltpu.CompilerParams(dimension_semantics=("arbitrary",) * n_axes, vmem_limit_bytes=vmem)


def _sigmoid(x):
    return 0.5 * jnp.tanh(0.5 * x) + 0.5


HIGH_HALF = -65536


def _pack_rows(x):
    bits = pltpu.bitcast(x.astype(BF16).astype(F32), jnp.int32)
    words = lax.shift_right_logical(bits[:, :PACKED_WIDTH], 16) | (bits[:, PACKED_WIDTH:] & HIGH_HALF)
    return [words[:, c * SC_ROW:(c + 1) * SC_ROW] for c in range(SC_PIECES)]


def _unpack_rows(pieces):
    low = [pltpu.bitcast(lax.shift_left(w, 16), F32) for w in pieces]
    high = [pltpu.bitcast(w & HIGH_HALF, F32) for w in pieces]
    return jnp.concatenate(low + high, axis=1).astype(BF16)


def _in_proj_kernel(x_ref, g_ref, w_ref, qg_ref, kg_ref, bdq_ref, bdk_ref,
                    qa_ref, ka_ref, va_ref, qr_ref, kr_ref, vr_ref, gr_ref, ga_ref, gt_ref):
    x = x_ref[...]
    ms = jnp.mean(x * x, axis=-1, keepdims=True)
    h = (x * lax.rsqrt(ms + EPS) * g_ref[...]).astype(BF16)

    def mm(k):
        return jnp.dot(h, w_ref[:, IN_OFFSETS[k]:IN_OFFSETS[k + 1]], preferred_element_type=F32)

    def head_norm(t, bd_ref, gain_ref):
        ss = jnp.dot((t * t).astype(BF16), bd_ref[...], preferred_element_type=F32)
        return t * lax.rsqrt(ss * (1.0 / ATTN_HEAD_DIM) + EPS) * gain_ref[...]

    qa_ref[...] = head_norm(mm(0), bdq_ref, qg_ref).astype(BF16)
    ka_ref[...] = head_norm(mm(1), bdk_ref, kg_ref).astype(BF16)
    for k, ref in ((2, va_ref), (3, qr_ref), (4, kr_ref), (5, vr_ref), (6, gr_ref), (7, ga_ref), (8, gt_ref)):
        ref[...] = mm(k).astype(BF16)


def _in_proj(x2, g, w, qg, kg, bdq, bdk):
    n = x2.shape[0]
    tm = IN_PROJ_TILE
    full = lambda a: pl.BlockSpec(a.shape, lambda i: (0,) * a.ndim, pipeline_mode=pl.Buffered(1))
    widths = IN_SPLITS
    return pl.pallas_call(
        _in_proj_kernel,
        grid=(n // tm,),
        in_specs=[pl.BlockSpec((tm, D_MODEL), lambda i: (i, 0)), full(g), full(w), full(qg), full(kg),
                  full(bdq), full(bdk)],
        out_specs=[pl.BlockSpec((tm, wd), lambda i: (i, 0)) for wd in widths],
        out_shape=[jax.ShapeDtypeStruct((n, wd), BF16) for wd in widths],
        compiler_params=_cparams(1),
        name="in_proj",
    )(x2, g, w, qg, kg, bdq, bdk)


def _attn_kernel(sink_ref, q_ref, kp_ref, kc_ref, kn_ref, vp_ref, vc_ref, vn_ref, bias_ref, o_ref, *, nsteps):
    nq = ATTN_HEADS // 2
    ni = pl.program_id(1)
    k = jnp.concatenate([kp_ref[...], kc_ref[...], kn_ref[...]], axis=0)
    v = jnp.concatenate([vp_ref[...], vc_ref[...], vn_ref[...]], axis=0)
    low = lax.broadcasted_iota(jnp.int32, k.shape, 1) < ATTN_HEAD_DIM
    zero = jnp.zeros_like(k)
    k_lo, k_hi = jnp.where(low, k, zero), jnp.where(low, zero, k)
    v_lo, v_hi = jnp.where(low, v, zero), jnp.where(low, zero, v)
    nk = 3 * BLOCK
    low_o = lax.broadcasted_iota(jnp.int32, (BLOCK, LANES), 1) < ATTN_HEAD_DIM
    key_low = lax.broadcasted_iota(jnp.int32, (2 * nk, LANES), 0) < nk
    lane_low = lax.broadcasted_iota(jnp.int32, (2 * nk, LANES), 1) < ATTN_HEAD_DIM
    ones_bd = (key_low == lane_low).astype(BF16)
    nsub = ATTN_QUERIES // BLOCK
    for sb in range(nsub):
        rows = slice(sb * BLOCK, (sb + 1) * BLOCK)
        keys = slice(sb * BLOCK, sb * BLOCK + nk)
        q = q_ref[rows, :]
        qs = jnp.concatenate([q[:, j * LANES:(j + 1) * LANES] for j in range(nq)], axis=0)
        kbd = jnp.concatenate([k_lo[keys], k_hi[keys]], axis=0)
        vbd = jnp.concatenate([v_lo[keys], v_hi[keys]], axis=0)
        s = lax.dot_general(qs, kbd, (((1,), (1,)), ((), ())), preferred_element_type=F32)
        if sb == 0:
            table = jnp.where(ni == 0, 0, 1)
        elif sb == nsub - 1:
            table = jnp.where(ni == nsteps - 1, 2, 1)
        else:
            table = 1
        s = s + bias_ref[table]
        probs, sink_terms = [], []
        for j in range(nq):
            row_p, row_sink = [], []
            for half in range(2):
                sj = s[j * BLOCK:(j + 1) * BLOCK, half * nk:(half + 1) * nk]
                sk = sink_ref[j + nq * half]
                m = jnp.maximum(jnp.max(sj, axis=-1, keepdims=True), sk)
                row_p.append(jnp.exp(sj - m).astype(BF16))
                row_sink.append(jnp.exp(sk - m))
            probs.append(jnp.concatenate(row_p, axis=1))
            sink_terms.append(jnp.where(low_o, row_sink[0], row_sink[1]))
        pm = jnp.concatenate(probs, axis=0)
        od = jnp.dot(pm, jnp.concatenate([vbd, ones_bd], axis=1), preferred_element_type=F32)
        o = od[:, :LANES] / (od[:, LANES:] + jnp.concatenate(sink_terms, axis=0))
        for j in range(nq):
            o_ref[rows, j * LANES:(j + 1) * LANES] = o[j * BLOCK:(j + 1) * BLOCK].astype(BF16)


def _attention(qa, ka, va, bias3, sink, b, s):
    tq = ATTN_QUERIES
    per = tq // BLOCK
    nb = s // BLOCK
    nsteps = s // tq
    assert s % tq == 0 and nb >= 2
    n = b * s
    main = lambda wd: pl.BlockSpec((tq, wd), lambda bi, ni: (bi * nsteps + ni, 0))
    prev = pl.BlockSpec((BLOCK, KV_WIDTH), lambda bi, ni: (bi * nb + jnp.maximum(ni * per - 1, 0), 0))
    nxt = pl.BlockSpec((BLOCK, KV_WIDTH), lambda bi, ni: (bi * nb + jnp.minimum(ni * per + per, nb - 1), 0))
    return pl.pallas_call(
        functools.partial(_attn_kernel, nsteps=nsteps),
        grid=(b, nsteps),
        in_specs=[pl.BlockSpec(memory_space=pltpu.SMEM), main(ATTN_WIDTH),
                  prev, main(KV_WIDTH), nxt, prev, main(KV_WIDTH), nxt,
                  pl.BlockSpec(bias3.shape, lambda bi, ni: (0, 0, 0))],
        out_specs=main(ATTN_WIDTH),
        out_shape=jax.ShapeDtypeStruct((n, ATTN_WIDTH), BF16),
        compiler_params=_cparams(2),
        name="attn",
    )(sink, qa, ka, ka, ka, va, va, va, bias3)


def _retn_kernel(dec_ref, q_ref, k_ref, v_ref, g_ref, dmask_ref, rowf_ref, rowb_ref, wkf_ref, wkb_ref, ng_ref,
                 o_ref, tstore, uf, tb, *, nsteps):
    p = pl.program_id(1)
    n = pl.program_id(2)
    cr = RET_CHUNK
    per = RET_STEP // cr
    tn = (((0,), (0,)), ((), ()))
    nt = (((1,), (1,)), ((), ()))
    hs = lambda h: slice(h * RET_DIM, (h + 1) * RET_DIM)

    @pl.when(p == 0)
    def _():
        @pl.when(n == 0)
        def _():
            tb[...] = jnp.zeros_like(tb)
        first_chunk = (nsteps - 1 - n) * per
        for sub in reversed(range(per)):
            rows = slice(sub * cr, (sub + 1) * cr)
            for h in range(RET_HEADS):
                tstore[first_chunk + sub, h] = tb[h].astype(BF16)
                kw = (k_ref[rows, hs(h)].astype(F32) * wkb_ref[h]).astype(BF16)
                upd = lax.dot_general(kw, v_ref[rows, hs(h)], tn, preferred_element_type=F32)
                tb[h] = dec_ref[RET_HEADS + h] * tb[h] + upd

    @pl.when(p == 1)
    def _():
        @pl.when(n == 0)
        def _():
            uf[...] = jnp.zeros_like(uf)
        for sub in range(per):
            rows = slice(sub * cr, (sub + 1) * cr)
            for h in range(RET_HEADS):
                qh = q_ref[rows, hs(h)]
                kh = k_ref[rows, hs(h)]
                vh = v_ref[rows, hs(h)]
                sc = lax.dot_general(qh, kh, nt, preferred_element_type=F32) * dmask_ref[h]
                intra = jnp.dot(sc.astype(BF16), vh, preferred_element_type=F32)
                states = jnp.concatenate([uf[h].astype(BF16), tstore[n * per + sub, h]], axis=1)
                cross = jnp.dot(qh, states, preferred_element_type=F32)
                o = intra + cross[:, :RET_DIM] * rowf_ref[h] + cross[:, RET_DIM:] * rowb_ref[h]
                mu = jnp.mean(o, axis=-1, keepdims=True)
                d = o - mu
                var = jnp.mean(d * d, axis=-1, keepdims=True)
                on = d * lax.rsqrt(var + EPS) * ng_ref[h:h + 1, :]
                gate = g_ref[rows, hs(h)].astype(F32)
                o_ref[rows, hs(h)] = (gate * _sigmoid(gate) * on).astype(BF16)
                kw = (kh.astype(F32) * wkf_ref[h]).astype(BF16)
                uf[h] = dec_ref[h] * uf[h] + lax.dot_general(kw, vh, tn, preferred_element_type=F32)


def _retention(qr, kr, vr, gr, tables, b, s):
    dec, dmask, rowf, rowb, wkf, wkb, ng = tables
    rs = RET_STEP
    nsteps = s // rs
    nc = s // RET_CHUNK
    assert s % rs == 0
    n = b * s
    full = lambda a: pl.BlockSpec(a.shape, lambda bi, pi, ni: (0,) * a.ndim)
    fwd_spec = pl.BlockSpec((rs, RET_WIDTH), lambda bi, pi, ni: (bi * nsteps + ni * pi, 0))
    kv_spec = pl.BlockSpec((rs, RET_WIDTH),
                           lambda bi, pi, ni: (bi * nsteps + ni * pi + (1 - pi) * (nsteps - 1 - ni), 0))
    return pl.pallas_call(
        functools.partial(_retn_kernel, nsteps=nsteps),
        grid=(b, 2, nsteps),
        in_specs=[pl.BlockSpec(memory_space=pltpu.SMEM), fwd_spec, kv_spec, kv_spec, fwd_spec,
                  full(dmask), full(rowf), full(rowb), full(wkf), full(wkb), full(ng)],
        out_specs=fwd_spec,
        out_shape=jax.ShapeDtypeStruct((n, RET_WIDTH), BF16),
        scratch_shapes=[pltpu.VMEM((nc, RET_HEADS, RET_DIM, RET_DIM), BF16),
                        pltpu.VMEM((RET_HEADS, RET_DIM, RET_DIM), F32),
                        pltpu.VMEM((RET_HEADS, RET_DIM, RET_DIM), F32)],
        compiler_params=_cparams(3),
        name="retention",
    )(dec, qr, kr, vr, gr, dmask, rowf, rowb, wkf, wkb, ng)


def _merge_kernel(attn_ref, retn_ref, ga_ref, gr_ref, x_ref, wba_ref, wbr_ref, wo_ref, g2_ref, wr_ref,
                  xmid_ref, afft_ref, *h2_refs):
    for sub in range(TOKEN_TILE // MERGE_ROWS):
        rows = slice(sub * MERGE_ROWS, (sub + 1) * MERGE_ROWS)
        a = jnp.dot(attn_ref[rows, :], wba_ref[...], preferred_element_type=F32)
        r = jnp.dot(retn_ref[rows, :], wbr_ref[...], preferred_element_type=F32)
        merged = _sigmoid(ga_ref[rows, :].astype(F32)) * a + _sigmoid(gr_ref[rows, :].astype(F32)) * r
        xn = x_ref[rows, :] + jnp.dot(merged.astype(BF16), wo_ref[...], preferred_element_type=F32)
        xmid_ref[rows, :] = xn
        ms = jnp.mean(xn * xn, axis=-1, keepdims=True)
        h2 = xn * lax.rsqrt(ms + EPS) * g2_ref[...]
        for ref, words in zip(h2_refs, _pack_rows(h2)):
            ref[rows, :] = words
        logits = jnp.dot(h2.astype(BF16), wr_ref[...], preferred_element_type=F32)
        real = lax.broadcasted_iota(jnp.int32, logits.shape, 1) < N_EXPERTS
        logits = jnp.where(real, logits, -jnp.inf)
        m = jnp.max(logits, axis=-1, keepdims=True)
        ex = jnp.exp(logits - m)
        aff = ex / jnp.sum(ex, axis=-1, keepdims=True)
        afft_ref[:, rows] = aff.T[:N_EXPERTS, :]


def _merge(attn, retn, ga, gr, x2, wba, wbr, wo, g2, wr):
    n = x2.shape[0]
    tm = TOKEN_TILE
    full = lambda a: pl.BlockSpec(a.shape, lambda i: (0,) * a.ndim)
    row = lambda wd: pl.BlockSpec((tm, wd), lambda i: (i, 0))
    return pl.pallas_call(
        _merge_kernel,
        grid=(n // tm,),
        in_specs=[row(ATTN_WIDTH), row(RET_WIDTH), row(D_MODEL), row(D_MODEL), row(D_MODEL),
                  full(wba), full(wbr), full(wo), full(g2), full(wr)],
        out_specs=[row(D_MODEL), pl.BlockSpec((N_EXPERTS, tm), lambda i: (0, i))] + [row(SC_ROW)] * SC_PIECES,
        out_shape=[jax.ShapeDtypeStruct((n, D_MODEL), F32), jax.ShapeDtypeStruct((N_EXPERTS, n), F32)]
        + [jax.ShapeDtypeStruct((n, SC_ROW), jnp.int32)] * SC_PIECES,
        compiler_params=_cparams(1),
        name="merge",
    )(attn, retn, ga, gr, x2, wba, wbr, wo, g2, wr)


def _select_kernel(aff_ref, u_ref, ls_ref, idx_ref, slot_ref, cs_ref, ce_ref,
                   thr, selbuf, cnt, csr, rank, pieces, offi, *, cap, tb):
    ps = pl.program_id(0)
    e = pl.program_id(1)
    j = pl.program_id(2)
    nblk = SELECT_BLOCKS
    pc = SELECT_SLOTS

    def cumsum(vals):
        inb = jnp.dot(vals.astype(BF16), u_ref[...], preferred_element_type=F32)
        tot = jnp.broadcast_to(inb[:, tb - 1:tb], (nblk, LANES))
        off = jnp.dot(ls_ref[...], tot, preferred_element_type=F32, precision=lax.Precision.HIGHEST)
        return inb, off[:, 0:1], tot[:, 0:1]

    @pl.when(jnp.logical_and(ps == 0, jnp.logical_and(e == 0, j == 0)))
    def _():
        def bit_step(t, curs):
            bit = jnp.left_shift(jnp.int32(1), 30 - t)
            out = []
            for x in range(N_EXPERTS):
                cand = curs[x] | bit
                n_ge = jnp.sum((pltpu.bitcast(aff_ref[x], jnp.int32) >= cand).astype(jnp.int32), keepdims=True)
                out.append(jnp.where(n_ge >= cap, cand, curs[x]))
            return tuple(out)

        found = lax.fori_loop(0, 31, bit_step, tuple(jnp.zeros((1, 1), jnp.int32) for _ in range(N_EXPERTS)))
        for x in range(N_EXPERTS):
            thr[x] = jnp.broadcast_to(found[x], thr.shape[1:])

    @pl.when(jnp.logical_and(ps == 0, j == 0))
    def _():
        bits = pltpu.bitcast(aff_ref[e], jnp.int32)
        limit = thr[e][0:1, 0:1]
        gt = bits > limit
        eq = bits == limit
        need = (cap - jnp.sum(gt.astype(jnp.int32), keepdims=True)).astype(F32)
        eqf = eq.astype(F32)
        eq_in, eq_off, _ = cumsum(eqf)
        eq_rank = eq_in + eq_off - eqf
        sel = jnp.logical_or(gt, jnp.logical_and(eq, eq_rank < need)).astype(F32)
        selbuf[e] = sel.astype(BF16)

        @pl.when(e == 0)
        def _():
            cnt[...] = sel

        @pl.when(e > 0)
        def _():
            cnt[...] = cnt[...] + sel

    @pl.when(jnp.logical_and(ps == 1, j == 0))
    def _():
        @pl.when(e == 0)
        def _():
            c = cnt[...]
            c_in, c_off, _ = cumsum(c)
            start = c_in + c_off - c
            csr[...] = start
            cs_ref[...] = start.astype(jnp.int32)
            ce_ref[...] = (start + c).astype(jnp.int32)
            rank[...] = jnp.zeros_like(rank)

        sel = selbuf[e].astype(F32)
        s_in, s_off, s_tot = cumsum(sel)
        count_t = (s_in + s_off).T
        high = jnp.floor(count_t * (1.0 / 256.0))
        pieces[0] = high.astype(BF16)
        pieces[1] = (count_t - 256.0 * high).astype(BF16)
        slot_ref[0] = (csr[...] + rank[...]).astype(jnp.int32)
        rank[...] = rank[...] + sel
        offi[...] = jnp.broadcast_to(s_off + s_tot, (nblk, LANES))

    @pl.when(ps == 1)
    def _():
        slot = (j * pc + lax.broadcasted_iota(jnp.int32, (1, pc), 1)).astype(F32)
        blk = jnp.sum((offi[:, 0:1] <= slot).astype(jnp.int32), axis=0, keepdims=True)
        onehot = (lax.broadcasted_iota(jnp.int32, (nblk, pc), 0) == blk).astype(BF16)
        counts = (256.0 * jnp.dot(pieces[0], onehot, preferred_element_type=F32)
                  + jnp.dot(pieces[1], onehot, preferred_element_type=F32))
        inb = jnp.sum((counts <= slot + 0.5).astype(jnp.int32), axis=0, keepdims=True)
        idx_ref[0] = blk * tb + inb


def _select(afft, cap):
    n = afft.shape[1]
    nblk = SELECT_BLOCKS
    tb = n // nblk
    pc = SELECT_SLOTS
    assert n % nblk == 0 and tb % LANES == 0 and cap % pc == 0 and cap < 65536
    nch = cap // pc
    aff3 = afft.reshape(N_EXPERTS, nblk, tb)
    upper = jnp.asarray(np.triu(np.ones((tb, tb), np.float32)), BF16)
    lstrict = jnp.asarray(np.tril(np.ones((nblk, nblk), np.float32), -1))
    full = lambda a: pl.BlockSpec(a.shape, lambda ps, e, j: (0,) * a.ndim)
    idx_spec = pl.BlockSpec((1, 1, pc), lambda ps, e, j: (ps * (e * nch + j), 0, 0))
    slot_spec = pl.BlockSpec((1, nblk, tb), lambda ps, e, j: (ps * e, 0, 0))
    tok_spec = pl.BlockSpec((nblk, tb), lambda ps, e, j: (0, 0))
    idx, slots, cs, ce = pl.pallas_call(
        functools.partial(_select_kernel, cap=cap, tb=tb),
        grid=(2, N_EXPERTS, nch),
        in_specs=[full(aff3), full(upper), full(lstrict)],
        out_specs=[idx_spec, slot_spec, tok_spec, tok_spec],
        out_shape=[jax.ShapeDtypeStruct((N_EXPERTS * nch, 1, pc), jnp.int32),
                   jax.ShapeDtypeStruct((N_EXPERTS, nblk, tb), jnp.int32),
                   jax.ShapeDtypeStruct((nblk, tb), jnp.int32), jax.ShapeDtypeStruct((nblk, tb), jnp.int32)],
        scratch_shapes=[pltpu.VMEM((N_EXPERTS, 8, LANES), jnp.int32), pltpu.VMEM((N_EXPERTS, nblk, tb), BF16),
                        pltpu.VMEM((nblk, tb), F32), pltpu.VMEM((nblk, tb), F32), pltpu.VMEM((nblk, tb), F32),
                        pltpu.VMEM((2, tb, nblk), BF16), pltpu.VMEM((nblk, LANES), F32)],
        compiler_params=_cparams(3),
        name="select",
    )(aff3, upper, lstrict)
    return idx.reshape(-1), slots.reshape(N_EXPERTS, n), cs.reshape(-1), ce.reshape(-1)


def _slot_rows(afft, slots, idx):
    n = afft.shape[1]
    table = jnp.concatenate([lax.bitcast_convert_type(afft, jnp.int32), slots,
                             jnp.zeros((LANES - 2 * N_EXPERTS, n), jnp.int32)], axis=0).T
    return _sc_gather(table, idx)


def _sc_mesh():
    return plsc.VectorSubcoreMesh(core_axis_name="c", subcore_axis_name="s")


def _sc_scatter(rows, idx, m_out):
    m, d = rows.shape
    assert m % SC_WINDOW == 0

    @functools.partial(pl.kernel, out_type=jax.ShapeDtypeStruct((m_out, d), rows.dtype), mesh=_sc_mesh(),
                       name="sc_scatter")
    def scatter(x_hbm, i_hbm, o_hbm):
        def body(x_vmem, i_vmem):
            pltpu.sync_copy(x_vmem, o_hbm.at[i_vmem.at[0]])

        pltpu.emit_pipeline(
            body,
            grid=(m // SC_WINDOW,),
            in_specs=[pl.BlockSpec((SC_WINDOW, d), lambda i: (i, 0)),
                      pl.BlockSpec((1, SC_WINDOW), lambda i: (0, i))],
            out_specs=[],
            core_axis_name=("c", "s"),
            dimension_semantics=(pltpu.PARALLEL,),
        )(x_hbm, i_hbm)

    return scatter(rows, idx.reshape(1, m))


def _sc_gather(table, idx):
    m = idx.shape[0]
    d = table.shape[1]
    assert m % SC_WINDOW == 0

    @functools.partial(pl.kernel, out_type=jax.ShapeDtypeStruct((m, d), table.dtype), mesh=_sc_mesh(),
                       name="sc_gather")
    def gather(x_hbm, i_hbm, o_hbm):
        def body(i_vmem, o_vmem):
            pltpu.sync_copy(x_hbm.at[i_vmem.at[0]], o_vmem)

        pltpu.emit_pipeline(
            body,
            grid=(m // SC_WINDOW,),
            in_specs=[pl.BlockSpec((1, SC_WINDOW), lambda i: (0, i))],
            out_specs=[pl.BlockSpec((SC_WINDOW, d), lambda i: (i, 0))],
            core_axis_name=("c", "s"),
            dimension_semantics=(pltpu.PARALLEL,),
        )(i_hbm, o_hbm)

    return gather(table, idx.reshape(1, m))


def _row_to_col(row):
    n = row.shape[1]
    eye = lax.broadcasted_iota(jnp.int32, (n, n), 0) == lax.broadcasted_iota(jnp.int32, (n, n), 1)
    return jnp.sum(jnp.where(eye, row, jnp.zeros_like(row)), axis=1, keepdims=True)


def _ffn_kernel(slot_ref, x0_ref, x1_ref, w1_hbm, w3_hbm, w2_hbm, o0_ref, o1_ref, dst_ref, wstage, w1b, w3b, w2b, wsem,
                *, layer):
    e = pl.program_id(0)
    i = pl.program_id(1)

    def weight_copies(expert):
        return [pltpu.make_async_copy(w_hbm.at[layer, expert], wstage.at[k], wsem.at[k])
                for k, w_hbm in enumerate((w1_hbm, w3_hbm, w2_hbm))]

    @pl.when(i == 0)
    def _():
        @pl.when(e == 0)
        def _():
            for cp in weight_copies(0):
                cp.start()
        for cp in weight_copies(e):
            cp.wait()
        w1b[...] = wstage[0].astype(BF16)
        w3b[...] = wstage[1].astype(BF16)
        w2b[...] = wstage[2].astype(BF16)

    @pl.when(jnp.logical_and(i == 1, e + 1 < N_EXPERTS))
    def _():
        for cp in weight_copies(e + 1):
            cp.start()

    info = slot_ref[...]
    lane = lax.broadcasted_iota(jnp.int32, info.shape, 1)
    gate = jnp.sum(jnp.where(lane == e, pltpu.bitcast(info, F32), 0.0), axis=1, keepdims=True)
    info_t = info.T
    word = lax.broadcasted_iota(jnp.int32, info_t.shape, 0)
    dst_ref[0] = jnp.sum(jnp.where(word == N_EXPERTS + e, info_t, 0), axis=0, keepdims=True)

    xs = _unpack_rows([x0_ref[...], x1_ref[...]])
    hg = jnp.dot(xs, w1b[...], preferred_element_type=F32)
    hu = jnp.dot(xs, w3b[...], preferred_element_type=F32)
    hid = (hg * _sigmoid(hg) * hu).astype(BF16)
    out = jnp.dot(hid, w2b[...], preferred_element_type=F32) * gate
    for ref, words in zip((o0_ref, o1_ref), _pack_rows(out)):
        ref[...] = words


def _expert_ffn(xs, slot_rows, w1, w3, w2, layer):
    m = xs[0].shape[0]
    rows = FFN_ROWS
    nt = m // (N_EXPERTS * rows)
    assert SC_PIECES == 2 and nt >= 2
    piece = pl.BlockSpec((rows, SC_ROW), lambda e, i: (e * nt + i, 0))
    any_spec = pl.BlockSpec(memory_space=pl.ANY)
    *outs, dst = pl.pallas_call(
        functools.partial(_ffn_kernel, layer=layer),
        grid=(N_EXPERTS, nt),
        in_specs=[pl.BlockSpec((rows, LANES), lambda e, i: (e * nt + i, 0))] + [piece] * SC_PIECES + [any_spec] * 3,
        out_specs=[piece] * SC_PIECES + [pl.BlockSpec((1, 1, rows), lambda e, i: (e * nt + i, 0, 0))],
        out_shape=[jax.ShapeDtypeStruct((m, SC_ROW), jnp.int32)] * SC_PIECES
        + [jax.ShapeDtypeStruct((m // rows, 1, rows), jnp.int32)],
        scratch_shapes=[pltpu.VMEM((3, D_MODEL, EXPERT_FF), F32), pltpu.VMEM((D_MODEL, EXPERT_FF), BF16),
                        pltpu.VMEM((D_MODEL, EXPERT_FF), BF16), pltpu.VMEM((EXPERT_FF, D_MODEL), BF16),
                        pltpu.SemaphoreType.DMA((3,))],
        compiler_params=_cparams(2),
        name="expert_ffn",
    )(slot_rows, *xs, w1, w3, w2)
    return outs, dst.reshape(-1)


def _combine_kernel(tsub_ref, x_ref, cs_ref, ce_ref, r0_hbm, r1_hbm, o_ref, rbuf, obuf, rows16, sems, osem,
                    *, ntile, total):
    pieces_hbm = (r0_hbm, r1_hbm)
    win = COMBINE_WINDOW
    sub = COMBINE_SUB
    subwin = COMBINE_SUBWIN
    per = COMBINE_TOKENS // sub
    i = pl.program_id(0)
    slot = lax.rem(i, COMBINE_BUFFERS)

    def window_start(t):
        return pl.multiple_of((tsub_ref[t * per] // 16) * 16, 16)

    def copies(t, b):
        s = window_start(t)
        return [pltpu.make_async_copy(pieces_hbm[c].at[pl.ds(s, win)], rbuf.at[b, c], sems.at[b, c])
                for c in range(SC_PIECES)]

    ahead = COMBINE_BUFFERS - 1

    @pl.when(i == 0)
    def _():
        for t in range(min(ahead, ntile)):
            for cp in copies(t, t):
                cp.start()

    @pl.when(i + ahead < ntile)
    def _():
        for cp in copies(i + ahead, lax.rem(i + ahead, COMBINE_BUFFERS)):
            cp.start()

    for cp in copies(i, slot):
        cp.wait()

    first = [_row_to_col(cs_ref[0][:, g * sub:(g + 1) * sub]) for g in range(per)]
    last = [_row_to_col(ce_ref[0][:, g * sub:(g + 1) * sub]) for g in range(per)]

    def zero_unwritten(words, base):
        written = (base + lax.broadcasted_iota(jnp.int32, (win, 1), 0)) < total
        return jnp.where(written, words, 0)

    def owner_matrix(g, base, width):
        r = base + lax.broadcasted_iota(jnp.int32, (1, width), 1)
        return jnp.logical_and(first[g] <= r, r < last[g]).astype(BF16)

    s0 = window_start(i)
    tail = s0 + win > total

    @pl.when(tail)
    def _():
        rows16[...] = _unpack_rows([zero_unwritten(rbuf[slot, c], s0) for c in range(SC_PIECES)])

    @pl.when(jnp.logical_not(tail))
    def _():
        rows16[...] = _unpack_rows([rbuf[slot, c] for c in range(SC_PIECES)])

    offsets = []
    fits = None
    for g in range(per):
        off = (tsub_ref[i * per + g] // 16) * 16 - s0
        ok = jnp.logical_and(tsub_ref[i * per + g + 1] - s0 <= off + subwin, off + subwin <= win)
        fits = ok if fits is None else jnp.logical_and(fits, ok)
        offsets.append(off)

    @pl.when(fits)
    def _():
        for g in range(per):
            tokens = slice(g * sub, (g + 1) * sub)
            off = pl.multiple_of(offsets[g], 16)
            q = owner_matrix(g, s0 + off, subwin)
            o_ref[tokens, :] = x_ref[tokens, :] + jnp.dot(q, rows16[pl.ds(off, subwin), :],
                                                         preferred_element_type=F32)

    @pl.when(jnp.logical_not(fits))
    def _():
        def everyone(base):
            return jnp.concatenate([owner_matrix(g, base, win) for g in range(per)], axis=0)

        y = x_ref[...] + jnp.dot(everyone(s0), rows16[...], preferred_element_type=F32)
        n_extra = jnp.maximum(tsub_ref[(i + 1) * per] - (s0 + win) + win - 1, 0) // win

        def extra(k, acc):
            base = pl.multiple_of(s0 + (k + 1) * win, 16)
            cps = [pltpu.make_async_copy(pieces_hbm[c].at[pl.ds(base, win)], obuf.at[c], osem.at[c])
                   for c in range(SC_PIECES)]
            for cp in cps:
                cp.start()
            for cp in cps:
                cp.wait()
            rows = _unpack_rows([zero_unwritten(obuf[c], base) for c in range(SC_PIECES)])
            return acc + jnp.dot(everyone(base), rows, preferred_element_type=F32)

        o_ref[...] = lax.fori_loop(0, n_extra, extra, y)


def _combine(xmid, cs, ce, pieces, total):
    n = xmid.shape[0]
    tt = COMBINE_TOKENS
    win = COMBINE_WINDOW
    ntile = n // tt
    tsub = jnp.concatenate([cs[::COMBINE_SUB], jnp.full((1,), total, jnp.int32)])
    cs3 = cs.reshape(ntile, 1, tt)
    ce3 = ce.reshape(ntile, 1, tt)
    any_spec = pl.BlockSpec(memory_space=pl.ANY)
    tok = pl.BlockSpec((1, 1, tt), lambda i, ts: (i, 0, 0))
    grid_spec = pltpu.PrefetchScalarGridSpec(
        num_scalar_prefetch=1,
        grid=(ntile,),
        in_specs=[pl.BlockSpec((tt, D_MODEL), lambda i, ts: (i, 0)), tok, tok] + [any_spec] * SC_PIECES,
        out_specs=pl.BlockSpec((tt, D_MODEL), lambda i, ts: (i, 0)),
        scratch_shapes=[pltpu.VMEM((COMBINE_BUFFERS, SC_PIECES, win, SC_ROW), jnp.int32),
                        pltpu.VMEM((SC_PIECES, win, SC_ROW), jnp.int32), pltpu.VMEM((win, D_MODEL), BF16),
                        pltpu.SemaphoreType.DMA((COMBINE_BUFFERS, SC_PIECES)), pltpu.SemaphoreType.DMA((SC_PIECES,))],
    )
    return pl.pallas_call(
        functools.partial(_combine_kernel, ntile=ntile, total=total),
        grid_spec=grid_spec,
        out_shape=jax.ShapeDtypeStruct((n, D_MODEL), F32),
        compiler_params=_cparams(1),
        name="combine",
    )(tsub, xmid, cs3, ce3, *pieces)


def _t5_bucket(rel):
    half = REL_BUCKETS // 2
    max_exact = half // 2
    base = np.where(rel > 0, half, 0)
    n = np.abs(rel)
    large = max_exact + (np.log(np.maximum(n, 1) / max_exact) / math.log(REL_MAX_DIST / max_exact)
                         * (half - max_exact)).astype(np.int32)
    large = np.minimum(large, half - 1)
    return (base + np.where(n < max_exact, n, large)).astype(np.int32)


def _head_perm():
    nq = ATTN_HEADS // 2
    cols = []
    for j in range(nq):
        for half in range(2):
            h = j + nq * half
            cols.extend(range(h * ATTN_HEAD_DIM, (h + 1) * ATTN_HEAD_DIM))
    return np.asarray(cols, np.int32)


def _attn_bias_tables(rel_bias):
    q_pos = np.arange(BLOCK)[:, None]
    k_off = np.arange(3 * BLOCK)[None, :] - BLOCK
    rel = k_off - q_pos
    in_window = np.abs(rel) <= WINDOW
    onehot = jnp.asarray(_t5_bucket(rel)[:, :, None] == np.arange(REL_BUCKETS)[None, None, :], F32)
    bias = jnp.einsum("qkb,bh->hqk", onehot, rel_bias.astype(F32), precision=lax.Precision.HIGHEST)
    col = np.arange(3 * BLOCK)[None, :]
    tables = []
    for valid in (col >= BLOCK, np.ones_like(col, bool), col < 2 * BLOCK):
        t = jnp.where(jnp.asarray(in_window & valid)[None], bias, NEG)
        nq = ATTN_HEADS // 2
        rows = [jnp.concatenate([t[j], t[j + nq]], axis=1) for j in range(nq)]
        tables.append(jnp.concatenate(rows, axis=0))
    return jnp.stack(tables)


def _retention_tables(decay_logit, norm_g):
    cr = RET_CHUNK
    lg = jax.nn.log_sigmoid(decay_logit.astype(F32))
    lgf, lgb = lg[0][:, None, None], lg[1][:, None, None]
    pos = np.arange(cr, dtype=np.float32)
    dist = pos[:, None] - pos[None, :]
    scale = RET_DIM ** -0.5
    dmask = jnp.where(jnp.asarray(dist >= 0)[None],
                      jnp.exp(lgf * np.maximum(dist, 0.0)[None]),
                      jnp.exp(lgb * np.maximum(-dist, 0.0)[None])) * scale
    col = lambda v: jnp.broadcast_to(v[:, :, None], (RET_HEADS, cr, RET_DIM))
    rowf = col(jnp.exp(lg[0][:, None] * pos[None]))
    rowb = col(jnp.exp(lg[1][:, None] * (cr - 1.0 - pos)[None]))
    wkf = col(jnp.exp(lg[0][:, None] * (cr - pos)[None]) * scale)
    wkb = col(jnp.exp(lg[1][:, None] * (pos + 1.0)[None]) * scale)
    dec = jnp.concatenate([jnp.exp(lg[0] * cr), jnp.exp(lg[1] * cr)])
    return dec, dmask, rowf, rowb, wkf, wkb, norm_g.astype(F32)


def _layer(x2, b, s, p):
    qa, ka, va, qr, kr, vr, gr, ga, gt = _in_proj(x2, p["g1"], p["w_in"], p["qg"], p["kg"], p["bdq"], p["bdk"])
    attn = _attention(qa, ka, va, p["bias3"], p["sink"], b, s)
    retn = _retention(qr, kr, vr, gr, p["retn"], b, s)
    xmid, afft, *h2 = _merge(attn, retn, ga, gt, x2, p["wba"], p["wbr"], p["wo"], p["g2"], p["wr"])
    n = b * s
    cap = max(1, EC_CAPACITY_FACTOR * n // N_EXPERTS)
    total = N_EXPERTS * cap
    idx, slots, cs, ce = _select(afft, cap)
    slot_rows = _slot_rows(afft, slots, idx)
    xs = [_sc_gather(piece, idx) for piece in h2]
    outs, dst = _expert_ffn(xs, slot_rows, p["w1"], p["w3"], p["w2"], p["layer"])
    by_token = [_sc_scatter(o, dst, total + COMBINE_WINDOW) for o in outs]
    return _combine(xmid, cs, ce, by_token, total)


def kernel(x_prompt, x_sample, norm_mix_g, w_in, q_norm_g, k_norm_g, attn_sink, rel_bias, retn_decay_logit, retn_norm_g, w_branch_attn, w_branch_retn, w_out, norm_ffn_g, w_router, w_exp_gate, w_exp_up, w_exp_down):
    depth = w_in.shape[0]
    perm = _head_perm()
    bias3 = _attn_bias_tables(rel_bias)
    bdq = jnp.asarray(np.kron(np.eye(ATTN_HEADS), np.ones((ATTN_HEAD_DIM, ATTN_HEAD_DIM))), BF16)
    bdk = jnp.asarray(np.kron(np.eye(ATTN_KV_HEADS), np.ones((ATTN_HEAD_DIM, ATTN_HEAD_DIM))), BF16)
    layers = []
    for l in range(depth):
        w = w_in[l]
        w = jnp.concatenate([w[:, :ATTN_WIDTH][:, perm], w[:, ATTN_WIDTH:]], axis=1).astype(BF16)
        wr = jnp.pad(w_router[l], ((0, 0), (0, LANES - N_EXPERTS))).astype(BF16)
        layers.append(dict(
            g1=norm_mix_g[l].astype(F32)[None], w_in=w,
            qg=(jnp.tile(q_norm_g[l].astype(F32), ATTN_HEADS) * (ATTN_HEAD_DIM ** -0.5))[None],
            kg=jnp.tile(k_norm_g[l].astype(F32), ATTN_KV_HEADS)[None],
            bdq=bdq, bdk=bdk, bias3=bias3, sink=attn_sink[l].astype(F32),
            retn=_retention_tables(retn_decay_logit[l], retn_norm_g[l]),
            wba=w_branch_attn[l][perm, :].astype(BF16), wbr=w_branch_retn[l].astype(BF16),
            wo=w_out[l].astype(BF16), g2=norm_ffn_g[l].astype(F32)[None], wr=wr,
            w1=w_exp_gate, w3=w_exp_up, w2=w_exp_down, layer=l))

    def trunk(x):
        b, s, d = x.shape
        x2 = x.reshape(b * s, d)
        for p in layers:
            x2 = _layer(x2, b, s, p)
        return x2.reshape(b, s, d)

    return (trunk(x_prompt), trunk(x_sample))
```

```python
import functools
import math

import numpy as np
import jax
import jax.numpy as jnp
from jax import lax
from jax.experimental import pallas as pl
from jax.experimental.pallas import tpu as pltpu
from jax.experimental.pallas import tpu_sc as plsc

D_MODEL = 1024
ATTN_HEADS = 8
ATTN_KV_HEADS = 2
ATTN_HEAD_DIM = 64
WINDOW = 128
BLOCK = 128
REL_BUCKETS = 32
REL_MAX_DIST = 128
RET_HEADS = 4
RET_DIM = 128
N_EXPERTS = 16
EC_CAPACITY_FACTOR = 2
EXPERT_FF = 1024
EPS = 1e-6

ATTN_WIDTH = ATTN_HEADS * ATTN_HEAD_DIM
KV_WIDTH = ATTN_KV_HEADS * ATTN_HEAD_DIM
RET_WIDTH = RET_HEADS * RET_DIM
IN_SPLITS = (ATTN_WIDTH, KV_WIDTH, KV_WIDTH, RET_WIDTH, RET_WIDTH, RET_WIDTH, RET_WIDTH, D_MODEL, D_MODEL)
IN_OFFSETS = tuple(int(o) for o in np.cumsum((0,) + IN_SPLITS))

LANES = 128
VMEM_LIMIT_BYTES = 56 * 1024 * 1024

TOKEN_TILE = 512
MERGE_ROWS = 512
IN_PROJ_TILE = 1024
ATTN_QUERIES = 1024
RET_CHUNK = 256
RET_STEP = 2048
FFN_ROWS = 1024
SELECT_BLOCKS = 128
SELECT_SLOTS = 1024
SC_WINDOW = 128
SC_ROW = 256
PACKED_WIDTH = D_MODEL // 2
SC_PIECES = PACKED_WIDTH // SC_ROW
COMBINE_TOKENS = 512
COMBINE_WINDOW = 1280
COMBINE_SUB = 128
COMBINE_SUBWIN = 384
COMBINE_BUFFERS = 3

F32 = jnp.float32
BF16 = jnp.bfloat16
NEG = -1e30


def _cparams(n_axes, vmem=VMEM_LIMIT_BYTES):
    return pltpu.CompilerParams(dimension_semantics=("arbitrary",) * n_axes, vmem_limit_bytes=vmem)


def _sigmoid(x):
    return 0.5 * jnp.tanh(0.5 * x) + 0.5


HIGH_HALF = -65536


def _pack_rows(x):
    bits = pltpu.bitcast(x.astype(BF16).astype(F32), jnp.int32)
    words = lax.shift_right_logical(bits[:, :PACKED_WIDTH], 16) | (bits[:, PACKED_WIDTH:] & HIGH_HALF)
    return [words[:, c * SC_ROW:(c + 1) * SC_ROW] for c in range(SC_PIECES)]


def _unpack_rows(pieces):
    low = [pltpu.bitcast(lax.shift_left(w, 16), F32) for w in pieces]
    high = [pltpu.bitcast(w & HIGH_HALF, F32) for w in pieces]
    return jnp.concatenate(low + high, axis=1).astype(BF16)


def _in_proj_kernel(x_ref, g_ref, w_ref, qg_ref, kg_ref, bdq_ref, bdk_ref,
                    qa_ref, ka_ref, va_ref, qr_ref, kr_ref, vr_ref, gr_ref, ga_ref, gt_ref):
    x = x_ref[...]
    ms = jnp.mean(x * x, axis=-1, keepdims=True)
    h = (x * lax.rsqrt(ms + EPS) * g_ref[...]).astype(BF16)

    def mm(k):
        return jnp.dot(h, w_ref[:, IN_OFFSETS[k]:IN_OFFSETS[k + 1]], preferred_element_type=F32)

    def head_norm(t, bd_ref, gain_ref):
        ss = jnp.dot((t * t).astype(BF16), bd_ref[...], preferred_element_type=F32)
        return t * lax.rsqrt(ss * (1.0 / ATTN_HEAD_DIM) + EPS) * gain_ref[...]

    qa_ref[...] = head_norm(mm(0), bdq_ref, qg_ref).astype(BF16)
    ka_ref[...] = head_norm(mm(1), bdk_ref, kg_ref).astype(BF16)
    for k, ref in ((2, va_ref), (3, qr_ref), (4, kr_ref), (5, vr_ref)):
        ref[...] = mm(k).astype(BF16)
    g = mm(6)
    gr_ref[...] = (g * _sigmoid(g)).astype(BF16)
    ga_ref[...] = _sigmoid(mm(7)).astype(BF16)
    gt_ref[...] = _sigmoid(mm(8)).astype(BF16)


def _in_proj(x2, g, w, qg, kg, bdq, bdk):
    n = x2.shape[0]
    tm = IN_PROJ_TILE
    full = lambda a: pl.BlockSpec(a.shape, lambda i: (0,) * a.ndim, pipeline_mode=pl.Buffered(1))
    widths = IN_SPLITS
    return pl.pallas_call(
        _in_proj_kernel,
        grid=(n // tm,),
        in_specs=[pl.BlockSpec((tm, D_MODEL), lambda i: (i, 0)), full(g), full(w), full(qg), full(kg),
                  full(bdq), full(bdk)],
        out_specs=[pl.BlockSpec((tm, wd), lambda i: (i, 0)) for wd in widths],
        out_shape=[jax.ShapeDtypeStruct((n, wd), BF16) for wd in widths],
        compiler_params=_cparams(1),
        name="in_proj",
    )(x2, g, w, qg, kg, bdq, bdk)


def _attn_kernel(sink_ref, q_ref, kp_ref, kc_ref, kn_ref, vp_ref, vc_ref, vn_ref, bias_ref, o_ref, *, nsteps):
    nq = ATTN_HEADS // 2
    ni = pl.program_id(1)
    k = jnp.concatenate([kp_ref[...], kc_ref[...], kn_ref[...]], axis=0)
    v = jnp.concatenate([vp_ref[...], vc_ref[...], vn_ref[...]], axis=0)
    low = lax.broadcasted_iota(jnp.int32, k.shape, 1) < ATTN_HEAD_DIM
    zero = jnp.zeros_like(k)
    k_lo, k_hi = jnp.where(low, k, zero), jnp.where(low, zero, k)
    v_lo, v_hi = jnp.where(low, v, zero), jnp.where(low, zero, v)
    nk = 3 * BLOCK
    low_o = lax.broadcasted_iota(jnp.int32, (BLOCK, LANES), 1) < ATTN_HEAD_DIM
    key_low = lax.broadcasted_iota(jnp.int32, (2 * nk, LANES), 0) < nk
    lane_low = lax.broadcasted_iota(jnp.int32, (2 * nk, LANES), 1) < ATTN_HEAD_DIM
    ones_bd = (key_low == lane_low).astype(BF16)
    nsub = ATTN_QUERIES // BLOCK
    for sb in range(nsub):
        rows = slice(sb * BLOCK, (sb + 1) * BLOCK)
        keys = slice(sb * BLOCK, sb * BLOCK + nk)
        q = q_ref[rows, :]
        qs = jnp.concatenate([q[:, j * LANES:(j + 1) * LANES] for j in range(nq)], axis=0)
        kbd = jnp.concatenate([k_lo[keys], k_hi[keys]], axis=0)
        vbd = jnp.concatenate([v_lo[keys], v_hi[keys]], axis=0)
        s = lax.dot_general(qs, kbd, (((1,), (1,)), ((), ())), preferred_element_type=F32)
        if sb == 0:
            table = jnp.where(ni == 0, 0, 1)
        elif sb == nsub - 1:
            table = jnp.where(ni == nsteps - 1, 2, 1)
        else:
            table = 1
        s = s + bias_ref[table]
        probs, sink_terms = [], []
        for j in range(nq):
            row_p, row_sink = [], []
            for half in range(2):
                sj = s[j * BLOCK:(j + 1) * BLOCK, half * nk:(half + 1) * nk]
                sk = sink_ref[j + nq * half]
                m = jnp.maximum(jnp.max(sj, axis=-1, keepdims=True), sk)
                row_p.append(jnp.exp(sj - m).astype(BF16))
                row_sink.append(jnp.exp(sk - m))
            probs.append(jnp.concatenate(row_p, axis=1))
            sink_terms.append(jnp.where(low_o, row_sink[0], row_sink[1]))
        pm = jnp.concatenate(probs, axis=0)
        od = jnp.dot(pm, jnp.concatenate([vbd, ones_bd], axis=1), preferred_element_type=F32)
        o = od[:, :LANES] / (od[:, LANES:] + jnp.concatenate(sink_terms, axis=0))
        for j in range(nq):
            o_ref[rows, j * LANES:(j + 1) * LANES] = o[j * BLOCK:(j + 1) * BLOCK].astype(BF16)


def _attention(qa, ka, va, bias3, sink, b, s):
    tq = ATTN_QUERIES
    per = tq // BLOCK
    nb = s // BLOCK
    nsteps = s // tq
    assert s % tq == 0 and nb >= 2
    n = b * s
    main = lambda wd: pl.BlockSpec((tq, wd), lambda bi, ni: (bi * nsteps + ni, 0))
    prev = pl.BlockSpec((BLOCK, KV_WIDTH), lambda bi, ni: (bi * nb + jnp.maximum(ni * per - 1, 0), 0))
    nxt = pl.BlockSpec((BLOCK, KV_WIDTH), lambda bi, ni: (bi * nb + jnp.minimum(ni * per + per, nb - 1), 0))
    return pl.pallas_call(
        functools.partial(_attn_kernel, nsteps=nsteps),
        grid=(b, nsteps),
        in_specs=[pl.BlockSpec(memory_space=pltpu.SMEM), main(ATTN_WIDTH),
                  prev, main(KV_WIDTH), nxt, prev, main(KV_WIDTH), nxt,
                  pl.BlockSpec(bias3.shape, lambda bi, ni: (0, 0, 0))],
        out_specs=main(ATTN_WIDTH),
        out_shape=jax.ShapeDtypeStruct((n, ATTN_WIDTH), BF16),
        compiler_params=_cparams(2),
        name="attn",
    )(sink, qa, ka, ka, ka, va, va, va, bias3)


def _retn_kernel(dec_ref, q_ref, k_ref, v_ref, g_ref, dmask_ref, rowf_ref, rowb_ref, wkf_ref, wkb_ref, ng_ref,
                 o_ref, tstore, uf, tb, *, nsteps):
    p = pl.program_id(1)
    n = pl.program_id(2)
    cr = RET_CHUNK
    per = RET_STEP // cr
    tn = (((0,), (0,)), ((), ()))
    nt = (((1,), (1,)), ((), ()))
    hs = lambda h: slice(h * RET_DIM, (h + 1) * RET_DIM)

    @pl.when(p == 0)
    def _():
        @pl.when(n == 0)
        def _():
            tb[...] = jnp.zeros_like(tb)
        first_chunk = (nsteps - 1 - n) * per
        for sub in reversed(range(per)):
            rows = slice(sub * cr, (sub + 1) * cr)
            for h in range(RET_HEADS):
                tstore[first_chunk + sub, h] = tb[h].astype(BF16)
                kw = (k_ref[rows, hs(h)].astype(F32) * wkb_ref[h]).astype(BF16)
                upd = lax.dot_general(kw, v_ref[rows, hs(h)], tn, preferred_element_type=F32)
                tb[h] = dec_ref[RET_HEADS + h] * tb[h] + upd

    @pl.when(p == 1)
    def _():
        @pl.when(n == 0)
        def _():
            uf[...] = jnp.zeros_like(uf)
        for sub in range(per):
            rows = slice(sub * cr, (sub + 1) * cr)
            for h in range(RET_HEADS):
                qh = q_ref[rows, hs(h)]
                kh = k_ref[rows, hs(h)]
                vh = v_ref[rows, hs(h)]
                sc = lax.dot_general(qh, kh, nt, preferred_element_type=F32) * dmask_ref[h]
                intra = jnp.dot(sc.astype(BF16), vh, preferred_element_type=F32)
                states = jnp.concatenate([uf[h].astype(BF16), tstore[n * per + sub, h]], axis=1)
                cross = jnp.dot(qh, states, preferred_element_type=F32)
                o = intra + cross[:, :RET_DIM] * rowf_ref[h] + cross[:, RET_DIM:] * rowb_ref[h]
                mu = jnp.mean(o, axis=-1, keepdims=True)
                d = o - mu
                var = jnp.mean(d * d, axis=-1, keepdims=True)
                on = d * lax.rsqrt(var + EPS) * ng_ref[h:h + 1, :]
                gate = g_ref[rows, hs(h)].astype(F32)
                o_ref[rows, hs(h)] = (gate * on).astype(BF16)
                kw = (kh.astype(F32) * wkf_ref[h]).astype(BF16)
                uf[h] = dec_ref[h] * uf[h] + lax.dot_general(kw, vh, tn, preferred_element_type=F32)


def _retention(qr, kr, vr, gr, tables, b, s):
    dec, dmask, rowf, rowb, wkf, wkb, ng = tables
    rs = RET_STEP
    nsteps = s // rs
    nc = s // RET_CHUNK
    assert s % rs == 0
    n = b * s
    full = lambda a: pl.BlockSpec(a.shape, lambda bi, pi, ni: (0,) * a.ndim)
    fwd_spec = pl.BlockSpec((rs, RET_WIDTH), lambda bi, pi, ni: (bi * nsteps + ni * pi, 0))
    kv_spec = pl.BlockSpec((rs, RET_WIDTH),
                           lambda bi, pi, ni: (bi * nsteps + ni * pi + (1 - pi) * (nsteps - 1 - ni), 0))
    return pl.pallas_call(
        functools.partial(_retn_kernel, nsteps=nsteps),
        grid=(b, 2, nsteps),
        in_specs=[pl.BlockSpec(memory_space=pltpu.SMEM), fwd_spec, kv_spec, kv_spec, fwd_spec,
                  full(dmask), full(rowf), full(rowb), full(wkf), full(wkb), full(ng)],
        out_specs=fwd_spec,
        out_shape=jax.ShapeDtypeStruct((n, RET_WIDTH), BF16),
        scratch_shapes=[pltpu.VMEM((nc, RET_HEADS, RET_DIM, RET_DIM), BF16),
                        pltpu.VMEM((RET_HEADS, RET_DIM, RET_DIM), F32),
                        pltpu.VMEM((RET_HEADS, RET_DIM, RET_DIM), F32)],
        compiler_params=_cparams(3),
        name="retention",
    )(dec, qr, kr, vr, gr, dmask, rowf, rowb, wkf, wkb, ng)


def _merge_kernel(attn_ref, retn_ref, ga_ref, gr_ref, x_ref, wba_ref, wbr_ref, wo_ref, g2_ref, wr_ref,
                  xmid_ref, afft_ref, *h2_refs):
    for sub in range(TOKEN_TILE // MERGE_ROWS):
        rows = slice(sub * MERGE_ROWS, (sub + 1) * MERGE_ROWS)
        a = jnp.dot(attn_ref[rows, :], wba_ref[...], preferred_element_type=F32)
        r = jnp.dot(retn_ref[rows, :], wbr_ref[...], preferred_element_type=F32)
        merged = ga_ref[rows, :].astype(F32) * a + gr_ref[rows, :].astype(F32) * r
        xn = x_ref[rows, :] + jnp.dot(merged.astype(BF16), wo_ref[...], preferred_element_type=F32)
        xmid_ref[rows, :] = xn
        ms = jnp.mean(xn * xn, axis=-1, keepdims=True)
        h2 = xn * lax.rsqrt(ms + EPS) * g2_ref[...]
        for ref, words in zip(h2_refs, _pack_rows(h2)):
            ref[rows, :] = words
        logits = jnp.dot(h2.astype(BF16), wr_ref[...], preferred_element_type=F32)
        real = lax.broadcasted_iota(jnp.int32, logits.shape, 1) < N_EXPERTS
        logits = jnp.where(real, logits, -jnp.inf)
        m = jnp.max(logits, axis=-1, keepdims=True)
        ex = jnp.exp(logits - m)
        aff = ex / jnp.sum(ex, axis=-1, keepdims=True)
        afft_ref[:, rows] = aff.T[:N_EXPERTS, :]


def _merge(attn, retn, ga, gr, x2, wba, wbr, wo, g2, wr):
    n = x2.shape[0]
    tm = TOKEN_TILE
    full = lambda a: pl.BlockSpec(a.shape, lambda i: (0,) * a.ndim)
    row = lambda wd: pl.BlockSpec((tm, wd), lambda i: (i, 0))
    return pl.pallas_call(
        _merge_kernel,
        grid=(n // tm,),
        in_specs=[row(ATTN_WIDTH), row(RET_WIDTH), row(D_MODEL), row(D_MODEL), row(D_MODEL),
                  full(wba), full(wbr), full(wo), full(g2), full(wr)],
        out_specs=[row(D_MODEL), pl.BlockSpec((N_EXPERTS, tm), lambda i: (0, i))] + [row(SC_ROW)] * SC_PIECES,
        out_shape=[jax.ShapeDtypeStruct((n, D_MODEL), F32), jax.ShapeDtypeStruct((N_EXPERTS, n), F32)]
        + [jax.ShapeDtypeStruct((n, SC_ROW), jnp.int32)] * SC_PIECES,
        compiler_params=_cparams(1),
        name="merge",
    )(attn, retn, ga, gr, x2, wba, wbr, wo, g2, wr)


def _select_kernel(aff_ref, u_ref, ls_ref, idx_ref, slot_ref, cs_ref, ce_ref,
                   thr, selbuf, cnt, csr, rank, pieces, offi, *, cap, tb):
    s = pl.program_id(0)
    nch = cap // SELECT_SLOTS
    ps = jnp.where(s < N_EXPERTS, 0, 1)
    later = jnp.maximum(s - N_EXPERTS, 0)
    e = jnp.where(s < N_EXPERTS, s, later // nch)
    j = jnp.where(s < N_EXPERTS, 0, later % nch)
    nblk = SELECT_BLOCKS
    pc = SELECT_SLOTS

    def cumsum(vals):
        inb = jnp.dot(vals.astype(BF16), u_ref[...], preferred_element_type=F32)
        tot = jnp.broadcast_to(inb[:, tb - 1:tb], (nblk, LANES))
        off = jnp.dot(ls_ref[...], tot, preferred_element_type=F32, precision=lax.Precision.HIGHEST)
        return inb, off[:, 0:1], tot[:, 0:1]

    @pl.when(jnp.logical_and(ps == 0, jnp.logical_and(e == 0, j == 0)))
    def _():
        def bit_step(t, curs):
            bit = jnp.left_shift(jnp.int32(1), 30 - t)
            out = []
            for x in range(N_EXPERTS):
                cand = curs[x] | bit
                n_ge = jnp.sum((pltpu.bitcast(aff_ref[x], jnp.int32) >= cand).astype(jnp.int32), keepdims=True)
                out.append(jnp.where(n_ge >= cap, cand, curs[x]))
            return tuple(out)

        found = lax.fori_loop(0, 31, bit_step, tuple(jnp.zeros((1, 1), jnp.int32) for _ in range(N_EXPERTS)))
        for x in range(N_EXPERTS):
            thr[x] = jnp.broadcast_to(found[x], thr.shape[1:])

    @pl.when(jnp.logical_and(ps == 0, j == 0))
    def _():
        bits = pltpu.bitcast(aff_ref[e], jnp.int32)
        limit = thr[e][0:1, 0:1]
        gt = bits > limit
        eq = bits == limit
        need = (cap - jnp.sum(gt.astype(jnp.int32), keepdims=True)).astype(F32)
        eqf = eq.astype(F32)
        eq_in, eq_off, _ = cumsum(eqf)
        eq_rank = eq_in + eq_off - eqf
        sel = jnp.logical_or(gt, jnp.logical_and(eq, eq_rank < need)).astype(F32)
        selbuf[e] = sel.astype(BF16)

        @pl.when(e == 0)
        def _():
            cnt[...] = sel

        @pl.when(e > 0)
        def _():
            cnt[...] = cnt[...] + sel

    @pl.when(jnp.logical_and(ps == 1, j == 0))
    def _():
        @pl.when(e == 0)
        def _():
            c = cnt[...]
            c_in, c_off, _ = cumsum(c)
            start = c_in + c_off - c
            csr[...] = start
            cs_ref[...] = start.astype(jnp.int32)
            ce_ref[...] = (start + c).astype(jnp.int32)
            rank[...] = jnp.zeros_like(rank)

        sel = selbuf[e].astype(F32)
        s_in, s_off, s_tot = cumsum(sel)
        count_t = (s_in + s_off).T
        high = jnp.floor(count_t * (1.0 / 256.0))
        pieces[0] = high.astype(BF16)
        pieces[1] = (count_t - 256.0 * high).astype(BF16)
        slot_ref[0] = (csr[...] + rank[...]).astype(jnp.int32)
        rank[...] = rank[...] + sel
        offi[...] = jnp.broadcast_to(s_off + s_tot, (nblk, LANES))

    @pl.when(ps == 1)
    def _():
        slot = (j * pc + lax.broadcasted_iota(jnp.int32, (1, pc), 1)).astype(F32)
        blk = jnp.sum((offi[:, 0:1] <= slot).astype(jnp.int32), axis=0, keepdims=True)
        onehot = (lax.broadcasted_iota(jnp.int32, (nblk, pc), 0) == blk).astype(BF16)
        counts = (256.0 * jnp.dot(pieces[0], onehot, preferred_element_type=F32)
                  + jnp.dot(pieces[1], onehot, preferred_element_type=F32))
        inb = jnp.sum((counts <= slot + 0.5).astype(jnp.int32), axis=0, keepdims=True)
        idx_ref[0] = blk * tb + inb


def _select(afft, cap):
    n = afft.shape[1]
    nblk = SELECT_BLOCKS
    tb = n // nblk
    pc = SELECT_SLOTS
    assert n % nblk == 0 and tb % LANES == 0 and cap % pc == 0 and cap < 65536
    nch = cap // pc
    aff3 = afft.reshape(N_EXPERTS, nblk, tb)
    upper = jnp.asarray(np.triu(np.ones((tb, tb), np.float32)), BF16)
    lstrict = jnp.asarray(np.tril(np.ones((nblk, nblk), np.float32), -1))
    full = lambda a: pl.BlockSpec(a.shape, lambda s: (0,) * a.ndim)
    idx_spec = pl.BlockSpec((1, 1, pc), lambda s: (jnp.maximum(s - N_EXPERTS, 0), 0, 0))
    slot_spec = pl.BlockSpec((1, nblk, tb), lambda s: (jnp.maximum(s - N_EXPERTS, 0) // nch, 0, 0))
    tok_spec = pl.BlockSpec((nblk, tb), lambda s: (0, 0))
    idx, slots, cs, ce = pl.pallas_call(
        functools.partial(_select_kernel, cap=cap, tb=tb),
        grid=(N_EXPERTS + N_EXPERTS * nch,),
        in_specs=[full(aff3), full(upper), full(lstrict)],
        out_specs=[idx_spec, slot_spec, tok_spec, tok_spec],
        out_shape=[jax.ShapeDtypeStruct((N_EXPERTS * nch, 1, pc), jnp.int32),
                   jax.ShapeDtypeStruct((N_EXPERTS, nblk, tb), jnp.int32),
                   jax.ShapeDtypeStruct((nblk, tb), jnp.int32), jax.ShapeDtypeStruct((nblk, tb), jnp.int32)],
        scratch_shapes=[pltpu.VMEM((N_EXPERTS, 8, LANES), jnp.int32), pltpu.VMEM((N_EXPERTS, nblk, tb), BF16),
                        pltpu.VMEM((nblk, tb), F32), pltpu.VMEM((nblk, tb), F32), pltpu.VMEM((nblk, tb), F32),
                        pltpu.VMEM((2, tb, nblk), BF16), pltpu.VMEM((nblk, LANES), F32)],
        compiler_params=_cparams(1),
        name="select",
    )(aff3, upper, lstrict)
    return idx.reshape(-1), slots.reshape(N_EXPERTS, n), cs.reshape(-1), ce.reshape(-1)


def _slot_rows(afft, slots, idx):
    n = afft.shape[1]
    table = jnp.concatenate([lax.bitcast_convert_type(afft, jnp.int32), slots,
                             jnp.zeros((LANES - 2 * N_EXPERTS, n), jnp.int32)], axis=0).T
    return _sc_gather(table, idx)


def _sc_mesh():
    return plsc.VectorSubcoreMesh(core_axis_name="c", subcore_axis_name="s")


def _sc_scatter(rows, idx, m_out):
    m, d = rows.shape
    assert m % SC_WINDOW == 0

    @functools.partial(pl.kernel, out_type=jax.ShapeDtypeStruct((m_out, d), rows.dtype), mesh=_sc_mesh(),
                       name="sc_scatter")
    def scatter(x_hbm, i_hbm, o_hbm):
        def body(x_vmem, i_vmem):
            pltpu.sync_copy(x_vmem, o_hbm.at[i_vmem.at[0]])

        pltpu.emit_pipeline(
            body,
            grid=(m // SC_WINDOW,),
            in_specs=[pl.BlockSpec((SC_WINDOW, d), lambda i: (i, 0)),
                      pl.BlockSpec((1, SC_WINDOW), lambda i: (0, i))],
            out_specs=[],
            core_axis_name=("c", "s"),
            dimension_semantics=(pltpu.PARALLEL,),
        )(x_hbm, i_hbm)

    return scatter(rows, idx.reshape(1, m))


def _sc_gather(table, idx):
    m = idx.shape[0]
    d = table.shape[1]
    assert m % SC_WINDOW == 0

    @functools.partial(pl.kernel, out_type=jax.ShapeDtypeStruct((m, d), table.dtype), mesh=_sc_mesh(),
                       name="sc_gather")
    def gather(x_hbm, i_hbm, o_hbm):
        def body(i_vmem, o_vmem):
            pltpu.sync_copy(x_hbm.at[i_vmem.at[0]], o_vmem)

        pltpu.emit_pipeline(
            body,
            grid=(m // SC_WINDOW,),
            in_specs=[pl.BlockSpec((1, SC_WINDOW), lambda i: (0, i))],
            out_specs=[pl.BlockSpec((SC_WINDOW, d), lambda i: (i, 0))],
            core_axis_name=("c", "s"),
            dimension_semantics=(pltpu.PARALLEL,),
        )(i_hbm, o_hbm)

    return gather(table, idx.reshape(1, m))


def _row_to_col(row):
    n = row.shape[1]
    eye = lax.broadcasted_iota(jnp.int32, (n, n), 0) == lax.broadcasted_iota(jnp.int32, (n, n), 1)
    return jnp.sum(jnp.where(eye, row, jnp.zeros_like(row)), axis=1, keepdims=True)


def _ffn_kernel(slot_ref, x0_ref, x1_ref, w1_hbm, w3_hbm, w2_hbm, o0_ref, o1_ref, dst_ref, wstage, w1b, w3b, w2b, wsem,
                *, layer):
    e = pl.program_id(0)
    i = pl.program_id(1)

    def weight_copies(expert):
        return [pltpu.make_async_copy(w_hbm.at[layer, expert], wstage.at[k], wsem.at[k])
                for k, w_hbm in enumerate((w1_hbm, w3_hbm, w2_hbm))]

    @pl.when(i == 0)
    def _():
        @pl.when(e == 0)
        def _():
            for cp in weight_copies(0):
                cp.start()
        for cp in weight_copies(e):
            cp.wait()
        w1b[...] = wstage[0].astype(BF16)
        w3b[...] = wstage[1].astype(BF16)
        w2b[...] = wstage[2].astype(BF16)

    @pl.when(jnp.logical_and(i == 1, e + 1 < N_EXPERTS))
    def _():
        for cp in weight_copies(e + 1):
            cp.start()

    info = slot_ref[...]
    lane = lax.broadcasted_iota(jnp.int32, info.shape, 1)
    gate = jnp.sum(jnp.where(lane == e, pltpu.bitcast(info, F32), 0.0), axis=1, keepdims=True)
    info_t = info.T
    word = lax.broadcasted_iota(jnp.int32, info_t.shape, 0)
    dst_ref[0] = jnp.sum(jnp.where(word == N_EXPERTS + e, info_t, 0), axis=0, keepdims=True)

    xs = _unpack_rows([x0_ref[...], x1_ref[...]])
    hg = jnp.dot(xs, w1b[...], preferred_element_type=F32)
    hu = jnp.dot(xs, w3b[...], preferred_element_type=F32)
    hid = (hg * _sigmoid(hg) * hu).astype(BF16)
    out = jnp.dot(hid, w2b[...], preferred_element_type=F32) * gate
    for ref, words in zip((o0_ref, o1_ref), _pack_rows(out)):
        ref[...] = words


def _expert_ffn(xs, slot_rows, w1, w3, w2, layer):
    m = xs[0].shape[0]
    rows = FFN_ROWS
    nt = m // (N_EXPERTS * rows)
    assert SC_PIECES == 2 and nt >= 2
    piece = pl.BlockSpec((rows, SC_ROW), lambda e, i: (e * nt + i, 0))
    any_spec = pl.BlockSpec(memory_space=pl.ANY)
    *outs, dst = pl.pallas_call(
        functools.partial(_ffn_kernel, layer=layer),
        grid=(N_EXPERTS, nt),
        in_specs=[pl.BlockSpec((rows, LANES), lambda e, i: (e * nt + i, 0))] + [piece] * SC_PIECES + [any_spec] * 3,
        out_specs=[piece] * SC_PIECES + [pl.BlockSpec((1, 1, rows), lambda e, i: (e * nt + i, 0, 0))],
        out_shape=[jax.ShapeDtypeStruct((m, SC_ROW), jnp.int32)] * SC_PIECES
        + [jax.ShapeDtypeStruct((m // rows, 1, rows), jnp.int32)],
        scratch_shapes=[pltpu.VMEM((3, D_MODEL, EXPERT_FF), F32), pltpu.VMEM((D_MODEL, EXPERT_FF), BF16),
                        pltpu.VMEM((D_MODEL, EXPERT_FF), BF16), pltpu.VMEM((EXPERT_FF, D_MODEL), BF16),
                        pltpu.SemaphoreType.DMA((3,))],
        compiler_params=_cparams(2),
        name="expert_ffn",
    )(slot_rows, *xs, w1, w3, w2)
    return outs, dst.reshape(-1)


def _combine_kernel(tsub_ref, x_ref, cs_ref, ce_ref, r0_hbm, r1_hbm, o_ref, rbuf, obuf, rows16, sems, osem,
                    *, ntile, total):
    pieces_hbm = (r0_hbm, r1_hbm)
    win = COMBINE_WINDOW
    sub = COMBINE_SUB
    subwin = COMBINE_SUBWIN
    per = COMBINE_TOKENS // sub
    i = pl.program_id(0)
    slot = lax.rem(i, COMBINE_BUFFERS)

    def window_start(t):
        return pl.multiple_of((tsub_ref[t * per] // 16) * 16, 16)

    def copies(t, b):
        s = window_start(t)
        return [pltpu.make_async_copy(pieces_hbm[c].at[pl.ds(s, win)], rbuf.at[b, c], sems.at[b, c])
                for c in range(SC_PIECES)]

    ahead = COMBINE_BUFFERS - 1

    @pl.when(i == 0)
    def _():
        for t in range(min(ahead, ntile)):
            for cp in copies(t, t):
                cp.start()

    @pl.when(i + ahead < ntile)
    def _():
        for cp in copies(i + ahead, lax.rem(i + ahead, COMBINE_BUFFERS)):
            cp.start()

    for cp in copies(i, slot):
        cp.wait()

    first = [_row_to_col(cs_ref[0][:, g * sub:(g + 1) * sub]) for g in range(per)]
    last = [_row_to_col(ce_ref[0][:, g * sub:(g + 1) * sub]) for g in range(per)]

    def zero_unwritten(words, base):
        written = (base + lax.broadcasted_iota(jnp.int32, (win, 1), 0)) < total
        return jnp.where(written, words, 0)

    def owner_matrix(g, base, width):
        r = base + lax.broadcasted_iota(jnp.int32, (1, width), 1)
        return jnp.logical_and(first[g] <= r, r < last[g]).astype(BF16)

    s0 = window_start(i)
    tail = s0 + win > total

    @pl.when(tail)
    def _():
        rows16[...] = _unpack_rows([zero_unwritten(rbuf[slot, c], s0) for c in range(SC_PIECES)])

    @pl.when(jnp.logical_not(tail))
    def _():
        rows16[...] = _unpack_rows([rbuf[slot, c] for c in range(SC_PIECES)])

    offsets = []
    fits = None
    for g in range(per):
        off = (tsub_ref[i * per + g] // 16) * 16 - s0
        ok = jnp.logical_and(tsub_ref[i * per + g + 1] - s0 <= off + subwin, off + subwin <= win)
        fits = ok if fits is None else jnp.logical_and(fits, ok)
        offsets.append(off)

    @pl.when(fits)
    def _():
        for g in range(per):
            tokens = slice(g * sub, (g + 1) * sub)
            off = pl.multiple_of(offsets[g], 16)
            q = owner_matrix(g, s0 + off, subwin)
            o_ref[tokens, :] = x_ref[tokens, :] + jnp.dot(q, rows16[pl.ds(off, subwin), :],
                                                         preferred_element_type=F32)

    @pl.when(jnp.logical_not(fits))
    def _():
        def everyone(base):
            return jnp.concatenate([owner_matrix(g, base, win) for g in range(per)], axis=0)

        y = x_ref[...] + jnp.dot(everyone(s0), rows16[...], preferred_element_type=F32)
        n_extra = jnp.maximum(tsub_ref[(i + 1) * per] - (s0 + win) + win - 1, 0) // win

        def extra(k, acc):
            base = pl.multiple_of(s0 + (k + 1) * win, 16)
            cps = [pltpu.make_async_copy(pieces_hbm[c].at[pl.ds(base, win)], obuf.at[c], osem.at[c])
                   for c in range(SC_PIECES)]
            for cp in cps:
                cp.start()
            for cp in cps:
                cp.wait()
            rows = _unpack_rows([zero_unwritten(obuf[c], base) for c in range(SC_PIECES)])
            return acc + jnp.dot(everyone(base), rows, preferred_element_type=F32)

        o_ref[...] = lax.fori_loop(0, n_extra, extra, y)


def _combine(xmid, cs, ce, pieces, total):
    n = xmid.shape[0]
    tt = COMBINE_TOKENS
    win = COMBINE_WINDOW
    ntile = n // tt
    tsub = jnp.concatenate([cs[::COMBINE_SUB], jnp.full((1,), total, jnp.int32)])
    cs3 = cs.reshape(ntile, 1, tt)
    ce3 = ce.reshape(ntile, 1, tt)
    any_spec = pl.BlockSpec(memory_space=pl.ANY)
    tok = pl.BlockSpec((1, 1, tt), lambda i, ts: (i, 0, 0))
    grid_spec = pltpu.PrefetchScalarGridSpec(
        num_scalar_prefetch=1,
        grid=(ntile,),
        in_specs=[pl.BlockSpec((tt, D_MODEL), lambda i, ts: (i, 0)), tok, tok] + [any_spec] * SC_PIECES,
        out_specs=pl.BlockSpec((tt, D_MODEL), lambda i, ts: (i, 0)),
        scratch_shapes=[pltpu.VMEM((COMBINE_BUFFERS, SC_PIECES, win, SC_ROW), jnp.int32),
                        pltpu.VMEM((SC_PIECES, win, SC_ROW), jnp.int32), pltpu.VMEM((win, D_MODEL), BF16),
                        pltpu.SemaphoreType.DMA((COMBINE_BUFFERS, SC_PIECES)), pltpu.SemaphoreType.DMA((SC_PIECES,))],
    )
    return pl.pallas_call(
        functools.partial(_combine_kernel, ntile=ntile, total=total),
        grid_spec=grid_spec,
        out_shape=jax.ShapeDtypeStruct((n, D_MODEL), F32),
        compiler_params=_cparams(1),
        name="combine",
    )(tsub, xmid, cs3, ce3, *pieces)


def _t5_bucket(rel):
    half = REL_BUCKETS // 2
    max_exact = half // 2
    base = np.where(rel > 0, half, 0)
    n = np.abs(rel)
    large = max_exact + (np.log(np.maximum(n, 1) / max_exact) / math.log(REL_MAX_DIST / max_exact)
                         * (half - max_exact)).astype(np.int32)
    large = np.minimum(large, half - 1)
    return (base + np.where(n < max_exact, n, large)).astype(np.int32)


def _head_perm():
    nq = ATTN_HEADS // 2
    cols = []
    for j in range(nq):
        for half in range(2):
            h = j + nq * half
            cols.extend(range(h * ATTN_HEAD_DIM, (h + 1) * ATTN_HEAD_DIM))
    return np.asarray(cols, np.int32)


def _attn_bias_tables(rel_bias):
    q_pos = np.arange(BLOCK)[:, None]
    k_off = np.arange(3 * BLOCK)[None, :] - BLOCK
    rel = k_off - q_pos
    in_window = np.abs(rel) <= WINDOW
    onehot = jnp.asarray(_t5_bucket(rel)[:, :, None] == np.arange(REL_BUCKETS)[None, None, :], F32)
    bias = jnp.einsum("qkb,bh->hqk", onehot, rel_bias.astype(F32), precision=lax.Precision.HIGHEST)
    col = np.arange(3 * BLOCK)[None, :]
    tables = []
    for valid in (col >= BLOCK, np.ones_like(col, bool), col < 2 * BLOCK):
        t = jnp.where(jnp.asarray(in_window & valid)[None], bias, NEG)
        nq = ATTN_HEADS // 2
        rows = [jnp.concatenate([t[j], t[j + nq]], axis=1) for j in range(nq)]
        tables.append(jnp.concatenate(rows, axis=0))
    return jnp.stack(tables)


def _retention_tables(decay_logit, norm_g):
    cr = RET_CHUNK
    lg = jax.nn.log_sigmoid(decay_logit.astype(F32))
    lgf, lgb = lg[0][:, None, None], lg[1][:, None, None]
    pos = np.arange(cr, dtype=np.float32)
    dist = pos[:, None] - pos[None, :]
    scale = RET_DIM ** -0.5
    dmask = jnp.where(jnp.asarray(dist >= 0)[None],
                      jnp.exp(lgf * np.maximum(dist, 0.0)[None]),
                      jnp.exp(lgb * np.maximum(-dist, 0.0)[None])) * scale
    col = lambda v: jnp.broadcast_to(v[:, :, None], (RET_HEADS, cr, RET_DIM))
    rowf = col(jnp.exp(lg[0][:, None] * pos[None]))
    rowb = col(jnp.exp(lg[1][:, None] * (cr - 1.0 - pos)[None]))
    wkf = col(jnp.exp(lg[0][:, None] * (cr - pos)[None]) * scale)
    wkb = col(jnp.exp(lg[1][:, None] * (pos + 1.0)[None]) * scale)
    dec = jnp.concatenate([jnp.exp(lg[0] * cr), jnp.exp(lg[1] * cr)])
    return dec, dmask, rowf, rowb, wkf, wkb, norm_g.astype(F32)


def _layer(x2, b, s, p):
    qa, ka, va, qr, kr, vr, gr, ga, gt = _in_proj(x2, p["g1"], p["w_in"], p["qg"], p["kg"], p["bdq"], p["bdk"])
    attn = _attention(qa, ka, va, p["bias3"], p["sink"], b, s)
    retn = _retention(qr, kr, vr, gr, p["retn"], b, s)
    xmid, afft, *h2 = _merge(attn, retn, ga, gt, x2, p["wba"], p["wbr"], p["wo"], p["g2"], p["wr"])
    n = b * s
    cap = max(1, EC_CAPACITY_FACTOR * n // N_EXPERTS)
    total = N_EXPERTS * cap
    idx, slots, cs, ce = _select(afft, cap)
    slot_rows = _slot_rows(afft, slots, idx)
    xs = [_sc_gather(piece, idx) for piece in h2]
    outs, dst = _expert_ffn(xs, slot_rows, p["w1"], p["w3"], p["w2"], p["layer"])
    by_token = [_sc_scatter(o, dst, total + COMBINE_WINDOW) for o in outs]
    return _combine(xmid, cs, ce, by_token, total)


def kernel(x_prompt, x_sample, norm_mix_g, w_in, q_norm_g, k_norm_g, attn_sink, rel_bias, retn_decay_logit, retn_norm_g, w_branch_attn, w_branch_retn, w_out, norm_ffn_g, w_router, w_exp_gate, w_exp_up, w_exp_down):
    depth = w_in.shape[0]
    perm = _head_perm()
    bias3 = _attn_bias_tables(rel_bias)
    bdq = jnp.asarray(np.kron(np.eye(ATTN_HEADS), np.ones((ATTN_HEAD_DIM, ATTN_HEAD_DIM))), BF16)
    bdk = jnp.asarray(np.kron(np.eye(ATTN_KV_HEADS), np.ones((ATTN_HEAD_DIM, ATTN_HEAD_DIM))), BF16)
    layers = []
    for l in range(depth):
        w = w_in[l]
        w = jnp.concatenate([w[:, :ATTN_WIDTH][:, perm], w[:, ATTN_WIDTH:]], axis=1).astype(BF16)
        wr = jnp.pad(w_router[l], ((0, 0), (0, LANES - N_EXPERTS))).astype(BF16)
        layers.append(dict(
            g1=norm_mix_g[l].astype(F32)[None], w_in=w,
            qg=(jnp.tile(q_norm_g[l].astype(F32), ATTN_HEADS) * (ATTN_HEAD_DIM ** -0.5))[None],
            kg=jnp.tile(k_norm_g[l].astype(F32), ATTN_KV_HEADS)[None],
            bdq=bdq, bdk=bdk, bias3=bias3, sink=attn_sink[l].astype(F32),
            retn=_retention_tables(retn_decay_logit[l], retn_norm_g[l]),
            wba=w_branch_attn[l][perm, :].astype(BF16), wbr=w_branch_retn[l].astype(BF16),
            wo=w_out[l].astype(BF16), g2=norm_ffn_g[l].astype(F32)[None], wr=wr,
            w1=w_exp_gate, w3=w_exp_up, w2=w_exp_down, layer=l))

    def trunk(x):
        b, s, d = x.shape
        x2 = x.reshape(b * s, d)
        for p in layers:
            x2 = _layer(x2, b, s, p)
        return x2.reshape(b, s, d)

    return (trunk(x_prompt), trunk(x_sample))
```

```python
import functools
import math

import numpy as np
import jax
import jax.numpy as jnp
from jax import lax
from jax.experimental import pallas as pl
from jax.experimental.pallas import tpu as pltpu
from jax.experimental.pallas import tpu_sc as plsc

D_MODEL = 1024
ATTN_HEADS = 8
ATTN_KV_HEADS = 2
ATTN_HEAD_DIM = 64
WINDOW = 128
BLOCK = 128
REL_BUCKETS = 32
REL_MAX_DIST = 128
RET_HEADS = 4
RET_DIM = 128
N_EXPERTS = 16
EC_CAPACITY_FACTOR = 2
EXPERT_FF = 1024
EPS = 1e-6

ATTN_WIDTH = ATTN_HEADS * ATTN_HEAD_DIM
KV_WIDTH = ATTN_KV_HEADS * ATTN_HEAD_DIM
RET_WIDTH = RET_HEADS * RET_DIM
IN_SPLITS = (ATTN_WIDTH, KV_WIDTH, KV_WIDTH, RET_WIDTH, RET_WIDTH, RET_WIDTH, RET_WIDTH, D_MODEL, D_MODEL)
IN_OFFSETS = tuple(int(o) for o in np.cumsum((0,) + IN_SPLITS))

LANES = 128
VMEM_LIMIT_BYTES = 56 * 1024 * 1024

TOKEN_TILE = 1024
MERGE_ROWS = 256
IN_PROJ_TILE = 1024
ATTN_QUERIES = 1024
RET_CHUNK = 256
RET_STEP = 2048
FFN_ROWS = 1024
SELECT_BLOCKS = 128
SELECT_SLOTS = 1024
SC_WINDOW = 128
SC_ROW = 256
PACKED_WIDTH = D_MODEL // 2
SC_PIECES = PACKED_WIDTH // SC_ROW
COMBINE_TOKENS = 512
COMBINE_WINDOW = 1280
COMBINE_SUB = 128
COMBINE_SUBWIN = 384
COMBINE_BUFFERS = 3

F32 = jnp.float32
BF16 = jnp.bfloat16
NEG = -1e30


def _cparams(n_axes, vmem=VMEM_LIMIT_BYTES):
    return pltpu.CompilerParams(dimension_semantics=("arbitrary",) * n_axes, vmem_limit_bytes=vmem)


def _sigmoid(x):
    return 0.5 * jnp.tanh(0.5 * x) + 0.5


HIGH_HALF = -65536


def _pack_rows(x):
    bits = pltpu.bitcast(x.astype(BF16).astype(F32), jnp.int32)
    words = lax.shift_right_logical(bits[:, :PACKED_WIDTH], 16) | (bits[:, PACKED_WIDTH:] & HIGH_HALF)
    return [words[:, c * SC_ROW:(c + 1) * SC_ROW] for c in range(SC_PIECES)]


def _unpack_rows(pieces):
    low = [pltpu.bitcast(lax.shift_left(w, 16), F32) for w in pieces]
    high = [pltpu.bitcast(w & HIGH_HALF, F32) for w in pieces]
    return jnp.concatenate(low + high, axis=1).astype(BF16)


def _in_proj_kernel(x_ref, g_ref, w_ref, qg_ref, kg_ref, bdq_ref, bdk_ref,
                    qa_ref, ka_ref, va_ref, qr_ref, kr_ref, vr_ref, gr_ref, ga_ref, gt_ref):
    x = x_ref[...]
    ms = jnp.mean(x * x, axis=-1, keepdims=True)
    h = (x * lax.rsqrt(ms + EPS) * g_ref[...]).astype(BF16)

    def mm(k):
        return jnp.dot(h, w_ref[:, IN_OFFSETS[k]:IN_OFFSETS[k + 1]], preferred_element_type=F32)

    def head_norm(t, bd_ref, gain_ref):
        ss = jnp.dot((t * t).astype(BF16), bd_ref[...], preferred_element_type=F32)
        return t * lax.rsqrt(ss * (1.0 / ATTN_HEAD_DIM) + EPS) * gain_ref[...]

    q_raw = mm(0)
    k_raw = mm(1)
    for k, ref in ((2, va_ref), (3, qr_ref), (4, kr_ref), (5, vr_ref)):
        ref[...] = mm(k).astype(BF16)
    g = mm(6)
    gr_ref[...] = (g * _sigmoid(g)).astype(BF16)
    ga_ref[...] = _sigmoid(mm(7)).astype(BF16)
    gt_ref[...] = _sigmoid(mm(8)).astype(BF16)
    qa_ref[...] = head_norm(q_raw, bdq_ref, qg_ref).astype(BF16)
    ka_ref[...] = head_norm(k_raw, bdk_ref, kg_ref).astype(BF16)


def _in_proj(x2, g, w, qg, kg, bdq, bdk):
    n = x2.shape[0]
    tm = IN_PROJ_TILE
    full = lambda a: pl.BlockSpec(a.shape, lambda i: (0,) * a.ndim, pipeline_mode=pl.Buffered(1))
    widths = IN_SPLITS
    return pl.pallas_call(
        _in_proj_kernel,
        grid=(n // tm,),
        in_specs=[pl.BlockSpec((tm, D_MODEL), lambda i: (i, 0)), full(g), full(w), full(qg), full(kg),
                  full(bdq), full(bdk)],
        out_specs=[pl.BlockSpec((tm, wd), lambda i: (i, 0)) for wd in widths],
        out_shape=[jax.ShapeDtypeStruct((n, wd), BF16) for wd in widths],
        compiler_params=_cparams(1),
        name="in_proj",
    )(x2, g, w, qg, kg, bdq, bdk)


def _attn_kernel(sink_ref, q_ref, kp_ref, kc_ref, kn_ref, vp_ref, vc_ref, vn_ref, bias_ref, o_ref, *, nsteps):
    nq = ATTN_HEADS // 2
    ni = pl.program_id(1)
    k = jnp.concatenate([kp_ref[...], kc_ref[...], kn_ref[...]], axis=0)
    v = jnp.concatenate([vp_ref[...], vc_ref[...], vn_ref[...]], axis=0)
    low = lax.broadcasted_iota(jnp.int32, k.shape, 1) < ATTN_HEAD_DIM
    zero = jnp.zeros_like(k)
    k_lo, k_hi = jnp.where(low, k, zero), jnp.where(low, zero, k)
    v_lo, v_hi = jnp.where(low, v, zero), jnp.where(low, zero, v)
    nk = 3 * BLOCK
    low_o = lax.broadcasted_iota(jnp.int32, (BLOCK, LANES), 1) < ATTN_HEAD_DIM
    key_low = lax.broadcasted_iota(jnp.int32, (2 * nk, LANES), 0) < nk
    lane_low = lax.broadcasted_iota(jnp.int32, (2 * nk, LANES), 1) < ATTN_HEAD_DIM
    ones_bd = (key_low == lane_low).astype(BF16)
    nsub = ATTN_QUERIES // BLOCK
    for sb in range(nsub):
        rows = slice(sb * BLOCK, (sb + 1) * BLOCK)
        keys = slice(sb * BLOCK, sb * BLOCK + nk)
        q = q_ref[rows, :]
        qs = jnp.concatenate([q[:, j * LANES:(j + 1) * LANES] for j in range(nq)], axis=0)
        kbd = jnp.concatenate([k_lo[keys], k_hi[keys]], axis=0)
        vbd = jnp.concatenate([v_lo[keys], v_hi[keys]], axis=0)
        s = lax.dot_general(qs, kbd, (((1,), (1,)), ((), ())), preferred_element_type=F32)
        if sb == 0:
            table = jnp.where(ni == 0, 0, 1)
        elif sb == nsub - 1:
            table = jnp.where(ni == nsteps - 1, 2, 1)
        else:
            table = 1
        s = s + bias_ref[table]
        probs, sink_terms = [], []
        for j in range(nq):
            row_p, row_sink = [], []
            for half in range(2):
                sj = s[j * BLOCK:(j + 1) * BLOCK, half * nk:(half + 1) * nk]
                sk = sink_ref[j + nq * half]
                m = jnp.maximum(jnp.max(sj, axis=-1, keepdims=True), sk)
                row_p.append(jnp.exp(sj - m).astype(BF16))
                row_sink.append(jnp.exp(sk - m))
            probs.append(jnp.concatenate(row_p, axis=1))
            sink_terms.append(jnp.where(low_o, row_sink[0], row_sink[1]))
        pm = jnp.concatenate(probs, axis=0)
        od = jnp.dot(pm, jnp.concatenate([vbd, ones_bd], axis=1), preferred_element_type=F32)
        o = od[:, :LANES] / (od[:, LANES:] + jnp.concatenate(sink_terms, axis=0))
        for j in range(nq):
            o_ref[rows, j * LANES:(j + 1) * LANES] = o[j * BLOCK:(j + 1) * BLOCK].astype(BF16)


def _attention(qa, ka, va, bias3, sink, b, s):
    tq = ATTN_QUERIES
    per = tq // BLOCK
    nb = s // BLOCK
    nsteps = s // tq
    assert s % tq == 0 and nb >= 2
    n = b * s
    main = lambda wd: pl.BlockSpec((tq, wd), lambda bi, ni: (bi * nsteps + ni, 0))
    prev = pl.BlockSpec((BLOCK, KV_WIDTH), lambda bi, ni: (bi * nb + jnp.maximum(ni * per - 1, 0), 0))
    nxt = pl.BlockSpec((BLOCK, KV_WIDTH), lambda bi, ni: (bi * nb + jnp.minimum(ni * per + per, nb - 1), 0))
    return pl.pallas_call(
        functools.partial(_attn_kernel, nsteps=nsteps),
        grid=(b, nsteps),
        in_specs=[pl.BlockSpec(memory_space=pltpu.SMEM), main(ATTN_WIDTH),
                  prev, main(KV_WIDTH), nxt, prev, main(KV_WIDTH), nxt,
                  pl.BlockSpec(bias3.shape, lambda bi, ni: (0, 0, 0))],
        out_specs=main(ATTN_WIDTH),
        out_shape=jax.ShapeDtypeStruct((n, ATTN_WIDTH), BF16),
        compiler_params=_cparams(2),
        name="attn",
    )(sink, qa, ka, ka, ka, va, va, va, bias3)


def _retn_kernel(dec_ref, q_ref, k_ref, v_ref, g_ref, dmask_ref, rowf_ref, rowb_ref, wkf_ref, wkb_ref, ng_ref,
                 o_ref, tstore, uf, tb, *, nsteps):
    p = pl.program_id(1)
    n = pl.program_id(2)
    cr = RET_CHUNK
    per = RET_STEP // cr
    tn = (((0,), (0,)), ((), ()))
    nt = (((1,), (1,)), ((), ()))
    hs = lambda h: slice(h * RET_DIM, (h + 1) * RET_DIM)

    @pl.when(p == 0)
    def _():
        @pl.when(n == 0)
        def _():
            tb[...] = jnp.zeros_like(tb)
        first_chunk = (nsteps - 1 - n) * per
        for sub in reversed(range(per)):
            rows = slice(sub * cr, (sub + 1) * cr)
            for h in range(RET_HEADS):
                tstore[first_chunk + sub, h] = tb[h].astype(BF16)
                kw = (k_ref[rows, hs(h)].astype(F32) * wkb_ref[h]).astype(BF16)
                upd = lax.dot_general(kw, v_ref[rows, hs(h)], tn, preferred_element_type=F32)
                tb[h] = dec_ref[RET_HEADS + h] * tb[h] + upd

    @pl.when(p == 1)
    def _():
        @pl.when(n == 0)
        def _():
            uf[...] = jnp.zeros_like(uf)
        heads = range(RET_HEADS)
        for sub in range(per):
            rows = slice(sub * cr, (sub + 1) * cr)
            qs = [q_ref[rows, hs(h)] for h in heads]
            ks = [k_ref[rows, hs(h)] for h in heads]
            vs = [v_ref[rows, hs(h)] for h in heads]
            scores = [lax.dot_general(qs[h], ks[h], nt, preferred_element_type=F32) for h in heads]
            states = [jnp.concatenate([uf[h].astype(BF16), tstore[n * per + sub, h]], axis=1) for h in heads]
            cross = [jnp.dot(qs[h], states[h], preferred_element_type=F32) for h in heads]
            update = [lax.dot_general((ks[h].astype(F32) * wkf_ref[h]).astype(BF16), vs[h], tn,
                                      preferred_element_type=F32) for h in heads]
            intra = [jnp.dot((scores[h] * dmask_ref[h]).astype(BF16), vs[h], preferred_element_type=F32)
                     for h in heads]
            for h in heads:
                o = intra[h] + cross[h][:, :RET_DIM] * rowf_ref[h] + cross[h][:, RET_DIM:] * rowb_ref[h]
                mu = jnp.mean(o, axis=-1, keepdims=True)
                d = o - mu
                var = jnp.mean(d * d, axis=-1, keepdims=True)
                on = d * lax.rsqrt(var + EPS) * ng_ref[h:h + 1, :]
                gate = g_ref[rows, hs(h)].astype(F32)
                o_ref[rows, hs(h)] = (gate * on).astype(BF16)
                uf[h] = dec_ref[h] * uf[h] + update[h]


def _retention(qr, kr, vr, gr, tables, b, s):
    dec, dmask, rowf, rowb, wkf, wkb, ng = tables
    rs = RET_STEP
    nsteps = s // rs
    nc = s // RET_CHUNK
    assert s % rs == 0
    n = b * s
    full = lambda a: pl.BlockSpec(a.shape, lambda bi, pi, ni: (0,) * a.ndim)
    fwd_spec = pl.BlockSpec((rs, RET_WIDTH), lambda bi, pi, ni: (bi * nsteps + ni * pi, 0))
    kv_spec = pl.BlockSpec((rs, RET_WIDTH),
                           lambda bi, pi, ni: (bi * nsteps + ni * pi + (1 - pi) * (nsteps - 1 - ni), 0))
    return pl.pallas_call(
        functools.partial(_retn_kernel, nsteps=nsteps),
        grid=(b, 2, nsteps),
        in_specs=[pl.BlockSpec(memory_space=pltpu.SMEM), fwd_spec, kv_spec, kv_spec, fwd_spec,
                  full(dmask), full(rowf), full(rowb), full(wkf), full(wkb), full(ng)],
        out_specs=fwd_spec,
        out_shape=jax.ShapeDtypeStruct((n, RET_WIDTH), BF16),
        scratch_shapes=[pltpu.VMEM((nc, RET_HEADS, RET_DIM, RET_DIM), BF16),
                        pltpu.VMEM((RET_HEADS, RET_DIM, RET_DIM), F32),
                        pltpu.VMEM((RET_HEADS, RET_DIM, RET_DIM), F32)],
        compiler_params=_cparams(3),
        name="retention",
    )(dec, qr, kr, vr, gr, dmask, rowf, rowb, wkf, wkb, ng)


def _merge_kernel(attn_ref, retn_ref, ga_ref, gr_ref, x_ref, wba_ref, wbr_ref, wo_ref, g2_ref, wr_ref,
                  xmid_ref, afft_ref, *h2_refs):
    tiles = [slice(t * MERGE_ROWS, (t + 1) * MERGE_ROWS) for t in range(TOKEN_TILE // MERGE_ROWS)]
    branch = [(jnp.dot(attn_ref[rows, :], wba_ref[...], preferred_element_type=F32),
               jnp.dot(retn_ref[rows, :], wbr_ref[...], preferred_element_type=F32)) for rows in tiles]
    resid = []
    for rows, (a, r) in zip(tiles, branch):
        merged = ga_ref[rows, :].astype(F32) * a + gr_ref[rows, :].astype(F32) * r
        xn = x_ref[rows, :] + jnp.dot(merged.astype(BF16), wo_ref[...], preferred_element_type=F32)
        xmid_ref[rows, :] = xn
        resid.append(xn)
    scores = []
    for rows, xn in zip(tiles, resid):
        ms = jnp.mean(xn * xn, axis=-1, keepdims=True)
        h2 = xn * lax.rsqrt(ms + EPS) * g2_ref[...]
        for ref, words in zip(h2_refs, _pack_rows(h2)):
            ref[rows, :] = words
        scores.append(jnp.dot(h2.astype(BF16), wr_ref[...], preferred_element_type=F32))
    for rows, logits in zip(tiles, scores):
        real = lax.broadcasted_iota(jnp.int32, logits.shape, 1) < N_EXPERTS
        logits = jnp.where(real, logits, -jnp.inf)
        m = jnp.max(logits, axis=-1, keepdims=True)
        ex = jnp.exp(logits - m)
        aff = ex / jnp.sum(ex, axis=-1, keepdims=True)
        afft_ref[:, rows] = aff.T[:N_EXPERTS, :]


def _merge(attn, retn, ga, gr, x2, wba, wbr, wo, g2, wr):
    n = x2.shape[0]
    tm = TOKEN_TILE
    full = lambda a: pl.BlockSpec(a.shape, lambda i: (0,) * a.ndim)
    row = lambda wd: pl.BlockSpec((tm, wd), lambda i: (i, 0))
    return pl.pallas_call(
        _merge_kernel,
        grid=(n // tm,),
        in_specs=[row(ATTN_WIDTH), row(RET_WIDTH), row(D_MODEL), row(D_MODEL), row(D_MODEL),
                  full(wba), full(wbr), full(wo), full(g2), full(wr)],
        out_specs=[row(D_MODEL), pl.BlockSpec((N_EXPERTS, tm), lambda i: (0, i))] + [row(SC_ROW)] * SC_PIECES,
        out_shape=[jax.ShapeDtypeStruct((n, D_MODEL), F32), jax.ShapeDtypeStruct((N_EXPERTS, n), F32)]
        + [jax.ShapeDtypeStruct((n, SC_ROW), jnp.int32)] * SC_PIECES,
        compiler_params=_cparams(1),
        name="merge",
    )(attn, retn, ga, gr, x2, wba, wbr, wo, g2, wr)


def _select_kernel(aff_ref, u_ref, ls_ref, idx_ref, slot_ref, cs_ref, ce_ref,
                   thr, selbuf, cnt, csr, rank, pieces, offi, *, cap, tb):
    s = pl.program_id(0)
    nch = cap // SELECT_SLOTS
    ps = jnp.where(s < N_EXPERTS, 0, 1)
    later = jnp.maximum(s - N_EXPERTS, 0)
    e = jnp.where(s < N_EXPERTS, s, later // nch)
    j = jnp.where(s < N_EXPERTS, 0, later % nch)
    nblk = SELECT_BLOCKS
    pc = SELECT_SLOTS

    def cumsum(vals):
        inb = jnp.dot(vals.astype(BF16), u_ref[...], preferred_element_type=F32)
        tot = jnp.broadcast_to(inb[:, tb - 1:tb], (nblk, LANES))
        off = jnp.dot(ls_ref[...], tot, preferred_element_type=F32, precision=lax.Precision.HIGHEST)
        return inb, off[:, 0:1], tot[:, 0:1]

    @pl.when(jnp.logical_and(ps == 0, jnp.logical_and(e == 0, j == 0)))
    def _():
        def bit_step(t, curs):
            bit = jnp.left_shift(jnp.int32(1), 30 - t)
            out = []
            for x in range(N_EXPERTS):
                cand = curs[x] | bit
                n_ge = jnp.sum((pltpu.bitcast(aff_ref[x], jnp.int32) >= cand).astype(jnp.int32), keepdims=True)
                out.append(jnp.where(n_ge >= cap, cand, curs[x]))
            return tuple(out)

        found = lax.fori_loop(0, 31, bit_step, tuple(jnp.zeros((1, 1), jnp.int32) for _ in range(N_EXPERTS)))
        for x in range(N_EXPERTS):
            thr[x] = jnp.broadcast_to(found[x], thr.shape[1:])

    @pl.when(jnp.logical_and(ps == 0, j == 0))
    def _():
        bits = pltpu.bitcast(aff_ref[e], jnp.int32)
        limit = thr[e][0:1, 0:1]
        gt = bits > limit
        eq = bits == limit
        need = (cap - jnp.sum(gt.astype(jnp.int32), keepdims=True)).astype(F32)
        eqf = eq.astype(F32)
        eq_in, eq_off, _ = cumsum(eqf)
        eq_rank = eq_in + eq_off - eqf
        sel = jnp.logical_or(gt, jnp.logical_and(eq, eq_rank < need)).astype(F32)
        selbuf[e] = sel.astype(BF16)

        @pl.when(e == 0)
        def _():
            cnt[...] = sel

        @pl.when(e > 0)
        def _():
            cnt[...] = cnt[...] + sel

    @pl.when(jnp.logical_and(ps == 1, j == 0))
    def _():
        @pl.when(e == 0)
        def _():
            c = cnt[...]
            c_in, c_off, _ = cumsum(c)
            start = c_in + c_off - c
            csr[...] = start
            cs_ref[...] = start.astype(jnp.int32)
            ce_ref[...] = (start + c).astype(jnp.int32)
            rank[...] = jnp.zeros_like(rank)

        sel = selbuf[e].astype(F32)
        s_in, s_off, s_tot = cumsum(sel)
        count_t = (s_in + s_off).T
        high = jnp.floor(count_t * (1.0 / 256.0))
        pieces[0] = high.astype(BF16)
        pieces[1] = (count_t - 256.0 * high).astype(BF16)
        slot_ref[0] = (csr[...] + rank[...]).astype(jnp.int32)
        rank[...] = rank[...] + sel
        offi[...] = jnp.broadcast_to(s_off + s_tot, (nblk, LANES))

    @pl.when(ps == 1)
    def _():
        slot = (j * pc + lax.broadcasted_iota(jnp.int32, (1, pc), 1)).astype(F32)
        blk = jnp.sum((offi[:, 0:1] <= slot).astype(jnp.int32), axis=0, keepdims=True)
        onehot = (lax.broadcasted_iota(jnp.int32, (nblk, pc), 0) == blk).astype(BF16)
        counts = (256.0 * jnp.dot(pieces[0], onehot, preferred_element_type=F32)
                  + jnp.dot(pieces[1], onehot, preferred_element_type=F32))
        inb = jnp.sum((counts <= slot + 0.5).astype(jnp.int32), axis=0, keepdims=True)
        idx_ref[0] = blk * tb + inb


def _select(afft, cap):
    n = afft.shape[1]
    nblk = SELECT_BLOCKS
    tb = n // nblk
    pc = SELECT_SLOTS
    assert n % nblk == 0 and tb % LANES == 0 and cap % pc == 0 and cap < 65536
    nch = cap // pc
    aff3 = afft.reshape(N_EXPERTS, nblk, tb)
    upper = jnp.asarray(np.triu(np.ones((tb, tb), np.float32)), BF16)
    lstrict = jnp.asarray(np.tril(np.ones((nblk, nblk), np.float32), -1))
    full = lambda a: pl.BlockSpec(a.shape, lambda s: (0,) * a.ndim)
    idx_spec = pl.BlockSpec((1, 1, pc), lambda s: (jnp.maximum(s - N_EXPERTS, 0), 0, 0))
    slot_spec = pl.BlockSpec((1, nblk, tb), lambda s: (jnp.maximum(s - N_EXPERTS, 0) // nch, 0, 0))
    tok_spec = pl.BlockSpec((nblk, tb), lambda s: (0, 0))
    idx, slots, cs, ce = pl.pallas_call(
        functools.partial(_select_kernel, cap=cap, tb=tb),
        grid=(N_EXPERTS + N_EXPERTS * nch,),
        in_specs=[full(aff3), full(upper), full(lstrict)],
        out_specs=[idx_spec, slot_spec, tok_spec, tok_spec],
        out_shape=[jax.ShapeDtypeStruct((N_EXPERTS * nch, 1, pc), jnp.int32),
                   jax.ShapeDtypeStruct((N_EXPERTS, nblk, tb), jnp.int32),
                   jax.ShapeDtypeStruct((nblk, tb), jnp.int32), jax.ShapeDtypeStruct((nblk, tb), jnp.int32)],
        scratch_shapes=[pltpu.VMEM((N_EXPERTS, 8, LANES), jnp.int32), pltpu.VMEM((N_EXPERTS, nblk, tb), BF16),
                        pltpu.VMEM((nblk, tb), F32), pltpu.VMEM((nblk, tb), F32), pltpu.VMEM((nblk, tb), F32),
                        pltpu.VMEM((2, tb, nblk), BF16), pltpu.VMEM((nblk, LANES), F32)],
        compiler_params=_cparams(1),
        name="select",
    )(aff3, upper, lstrict)
    return idx.reshape(-1), slots.reshape(N_EXPERTS, n), cs.reshape(-1), ce.reshape(-1)


def _slot_rows(afft, slots, idx):
    n = afft.shape[1]
    table = jnp.concatenate([lax.bitcast_convert_type(afft, jnp.int32), slots,
                             jnp.zeros((LANES - 2 * N_EXPERTS, n), jnp.int32)], axis=0).T
    return _sc_gather(table, idx)


def _sc_mesh():
    return plsc.VectorSubcoreMesh(core_axis_name="c", subcore_axis_name="s")


def _sc_scatter(rows, idx, m_out):
    m, d = rows.shape
    assert m % SC_WINDOW == 0

    @functools.partial(pl.kernel, out_type=jax.ShapeDtypeStruct((m_out, d), rows.dtype), mesh=_sc_mesh(),
                       name="sc_scatter")
    def scatter(x_hbm, i_hbm, o_hbm):
        def body(x_vmem, i_vmem):
            pltpu.sync_copy(x_vmem, o_hbm.at[i_vmem.at[0]])

        pltpu.emit_pipeline(
            body,
            grid=(m // SC_WINDOW,),
            in_specs=[pl.BlockSpec((SC_WINDOW, d), lambda i: (i, 0)),
                      pl.BlockSpec((1, SC_WINDOW), lambda i: (0, i))],
            out_specs=[],
            core_axis_name=("c", "s"),
            dimension_semantics=(pltpu.PARALLEL,),
        )(x_hbm, i_hbm)

    return scatter(rows, idx.reshape(1, m))


def _sc_gather(table, idx):
    m = idx.shape[0]
    d = table.shape[1]
    assert m % SC_WINDOW == 0

    @functools.partial(pl.kernel, out_type=jax.ShapeDtypeStruct((m, d), table.dtype), mesh=_sc_mesh(),
                       name="sc_gather")
    def gather(x_hbm, i_hbm, o_hbm):
        def body(i_vmem, o_vmem):
            pltpu.sync_copy(x_hbm.at[i_vmem.at[0]], o_vmem)

        pltpu.emit_pipeline(
            body,
            grid=(m // SC_WINDOW,),
            in_specs=[pl.BlockSpec((1, SC_WINDOW), lambda i: (0, i))],
            out_specs=[pl.BlockSpec((SC_WINDOW, d), lambda i: (i, 0))],
            core_axis_name=("c", "s"),
            dimension_semantics=(pltpu.PARALLEL,),
        )(i_hbm, o_hbm)

    return gather(table, idx.reshape(1, m))


def _row_to_col(row):
    n = row.shape[1]
    eye = lax.broadcasted_iota(jnp.int32, (n, n), 0) == lax.broadcasted_iota(jnp.int32, (n, n), 1)
    return jnp.sum(jnp.where(eye, row, jnp.zeros_like(row)), axis=1, keepdims=True)


def _ffn_kernel(slot_ref, x0_ref, x1_ref, w1_hbm, w3_hbm, w2_hbm, o0_ref, o1_ref, dst_ref, wstage, w1b, w3b, w2b, wsem,
                *, layer):
    e = pl.program_id(0)
    i = pl.program_id(1)

    def weight_copies(expert):
        return [pltpu.make_async_copy(w_hbm.at[layer, expert], wstage.at[k], wsem.at[k])
                for k, w_hbm in enumerate((w1_hbm, w3_hbm, w2_hbm))]

    @pl.when(i == 0)
    def _():
        @pl.when(e == 0)
        def _():
            for cp in weight_copies(0):
                cp.start()
        for cp in weight_copies(e):
            cp.wait()
        w1b[...] = wstage[0].astype(BF16)
        w3b[...] = wstage[1].astype(BF16)
        w2b[...] = wstage[2].astype(BF16)

    @pl.when(jnp.logical_and(i == 1, e + 1 < N_EXPERTS))
    def _():
        for cp in weight_copies(e + 1):
            cp.start()

    info = slot_ref[...]
    lane = lax.broadcasted_iota(jnp.int32, info.shape, 1)
    gate = jnp.sum(jnp.where(lane == e, pltpu.bitcast(info, F32), 0.0), axis=1, keepdims=True)
    info_t = info.T
    word = lax.broadcasted_iota(jnp.int32, info_t.shape, 0)
    dst_ref[0] = jnp.sum(jnp.where(word == N_EXPERTS + e, info_t, 0), axis=0, keepdims=True)

    xs = _unpack_rows([x0_ref[...], x1_ref[...]])
    hg = jnp.dot(xs, w1b[...], preferred_element_type=F32)
    hu = jnp.dot(xs, w3b[...], preferred_element_type=F32)
    hid = (hg * _sigmoid(hg) * hu).astype(BF16)
    out = jnp.dot(hid, w2b[...], preferred_element_type=F32) * gate
    for ref, words in zip((o0_ref, o1_ref), _pack_rows(out)):
        ref[...] = words


def _expert_ffn(xs, slot_rows, w1, w3, w2, layer):
    m = xs[0].shape[0]
    rows = FFN_ROWS
    nt = m // (N_EXPERTS * rows)
    assert SC_PIECES == 2 and nt >= 2
    piece = pl.BlockSpec((rows, SC_ROW), lambda e, i: (e * nt + i, 0))
    any_spec = pl.BlockSpec(memory_space=pl.ANY)
    *outs, dst = pl.pallas_call(
        functools.partial(_ffn_kernel, layer=layer),
        grid=(N_EXPERTS, nt),
        in_specs=[pl.BlockSpec((rows, LANES), lambda e, i: (e * nt + i, 0))] + [piece] * SC_PIECES + [any_spec] * 3,
        out_specs=[piece] * SC_PIECES + [pl.BlockSpec((1, 1, rows), lambda e, i: (e * nt + i, 0, 0))],
        out_shape=[jax.ShapeDtypeStruct((m, SC_ROW), jnp.int32)] * SC_PIECES
        + [jax.ShapeDtypeStruct((m // rows, 1, rows), jnp.int32)],
        scratch_shapes=[pltpu.VMEM((3, D_MODEL, EXPERT_FF), F32), pltpu.VMEM((D_MODEL, EXPERT_FF), BF16),
                        pltpu.VMEM((D_MODEL, EXPERT_FF), BF16), pltpu.VMEM((EXPERT_FF, D_MODEL), BF16),
                        pltpu.SemaphoreType.DMA((3,))],
        compiler_params=_cparams(2),
        name="expert_ffn",
    )(slot_rows, *xs, w1, w3, w2)
    return outs, dst.reshape(-1)


def _combine_kernel(tsub_ref, x_ref, cs_ref, ce_ref, r0_hbm, r1_hbm, o_ref, rbuf, obuf, rows16, sems, osem,
                    *, ntile, total):
    pieces_hbm = (r0_hbm, r1_hbm)
    win = COMBINE_WINDOW
    sub = COMBINE_SUB
    subwin = COMBINE_SUBWIN
    per = COMBINE_TOKENS // sub
    i = pl.program_id(0)
    slot = lax.rem(i, COMBINE_BUFFERS)

    def window_start(t):
        return pl.multiple_of((tsub_ref[t * per] // 16) * 16, 16)

    def copies(t, b):
        s = window_start(t)
        return [pltpu.make_async_copy(pieces_hbm[c].at[pl.ds(s, win)], rbuf.at[b, c], sems.at[b, c])
                for c in range(SC_PIECES)]

    ahead = COMBINE_BUFFERS - 1

    @pl.when(i == 0)
    def _():
        for t in range(min(ahead, ntile)):
            for cp in copies(t, t):
                cp.start()

    @pl.when(i + ahead < ntile)
    def _():
        for cp in copies(i + ahead, lax.rem(i + ahead, COMBINE_BUFFERS)):
            cp.start()

    for cp in copies(i, slot):
        cp.wait()

    first = [_row_to_col(cs_ref[0][:, g * sub:(g + 1) * sub]) for g in range(per)]
    last = [_row_to_col(ce_ref[0][:, g * sub:(g + 1) * sub]) for g in range(per)]

    def zero_unwritten(words, base):
        written = (base + lax.broadcasted_iota(jnp.int32, (win, 1), 0)) < total
        return jnp.where(written, words, 0)

    def owner_matrix(g, base, width):
        r = base + lax.broadcasted_iota(jnp.int32, (1, width), 1)
        return jnp.logical_and(first[g] <= r, r < last[g]).astype(BF16)

    s0 = window_start(i)
    tail = s0 + win > total

    @pl.when(tail)
    def _():
        rows16[...] = _unpack_rows([zero_unwritten(rbuf[slot, c], s0) for c in range(SC_PIECES)])

    @pl.when(jnp.logical_not(tail))
    def _():
        rows16[...] = _unpack_rows([rbuf[slot, c] for c in range(SC_PIECES)])

    offsets = []
    fits = None
    for g in range(per):
        off = (tsub_ref[i * per + g] // 16) * 16 - s0
        ok = jnp.logical_and(tsub_ref[i * per + g + 1] - s0 <= off + subwin, off + subwin <= win)
        fits = ok if fits is None else jnp.logical_and(fits, ok)
        offsets.append(off)

    @pl.when(fits)
    def _():
        for g in range(per):
            tokens = slice(g * sub, (g + 1) * sub)
            off = pl.multiple_of(offsets[g], 16)
            q = owner_matrix(g, s0 + off, subwin)
            o_ref[tokens, :] = x_ref[tokens, :] + jnp.dot(q, rows16[pl.ds(off, subwin), :],
                                                         preferred_element_type=F32)

    @pl.when(jnp.logical_not(fits))
    def _():
        def everyone(base):
            return jnp.concatenate([owner_matrix(g, base, win) for g in range(per)], axis=0)

        y = x_ref[...] + jnp.dot(everyone(s0), rows16[...], preferred_element_type=F32)
        n_extra = jnp.maximum(tsub_ref[(i + 1) * per] - (s0 + win) + win - 1, 0) // win

        def extra(k, acc):
            base = pl.multiple_of(s0 + (k + 1) * win, 16)
            cps = [pltpu.make_async_copy(pieces_hbm[c].at[pl.ds(base, win)], obuf.at[c], osem.at[c])
                   for c in range(SC_PIECES)]
            for cp in cps:
                cp.start()
            for cp in cps:
                cp.wait()
            rows = _unpack_rows([zero_unwritten(obuf[c], base) for c in range(SC_PIECES)])
            return acc + jnp.dot(everyone(base), rows, preferred_element_type=F32)

        o_ref[...] = lax.fori_loop(0, n_extra, extra, y)


def _combine(xmid, cs, ce, pieces, total):
    n = xmid.shape[0]
    tt = COMBINE_TOKENS
    win = COMBINE_WINDOW
    ntile = n // tt
    tsub = jnp.concatenate([cs[::COMBINE_SUB], jnp.full((1,), total, jnp.int32)])
    cs3 = cs.reshape(ntile, 1, tt)
    ce3 = ce.reshape(ntile, 1, tt)
    any_spec = pl.BlockSpec(memory_space=pl.ANY)
    tok = pl.BlockSpec((1, 1, tt), lambda i, ts: (i, 0, 0))
    grid_spec = pltpu.PrefetchScalarGridSpec(
        num_scalar_prefetch=1,
        grid=(ntile,),
        in_specs=[pl.BlockSpec((tt, D_MODEL), lambda i, ts: (i, 0)), tok, tok] + [any_spec] * SC_PIECES,
        out_specs=pl.BlockSpec((tt, D_MODEL), lambda i, ts: (i, 0)),
        scratch_shapes=[pltpu.VMEM((COMBINE_BUFFERS, SC_PIECES, win, SC_ROW), jnp.int32),
                        pltpu.VMEM((SC_PIECES, win, SC_ROW), jnp.int32), pltpu.VMEM((win, D_MODEL), BF16),
                        pltpu.SemaphoreType.DMA((COMBINE_BUFFERS, SC_PIECES)), pltpu.SemaphoreType.DMA((SC_PIECES,))],
    )
    return pl.pallas_call(
        functools.partial(_combine_kernel, ntile=ntile, total=total),
        grid_spec=grid_spec,
        out_shape=jax.ShapeDtypeStruct((n, D_MODEL), F32),
        compiler_params=_cparams(1),
        name="combine",
    )(tsub, xmid, cs3, ce3, *pieces)


def _t5_bucket(rel):
    half = REL_BUCKETS // 2
    max_exact = half // 2
    base = np.where(rel > 0, half, 0)
    n = np.abs(rel)
    large = max_exact + (np.log(np.maximum(n, 1) / max_exact) / math.log(REL_MAX_DIST / max_exact)
                         * (half - max_exact)).astype(np.int32)
    large = np.minimum(large, half - 1)
    return (base + np.where(n < max_exact, n, large)).astype(np.int32)


def _head_perm():
    nq = ATTN_HEADS // 2
    cols = []
    for j in range(nq):
        for half in range(2):
            h = j + nq * half
            cols.extend(range(h * ATTN_HEAD_DIM, (h + 1) * ATTN_HEAD_DIM))
    return np.asarray(cols, np.int32)


def _attn_bias_tables(rel_bias):
    q_pos = np.arange(BLOCK)[:, None]
    k_off = np.arange(3 * BLOCK)[None, :] - BLOCK
    rel = k_off - q_pos
    in_window = np.abs(rel) <= WINDOW
    onehot = jnp.asarray(_t5_bucket(rel)[:, :, None] == np.arange(REL_BUCKETS)[None, None, :], F32)
    bias = jnp.einsum("qkb,bh->hqk", onehot, rel_bias.astype(F32), precision=lax.Precision.HIGHEST)
    col = np.arange(3 * BLOCK)[None, :]
    tables = []
    for valid in (col >= BLOCK, np.ones_like(col, bool), col < 2 * BLOCK):
        t = jnp.where(jnp.asarray(in_window & valid)[None], bias, NEG)
        nq = ATTN_HEADS // 2
        rows = [jnp.concatenate([t[j], t[j + nq]], axis=1) for j in range(nq)]
        tables.append(jnp.concatenate(rows, axis=0))
    return jnp.stack(tables)


def _retention_tables(decay_logit, norm_g):
    cr = RET_CHUNK
    lg = jax.nn.log_sigmoid(decay_logit.astype(F32))
    lgf, lgb = lg[0][:, None, None], lg[1][:, None, None]
    pos = np.arange(cr, dtype=np.float32)
    dist = pos[:, None] - pos[None, :]
    scale = RET_DIM ** -0.5
    dmask = jnp.where(jnp.asarray(dist >= 0)[None],
                      jnp.exp(lgf * np.maximum(dist, 0.0)[None]),
                      jnp.exp(lgb * np.maximum(-dist, 0.0)[None])) * scale
    col = lambda v: jnp.broadcast_to(v[:, :, None], (RET_HEADS, cr, RET_DIM))
    rowf = col(jnp.exp(lg[0][:, None] * pos[None]))
    rowb = col(jnp.exp(lg[1][:, None] * (cr - 1.0 - pos)[None]))
    wkf = col(jnp.exp(lg[0][:, None] * (cr - pos)[None]) * scale)
    wkb = col(jnp.exp(lg[1][:, None] * (pos + 1.0)[None]) * scale)
    dec = jnp.concatenate([jnp.exp(lg[0] * cr), jnp.exp(lg[1] * cr)])
    return dec, dmask, rowf, rowb, wkf, wkb, norm_g.astype(F32)


def _layer(x2, b, s, p):
    qa, ka, va, qr, kr, vr, gr, ga, gt = _in_proj(x2, p["g1"], p["w_in"], p["qg"], p["kg"], p["bdq"], p["bdk"])
    attn = _attention(qa, ka, va, p["bias3"], p["sink"], b, s)
    retn = _retention(qr, kr, vr, gr, p["retn"], b, s)
    xmid, afft, *h2 = _merge(attn, retn, ga, gt, x2, p["wba"], p["wbr"], p["wo"], p["g2"], p["wr"])
    n = b * s
    cap = max(1, EC_CAPACITY_FACTOR * n // N_EXPERTS)
    total = N_EXPERTS * cap
    idx, slots, cs, ce = _select(afft, cap)
    slot_rows = _slot_rows(afft, slots, idx)
    xs = [_sc_gather(piece, idx) for piece in h2]
    outs, dst = _expert_ffn(xs, slot_rows, p["w1"], p["w3"], p["w2"], p["layer"])
    by_token = [_sc_scatter(o, dst, total + COMBINE_WINDOW) for o in outs]
    return _combine(xmid, cs, ce, by_token, total)


def kernel(x_prompt, x_sample, norm_mix_g, w_in, q_norm_g, k_norm_g, attn_sink, rel_bias, retn_decay_logit, retn_norm_g, w_branch_attn, w_branch_retn, w_out, norm_ffn_g, w_router, w_exp_gate, w_exp_up, w_exp_down):
    depth = w_in.shape[0]
    perm = _head_perm()
    bias3 = _attn_bias_tables(rel_bias)
    bdq = jnp.asarray(np.kron(np.eye(ATTN_HEADS), np.ones((ATTN_HEAD_DIM, ATTN_HEAD_DIM))), BF16)
    bdk = jnp.asarray(np.kron(np.eye(ATTN_KV_HEADS), np.ones((ATTN_HEAD_DIM, ATTN_HEAD_DIM))), BF16)
    layers = []
    for l in range(depth):
        w = w_in[l]
        w = jnp.concatenate([w[:, :ATTN_WIDTH][:, perm], w[:, ATTN_WIDTH:]], axis=1).astype(BF16)
        wr = jnp.pad(w_router[l], ((0, 0), (0, LANES - N_EXPERTS))).astype(BF16)
        layers.append(dict(
            g1=norm_mix_g[l].astype(F32)[None], w_in=w,
            qg=(jnp.tile(q_norm_g[l].astype(F32), ATTN_HEADS) * (ATTN_HEAD_DIM ** -0.5))[None],
            kg=jnp.tile(k_norm_g[l].astype(F32), ATTN_KV_HEADS)[None],
            bdq=bdq, bdk=bdk, bias3=bias3, sink=attn_sink[l].astype(F32),
            retn=_retention_tables(retn_decay_logit[l], retn_norm_g[l]),
            wba=w_branch_attn[l][perm, :].astype(BF16), wbr=w_branch_retn[l].astype(BF16),
            wo=w_out[l].astype(BF16), g2=norm_ffn_g[l].astype(F32)[None], wr=wr,
            w1=w_exp_gate, w3=w_exp_up, w2=w_exp_down, layer=l))

    def trunk(x):
        b, s, d = x.shape
        x2 = x.reshape(b * s, d)
        for p in layers:
            x2 = _layer(x2, b, s, p)
        return x2.reshape(b, s, d)

    return (trunk(x_prompt), trunk(x_sample))
```

```python
import functools
import math

import numpy as np
import jax
import jax.numpy as jnp
from jax import lax
from jax.experimental import pallas as pl
from jax.experimental.pallas import tpu as pltpu
from jax.experimental.pallas import tpu_sc as plsc

D_MODEL = 1024
ATTN_HEADS = 8
ATTN_KV_HEADS = 2
ATTN_HEAD_DIM = 64
WINDOW = 128
BLOCK = 128
REL_BUCKETS = 32
REL_MAX_DIST = 128
RET_HEADS = 4
RET_DIM = 128
N_EXPERTS = 16
EC_CAPACITY_FACTOR = 2
EXPERT_FF = 1024
EPS = 1e-6

ATTN_WIDTH = ATTN_HEADS * ATTN_HEAD_DIM
KV_WIDTH = ATTN_KV_HEADS * ATTN_HEAD_DIM
RET_WIDTH = RET_HEADS * RET_DIM
IN_SPLITS = (ATTN_WIDTH, KV_WIDTH, KV_WIDTH, RET_WIDTH, RET_WIDTH, RET_WIDTH, RET_WIDTH, D_MODEL, D_MODEL)
IN_OFFSETS = tuple(int(o) for o in np.cumsum((0,) + IN_SPLITS))

LANES = 128
VMEM_LIMIT_BYTES = 56 * 1024 * 1024

TOKEN_TILE = 1024
MERGE_ROWS = 256
IN_PROJ_TILE = 1024
ATTN_QUERIES = 1024
RET_CHUNK = 256
RET_STEP = 2048
FFN_ROWS = 1024
SELECT_BLOCKS = 128
SELECT_SLOTS = 2048
SC_WINDOW = 128
SC_ROW = 256
PACKED_WIDTH = D_MODEL // 2
SC_PIECES = PACKED_WIDTH // SC_ROW
COMBINE_TOKENS = 512
COMBINE_WINDOW = 1280
COMBINE_SUB = 128
COMBINE_SUBWIN = 384
COMBINE_BUFFERS = 3

F32 = jnp.float32
BF16 = jnp.bfloat16
NEG = -1e30


def _cparams(n_axes, vmem=VMEM_LIMIT_BYTES):
    return pltpu.CompilerParams(dimension_semantics=("arbitrary",) * n_axes, vmem_limit_bytes=vmem)


def _sigmoid(x):
    return 0.5 * jnp.tanh(0.5 * x) + 0.5


HIGH_HALF = -65536


def _pack_rows(x):
    bits = pltpu.bitcast(x.astype(BF16).astype(F32), jnp.int32)
    words = lax.shift_right_logical(bits[:, :PACKED_WIDTH], 16) | (bits[:, PACKED_WIDTH:] & HIGH_HALF)
    return [words[:, c * SC_ROW:(c + 1) * SC_ROW] for c in range(SC_PIECES)]


def _unpack_rows(pieces):
    low = [pltpu.bitcast(lax.shift_left(w, 16), F32) for w in pieces]
    high = [pltpu.bitcast(w & HIGH_HALF, F32) for w in pieces]
    return jnp.concatenate(low + high, axis=1).astype(BF16)


def _in_proj_kernel(x_ref, g_ref, w_ref, qg_ref, kg_ref, bdq_ref, bdk_ref,
                    qa_ref, ka_ref, va_ref, qr_ref, kr_ref, vr_ref, gr_ref, ga_ref, gt_ref):
    x = x_ref[...]
    ms = jnp.mean(x * x, axis=-1, keepdims=True)
    h = (x * lax.rsqrt(ms + EPS) * g_ref[...]).astype(BF16)

    def mm(k):
        return jnp.dot(h, w_ref[:, IN_OFFSETS[k]:IN_OFFSETS[k + 1]], preferred_element_type=F32)

    def head_norm(t, bd_ref, gain_ref):
        ss = jnp.dot((t * t).astype(BF16), bd_ref[...], preferred_element_type=F32)
        return t * lax.rsqrt(ss * (1.0 / ATTN_HEAD_DIM) + EPS) * gain_ref[...]

    q_raw = mm(0)
    k_raw = mm(1)
    for k, ref in ((2, va_ref), (3, qr_ref), (4, kr_ref), (5, vr_ref)):
        ref[...] = mm(k).astype(BF16)
    g = mm(6)
    gr_ref[...] = (g * _sigmoid(g)).astype(BF16)
    ga_ref[...] = _sigmoid(mm(7)).astype(BF16)
    gt_ref[...] = _sigmoid(mm(8)).astype(BF16)
    qa_ref[...] = head_norm(q_raw, bdq_ref, qg_ref).astype(BF16)
    ka_ref[...] = head_norm(k_raw, bdk_ref, kg_ref).astype(BF16)


def _in_proj(x2, g, w, qg, kg, bdq, bdk):
    n = x2.shape[0]
    tm = IN_PROJ_TILE
    full = lambda a: pl.BlockSpec(a.shape, lambda i: (0,) * a.ndim, pipeline_mode=pl.Buffered(1))
    widths = IN_SPLITS
    return pl.pallas_call(
        _in_proj_kernel,
        grid=(n // tm,),
        in_specs=[pl.BlockSpec((tm, D_MODEL), lambda i: (i, 0)), full(g), full(w), full(qg), full(kg),
                  full(bdq), full(bdk)],
        out_specs=[pl.BlockSpec((tm, wd), lambda i: (i, 0)) for wd in widths],
        out_shape=[jax.ShapeDtypeStruct((n, wd), BF16) for wd in widths],
        compiler_params=_cparams(1),
        name="in_proj",
    )(x2, g, w, qg, kg, bdq, bdk)


def _attn_kernel(sink_ref, q_ref, kp_ref, kc_ref, kn_ref, vp_ref, vc_ref, vn_ref, bias_ref, o_ref, *, nsteps):
    nq = ATTN_HEADS // 2
    ni = pl.program_id(1)
    k = jnp.concatenate([kp_ref[...], kc_ref[...], kn_ref[...]], axis=0)
    v = jnp.concatenate([vp_ref[...], vc_ref[...], vn_ref[...]], axis=0)
    low = lax.broadcasted_iota(jnp.int32, k.shape, 1) < ATTN_HEAD_DIM
    zero = jnp.zeros_like(k)
    k_lo, k_hi = jnp.where(low, k, zero), jnp.where(low, zero, k)
    v_lo, v_hi = jnp.where(low, v, zero), jnp.where(low, zero, v)
    nk = 3 * BLOCK
    low_o = lax.broadcasted_iota(jnp.int32, (BLOCK, LANES), 1) < ATTN_HEAD_DIM
    key_low = lax.broadcasted_iota(jnp.int32, (2 * nk, LANES), 0) < nk
    lane_low = lax.broadcasted_iota(jnp.int32, (2 * nk, LANES), 1) < ATTN_HEAD_DIM
    ones_bd = (key_low == lane_low).astype(BF16)
    nsub = ATTN_QUERIES // BLOCK
    for sb in range(nsub):
        rows = slice(sb * BLOCK, (sb + 1) * BLOCK)
        keys = slice(sb * BLOCK, sb * BLOCK + nk)
        q = q_ref[rows, :]
        qs = jnp.concatenate([q[:, j * LANES:(j + 1) * LANES] for j in range(nq)], axis=0)
        kbd = jnp.concatenate([k_lo[keys], k_hi[keys]], axis=0)
        vbd = jnp.concatenate([v_lo[keys], v_hi[keys]], axis=0)
        s = lax.dot_general(qs, kbd, (((1,), (1,)), ((), ())), preferred_element_type=F32)
        if sb == 0:
            table = jnp.where(ni == 0, 0, 1)
        elif sb == nsub - 1:
            table = jnp.where(ni == nsteps - 1, 2, 1)
        else:
            table = 1
        s = s + bias_ref[table]
        probs, sink_terms = [], []
        for j in range(nq):
            row_p, row_sink = [], []
            for half in range(2):
                sj = s[j * BLOCK:(j + 1) * BLOCK, half * nk:(half + 1) * nk]
                sk = sink_ref[j + nq * half]
                m = jnp.maximum(jnp.max(sj, axis=-1, keepdims=True), sk)
                row_p.append(jnp.exp(sj - m).astype(BF16))
                row_sink.append(jnp.exp(sk - m))
            probs.append(jnp.concatenate(row_p, axis=1))
            sink_terms.append(jnp.where(low_o, row_sink[0], row_sink[1]))
        pm = jnp.concatenate(probs, axis=0)
        od = jnp.dot(pm, jnp.concatenate([vbd, ones_bd], axis=1), preferred_element_type=F32)
        o = od[:, :LANES] / (od[:, LANES:] + jnp.concatenate(sink_terms, axis=0))
        for j in range(nq):
            o_ref[rows, j * LANES:(j + 1) * LANES] = o[j * BLOCK:(j + 1) * BLOCK].astype(BF16)


def _attention(qa, ka, va, bias3, sink, b, s):
    tq = ATTN_QUERIES
    per = tq // BLOCK
    nb = s // BLOCK
    nsteps = s // tq
    assert s % tq == 0 and nb >= 2
    n = b * s
    main = lambda wd: pl.BlockSpec((tq, wd), lambda bi, ni: (bi * nsteps + ni, 0))
    prev = pl.BlockSpec((BLOCK, KV_WIDTH), lambda bi, ni: (bi * nb + jnp.maximum(ni * per - 1, 0), 0))
    nxt = pl.BlockSpec((BLOCK, KV_WIDTH), lambda bi, ni: (bi * nb + jnp.minimum(ni * per + per, nb - 1), 0))
    return pl.pallas_call(
        functools.partial(_attn_kernel, nsteps=nsteps),
        grid=(b, nsteps),
        in_specs=[pl.BlockSpec(memory_space=pltpu.SMEM), main(ATTN_WIDTH),
                  prev, main(KV_WIDTH), nxt, prev, main(KV_WIDTH), nxt,
                  pl.BlockSpec(bias3.shape, lambda bi, ni: (0, 0, 0))],
        out_specs=main(ATTN_WIDTH),
        out_shape=jax.ShapeDtypeStruct((n, ATTN_WIDTH), BF16),
        compiler_params=_cparams(2),
        name="attn",
    )(sink, qa, ka, ka, ka, va, va, va, bias3)


def _retn_kernel(dec_ref, q_ref, k_ref, v_ref, g_ref, dmask_ref, rowf_ref, rowb_ref, wkf_ref, wkb_ref, ng_ref,
                 o_ref, tstore, uf, tb, *, nsteps):
    p = pl.program_id(1)
    n = pl.program_id(2)
    cr = RET_CHUNK
    per = RET_STEP // cr
    tn = (((0,), (0,)), ((), ()))
    nt = (((1,), (1,)), ((), ()))
    hs = lambda h: slice(h * RET_DIM, (h + 1) * RET_DIM)

    @pl.when(p == 0)
    def _():
        @pl.when(n == 0)
        def _():
            tb[...] = jnp.zeros_like(tb)
        first_chunk = (nsteps - 1 - n) * per
        for sub in reversed(range(per)):
            rows = slice(sub * cr, (sub + 1) * cr)
            for h in range(RET_HEADS):
                tstore[first_chunk + sub, h] = tb[h].astype(BF16)
                kw = (k_ref[rows, hs(h)].astype(F32) * wkb_ref[h]).astype(BF16)
                upd = lax.dot_general(kw, v_ref[rows, hs(h)], tn, preferred_element_type=F32)
                tb[h] = dec_ref[RET_HEADS + h] * tb[h] + upd

    @pl.when(p == 1)
    def _():
        @pl.when(n == 0)
        def _():
            uf[...] = jnp.zeros_like(uf)
        heads = range(RET_HEADS)
        for sub in range(per):
            rows = slice(sub * cr, (sub + 1) * cr)
            qs = [q_ref[rows, hs(h)] for h in heads]
            ks = [k_ref[rows, hs(h)] for h in heads]
            vs = [v_ref[rows, hs(h)] for h in heads]
            scores = [lax.dot_general(qs[h], ks[h], nt, preferred_element_type=F32) for h in heads]
            states = [jnp.concatenate([uf[h].astype(BF16), tstore[n * per + sub, h]], axis=1) for h in heads]
            cross = [jnp.dot(qs[h], states[h], preferred_element_type=F32) for h in heads]
            update = [lax.dot_general((ks[h].astype(F32) * wkf_ref[h]).astype(BF16), vs[h], tn,
                                      preferred_element_type=F32) for h in heads]
            intra = [jnp.dot((scores[h] * dmask_ref[h]).astype(BF16), vs[h], preferred_element_type=F32)
                     for h in heads]
            for h in heads:
                o = intra[h] + cross[h][:, :RET_DIM] * rowf_ref[h] + cross[h][:, RET_DIM:] * rowb_ref[h]
                mu = jnp.mean(o, axis=-1, keepdims=True)
                d = o - mu
                var = jnp.mean(d * d, axis=-1, keepdims=True)
                on = d * lax.rsqrt(var + EPS) * ng_ref[h:h + 1, :]
                gate = g_ref[rows, hs(h)].astype(F32)
                o_ref[rows, hs(h)] = (gate * on).astype(BF16)
                uf[h] = dec_ref[h] * uf[h] + update[h]


def _retention(qr, kr, vr, gr, tables, b, s):
    dec, dmask, rowf, rowb, wkf, wkb, ng = tables
    rs = RET_STEP
    nsteps = s // rs
    nc = s // RET_CHUNK
    assert s % rs == 0
    n = b * s
    full = lambda a: pl.BlockSpec(a.shape, lambda bi, pi, ni: (0,) * a.ndim)
    fwd_spec = pl.BlockSpec((rs, RET_WIDTH), lambda bi, pi, ni: (bi * nsteps + ni * pi, 0))
    kv_spec = pl.BlockSpec((rs, RET_WIDTH),
                           lambda bi, pi, ni: (bi * nsteps + ni * pi + (1 - pi) * (nsteps - 1 - ni), 0))
    return pl.pallas_call(
        functools.partial(_retn_kernel, nsteps=nsteps),
        grid=(b, 2, nsteps),
        in_specs=[pl.BlockSpec(memory_space=pltpu.SMEM), fwd_spec, kv_spec, kv_spec, fwd_spec,
                  full(dmask), full(rowf), full(rowb), full(wkf), full(wkb), full(ng)],
        out_specs=fwd_spec,
        out_shape=jax.ShapeDtypeStruct((n, RET_WIDTH), BF16),
        scratch_shapes=[pltpu.VMEM((nc, RET_HEADS, RET_DIM, RET_DIM), BF16),
                        pltpu.VMEM((RET_HEADS, RET_DIM, RET_DIM), F32),
                        pltpu.VMEM((RET_HEADS, RET_DIM, RET_DIM), F32)],
        compiler_params=_cparams(3),
        name="retention",
    )(dec, qr, kr, vr, gr, dmask, rowf, rowb, wkf, wkb, ng)


def _merge_kernel(attn_ref, retn_ref, ga_ref, gr_ref, x_ref, wba_ref, wbr_ref, wo_ref, g2_ref, wr_ref,
                  xmid_ref, afft_ref, *h2_refs):
    tiles = [slice(t * MERGE_ROWS, (t + 1) * MERGE_ROWS) for t in range(TOKEN_TILE // MERGE_ROWS)]
    branch = [(jnp.dot(attn_ref[rows, :], wba_ref[...], preferred_element_type=F32),
               jnp.dot(retn_ref[rows, :], wbr_ref[...], preferred_element_type=F32)) for rows in tiles]
    resid = []
    for rows, (a, r) in zip(tiles, branch):
        merged = ga_ref[rows, :].astype(F32) * a + gr_ref[rows, :].astype(F32) * r
        xn = x_ref[rows, :] + jnp.dot(merged.astype(BF16), wo_ref[...], preferred_element_type=F32)
        xmid_ref[rows, :] = xn
        resid.append(xn)
    scores = []
    for rows, xn in zip(tiles, resid):
        ms = jnp.mean(xn * xn, axis=-1, keepdims=True)
        h2 = xn * lax.rsqrt(ms + EPS) * g2_ref[...]
        for ref, words in zip(h2_refs, _pack_rows(h2)):
            ref[rows, :] = words
        scores.append(jnp.dot(h2.astype(BF16), wr_ref[...], preferred_element_type=F32))
    for rows, logits in zip(tiles, scores):
        real = lax.broadcasted_iota(jnp.int32, logits.shape, 1) < N_EXPERTS
        logits = jnp.where(real, logits, -jnp.inf)
        m = jnp.max(logits, axis=-1, keepdims=True)
        ex = jnp.exp(logits - m)
        aff = ex / jnp.sum(ex, axis=-1, keepdims=True)
        afft_ref[:, rows] = aff.T[:N_EXPERTS, :]


def _merge(attn, retn, ga, gr, x2, wba, wbr, wo, g2, wr):
    n = x2.shape[0]
    tm = TOKEN_TILE
    full = lambda a: pl.BlockSpec(a.shape, lambda i: (0,) * a.ndim)
    row = lambda wd: pl.BlockSpec((tm, wd), lambda i: (i, 0))
    return pl.pallas_call(
        _merge_kernel,
        grid=(n // tm,),
        in_specs=[row(ATTN_WIDTH), row(RET_WIDTH), row(D_MODEL), row(D_MODEL), row(D_MODEL),
                  full(wba), full(wbr), full(wo), full(g2), full(wr)],
        out_specs=[row(D_MODEL), pl.BlockSpec((N_EXPERTS, tm), lambda i: (0, i))] + [row(SC_ROW)] * SC_PIECES,
        out_shape=[jax.ShapeDtypeStruct((n, D_MODEL), F32), jax.ShapeDtypeStruct((N_EXPERTS, n), F32)]
        + [jax.ShapeDtypeStruct((n, SC_ROW), jnp.int32)] * SC_PIECES,
        compiler_params=_cparams(1),
        name="merge",
    )(attn, retn, ga, gr, x2, wba, wbr, wo, g2, wr)


def _select_kernel(aff_ref, u_ref, ls_ref, idx_ref, slot_ref, cs_ref, ce_ref,
                   thr, selbuf, cnt, csr, rank, digits, offi, *, cap, tb):
    s = pl.program_id(0)
    nch = cap // SELECT_SLOTS
    ps = jnp.where(s < N_EXPERTS, 0, 1)
    later = jnp.maximum(s - N_EXPERTS, 0)
    e = jnp.where(s < N_EXPERTS, s, later // nch)
    j = jnp.where(s < N_EXPERTS, 0, later % nch)
    nblk = SELECT_BLOCKS
    pc = SELECT_SLOTS

    def cumsum(vals):
        inb = jnp.dot(vals.astype(BF16), u_ref[...], preferred_element_type=F32)
        tot = jnp.broadcast_to(inb[:, tb - 1:tb], (nblk, LANES))
        off = jnp.dot(ls_ref[...], tot, preferred_element_type=F32, precision=lax.Precision.HIGHEST)
        return inb, off[:, 0:1], tot[:, 0:1]

    @pl.when(jnp.logical_and(ps == 0, jnp.logical_and(e == 0, j == 0)))
    def _():
        def bit_step(t, curs):
            bit = jnp.left_shift(jnp.int32(1), 30 - t)
            out = []
            for x in range(N_EXPERTS):
                cand = curs[x] | bit
                n_ge = jnp.sum((pltpu.bitcast(aff_ref[x], jnp.int32) >= cand).astype(jnp.int32), keepdims=True)
                out.append(jnp.where(n_ge >= cap, cand, curs[x]))
            return tuple(out)

        found = lax.fori_loop(0, 31, bit_step, tuple(jnp.zeros((1, 1), jnp.int32) for _ in range(N_EXPERTS)))
        for x in range(N_EXPERTS):
            thr[x] = jnp.broadcast_to(found[x], thr.shape[1:])

    @pl.when(jnp.logical_and(ps == 0, j == 0))
    def _():
        bits = pltpu.bitcast(aff_ref[e], jnp.int32)
        limit = thr[e][0:1, 0:1]
        gt = bits > limit
        eq = bits == limit
        need = (cap - jnp.sum(gt.astype(jnp.int32), keepdims=True)).astype(F32)
        eqf = eq.astype(F32)
        eq_in, eq_off, _ = cumsum(eqf)
        eq_rank = eq_in + eq_off - eqf
        sel = jnp.logical_or(gt, jnp.logical_and(eq, eq_rank < need)).astype(F32)
        selbuf[e] = sel.astype(BF16)

        @pl.when(e == 0)
        def _():
            cnt[...] = sel

        @pl.when(e > 0)
        def _():
            cnt[...] = cnt[...] + sel

    @pl.when(jnp.logical_and(ps == 1, j == 0))
    def _():
        @pl.when(e == 0)
        def _():
            c = cnt[...]
            c_in, c_off, _ = cumsum(c)
            start = c_in + c_off - c
            csr[...] = start
            cs_ref[...] = start.astype(jnp.int32)
            ce_ref[...] = (start + c).astype(jnp.int32)
            rank[...] = jnp.zeros_like(rank)

        sel = selbuf[e].astype(F32)
        s_in, s_off, s_tot = cumsum(sel)
        count_t = (s_in + s_off).T
        high = jnp.floor(count_t * (1.0 / 256.0))
        digits[:, :nblk] = high.astype(BF16)
        digits[:, nblk:] = (count_t - 256.0 * high).astype(BF16)
        slot_ref[0] = (csr[...] + rank[...]).astype(jnp.int32)
        rank[...] = rank[...] + sel
        offi[...] = jnp.broadcast_to(s_off + s_tot, (nblk, LANES))

    @pl.when(ps == 1)
    def _():
        slot = (j * pc + lax.broadcasted_iota(jnp.int32, (1, pc), 1)).astype(F32)
        blk = jnp.sum((offi[:, 0:1] <= slot).astype(jnp.int32), axis=0, keepdims=True)
        owner = jnp.where(lax.broadcasted_iota(jnp.int32, (nblk, pc), 0) == blk, 1.0, 0.0)
        weights = jnp.concatenate([256.0 * owner, owner], axis=0).astype(BF16)
        counts = jnp.dot(digits[...], weights, preferred_element_type=F32)
        inb = jnp.sum((counts <= slot + 0.5).astype(jnp.int32), axis=0, keepdims=True)
        idx_ref[0] = blk * tb + inb


def _select(afft, cap):
    n = afft.shape[1]
    nblk = SELECT_BLOCKS
    tb = n // nblk
    pc = SELECT_SLOTS
    assert n % nblk == 0 and tb % LANES == 0 and cap % pc == 0 and cap < 65536
    nch = cap // pc
    aff3 = afft.reshape(N_EXPERTS, nblk, tb)
    upper = jnp.asarray(np.triu(np.ones((tb, tb), np.float32)), BF16)
    lstrict = jnp.asarray(np.tril(np.ones((nblk, nblk), np.float32), -1))
    full = lambda a: pl.BlockSpec(a.shape, lambda s: (0,) * a.ndim)
    idx_spec = pl.BlockSpec((1, 1, pc), lambda s: (jnp.maximum(s - N_EXPERTS, 0), 0, 0))
    slot_spec = pl.BlockSpec((1, nblk, tb), lambda s: (jnp.maximum(s - N_EXPERTS, 0) // nch, 0, 0))
    tok_spec = pl.BlockSpec((nblk, tb), lambda s: (0, 0))
    idx, slots, cs, ce = pl.pallas_call(
        functools.partial(_select_kernel, cap=cap, tb=tb),
        grid=(N_EXPERTS + N_EXPERTS * nch,),
        in_specs=[full(aff3), full(upper), full(lstrict)],
        out_specs=[idx_spec, slot_spec, tok_spec, tok_spec],
        out_shape=[jax.ShapeDtypeStruct((N_EXPERTS * nch, 1, pc), jnp.int32),
                   jax.ShapeDtypeStruct((N_EXPERTS, nblk, tb), jnp.int32),
                   jax.ShapeDtypeStruct((nblk, tb), jnp.int32), jax.ShapeDtypeStruct((nblk, tb), jnp.int32)],
        scratch_shapes=[pltpu.VMEM((N_EXPERTS, 8, LANES), jnp.int32), pltpu.VMEM((N_EXPERTS, nblk, tb), BF16),
                        pltpu.VMEM((nblk, tb), F32), pltpu.VMEM((nblk, tb), F32), pltpu.VMEM((nblk, tb), F32),
                        pltpu.VMEM((tb, 2 * nblk), BF16), pltpu.VMEM((nblk, LANES), F32)],
        compiler_params=_cparams(1),
        name="select",
    )(aff3, upper, lstrict)
    return idx.reshape(-1), slots.reshape(N_EXPERTS, n), cs.reshape(-1), ce.reshape(-1)


def _slot_rows(afft, slots, idx):
    n = afft.shape[1]
    table = jnp.concatenate([lax.bitcast_convert_type(afft, jnp.int32), slots,
                             jnp.zeros((LANES - 2 * N_EXPERTS, n), jnp.int32)], axis=0).T
    return _sc_gather(table, idx)


def _sc_mesh():
    return plsc.VectorSubcoreMesh(core_axis_name="c", subcore_axis_name="s")


def _sc_scatter(rows, idx, m_out):
    m, d = rows.shape
    assert m % SC_WINDOW == 0

    @functools.partial(pl.kernel, out_type=jax.ShapeDtypeStruct((m_out, d), rows.dtype), mesh=_sc_mesh(),
                       name="sc_scatter")
    def scatter(x_hbm, i_hbm, o_hbm):
        def body(x_vmem, i_vmem):
            pltpu.sync_copy(x_vmem, o_hbm.at[i_vmem.at[0]])

        pltpu.emit_pipeline(
            body,
            grid=(m // SC_WINDOW,),
            in_specs=[pl.BlockSpec((SC_WINDOW, d), lambda i: (i, 0)),
                      pl.BlockSpec((1, SC_WINDOW), lambda i: (0, i))],
            out_specs=[],
            core_axis_name=("c", "s"),
            dimension_semantics=(pltpu.PARALLEL,),
        )(x_hbm, i_hbm)

    return scatter(rows, idx.reshape(1, m))


def _sc_gather(table, idx):
    m = idx.shape[0]
    d = table.shape[1]
    assert m % SC_WINDOW == 0

    @functools.partial(pl.kernel, out_type=jax.ShapeDtypeStruct((m, d), table.dtype), mesh=_sc_mesh(),
                       name="sc_gather")
    def gather(x_hbm, i_hbm, o_hbm):
        def body(i_vmem, o_vmem):
            pltpu.sync_copy(x_hbm.at[i_vmem.at[0]], o_vmem)

        pltpu.emit_pipeline(
            body,
            grid=(m // SC_WINDOW,),
            in_specs=[pl.BlockSpec((1, SC_WINDOW), lambda i: (0, i))],
            out_specs=[pl.BlockSpec((SC_WINDOW, d), lambda i: (i, 0))],
            core_axis_name=("c", "s"),
            dimension_semantics=(pltpu.PARALLEL,),
        )(i_hbm, o_hbm)

    return gather(table, idx.reshape(1, m))


def _row_to_col(row):
    n = row.shape[1]
    eye = lax.broadcasted_iota(jnp.int32, (n, n), 0) == lax.broadcasted_iota(jnp.int32, (n, n), 1)
    return jnp.sum(jnp.where(eye, row, jnp.zeros_like(row)), axis=1, keepdims=True)


def _ffn_kernel(slot_ref, x0_ref, x1_ref, w1_hbm, w3_hbm, w2_hbm, o0_ref, o1_ref, dst_ref, wstage, w1b, w3b, w2b, wsem,
                *, layer):
    e = pl.program_id(0)
    i = pl.program_id(1)

    def weight_copies(expert):
        return [pltpu.make_async_copy(w_hbm.at[layer, expert], wstage.at[k], wsem.at[k])
                for k, w_hbm in enumerate((w1_hbm, w3_hbm, w2_hbm))]

    @pl.when(i == 0)
    def _():
        @pl.when(e == 0)
        def _():
            for cp in weight_copies(0):
                cp.start()
        for cp in weight_copies(e):
            cp.wait()
        w1b[...] = wstage[0].astype(BF16)
        w3b[...] = wstage[1].astype(BF16)
        w2b[...] = wstage[2].astype(BF16)

    @pl.when(jnp.logical_and(i == 1, e + 1 < N_EXPERTS))
    def _():
        for cp in weight_copies(e + 1):
            cp.start()

    info = slot_ref[...]
    lane = lax.broadcasted_iota(jnp.int32, info.shape, 1)
    gate = jnp.sum(jnp.where(lane == e, pltpu.bitcast(info, F32), 0.0), axis=1, keepdims=True)
    info_t = info.T
    word = lax.broadcasted_iota(jnp.int32, info_t.shape, 0)
    dst_ref[0] = jnp.sum(jnp.where(word == N_EXPERTS + e, info_t, 0), axis=0, keepdims=True)

    xs = _unpack_rows([x0_ref[...], x1_ref[...]])
    hg = jnp.dot(xs, w1b[...], preferred_element_type=F32)
    hu = jnp.dot(xs, w3b[...], preferred_element_type=F32)
    hid = (hg * _sigmoid(hg) * hu).astype(BF16)
    out = jnp.dot(hid, w2b[...], preferred_element_type=F32) * gate
    for ref, words in zip((o0_ref, o1_ref), _pack_rows(out)):
        ref[...] = words


def _expert_ffn(xs, slot_rows, w1, w3, w2, layer):
    m = xs[0].shape[0]
    rows = FFN_ROWS
    nt = m // (N_EXPERTS * rows)
    assert SC_PIECES == 2 and nt >= 2
    piece = pl.BlockSpec((rows, SC_ROW), lambda e, i: (e * nt + i, 0))
    any_spec = pl.BlockSpec(memory_space=pl.ANY)
    *outs, dst = pl.pallas_call(
        functools.partial(_ffn_kernel, layer=layer),
        grid=(N_EXPERTS, nt),
        in_specs=[pl.BlockSpec((rows, LANES), lambda e, i: (e * nt + i, 0))] + [piece] * SC_PIECES + [any_spec] * 3,
        out_specs=[piece] * SC_PIECES + [pl.BlockSpec((1, 1, rows), lambda e, i: (e * nt + i, 0, 0))],
        out_shape=[jax.ShapeDtypeStruct((m, SC_ROW), jnp.int32)] * SC_PIECES
        + [jax.ShapeDtypeStruct((m // rows, 1, rows), jnp.int32)],
        scratch_shapes=[pltpu.VMEM((3, D_MODEL, EXPERT_FF), F32), pltpu.VMEM((D_MODEL, EXPERT_FF), BF16),
                        pltpu.VMEM((D_MODEL, EXPERT_FF), BF16), pltpu.VMEM((EXPERT_FF, D_MODEL), BF16),
                        pltpu.SemaphoreType.DMA((3,))],
        compiler_params=_cparams(2),
        name="expert_ffn",
    )(slot_rows, *xs, w1, w3, w2)
    return outs, dst.reshape(-1)


def _combine_kernel(tsub_ref, x_ref, cs_ref, ce_ref, r0_hbm, r1_hbm, o_ref, rbuf, obuf, rows16, sems, osem,
                    *, ntile, total):
    pieces_hbm = (r0_hbm, r1_hbm)
    win = COMBINE_WINDOW
    sub = COMBINE_SUB
    subwin = COMBINE_SUBWIN
    per = COMBINE_TOKENS // sub
    i = pl.program_id(0)
    slot = lax.rem(i, COMBINE_BUFFERS)

    def window_start(t):
        return pl.multiple_of((tsub_ref[t * per] // 16) * 16, 16)

    def copies(t, b):
        s = window_start(t)
        return [pltpu.make_async_copy(pieces_hbm[c].at[pl.ds(s, win)], rbuf.at[b, c], sems.at[b, c])
                for c in range(SC_PIECES)]

    ahead = COMBINE_BUFFERS - 1

    @pl.when(i == 0)
    def _():
        for t in range(min(ahead, ntile)):
            for cp in copies(t, t):
                cp.start()

    @pl.when(i + ahead < ntile)
    def _():
        for cp in copies(i + ahead, lax.rem(i + ahead, COMBINE_BUFFERS)):
            cp.start()

    for cp in copies(i, slot):
        cp.wait()

    first = [_row_to_col(cs_ref[0][:, g * sub:(g + 1) * sub]) for g in range(per)]
    last = [_row_to_col(ce_ref[0][:, g * sub:(g + 1) * sub]) for g in range(per)]

    def zero_unwritten(words, base):
        written = (base + lax.broadcasted_iota(jnp.int32, (win, 1), 0)) < total
        return jnp.where(written, words, 0)

    def owner_matrix(g, base, width):
        r = base + lax.broadcasted_iota(jnp.int32, (1, width), 1)
        return jnp.logical_and(first[g] <= r, r < last[g]).astype(BF16)

    s0 = window_start(i)
    tail = s0 + win > total

    @pl.when(tail)
    def _():
        rows16[...] = _unpack_rows([zero_unwritten(rbuf[slot, c], s0) for c in range(SC_PIECES)])

    @pl.when(jnp.logical_not(tail))
    def _():
        rows16[...] = _unpack_rows([rbuf[slot, c] for c in range(SC_PIECES)])

    offsets = []
    fits = None
    for g in range(per):
        off = (tsub_ref[i * per + g] // 16) * 16 - s0
        ok = jnp.logical_and(tsub_ref[i * per + g + 1] - s0 <= off + subwin, off + subwin <= win)
        fits = ok if fits is None else jnp.logical_and(fits, ok)
        offsets.append(off)

    @pl.when(fits)
    def _():
        for g in range(per):
            tokens = slice(g * sub, (g + 1) * sub)
            off = pl.multiple_of(offsets[g], 16)
            q = owner_matrix(g, s0 + off, subwin)
            o_ref[tokens, :] = x_ref[tokens, :] + jnp.dot(q, rows16[pl.ds(off, subwin), :],
                                                         preferred_element_type=F32)

    @pl.when(jnp.logical_not(fits))
    def _():
        def everyone(base):
            return jnp.concatenate([owner_matrix(g, base, win) for g in range(per)], axis=0)

        y = x_ref[...] + jnp.dot(everyone(s0), rows16[...], preferred_element_type=F32)
        n_extra = jnp.maximum(tsub_ref[(i + 1) * per] - (s0 + win) + win - 1, 0) // win

        def extra(k, acc):
            base = pl.multiple_of(s0 + (k + 1) * win, 16)
            cps = [pltpu.make_async_copy(pieces_hbm[c].at[pl.ds(base, win)], obuf.at[c], osem.at[c])
                   for c in range(SC_PIECES)]
            for cp in cps:
                cp.start()
            for cp in cps:
                cp.wait()
            rows = _unpack_rows([zero_unwritten(obuf[c], base) for c in range(SC_PIECES)])
            return acc + jnp.dot(everyone(base), rows, preferred_element_type=F32)

        o_ref[...] = lax.fori_loop(0, n_extra, extra, y)


def _combine(xmid, cs, ce, pieces, total):
    n = xmid.shape[0]
    tt = COMBINE_TOKENS
    win = COMBINE_WINDOW
    ntile = n // tt
    tsub = jnp.concatenate([cs[::COMBINE_SUB], jnp.full((1,), total, jnp.int32)])
    cs3 = cs.reshape(ntile, 1, tt)
    ce3 = ce.reshape(ntile, 1, tt)
    any_spec = pl.BlockSpec(memory_space=pl.ANY)
    tok = pl.BlockSpec((1, 1, tt), lambda i, ts: (i, 0, 0))
    grid_spec = pltpu.PrefetchScalarGridSpec(
        num_scalar_prefetch=1,
        grid=(ntile,),
        in_specs=[pl.BlockSpec((tt, D_MODEL), lambda i, ts: (i, 0)), tok, tok] + [any_spec] * SC_PIECES,
        out_specs=pl.BlockSpec((tt, D_MODEL), lambda i, ts: (i, 0)),
        scratch_shapes=[pltpu.VMEM((COMBINE_BUFFERS, SC_PIECES, win, SC_ROW), jnp.int32),
                        pltpu.VMEM((SC_PIECES, win, SC_ROW), jnp.int32), pltpu.VMEM((win, D_MODEL), BF16),
                        pltpu.SemaphoreType.DMA((COMBINE_BUFFERS, SC_PIECES)), pltpu.SemaphoreType.DMA((SC_PIECES,))],
    )
    return pl.pallas_call(
        functools.partial(_combine_kernel, ntile=ntile, total=total),
        grid_spec=grid_spec,
        out_shape=jax.ShapeDtypeStruct((n, D_MODEL), F32),
        compiler_params=_cparams(1),
        name="combine",
    )(tsub, xmid, cs3, ce3, *pieces)


def _t5_bucket(rel):
    half = REL_BUCKETS // 2
    max_exact = half // 2
    base = np.where(rel > 0, half, 0)
    n = np.abs(rel)
    large = max_exact + (np.log(np.maximum(n, 1) / max_exact) / math.log(REL_MAX_DIST / max_exact)
                         * (half - max_exact)).astype(np.int32)
    large = np.minimum(large, half - 1)
    return (base + np.where(n < max_exact, n, large)).astype(np.int32)


def _head_perm():
    nq = ATTN_HEADS // 2
    cols = []
    for j in range(nq):
        for half in range(2):
            h = j + nq * half
            cols.extend(range(h * ATTN_HEAD_DIM, (h + 1) * ATTN_HEAD_DIM))
    return np.asarray(cols, np.int32)


def _attn_bias_tables(rel_bias):
    q_pos = np.arange(BLOCK)[:, None]
    k_off = np.arange(3 * BLOCK)[None, :] - BLOCK
    rel = k_off - q_pos
    in_window = np.abs(rel) <= WINDOW
    onehot = jnp.asarray(_t5_bucket(rel)[:, :, None] == np.arange(REL_BUCKETS)[None, None, :], F32)
    bias = jnp.einsum("qkb,bh->hqk", onehot, rel_bias.astype(F32), precision=lax.Precision.HIGHEST)
    col = np.arange(3 * BLOCK)[None, :]
    tables = []
    for valid in (col >= BLOCK, np.ones_like(col, bool), col < 2 * BLOCK):
        t = jnp.where(jnp.asarray(in_window & valid)[None], bias, NEG)
        nq = ATTN_HEADS // 2
        rows = [jnp.concatenate([t[j], t[j + nq]], axis=1) for j in range(nq)]
        tables.append(jnp.concatenate(rows, axis=0))
    return jnp.stack(tables)


def _retention_tables(decay_logit, norm_g):
    cr = RET_CHUNK
    lg = jax.nn.log_sigmoid(decay_logit.astype(F32))
    lgf, lgb = lg[0][:, None, None], lg[1][:, None, None]
    pos = np.arange(cr, dtype=np.float32)
    dist = pos[:, None] - pos[None, :]
    scale = RET_DIM ** -0.5
    dmask = jnp.where(jnp.asarray(dist >= 0)[None],
                      jnp.exp(lgf * np.maximum(dist, 0.0)[None]),
                      jnp.exp(lgb * np.maximum(-dist, 0.0)[None])) * scale
    col = lambda v: jnp.broadcast_to(v[:, :, None], (RET_HEADS, cr, RET_DIM))
    rowf = col(jnp.exp(lg[0][:, None] * pos[None]))
    rowb = col(jnp.exp(lg[1][:, None] * (cr - 1.0 - pos)[None]))
    wkf = col(jnp.exp(lg[0][:, None] * (cr - pos)[None]) * scale)
    wkb = col(jnp.exp(lg[1][:, None] * (pos + 1.0)[None]) * scale)
    dec = jnp.concatenate([jnp.exp(lg[0] * cr), jnp.exp(lg[1] * cr)])
    return dec, dmask, rowf, rowb, wkf, wkb, norm_g.astype(F32)


def _layer(x2, b, s, p):
    qa, ka, va, qr, kr, vr, gr, ga, gt = _in_proj(x2, p["g1"], p["w_in"], p["qg"], p["kg"], p["bdq"], p["bdk"])
    attn = _attention(qa, ka, va, p["bias3"], p["sink"], b, s)
    retn = _retention(qr, kr, vr, gr, p["retn"], b, s)
    xmid, afft, *h2 = _merge(attn, retn, ga, gt, x2, p["wba"], p["wbr"], p["wo"], p["g2"], p["wr"])
    n = b * s
    cap = max(1, EC_CAPACITY_FACTOR * n // N_EXPERTS)
    total = N_EXPERTS * cap
    idx, slots, cs, ce = _select(afft, cap)
    slot_rows = _slot_rows(afft, slots, idx)
    xs = [_sc_gather(piece, idx) for piece in h2]
    outs, dst = _expert_ffn(xs, slot_rows, p["w1"], p["w3"], p["w2"], p["layer"])
    by_token = [_sc_scatter(o, dst, total + COMBINE_WINDOW) for o in outs]
    return _combine(xmid, cs, ce, by_token, total)


def kernel(x_prompt, x_sample, norm_mix_g, w_in, q_norm_g, k_norm_g, attn_sink, rel_bias, retn_decay_logit, retn_norm_g, w_branch_attn, w_branch_retn, w_out, norm_ffn_g, w_router, w_exp_gate, w_exp_up, w_exp_down):
    depth = w_in.shape[0]
    perm = _head_perm()
    bias3 = _attn_bias_tables(rel_bias)
    bdq = jnp.asarray(np.kron(np.eye(ATTN_HEADS), np.ones((ATTN_HEAD_DIM, ATTN_HEAD_DIM))), BF16)
    bdk = jnp.asarray(np.kron(np.eye(ATTN_KV_HEADS), np.ones((ATTN_HEAD_DIM, ATTN_HEAD_DIM))), BF16)
    layers = []
    for l in range(depth):
        w = w_in[l]
        w = jnp.concatenate([w[:, :ATTN_WIDTH][:, perm], w[:, ATTN_WIDTH:]], axis=1).astype(BF16)
        wr = jnp.pad(w_router[l], ((0, 0), (0, LANES - N_EXPERTS))).astype(BF16)
        layers.append(dict(
            g1=norm_mix_g[l].astype(F32)[None], w_in=w,
            qg=(jnp.tile(q_norm_g[l].astype(F32), ATTN_HEADS) * (ATTN_HEAD_DIM ** -0.5))[None],
            kg=jnp.tile(k_norm_g[l].astype(F32), ATTN_KV_HEADS)[None],
            bdq=bdq, bdk=bdk, bias3=bias3, sink=attn_sink[l].astype(F32),
            retn=_retention_tables(retn_decay_logit[l], retn_norm_g[l]),
            wba=w_branch_attn[l][perm, :].astype(BF16), wbr=w_branch_retn[l].astype(BF16),
            wo=w_out[l].astype(BF16), g2=norm_ffn_g[l].astype(F32)[None], wr=wr,
            w1=w_exp_gate, w3=w_exp_up, w2=w_exp_down, layer=l))

    def trunk(x):
        b, s, d = x.shape
        x2 = x.reshape(b * s, d)
        for p in layers:
            x2 = _layer(x2, b, s, p)
        return x2.reshape(b, s, d)

    return (trunk(x_prompt), trunk(x_sample))
```

```python
import functools
import math

import numpy as np
import jax
import jax.numpy as jnp
from jax import lax
from jax.experimental import pallas as pl
from jax.experimental.pallas import tpu as pltpu
from jax.experimental.pallas import tpu_sc as plsc

D_MODEL = 1024
ATTN_HEADS = 8
ATTN_KV_HEADS = 2
ATTN_HEAD_DIM = 64
WINDOW = 128
BLOCK = 128
REL_BUCKETS = 32
REL_MAX_DIST = 128
RET_HEADS = 4
RET_DIM = 128
N_EXPERTS = 16
EC_CAPACITY_FACTOR = 2
EXPERT_FF = 1024
EPS = 1e-6

ATTN_WIDTH = ATTN_HEADS * ATTN_HEAD_DIM
KV_WIDTH = ATTN_KV_HEADS * ATTN_HEAD_DIM
RET_WIDTH = RET_HEADS * RET_DIM
IN_SPLITS = (ATTN_WIDTH, KV_WIDTH, KV_WIDTH, RET_WIDTH, RET_WIDTH, RET_WIDTH, RET_WIDTH, D_MODEL, D_MODEL)
IN_OFFSETS = tuple(int(o) for o in np.cumsum((0,) + IN_SPLITS))

LANES = 128
BF16_TILE_ROWS = 16
VMEM_LIMIT_BYTES = 56 * 1024 * 1024

TOKEN_TILE = 1024
MERGE_ROWS = 256
IN_PROJ_TILE = 1024
ATTN_QUERIES = 1024
RET_CHUNK = 256
RET_STEP = 2048
FFN_ROWS = 1024
SELECT_BLOCKS = 128
SELECT_SLOTS = 2048
SC_WINDOW = 128
SC_ROW = 256
PACKED_WIDTH = D_MODEL // 2
SC_PIECES = PACKED_WIDTH // SC_ROW
COMBINE_TOKENS = 512
COMBINE_WINDOW = 1280
COMBINE_SUB = 128
COMBINE_SUBWIN = 384
COMBINE_BUFFERS = 3

F32 = jnp.float32
BF16 = jnp.bfloat16
NEG = -1e30


def _cparams(n_axes, vmem=VMEM_LIMIT_BYTES):
    return pltpu.CompilerParams(dimension_semantics=("arbitrary",) * n_axes, vmem_limit_bytes=vmem)


def _sigmoid(x):
    return 0.5 * jnp.tanh(0.5 * x) + 0.5


HIGH_HALF = -65536


def _pack_rows(x):
    bits = pltpu.bitcast(x.astype(BF16).astype(F32), jnp.int32)
    words = lax.shift_right_logical(bits[:, :PACKED_WIDTH], 16) | (bits[:, PACKED_WIDTH:] & HIGH_HALF)
    return [words[:, c * SC_ROW:(c + 1) * SC_ROW] for c in range(SC_PIECES)]


def _unpack_rows(pieces):
    low = [pltpu.bitcast(lax.shift_left(w, 16), F32) for w in pieces]
    high = [pltpu.bitcast(w & HIGH_HALF, F32) for w in pieces]
    return jnp.concatenate(low + high, axis=1).astype(BF16)


def _in_proj_kernel(x_ref, g_ref, w_ref, qg_ref, kg_ref, bdq_ref, bdk_ref,
                    qa_ref, ka_ref, va_ref, qr_ref, kr_ref, vr_ref, gr_ref, ga_ref, gt_ref):
    x = x_ref[...]
    ms = jnp.mean(x * x, axis=-1, keepdims=True)
    h = (x * lax.rsqrt(ms + EPS) * g_ref[...]).astype(BF16)

    def mm(k):
        return jnp.dot(h, w_ref[:, IN_OFFSETS[k]:IN_OFFSETS[k + 1]], preferred_element_type=F32)

    def head_norm(t, bd_ref, gain_ref):
        ss = jnp.dot((t * t).astype(BF16), bd_ref[...], preferred_element_type=F32)
        return t * lax.rsqrt(ss * (1.0 / ATTN_HEAD_DIM) + EPS) * gain_ref[...]

    q_raw = mm(0)
    k_raw = mm(1)
    for k, ref in ((2, va_ref), (3, qr_ref), (4, kr_ref), (5, vr_ref)):
        ref[...] = mm(k).astype(BF16)
    g = mm(6)
    gr_ref[...] = (g * _sigmoid(g)).astype(BF16)
    ga_ref[...] = _sigmoid(mm(7)).astype(BF16)
    gt_ref[...] = _sigmoid(mm(8)).astype(BF16)
    qa_ref[...] = head_norm(q_raw, bdq_ref, qg_ref).astype(BF16)
    ka_ref[...] = head_norm(k_raw, bdk_ref, kg_ref).astype(BF16)


def _in_proj(x2, g, w, qg, kg, bdq, bdk):
    n = x2.shape[0]
    tm = IN_PROJ_TILE
    full = lambda a: pl.BlockSpec(a.shape, lambda i: (0,) * a.ndim, pipeline_mode=pl.Buffered(1))
    widths = IN_SPLITS
    return pl.pallas_call(
        _in_proj_kernel,
        grid=(n // tm,),
        in_specs=[pl.BlockSpec((tm, D_MODEL), lambda i: (i, 0)), full(g), full(w), full(qg), full(kg),
                  full(bdq), full(bdk)],
        out_specs=[pl.BlockSpec((tm, wd), lambda i: (i, 0)) for wd in widths],
        out_shape=[jax.ShapeDtypeStruct((n, wd), BF16) for wd in widths],
        compiler_params=_cparams(1),
        name="in_proj",
    )(x2, g, w, qg, kg, bdq, bdk)


def _attn_kernel(sink_ref, q_ref, kp_ref, kc_ref, kn_ref, vp_ref, vc_ref, vn_ref, bias_ref, o_ref, *, nsteps):
    nq = ATTN_HEADS // 2
    ni = pl.program_id(1)
    k = jnp.concatenate([kp_ref[...], kc_ref[...], kn_ref[...]], axis=0)
    v = jnp.concatenate([vp_ref[...], vc_ref[...], vn_ref[...]], axis=0)
    low = lax.broadcasted_iota(jnp.int32, k.shape, 1) < ATTN_HEAD_DIM
    zero = jnp.zeros_like(k)
    k_lo, k_hi = jnp.where(low, k, zero), jnp.where(low, zero, k)
    v_lo, v_hi = jnp.where(low, v, zero), jnp.where(low, zero, v)
    nk = 3 * BLOCK
    low_o = lax.broadcasted_iota(jnp.int32, (BLOCK, LANES), 1) < ATTN_HEAD_DIM
    key_low = lax.broadcasted_iota(jnp.int32, (2 * nk, LANES), 0) < nk
    lane_low = lax.broadcasted_iota(jnp.int32, (2 * nk, LANES), 1) < ATTN_HEAD_DIM
    ones_bd = (key_low == lane_low).astype(BF16)
    nsub = ATTN_QUERIES // BLOCK
    for sb in range(nsub):
        rows = slice(sb * BLOCK, (sb + 1) * BLOCK)
        keys = slice(sb * BLOCK, sb * BLOCK + nk)
        q = q_ref[rows, :]
        qs = jnp.concatenate([q[:, j * LANES:(j + 1) * LANES] for j in range(nq)], axis=0)
        kbd = jnp.concatenate([k_lo[keys], k_hi[keys]], axis=0)
        vbd = jnp.concatenate([v_lo[keys], v_hi[keys]], axis=0)
        s = lax.dot_general(qs, kbd, (((1,), (1,)), ((), ())), preferred_element_type=F32)
        if sb == 0:
            table = jnp.where(ni == 0, 0, 1)
        elif sb == nsub - 1:
            table = jnp.where(ni == nsteps - 1, 2, 1)
        else:
            table = 1
        s = s + bias_ref[table]
        probs, sink_terms = [], []
        for j in range(nq):
            row_p, row_sink = [], []
            for half in range(2):
                sj = s[j * BLOCK:(j + 1) * BLOCK, half * nk:(half + 1) * nk]
                sk = sink_ref[j + nq * half]
                m = jnp.maximum(jnp.max(sj, axis=-1, keepdims=True), sk)
                row_p.append(jnp.exp(sj - m).astype(BF16))
                row_sink.append(jnp.exp(sk - m))
            probs.append(jnp.concatenate(row_p, axis=1))
            sink_terms.append(jnp.where(low_o, row_sink[0], row_sink[1]))
        pm = jnp.concatenate(probs, axis=0)
        od = jnp.dot(pm, jnp.concatenate([vbd, ones_bd], axis=1), preferred_element_type=F32)
        o = od[:, :LANES] / (od[:, LANES:] + jnp.concatenate(sink_terms, axis=0))
        for j in range(nq):
            o_ref[rows, j * LANES:(j + 1) * LANES] = o[j * BLOCK:(j + 1) * BLOCK].astype(BF16)


def _attention(qa, ka, va, bias3, sink, b, s):
    tq = ATTN_QUERIES
    per = tq // BLOCK
    nb = s // BLOCK
    nsteps = s // tq
    assert s % tq == 0 and nb >= 2
    n = b * s
    main = lambda wd: pl.BlockSpec((tq, wd), lambda bi, ni: (bi * nsteps + ni, 0))
    prev = pl.BlockSpec((BLOCK, KV_WIDTH), lambda bi, ni: (bi * nb + jnp.maximum(ni * per - 1, 0), 0))
    nxt = pl.BlockSpec((BLOCK, KV_WIDTH), lambda bi, ni: (bi * nb + jnp.minimum(ni * per + per, nb - 1), 0))
    return pl.pallas_call(
        functools.partial(_attn_kernel, nsteps=nsteps),
        grid=(b, nsteps),
        in_specs=[pl.BlockSpec(memory_space=pltpu.SMEM), main(ATTN_WIDTH),
                  prev, main(KV_WIDTH), nxt, prev, main(KV_WIDTH), nxt,
                  pl.BlockSpec(bias3.shape, lambda bi, ni: (0, 0, 0))],
        out_specs=main(ATTN_WIDTH),
        out_shape=jax.ShapeDtypeStruct((n, ATTN_WIDTH), BF16),
        compiler_params=_cparams(2),
        name="attn",
    )(sink, qa, ka, ka, ka, va, va, va, bias3)


def _retn_kernel(dec_ref, q_ref, k_ref, v_ref, g_ref, dmask_ref, rowf_ref, rowb_ref, wkf_ref, wkb_ref, ng_ref,
                 o_ref, tstore, uf, tb, *, nsteps):
    p = pl.program_id(1)
    n = pl.program_id(2)
    cr = RET_CHUNK
    per = RET_STEP // cr
    tn = (((0,), (0,)), ((), ()))
    nt = (((1,), (1,)), ((), ()))
    hs = lambda h: slice(h * RET_DIM, (h + 1) * RET_DIM)

    @pl.when(p == 0)
    def _():
        @pl.when(n == 0)
        def _():
            tb[...] = jnp.zeros_like(tb)
        first_chunk = (nsteps - 1 - n) * per
        for sub in reversed(range(per)):
            rows = slice(sub * cr, (sub + 1) * cr)
            for h in range(RET_HEADS):
                tstore[first_chunk + sub, h] = tb[h].astype(BF16)
                kw = (k_ref[rows, hs(h)].astype(F32) * wkb_ref[h]).astype(BF16)
                upd = lax.dot_general(kw, v_ref[rows, hs(h)], tn, preferred_element_type=F32)
                tb[h] = dec_ref[RET_HEADS + h] * tb[h] + upd

    @pl.when(p == 1)
    def _():
        @pl.when(n == 0)
        def _():
            uf[...] = jnp.zeros_like(uf)
        heads = range(RET_HEADS)
        for sub in range(per):
            rows = slice(sub * cr, (sub + 1) * cr)
            qs = [q_ref[rows, hs(h)] for h in heads]
            ks = [k_ref[rows, hs(h)] for h in heads]
            vs = [v_ref[rows, hs(h)] for h in heads]
            scores = [lax.dot_general(qs[h], ks[h], nt, preferred_element_type=F32) for h in heads]
            states = [jnp.concatenate([uf[h].astype(BF16), tstore[n * per + sub, h]], axis=1) for h in heads]
            cross = [jnp.dot(qs[h], states[h], preferred_element_type=F32) for h in heads]
            update = [lax.dot_general((ks[h].astype(F32) * wkf_ref[h]).astype(BF16), vs[h], tn,
                                      preferred_element_type=F32) for h in heads]
            intra = [jnp.dot((scores[h] * dmask_ref[h]).astype(BF16), vs[h], preferred_element_type=F32)
                     for h in heads]
            for h in heads:
                o = intra[h] + cross[h][:, :RET_DIM] * rowf_ref[h] + cross[h][:, RET_DIM:] * rowb_ref[h]
                mu = jnp.mean(o, axis=-1, keepdims=True)
                d = o - mu
                var = jnp.mean(d * d, axis=-1, keepdims=True)
                on = d * lax.rsqrt(var + EPS) * ng_ref[h:h + 1, :]
                gate = g_ref[rows, hs(h)].astype(F32)
                o_ref[rows, hs(h)] = (gate * on).astype(BF16)
                uf[h] = dec_ref[h] * uf[h] + update[h]


def _retention(qr, kr, vr, gr, tables, b, s):
    dec, dmask, rowf, rowb, wkf, wkb, ng = tables
    rs = RET_STEP
    nsteps = s // rs
    nc = s // RET_CHUNK
    assert s % rs == 0
    n = b * s
    full = lambda a: pl.BlockSpec(a.shape, lambda bi, pi, ni: (0,) * a.ndim)
    fwd_spec = pl.BlockSpec((rs, RET_WIDTH), lambda bi, pi, ni: (bi * nsteps + ni * pi, 0))
    kv_spec = pl.BlockSpec((rs, RET_WIDTH),
                           lambda bi, pi, ni: (bi * nsteps + ni * pi + (1 - pi) * (nsteps - 1 - ni), 0))
    return pl.pallas_call(
        functools.partial(_retn_kernel, nsteps=nsteps),
        grid=(b, 2, nsteps),
        in_specs=[pl.BlockSpec(memory_space=pltpu.SMEM), fwd_spec, kv_spec, kv_spec, fwd_spec,
                  full(dmask), full(rowf), full(rowb), full(wkf), full(wkb), full(ng)],
        out_specs=fwd_spec,
        out_shape=jax.ShapeDtypeStruct((n, RET_WIDTH), BF16),
        scratch_shapes=[pltpu.VMEM((nc, RET_HEADS, RET_DIM, RET_DIM), BF16),
                        pltpu.VMEM((RET_HEADS, RET_DIM, RET_DIM), F32),
                        pltpu.VMEM((RET_HEADS, RET_DIM, RET_DIM), F32)],
        compiler_params=_cparams(3),
        name="retention",
    )(dec, qr, kr, vr, gr, dmask, rowf, rowb, wkf, wkb, ng)


def _merge_kernel(attn_ref, retn_ref, ga_ref, gr_ref, x_ref, wba_ref, wbr_ref, wo_ref, g2_ref, wr_ref,
                  xmid_ref, afft_ref, *h2_refs):
    tiles = [slice(t * MERGE_ROWS, (t + 1) * MERGE_ROWS) for t in range(TOKEN_TILE // MERGE_ROWS)]
    branch = [(jnp.dot(attn_ref[rows, :], wba_ref[...], preferred_element_type=F32),
               jnp.dot(retn_ref[rows, :], wbr_ref[...], preferred_element_type=F32)) for rows in tiles]
    resid = []
    for rows, (a, r) in zip(tiles, branch):
        merged = ga_ref[rows, :].astype(F32) * a + gr_ref[rows, :].astype(F32) * r
        xn = x_ref[rows, :] + jnp.dot(merged.astype(BF16), wo_ref[...], preferred_element_type=F32)
        xmid_ref[rows, :] = xn
        resid.append(xn)
    scores = []
    for rows, xn in zip(tiles, resid):
        ms = jnp.mean(xn * xn, axis=-1, keepdims=True)
        h2 = xn * lax.rsqrt(ms + EPS) * g2_ref[...]
        for ref, words in zip(h2_refs, _pack_rows(h2)):
            ref[rows, :] = words
        scores.append(jnp.dot(h2.astype(BF16), wr_ref[...], preferred_element_type=F32))
    for rows, logits in zip(tiles, scores):
        real = lax.broadcasted_iota(jnp.int32, logits.shape, 1) < N_EXPERTS
        logits = jnp.where(real, logits, -jnp.inf)
        m = jnp.max(logits, axis=-1, keepdims=True)
        ex = jnp.exp(logits - m)
        aff = ex / jnp.sum(ex, axis=-1, keepdims=True)
        afft_ref[:, rows] = aff.T[:N_EXPERTS, :]


def _merge(attn, retn, ga, gr, x2, wba, wbr, wo, g2, wr):
    n = x2.shape[0]
    tm = TOKEN_TILE
    full = lambda a: pl.BlockSpec(a.shape, lambda i: (0,) * a.ndim)
    row = lambda wd: pl.BlockSpec((tm, wd), lambda i: (i, 0))
    return pl.pallas_call(
        _merge_kernel,
        grid=(n // tm,),
        in_specs=[row(ATTN_WIDTH), row(RET_WIDTH), row(D_MODEL), row(D_MODEL), row(D_MODEL),
                  full(wba), full(wbr), full(wo), full(g2), full(wr)],
        out_specs=[row(D_MODEL), pl.BlockSpec((N_EXPERTS, tm), lambda i: (0, i))] + [row(SC_ROW)] * SC_PIECES,
        out_shape=[jax.ShapeDtypeStruct((n, D_MODEL), F32), jax.ShapeDtypeStruct((N_EXPERTS, n), F32)]
        + [jax.ShapeDtypeStruct((n, SC_ROW), jnp.int32)] * SC_PIECES,
        compiler_params=_cparams(1),
        name="merge",
    )(attn, retn, ga, gr, x2, wba, wbr, wo, g2, wr)


def _select_kernel(aff_ref, u_ref, ls_ref, idx_ref, slot_ref, cs_ref, ce_ref,
                   thr, selbuf, cnt, csr, rank, digits, offi, *, cap, tb):
    s = pl.program_id(0)
    nch = cap // SELECT_SLOTS
    ps = jnp.where(s < N_EXPERTS, 0, 1)
    later = jnp.maximum(s - N_EXPERTS, 0)
    e = jnp.where(s < N_EXPERTS, s, later // nch)
    j = jnp.where(s < N_EXPERTS, 0, later % nch)
    nblk = SELECT_BLOCKS
    pc = SELECT_SLOTS

    def cumsum(vals):
        inb = jnp.dot(vals.astype(BF16), u_ref[...], preferred_element_type=F32)
        tot = jnp.broadcast_to(inb[:, tb - 1:tb], (nblk, LANES))
        off = jnp.dot(ls_ref[...], tot, preferred_element_type=F32, precision=lax.Precision.HIGHEST)
        return inb, off[:, 0:1], tot[:, 0:1]

    @pl.when(jnp.logical_and(ps == 0, jnp.logical_and(e == 0, j == 0)))
    def _():
        def bit_step(t, curs):
            bit = jnp.left_shift(jnp.int32(1), 30 - t)
            out = []
            for x in range(N_EXPERTS):
                cand = curs[x] | bit
                n_ge = jnp.sum((pltpu.bitcast(aff_ref[x], jnp.int32) >= cand).astype(jnp.int32), keepdims=True)
                out.append(jnp.where(n_ge >= cap, cand, curs[x]))
            return tuple(out)

        found = lax.fori_loop(0, 31, bit_step, tuple(jnp.zeros((1, 1), jnp.int32) for _ in range(N_EXPERTS)))
        for x in range(N_EXPERTS):
            thr[x] = jnp.broadcast_to(found[x], thr.shape[1:])

    @pl.when(jnp.logical_and(ps == 0, j == 0))
    def _():
        bits = pltpu.bitcast(aff_ref[e], jnp.int32)
        limit = thr[e][0:1, 0:1]
        gt = bits > limit
        eq = bits == limit
        need = (cap - jnp.sum(gt.astype(jnp.int32), keepdims=True)).astype(F32)
        eqf = eq.astype(F32)
        eq_in, eq_off, _ = cumsum(eqf)
        eq_rank = eq_in + eq_off - eqf
        sel = jnp.logical_or(gt, jnp.logical_and(eq, eq_rank < need)).astype(F32)
        selbuf[e] = sel.astype(BF16)

        @pl.when(e == 0)
        def _():
            cnt[...] = sel

        @pl.when(e > 0)
        def _():
            cnt[...] = cnt[...] + sel

    @pl.when(jnp.logical_and(ps == 1, j == 0))
    def _():
        @pl.when(e == 0)
        def _():
            c = cnt[...]
            c_in, c_off, _ = cumsum(c)
            start = c_in + c_off - c
            csr[...] = start
            cs_ref[...] = start.astype(jnp.int32)
            ce_ref[...] = (start + c).astype(jnp.int32)
            rank[...] = jnp.zeros_like(rank)

        sel = selbuf[e].astype(F32)
        s_in, s_off, s_tot = cumsum(sel)
        count_t = (s_in + s_off).T
        high = jnp.floor(count_t * (1.0 / 256.0))
        digits[:, :nblk] = high.astype(BF16)
        digits[:, nblk:] = (count_t - 256.0 * high).astype(BF16)
        slot_ref[0] = (csr[...] + rank[...]).astype(jnp.int32)
        rank[...] = rank[...] + sel
        offi[...] = jnp.broadcast_to(s_off + s_tot, (nblk, LANES))

    @pl.when(ps == 1)
    def _():
        slot = (j * pc + lax.broadcasted_iota(jnp.int32, (1, pc), 1)).astype(F32)
        blk = jnp.sum((offi[:, 0:1] <= slot).astype(jnp.int32), axis=0, keepdims=True)
        owner = jnp.where(lax.broadcasted_iota(jnp.int32, (nblk, pc), 0) == blk, 1.0, 0.0)
        weights = jnp.concatenate([256.0 * owner, owner], axis=0).astype(BF16)
        counts = jnp.dot(digits[...], weights, preferred_element_type=F32)
        inb = jnp.sum((counts <= slot + 0.5).astype(jnp.int32), axis=0, keepdims=True)
        idx_ref[0] = blk * tb + inb


def _select(afft, cap):
    n = afft.shape[1]
    nblk = SELECT_BLOCKS
    tb = n // nblk
    pc = SELECT_SLOTS
    assert n % nblk == 0 and tb % LANES == 0 and cap % pc == 0 and cap < 65536
    nch = cap // pc
    aff3 = afft.reshape(N_EXPERTS, nblk, tb)
    upper = jnp.asarray(np.triu(np.ones((tb, tb), np.float32)), BF16)
    lstrict = jnp.asarray(np.tril(np.ones((nblk, nblk), np.float32), -1))
    full = lambda a: pl.BlockSpec(a.shape, lambda s: (0,) * a.ndim)
    idx_spec = pl.BlockSpec((1, 1, pc), lambda s: (jnp.maximum(s - N_EXPERTS, 0), 0, 0))
    slot_spec = pl.BlockSpec((1, nblk, tb), lambda s: (jnp.maximum(s - N_EXPERTS, 0) // nch, 0, 0))
    tok_spec = pl.BlockSpec((nblk, tb), lambda s: (0, 0))
    idx, slots, cs, ce = pl.pallas_call(
        functools.partial(_select_kernel, cap=cap, tb=tb),
        grid=(N_EXPERTS + N_EXPERTS * nch,),
        in_specs=[full(aff3), full(upper), full(lstrict)],
        out_specs=[idx_spec, slot_spec, tok_spec, tok_spec],
        out_shape=[jax.ShapeDtypeStruct((N_EXPERTS * nch, 1, pc), jnp.int32),
                   jax.ShapeDtypeStruct((N_EXPERTS, nblk, tb), jnp.int32),
                   jax.ShapeDtypeStruct((nblk, tb), jnp.int32), jax.ShapeDtypeStruct((nblk, tb), jnp.int32)],
        scratch_shapes=[pltpu.VMEM((N_EXPERTS, 8, LANES), jnp.int32), pltpu.VMEM((N_EXPERTS, nblk, tb), BF16),
                        pltpu.VMEM((nblk, tb), F32), pltpu.VMEM((nblk, tb), F32), pltpu.VMEM((nblk, tb), F32),
                        pltpu.VMEM((tb, 2 * nblk), BF16), pltpu.VMEM((nblk, LANES), F32)],
        compiler_params=_cparams(1),
        name="select",
    )(aff3, upper, lstrict)
    return idx.reshape(-1), slots.reshape(N_EXPERTS, n), cs.reshape(-1), ce.reshape(-1)


def _slot_rows(afft, slots, idx):
    n = afft.shape[1]
    table = jnp.concatenate([lax.bitcast_convert_type(afft, jnp.int32), slots,
                             jnp.zeros((LANES - 2 * N_EXPERTS, n), jnp.int32)], axis=0).T
    return _sc_gather(table, idx)


def _sc_mesh():
    return plsc.VectorSubcoreMesh(core_axis_name="c", subcore_axis_name="s")


def _sc_scatter(rows, idx, m_out):
    m, d = rows.shape
    assert m % SC_WINDOW == 0

    @functools.partial(pl.kernel, out_type=jax.ShapeDtypeStruct((m_out, d), rows.dtype), mesh=_sc_mesh(),
                       name="sc_scatter")
    def scatter(x_hbm, i_hbm, o_hbm):
        def body(x_vmem, i_vmem):
            pltpu.sync_copy(x_vmem, o_hbm.at[i_vmem.at[0]])

        pltpu.emit_pipeline(
            body,
            grid=(m // SC_WINDOW,),
            in_specs=[pl.BlockSpec((SC_WINDOW, d), lambda i: (i, 0)),
                      pl.BlockSpec((1, SC_WINDOW), lambda i: (0, i))],
            out_specs=[],
            core_axis_name=("c", "s"),
            dimension_semantics=(pltpu.PARALLEL,),
        )(x_hbm, i_hbm)

    return scatter(rows, idx.reshape(1, m))


def _sc_gather(table, idx):
    m = idx.shape[0]
    d = table.shape[1]
    assert m % SC_WINDOW == 0

    @functools.partial(pl.kernel, out_type=jax.ShapeDtypeStruct((m, d), table.dtype), mesh=_sc_mesh(),
                       name="sc_gather")
    def gather(x_hbm, i_hbm, o_hbm):
        def body(i_vmem, o_vmem):
            pltpu.sync_copy(x_hbm.at[i_vmem.at[0]], o_vmem)

        pltpu.emit_pipeline(
            body,
            grid=(m // SC_WINDOW,),
            in_specs=[pl.BlockSpec((1, SC_WINDOW), lambda i: (0, i))],
            out_specs=[pl.BlockSpec((SC_WINDOW, d), lambda i: (i, 0))],
            core_axis_name=("c", "s"),
            dimension_semantics=(pltpu.PARALLEL,),
        )(i_hbm, o_hbm)

    return gather(table, idx.reshape(1, m))


def _row_to_col(row):
    n = row.shape[1]
    eye = lax.broadcasted_iota(jnp.int32, (n, n), 0) == lax.broadcasted_iota(jnp.int32, (n, n), 1)
    return jnp.sum(jnp.where(eye, row, jnp.zeros_like(row)), axis=1, keepdims=True)


def _ffn_kernel(slot_ref, x0_ref, x1_ref, w1_hbm, w3_hbm, w2_hbm, o0_ref, o1_ref, dst_ref, wstage, w1b, w3b, w2b, wsem,
                *, layer):
    e = pl.program_id(0)
    i = pl.program_id(1)

    def weight_copies(expert):
        return [pltpu.make_async_copy(w_hbm.at[layer, expert], wstage.at[k], wsem.at[k])
                for k, w_hbm in enumerate((w1_hbm, w3_hbm, w2_hbm))]

    @pl.when(i == 0)
    def _():
        @pl.when(e == 0)
        def _():
            for cp in weight_copies(0):
                cp.start()
        for cp in weight_copies(e):
            cp.wait()
        w1b[...] = wstage[0].astype(BF16)
        w3b[...] = wstage[1].astype(BF16)
        w2b[...] = wstage[2].astype(BF16)

    @pl.when(jnp.logical_and(i == 1, e + 1 < N_EXPERTS))
    def _():
        for cp in weight_copies(e + 1):
            cp.start()

    info = slot_ref[...]
    lane = lax.broadcasted_iota(jnp.int32, info.shape, 1)
    gate = jnp.sum(jnp.where(lane == e, pltpu.bitcast(info, F32), 0.0), axis=1, keepdims=True)
    info_t = info.T
    word = lax.broadcasted_iota(jnp.int32, info_t.shape, 0)
    dst_ref[0] = jnp.sum(jnp.where(word == N_EXPERTS + e, info_t, 0), axis=0, keepdims=True)

    xs = _unpack_rows([x0_ref[...], x1_ref[...]])
    hg = jnp.dot(xs, w1b[...], preferred_element_type=F32)
    hu = jnp.dot(xs, w3b[...], preferred_element_type=F32)
    hid = (hg * _sigmoid(hg) * hu).astype(BF16)
    out = jnp.dot(hid, w2b[...], preferred_element_type=F32) * gate
    for ref, words in zip((o0_ref, o1_ref), _pack_rows(out)):
        ref[...] = words


def _expert_ffn(xs, slot_rows, w1, w3, w2, layer):
    m = xs[0].shape[0]
    rows = FFN_ROWS
    nt = m // (N_EXPERTS * rows)
    assert SC_PIECES == 2 and nt >= 2
    piece = pl.BlockSpec((rows, SC_ROW), lambda e, i: (e * nt + i, 0))
    any_spec = pl.BlockSpec(memory_space=pl.ANY)
    *outs, dst = pl.pallas_call(
        functools.partial(_ffn_kernel, layer=layer),
        grid=(N_EXPERTS, nt),
        in_specs=[pl.BlockSpec((rows, LANES), lambda e, i: (e * nt + i, 0))] + [piece] * SC_PIECES + [any_spec] * 3,
        out_specs=[piece] * SC_PIECES + [pl.BlockSpec((1, 1, rows), lambda e, i: (e * nt + i, 0, 0))],
        out_shape=[jax.ShapeDtypeStruct((m, SC_ROW), jnp.int32)] * SC_PIECES
        + [jax.ShapeDtypeStruct((m // rows, 1, rows), jnp.int32)],
        scratch_shapes=[pltpu.VMEM((3, D_MODEL, EXPERT_FF), F32), pltpu.VMEM((D_MODEL, EXPERT_FF), BF16),
                        pltpu.VMEM((D_MODEL, EXPERT_FF), BF16), pltpu.VMEM((EXPERT_FF, D_MODEL), BF16),
                        pltpu.SemaphoreType.DMA((3,))],
        compiler_params=_cparams(2),
        name="expert_ffn",
    )(slot_rows, *xs, w1, w3, w2)
    return outs, dst.reshape(-1)


def _combine_kernel(tsub_ref, x_ref, cs_ref, ce_ref, r0_hbm, r1_hbm, o_ref, rbuf, obuf, rows16, sems, osem,
                    *, ntile, total):
    pieces_hbm = (r0_hbm, r1_hbm)
    win = COMBINE_WINDOW
    sub = COMBINE_SUB
    subwin = COMBINE_SUBWIN
    per = COMBINE_TOKENS // sub
    i = pl.program_id(0)
    slot = lax.rem(i, COMBINE_BUFFERS)

    def window_start(t):
        return pl.multiple_of((tsub_ref[t * per] // BF16_TILE_ROWS) * BF16_TILE_ROWS, BF16_TILE_ROWS)

    def copies(t, b):
        s = window_start(t)
        return [pltpu.make_async_copy(pieces_hbm[c].at[pl.ds(s, win)], rbuf.at[b, c], sems.at[b, c])
                for c in range(SC_PIECES)]

    ahead = COMBINE_BUFFERS - 1

    @pl.when(i == 0)
    def _():
        for t in range(min(ahead, ntile)):
            for cp in copies(t, t):
                cp.start()

    @pl.when(i + ahead < ntile)
    def _():
        for cp in copies(i + ahead, lax.rem(i + ahead, COMBINE_BUFFERS)):
            cp.start()

    for cp in copies(i, slot):
        cp.wait()

    first = [_row_to_col(cs_ref[0][:, g * sub:(g + 1) * sub]) for g in range(per)]
    last = [_row_to_col(ce_ref[0][:, g * sub:(g + 1) * sub]) for g in range(per)]

    def zero_unwritten(words, base):
        written = (base + lax.broadcasted_iota(jnp.int32, (win, 1), 0)) < total
        return jnp.where(written, words, 0)

    def owner_matrix(g, base, width):
        r = base + lax.broadcasted_iota(jnp.int32, (1, width), 1)
        return jnp.logical_and(first[g] <= r, r < last[g]).astype(BF16)

    s0 = window_start(i)
    tail = s0 + win > total

    @pl.when(tail)
    def _():
        rows16[...] = _unpack_rows([zero_unwritten(rbuf[slot, c], s0) for c in range(SC_PIECES)])

    @pl.when(jnp.logical_not(tail))
    def _():
        rows16[...] = _unpack_rows([rbuf[slot, c] for c in range(SC_PIECES)])

    offsets = []
    fits = None
    for g in range(per):
        off = (tsub_ref[i * per + g] // BF16_TILE_ROWS) * BF16_TILE_ROWS - s0
        ok = jnp.logical_and(tsub_ref[i * per + g + 1] - s0 <= off + subwin, off + subwin <= win)
        fits = ok if fits is None else jnp.logical_and(fits, ok)
        offsets.append(off)

    @pl.when(fits)
    def _():
        for g in range(per):
            tokens = slice(g * sub, (g + 1) * sub)
            off = pl.multiple_of(offsets[g], BF16_TILE_ROWS)
            q = owner_matrix(g, s0 + off, subwin)
            o_ref[tokens, :] = x_ref[tokens, :] + jnp.dot(q, rows16[pl.ds(off, subwin), :],
                                                         preferred_element_type=F32)

    @pl.when(jnp.logical_not(fits))
    def _():
        def everyone(base):
            return jnp.concatenate([owner_matrix(g, base, win) for g in range(per)], axis=0)

        y = x_ref[...] + jnp.dot(everyone(s0), rows16[...], preferred_element_type=F32)
        n_extra = jnp.maximum(tsub_ref[(i + 1) * per] - (s0 + win) + win - 1, 0) // win

        def extra(k, acc):
            base = pl.multiple_of(s0 + (k + 1) * win, BF16_TILE_ROWS)
            cps = [pltpu.make_async_copy(pieces_hbm[c].at[pl.ds(base, win)], obuf.at[c], osem.at[c])
                   for c in range(SC_PIECES)]
            for cp in cps:
                cp.start()
            for cp in cps:
                cp.wait()
            rows = _unpack_rows([zero_unwritten(obuf[c], base) for c in range(SC_PIECES)])
            return acc + jnp.dot(everyone(base), rows, preferred_element_type=F32)

        o_ref[...] = lax.fori_loop(0, n_extra, extra, y)


def _combine(xmid, cs, ce, pieces, total):
    n = xmid.shape[0]
    tt = COMBINE_TOKENS
    win = COMBINE_WINDOW
    ntile = n // tt
    tsub = jnp.concatenate([cs[::COMBINE_SUB], jnp.full((1,), total, jnp.int32)])
    cs3 = cs.reshape(ntile, 1, tt)
    ce3 = ce.reshape(ntile, 1, tt)
    any_spec = pl.BlockSpec(memory_space=pl.ANY)
    tok = pl.BlockSpec((1, 1, tt), lambda i, ts: (i, 0, 0))
    grid_spec = pltpu.PrefetchScalarGridSpec(
        num_scalar_prefetch=1,
        grid=(ntile,),
        in_specs=[pl.BlockSpec((tt, D_MODEL), lambda i, ts: (i, 0)), tok, tok] + [any_spec] * SC_PIECES,
        out_specs=pl.BlockSpec((tt, D_MODEL), lambda i, ts: (i, 0)),
        scratch_shapes=[pltpu.VMEM((COMBINE_BUFFERS, SC_PIECES, win, SC_ROW), jnp.int32),
                        pltpu.VMEM((SC_PIECES, win, SC_ROW), jnp.int32), pltpu.VMEM((win, D_MODEL), BF16),
                        pltpu.SemaphoreType.DMA((COMBINE_BUFFERS, SC_PIECES)), pltpu.SemaphoreType.DMA((SC_PIECES,))],
    )
    return pl.pallas_call(
        functools.partial(_combine_kernel, ntile=ntile, total=total),
        grid_spec=grid_spec,
        out_shape=jax.ShapeDtypeStruct((n, D_MODEL), F32),
        compiler_params=_cparams(1),
        name="combine",
    )(tsub, xmid, cs3, ce3, *pieces)


def _t5_bucket(rel):
    half = REL_BUCKETS // 2
    max_exact = half // 2
    base = np.where(rel > 0, half, 0)
    n = np.abs(rel)
    large = max_exact + (np.log(np.maximum(n, 1) / max_exact) / math.log(REL_MAX_DIST / max_exact)
                         * (half - max_exact)).astype(np.int32)
    large = np.minimum(large, half - 1)
    return (base + np.where(n < max_exact, n, large)).astype(np.int32)


def _head_perm():
    nq = ATTN_HEADS // 2
    cols = []
    for j in range(nq):
        for half in range(2):
            h = j + nq * half
            cols.extend(range(h * ATTN_HEAD_DIM, (h + 1) * ATTN_HEAD_DIM))
    return np.asarray(cols, np.int32)


def _attn_bias_tables(rel_bias):
    q_pos = np.arange(BLOCK)[:, None]
    k_off = np.arange(3 * BLOCK)[None, :] - BLOCK
    rel = k_off - q_pos
    in_window = np.abs(rel) <= WINDOW
    onehot = jnp.asarray(_t5_bucket(rel)[:, :, None] == np.arange(REL_BUCKETS)[None, None, :], F32)
    bias = jnp.einsum("qkb,bh->hqk", onehot, rel_bias.astype(F32), precision=lax.Precision.HIGHEST)
    col = np.arange(3 * BLOCK)[None, :]
    tables = []
    for valid in (col >= BLOCK, np.ones_like(col, bool), col < 2 * BLOCK):
        t = jnp.where(jnp.asarray(in_window & valid)[None], bias, NEG)
        nq = ATTN_HEADS // 2
        rows = [jnp.concatenate([t[j], t[j + nq]], axis=1) for j in range(nq)]
        tables.append(jnp.concatenate(rows, axis=0))
    return jnp.stack(tables)


def _retention_tables(decay_logit, norm_g):
    cr = RET_CHUNK
    lg = jax.nn.log_sigmoid(decay_logit.astype(F32))
    lgf, lgb = lg[0][:, None, None], lg[1][:, None, None]
    pos = np.arange(cr, dtype=np.float32)
    dist = pos[:, None] - pos[None, :]
    scale = RET_DIM ** -0.5
    dmask = jnp.where(jnp.asarray(dist >= 0)[None],
                      jnp.exp(lgf * np.maximum(dist, 0.0)[None]),
                      jnp.exp(lgb * np.maximum(-dist, 0.0)[None])) * scale
    col = lambda v: jnp.broadcast_to(v[:, :, None], (RET_HEADS, cr, RET_DIM))
    rowf = col(jnp.exp(lg[0][:, None] * pos[None]))
    rowb = col(jnp.exp(lg[1][:, None] * (cr - 1.0 - pos)[None]))
    wkf = col(jnp.exp(lg[0][:, None] * (cr - pos)[None]) * scale)
    wkb = col(jnp.exp(lg[1][:, None] * (pos + 1.0)[None]) * scale)
    dec = jnp.concatenate([jnp.exp(lg[0] * cr), jnp.exp(lg[1] * cr)])
    return dec, dmask, rowf, rowb, wkf, wkb, norm_g.astype(F32)


def _layer(x2, b, s, p):
    qa, ka, va, qr, kr, vr, gr, ga, gt = _in_proj(x2, p["g1"], p["w_in"], p["qg"], p["kg"], p["bdq"], p["bdk"])
    attn = _attention(qa, ka, va, p["bias3"], p["sink"], b, s)
    retn = _retention(qr, kr, vr, gr, p["retn"], b, s)
    xmid, afft, *h2 = _merge(attn, retn, ga, gt, x2, p["wba"], p["wbr"], p["wo"], p["g2"], p["wr"])
    n = b * s
    cap = max(1, EC_CAPACITY_FACTOR * n // N_EXPERTS)
    total = N_EXPERTS * cap
    idx, slots, cs, ce = _select(afft, cap)
    slot_rows = _slot_rows(afft, slots, idx)
    xs = [_sc_gather(piece, idx) for piece in h2]
    outs, dst = _expert_ffn(xs, slot_rows, p["w1"], p["w3"], p["w2"], p["layer"])
    by_token = [_sc_scatter(o, dst, total + COMBINE_WINDOW) for o in outs]
    return _combine(xmid, cs, ce, by_token, total)


def kernel(x_prompt, x_sample, norm_mix_g, w_in, q_norm_g, k_norm_g, attn_sink, rel_bias, retn_decay_logit, retn_norm_g, w_branch_attn, w_branch_retn, w_out, norm_ffn_g, w_router, w_exp_gate, w_exp_up, w_exp_down):
    depth = w_in.shape[0]
    perm = _head_perm()
    bias3 = _attn_bias_tables(rel_bias)
    bdq = jnp.asarray(np.kron(np.eye(ATTN_HEADS), np.ones((ATTN_HEAD_DIM, ATTN_HEAD_DIM))), BF16)
    bdk = jnp.asarray(np.kron(np.eye(ATTN_KV_HEADS), np.ones((ATTN_HEAD_DIM, ATTN_HEAD_DIM))), BF16)
    layers = []
    for l in range(depth):
        w = w_in[l]
        w = jnp.concatenate([w[:, :ATTN_WIDTH][:, perm], w[:, ATTN_WIDTH:]], axis=1).astype(BF16)
        wr = jnp.pad(w_router[l], ((0, 0), (0, LANES - N_EXPERTS))).astype(BF16)
        layers.append(dict(
            g1=norm_mix_g[l].astype(F32)[None], w_in=w,
            qg=(jnp.tile(q_norm_g[l].astype(F32), ATTN_HEADS) * (ATTN_HEAD_DIM ** -0.5))[None],
            kg=jnp.tile(k_norm_g[l].astype(F32), ATTN_KV_HEADS)[None],
            bdq=bdq, bdk=bdk, bias3=bias3, sink=attn_sink[l].astype(F32),
            retn=_retention_tables(retn_decay_logit[l], retn_norm_g[l]),
            wba=w_branch_attn[l][perm, :].astype(BF16), wbr=w_branch_retn[l].astype(BF16),
            wo=w_out[l].astype(BF16), g2=norm_ffn_g[l].astype(F32)[None], wr=wr,
            w1=w_exp_gate, w3=w_exp_up, w2=w_exp_down, layer=l))

    def trunk(x):
        b, s, d = x.shape
        x2 = x.reshape(b * s, d)
        for p in layers:
            x2 = _layer(x2, b, s, p)
        return x2.reshape(b, s, d)

    return (trunk(x_prompt), trunk(x_sample))
```

```python
import functools
import math

import numpy as np
import jax
import jax.numpy as jnp
from jax import lax
from jax.experimental import pallas as pl
from jax.experimental.pallas import tpu as pltpu
from jax.experimental.pallas import tpu_sc as plsc

D_MODEL = 1024
ATTN_HEADS = 8
ATTN_KV_HEADS = 2
ATTN_HEAD_DIM = 64
WINDOW = 128
BLOCK = 128
REL_BUCKETS = 32
REL_MAX_DIST = 128
RET_HEADS = 4
RET_DIM = 128
N_EXPERTS = 16
EC_CAPACITY_FACTOR = 2
EXPERT_FF = 1024
EPS = 1e-6

ATTN_WIDTH = ATTN_HEADS * ATTN_HEAD_DIM
KV_WIDTH = ATTN_KV_HEADS * ATTN_HEAD_DIM
RET_WIDTH = RET_HEADS * RET_DIM
IN_SPLITS = (ATTN_WIDTH, KV_WIDTH, KV_WIDTH, RET_WIDTH, RET_WIDTH, RET_WIDTH, RET_WIDTH, D_MODEL, D_MODEL)
IN_OFFSETS = tuple(int(o) for o in np.cumsum((0,) + IN_SPLITS))

LANES = 128
BF16_TILE_ROWS = 16
VMEM_LIMIT_BYTES = 56 * 1024 * 1024

TOKEN_TILE = 1024
MERGE_ROWS = 256
IN_PROJ_TILE = 1024
ATTN_QUERIES = 2048
RET_CHUNK = 256
RET_STEP = 2048
FFN_ROWS = 1024
SELECT_BLOCKS = 128
SELECT_SLOTS = 2048
SC_WINDOW = 128
SC_ROW = 256
PACKED_WIDTH = D_MODEL // 2
SC_PIECES = PACKED_WIDTH // SC_ROW
COMBINE_TOKENS = 512
COMBINE_WINDOW = 1280
COMBINE_SUB = 128
COMBINE_SUBWIN = 384
COMBINE_BUFFERS = 3

F32 = jnp.float32
BF16 = jnp.bfloat16
NEG = -1e30


def _cparams(n_axes, vmem=VMEM_LIMIT_BYTES):
    return pltpu.CompilerParams(dimension_semantics=("arbitrary",) * n_axes, vmem_limit_bytes=vmem)


def _sigmoid(x):
    return 0.5 * jnp.tanh(0.5 * x) + 0.5


HIGH_HALF = -65536


def _pack_rows(x):
    bits = pltpu.bitcast(x.astype(BF16).astype(F32), jnp.int32)
    words = lax.shift_right_logical(bits[:, :PACKED_WIDTH], 16) | (bits[:, PACKED_WIDTH:] & HIGH_HALF)
    return [words[:, c * SC_ROW:(c + 1) * SC_ROW] for c in range(SC_PIECES)]


def _unpack_rows(pieces):
    low = [pltpu.bitcast(lax.shift_left(w, 16), F32) for w in pieces]
    high = [pltpu.bitcast(w & HIGH_HALF, F32) for w in pieces]
    return jnp.concatenate(low + high, axis=1).astype(BF16)


def _in_proj_kernel(x_ref, g_ref, w_ref, qg_ref, kg_ref, bdq_ref, bdk_ref,
                    qa_ref, ka_ref, va_ref, qr_ref, kr_ref, vr_ref, gr_ref, ga_ref, gt_ref):
    x = x_ref[...]
    ms = jnp.mean(x * x, axis=-1, keepdims=True)
    h = (x * lax.rsqrt(ms + EPS) * g_ref[...]).astype(BF16)

    def mm(k):
        return jnp.dot(h, w_ref[:, IN_OFFSETS[k]:IN_OFFSETS[k + 1]], preferred_element_type=F32)

    def head_norm(t, bd_ref, gain_ref):
        ss = jnp.dot((t * t).astype(BF16), bd_ref[...], preferred_element_type=F32)
        return t * lax.rsqrt(ss * (1.0 / ATTN_HEAD_DIM) + EPS) * gain_ref[...]

    q_raw = mm(0)
    k_raw = mm(1)
    for k, ref in ((2, va_ref), (3, qr_ref), (4, kr_ref), (5, vr_ref)):
        ref[...] = mm(k).astype(BF16)
    g = mm(6)
    gr_ref[...] = (g * _sigmoid(g)).astype(BF16)
    ga_ref[...] = _sigmoid(mm(7)).astype(BF16)
    gt_ref[...] = _sigmoid(mm(8)).astype(BF16)
    qa_ref[...] = head_norm(q_raw, bdq_ref, qg_ref).astype(BF16)
    ka_ref[...] = head_norm(k_raw, bdk_ref, kg_ref).astype(BF16)


def _in_proj(x2, g, w, qg, kg, bdq, bdk):
    n = x2.shape[0]
    tm = IN_PROJ_TILE
    full = lambda a: pl.BlockSpec(a.shape, lambda i: (0,) * a.ndim, pipeline_mode=pl.Buffered(1))
    widths = IN_SPLITS
    return pl.pallas_call(
        _in_proj_kernel,
        grid=(n // tm,),
        in_specs=[pl.BlockSpec((tm, D_MODEL), lambda i: (i, 0)), full(g), full(w), full(qg), full(kg),
                  full(bdq), full(bdk)],
        out_specs=[pl.BlockSpec((tm, wd), lambda i: (i, 0)) for wd in widths],
        out_shape=[jax.ShapeDtypeStruct((n, wd), BF16) for wd in widths],
        compiler_params=_cparams(1),
        name="in_proj",
    )(x2, g, w, qg, kg, bdq, bdk)


def _attn_kernel(sink_ref, q_ref, kp_ref, kc_ref, kn_ref, vp_ref, vc_ref, vn_ref, bias_ref, o_ref, *, nsteps):
    nq = ATTN_HEADS // 2
    ni = pl.program_id(1)
    k = jnp.concatenate([kp_ref[...], kc_ref[...], kn_ref[...]], axis=0)
    v = jnp.concatenate([vp_ref[...], vc_ref[...], vn_ref[...]], axis=0)
    low = lax.broadcasted_iota(jnp.int32, k.shape, 1) < ATTN_HEAD_DIM
    zero = jnp.zeros_like(k)
    k_lo, k_hi = jnp.where(low, k, zero), jnp.where(low, zero, k)
    v_lo, v_hi = jnp.where(low, v, zero), jnp.where(low, zero, v)
    nk = 3 * BLOCK
    low_o = lax.broadcasted_iota(jnp.int32, (BLOCK, LANES), 1) < ATTN_HEAD_DIM
    key_low = lax.broadcasted_iota(jnp.int32, (2 * nk, LANES), 0) < nk
    lane_low = lax.broadcasted_iota(jnp.int32, (2 * nk, LANES), 1) < ATTN_HEAD_DIM
    ones_bd = (key_low == lane_low).astype(BF16)
    nsub = ATTN_QUERIES // BLOCK
    for sb in range(nsub):
        rows = slice(sb * BLOCK, (sb + 1) * BLOCK)
        keys = slice(sb * BLOCK, sb * BLOCK + nk)
        q = q_ref[rows, :]
        qs = jnp.concatenate([q[:, j * LANES:(j + 1) * LANES] for j in range(nq)], axis=0)
        kbd = jnp.concatenate([k_lo[keys], k_hi[keys]], axis=0)
        vbd = jnp.concatenate([v_lo[keys], v_hi[keys]], axis=0)
        s = lax.dot_general(qs, kbd, (((1,), (1,)), ((), ())), preferred_element_type=F32)
        if sb == 0:
            table = jnp.where(ni == 0, 0, 1)
        elif sb == nsub - 1:
            table = jnp.where(ni == nsteps - 1, 2, 1)
        else:
            table = 1
        s = s + bias_ref[table]
        probs, sink_terms = [], []
        for j in range(nq):
            row_p, row_sink = [], []
            for half in range(2):
                sj = s[j * BLOCK:(j + 1) * BLOCK, half * nk:(half + 1) * nk]
                sk = sink_ref[j + nq * half]
                m = jnp.maximum(jnp.max(sj, axis=-1, keepdims=True), sk)
                row_p.append(jnp.exp(sj - m).astype(BF16))
                row_sink.append(jnp.exp(sk - m))
            probs.append(jnp.concatenate(row_p, axis=1))
            sink_terms.append(jnp.where(low_o, row_sink[0], row_sink[1]))
        pm = jnp.concatenate(probs, axis=0)
        od = jnp.dot(pm, jnp.concatenate([vbd, ones_bd], axis=1), preferred_element_type=F32)
        o = od[:, :LANES] / (od[:, LANES:] + jnp.concatenate(sink_terms, axis=0))
        for j in range(nq):
            o_ref[rows, j * LANES:(j + 1) * LANES] = o[j * BLOCK:(j + 1) * BLOCK].astype(BF16)


def _attention(qa, ka, va, bias3, sink, b, s):
    tq = ATTN_QUERIES
    per = tq // BLOCK
    nb = s // BLOCK
    nsteps = s // tq
    assert s % tq == 0 and nb >= 2
    n = b * s
    main = lambda wd: pl.BlockSpec((tq, wd), lambda bi, ni: (bi * nsteps + ni, 0))
    prev = pl.BlockSpec((BLOCK, KV_WIDTH), lambda bi, ni: (bi * nb + jnp.maximum(ni * per - 1, 0), 0))
    nxt = pl.BlockSpec((BLOCK, KV_WIDTH), lambda bi, ni: (bi * nb + jnp.minimum(ni * per + per, nb - 1), 0))
    return pl.pallas_call(
        functools.partial(_attn_kernel, nsteps=nsteps),
        grid=(b, nsteps),
        in_specs=[pl.BlockSpec(memory_space=pltpu.SMEM), main(ATTN_WIDTH),
                  prev, main(KV_WIDTH), nxt, prev, main(KV_WIDTH), nxt,
                  pl.BlockSpec(bias3.shape, lambda bi, ni: (0, 0, 0))],
        out_specs=main(ATTN_WIDTH),
        out_shape=jax.ShapeDtypeStruct((n, ATTN_WIDTH), BF16),
        compiler_params=_cparams(2),
        name="attn",
    )(sink, qa, ka, ka, ka, va, va, va, bias3)


def _retn_kernel(dec_ref, q_ref, k_ref, v_ref, g_ref, dmask_ref, rowf_ref, rowb_ref, wkf_ref, wkb_ref, ng_ref,
                 o_ref, tstore, uf, tb, *, nsteps):
    p = pl.program_id(1)
    n = pl.program_id(2)
    cr = RET_CHUNK
    per = RET_STEP // cr
    tn = (((0,), (0,)), ((), ()))
    nt = (((1,), (1,)), ((), ()))
    hs = lambda h: slice(h * RET_DIM, (h + 1) * RET_DIM)

    @pl.when(p == 0)
    def _():
        @pl.when(n == 0)
        def _():
            tb[...] = jnp.zeros_like(tb)
        first_chunk = (nsteps - 1 - n) * per
        for sub in reversed(range(per)):
            rows = slice(sub * cr, (sub + 1) * cr)
            for h in range(RET_HEADS):
                tstore[first_chunk + sub, h] = tb[h].astype(BF16)
                kw = (k_ref[rows, hs(h)].astype(F32) * wkb_ref[h]).astype(BF16)
                upd = lax.dot_general(kw, v_ref[rows, hs(h)], tn, preferred_element_type=F32)
                tb[h] = dec_ref[RET_HEADS + h] * tb[h] + upd

    @pl.when(p == 1)
    def _():
        @pl.when(n == 0)
        def _():
            uf[...] = jnp.zeros_like(uf)
        heads = range(RET_HEADS)
        for sub in range(per):
            rows = slice(sub * cr, (sub + 1) * cr)
            qs = [q_ref[rows, hs(h)] for h in heads]
            ks = [k_ref[rows, hs(h)] for h in heads]
            vs = [v_ref[rows, hs(h)] for h in heads]
            scores = [lax.dot_general(qs[h], ks[h], nt, preferred_element_type=F32) for h in heads]
            states = [jnp.concatenate([uf[h].astype(BF16), tstore[n * per + sub, h]], axis=1) for h in heads]
            cross = [jnp.dot(qs[h], states[h], preferred_element_type=F32) for h in heads]
            update = [lax.dot_general((ks[h].astype(F32) * wkf_ref[h]).astype(BF16), vs[h], tn,
                                      preferred_element_type=F32) for h in heads]
            intra = [jnp.dot((scores[h] * dmask_ref[h]).astype(BF16), vs[h], preferred_element_type=F32)
                     for h in heads]
            for h in heads:
                o = intra[h] + cross[h][:, :RET_DIM] * rowf_ref[h] + cross[h][:, RET_DIM:] * rowb_ref[h]
                mu = jnp.mean(o, axis=-1, keepdims=True)
                d = o - mu
                var = jnp.mean(d * d, axis=-1, keepdims=True)
                on = d * lax.rsqrt(var + EPS) * ng_ref[h:h + 1, :]
                gate = g_ref[rows, hs(h)].astype(F32)
                o_ref[rows, hs(h)] = (gate * on).astype(BF16)
                uf[h] = dec_ref[h] * uf[h] + update[h]


def _retention(qr, kr, vr, gr, tables, b, s):
    dec, dmask, rowf, rowb, wkf, wkb, ng = tables
    rs = RET_STEP
    nsteps = s // rs
    nc = s // RET_CHUNK
    assert s % rs == 0
    n = b * s
    full = lambda a: pl.BlockSpec(a.shape, lambda bi, pi, ni: (0,) * a.ndim)
    fwd_spec = pl.BlockSpec((rs, RET_WIDTH), lambda bi, pi, ni: (bi * nsteps + ni * pi, 0))
    kv_spec = pl.BlockSpec((rs, RET_WIDTH),
                           lambda bi, pi, ni: (bi * nsteps + ni * pi + (1 - pi) * (nsteps - 1 - ni), 0))
    return pl.pallas_call(
        functools.partial(_retn_kernel, nsteps=nsteps),
        grid=(b, 2, nsteps),
        in_specs=[pl.BlockSpec(memory_space=pltpu.SMEM), fwd_spec, kv_spec, kv_spec, fwd_spec,
                  full(dmask), full(rowf), full(rowb), full(wkf), full(wkb), full(ng)],
        out_specs=fwd_spec,
        out_shape=jax.ShapeDtypeStruct((n, RET_WIDTH), BF16),
        scratch_shapes=[pltpu.VMEM((nc, RET_HEADS, RET_DIM, RET_DIM), BF16),
                        pltpu.VMEM((RET_HEADS, RET_DIM, RET_DIM), F32),
                        pltpu.VMEM((RET_HEADS, RET_DIM, RET_DIM), F32)],
        compiler_params=_cparams(3),
        name="retention",
    )(dec, qr, kr, vr, gr, dmask, rowf, rowb, wkf, wkb, ng)


def _merge_kernel(attn_ref, retn_ref, ga_ref, gr_ref, x_ref, wba_ref, wbr_ref, wo_ref, g2_ref, wr_ref,
                  xmid_ref, afft_ref, *h2_refs):
    tiles = [slice(t * MERGE_ROWS, (t + 1) * MERGE_ROWS) for t in range(TOKEN_TILE // MERGE_ROWS)]
    branch = [(jnp.dot(attn_ref[rows, :], wba_ref[...], preferred_element_type=F32),
               jnp.dot(retn_ref[rows, :], wbr_ref[...], preferred_element_type=F32)) for rows in tiles]
    resid = []
    for rows, (a, r) in zip(tiles, branch):
        merged = ga_ref[rows, :].astype(F32) * a + gr_ref[rows, :].astype(F32) * r
        xn = x_ref[rows, :] + jnp.dot(merged.astype(BF16), wo_ref[...], preferred_element_type=F32)
        xmid_ref[rows, :] = xn
        resid.append(xn)
    scores = []
    for rows, xn in zip(tiles, resid):
        ms = jnp.mean(xn * xn, axis=-1, keepdims=True)
        h2 = xn * lax.rsqrt(ms + EPS) * g2_ref[...]
        for ref, words in zip(h2_refs, _pack_rows(h2)):
            ref[rows, :] = words
        scores.append(jnp.dot(h2.astype(BF16), wr_ref[...], preferred_element_type=F32))
    for rows, logits in zip(tiles, scores):
        real = lax.broadcasted_iota(jnp.int32, logits.shape, 1) < N_EXPERTS
        logits = jnp.where(real, logits, -jnp.inf)
        m = jnp.max(logits, axis=-1, keepdims=True)
        ex = jnp.exp(logits - m)
        aff = ex / jnp.sum(ex, axis=-1, keepdims=True)
        afft_ref[:, rows] = aff.T[:N_EXPERTS, :]


def _merge(attn, retn, ga, gr, x2, wba, wbr, wo, g2, wr):
    n = x2.shape[0]
    tm = TOKEN_TILE
    full = lambda a: pl.BlockSpec(a.shape, lambda i: (0,) * a.ndim)
    row = lambda wd: pl.BlockSpec((tm, wd), lambda i: (i, 0))
    return pl.pallas_call(
        _merge_kernel,
        grid=(n // tm,),
        in_specs=[row(ATTN_WIDTH), row(RET_WIDTH), row(D_MODEL), row(D_MODEL), row(D_MODEL),
                  full(wba), full(wbr), full(wo), full(g2), full(wr)],
        out_specs=[row(D_MODEL), pl.BlockSpec((N_EXPERTS, tm), lambda i: (0, i))] + [row(SC_ROW)] * SC_PIECES,
        out_shape=[jax.ShapeDtypeStruct((n, D_MODEL), F32), jax.ShapeDtypeStruct((N_EXPERTS, n), F32)]
        + [jax.ShapeDtypeStruct((n, SC_ROW), jnp.int32)] * SC_PIECES,
        compiler_params=_cparams(1),
        name="merge",
    )(attn, retn, ga, gr, x2, wba, wbr, wo, g2, wr)


def _select_kernel(aff_ref, u_ref, ls_ref, idx_ref, slot_ref, cs_ref, ce_ref,
                   thr, selbuf, cnt, csr, rank, digits, offi, *, cap, tb):
    s = pl.program_id(0)
    nch = cap // SELECT_SLOTS
    ps = jnp.where(s < N_EXPERTS, 0, 1)
    later = jnp.maximum(s - N_EXPERTS, 0)
    e = jnp.where(s < N_EXPERTS, s, later // nch)
    j = jnp.where(s < N_EXPERTS, 0, later % nch)
    nblk = SELECT_BLOCKS
    pc = SELECT_SLOTS

    def cumsum(vals):
        inb = jnp.dot(vals.astype(BF16), u_ref[...], preferred_element_type=F32)
        tot = jnp.broadcast_to(inb[:, tb - 1:tb], (nblk, LANES))
        off = jnp.dot(ls_ref[...], tot, preferred_element_type=F32, precision=lax.Precision.HIGHEST)
        return inb, off[:, 0:1], tot[:, 0:1]

    @pl.when(jnp.logical_and(ps == 0, jnp.logical_and(e == 0, j == 0)))
    def _():
        def bit_step(t, curs):
            bit = jnp.left_shift(jnp.int32(1), 30 - t)
            out = []
            for x in range(N_EXPERTS):
                cand = curs[x] | bit
                n_ge = jnp.sum((pltpu.bitcast(aff_ref[x], jnp.int32) >= cand).astype(jnp.int32), keepdims=True)
                out.append(jnp.where(n_ge >= cap, cand, curs[x]))
            return tuple(out)

        found = lax.fori_loop(0, 31, bit_step, tuple(jnp.zeros((1, 1), jnp.int32) for _ in range(N_EXPERTS)))
        for x in range(N_EXPERTS):
            thr[x] = jnp.broadcast_to(found[x], thr.shape[1:])

    @pl.when(jnp.logical_and(ps == 0, j == 0))
    def _():
        bits = pltpu.bitcast(aff_ref[e], jnp.int32)
        limit = thr[e][0:1, 0:1]
        gt = bits > limit
        eq = bits == limit
        need = (cap - jnp.sum(gt.astype(jnp.int32), keepdims=True)).astype(F32)
        eqf = eq.astype(F32)
        eq_in, eq_off, _ = cumsum(eqf)
        eq_rank = eq_in + eq_off - eqf
        sel = jnp.logical_or(gt, jnp.logical_and(eq, eq_rank < need)).astype(F32)
        selbuf[e] = sel.astype(BF16)

        @pl.when(e == 0)
        def _():
            cnt[...] = sel

        @pl.when(e > 0)
        def _():
            cnt[...] = cnt[...] + sel

    @pl.when(jnp.logical_and(ps == 1, j == 0))
    def _():
        @pl.when(e == 0)
        def _():
            c = cnt[...]
            c_in, c_off, _ = cumsum(c)
            start = c_in + c_off - c
            csr[...] = start
            cs_ref[...] = start.astype(jnp.int32)
            ce_ref[...] = (start + c).astype(jnp.int32)
            rank[...] = jnp.zeros_like(rank)

        sel = selbuf[e].astype(F32)
        s_in, s_off, s_tot = cumsum(sel)
        count_t = (s_in + s_off).T
        high = jnp.floor(count_t * (1.0 / 256.0))
        digits[:, :nblk] = high.astype(BF16)
        digits[:, nblk:] = (count_t - 256.0 * high).astype(BF16)
        slot_ref[0] = (csr[...] + rank[...]).astype(jnp.int32)
        rank[...] = rank[...] + sel
        offi[...] = jnp.broadcast_to(s_off + s_tot, (nblk, LANES))

    @pl.when(ps == 1)
    def _():
        slot = (j * pc + lax.broadcasted_iota(jnp.int32, (1, pc), 1)).astype(F32)
        blk = jnp.sum((offi[:, 0:1] <= slot).astype(jnp.int32), axis=0, keepdims=True)
        owner = jnp.where(lax.broadcasted_iota(jnp.int32, (nblk, pc), 0) == blk, 1.0, 0.0)
        weights = jnp.concatenate([256.0 * owner, owner], axis=0).astype(BF16)
        counts = jnp.dot(digits[...], weights, preferred_element_type=F32)
        inb = jnp.sum((counts <= slot + 0.5).astype(jnp.int32), axis=0, keepdims=True)
        idx_ref[0] = blk * tb + inb


def _select(afft, cap):
    n = afft.shape[1]
    nblk = SELECT_BLOCKS
    tb = n // nblk
    pc = SELECT_SLOTS
    assert n % nblk == 0 and tb % LANES == 0 and cap % pc == 0 and cap < 65536
    nch = cap // pc
    aff3 = afft.reshape(N_EXPERTS, nblk, tb)
    upper = jnp.asarray(np.triu(np.ones((tb, tb), np.float32)), BF16)
    lstrict = jnp.asarray(np.tril(np.ones((nblk, nblk), np.float32), -1))
    full = lambda a: pl.BlockSpec(a.shape, lambda s: (0,) * a.ndim)
    idx_spec = pl.BlockSpec((1, 1, pc), lambda s: (jnp.maximum(s - N_EXPERTS, 0), 0, 0))
    slot_spec = pl.BlockSpec((1, nblk, tb), lambda s: (jnp.maximum(s - N_EXPERTS, 0) // nch, 0, 0))
    tok_spec = pl.BlockSpec((nblk, tb), lambda s: (0, 0))
    idx, slots, cs, ce = pl.pallas_call(
        functools.partial(_select_kernel, cap=cap, tb=tb),
        grid=(N_EXPERTS + N_EXPERTS * nch,),
        in_specs=[full(aff3), full(upper), full(lstrict)],
        out_specs=[idx_spec, slot_spec, tok_spec, tok_spec],
        out_shape=[jax.ShapeDtypeStruct((N_EXPERTS * nch, 1, pc), jnp.int32),
                   jax.ShapeDtypeStruct((N_EXPERTS, nblk, tb), jnp.int32),
                   jax.ShapeDtypeStruct((nblk, tb), jnp.int32), jax.ShapeDtypeStruct((nblk, tb), jnp.int32)],
        scratch_shapes=[pltpu.VMEM((N_EXPERTS, 8, LANES), jnp.int32), pltpu.VMEM((N_EXPERTS, nblk, tb), BF16),
                        pltpu.VMEM((nblk, tb), F32), pltpu.VMEM((nblk, tb), F32), pltpu.VMEM((nblk, tb), F32),
                        pltpu.VMEM((tb, 2 * nblk), BF16), pltpu.VMEM((nblk, LANES), F32)],
        compiler_params=_cparams(1),
        name="select",
    )(aff3, upper, lstrict)
    return idx.reshape(-1), slots.reshape(N_EXPERTS, n), cs.reshape(-1), ce.reshape(-1)


def _slot_rows(afft, slots, idx):
    n = afft.shape[1]
    table = jnp.concatenate([lax.bitcast_convert_type(afft, jnp.int32), slots,
                             jnp.zeros((LANES - 2 * N_EXPERTS, n), jnp.int32)], axis=0).T
    return _sc_gather(table, idx)


def _sc_mesh():
    return plsc.VectorSubcoreMesh(core_axis_name="c", subcore_axis_name="s")


def _sc_scatter(rows, idx, m_out):
    m, d = rows.shape
    assert m % SC_WINDOW == 0

    @functools.partial(pl.kernel, out_type=jax.ShapeDtypeStruct((m_out, d), rows.dtype), mesh=_sc_mesh(),
                       name="sc_scatter")
    def scatter(x_hbm, i_hbm, o_hbm):
        def body(x_vmem, i_vmem):
            pltpu.sync_copy(x_vmem, o_hbm.at[i_vmem.at[0]])

        pltpu.emit_pipeline(
            body,
            grid=(m // SC_WINDOW,),
            in_specs=[pl.BlockSpec((SC_WINDOW, d), lambda i: (i, 0)),
                      pl.BlockSpec((1, SC_WINDOW), lambda i: (0, i))],
            out_specs=[],
            core_axis_name=("c", "s"),
            dimension_semantics=(pltpu.PARALLEL,),
        )(x_hbm, i_hbm)

    return scatter(rows, idx.reshape(1, m))


def _sc_gather(table, idx):
    m = idx.shape[0]
    d = table.shape[1]
    assert m % SC_WINDOW == 0

    @functools.partial(pl.kernel, out_type=jax.ShapeDtypeStruct((m, d), table.dtype), mesh=_sc_mesh(),
                       name="sc_gather")
    def gather(x_hbm, i_hbm, o_hbm):
        def body(i_vmem, o_vmem):
            pltpu.sync_copy(x_hbm.at[i_vmem.at[0]], o_vmem)

        pltpu.emit_pipeline(
            body,
            grid=(m // SC_WINDOW,),
            in_specs=[pl.BlockSpec((1, SC_WINDOW), lambda i: (0, i))],
            out_specs=[pl.BlockSpec((SC_WINDOW, d), lambda i: (i, 0))],
            core_axis_name=("c", "s"),
            dimension_semantics=(pltpu.PARALLEL,),
        )(i_hbm, o_hbm)

    return gather(table, idx.reshape(1, m))


def _row_to_col(row):
    n = row.shape[1]
    eye = lax.broadcasted_iota(jnp.int32, (n, n), 0) == lax.broadcasted_iota(jnp.int32, (n, n), 1)
    return jnp.sum(jnp.where(eye, row, jnp.zeros_like(row)), axis=1, keepdims=True)


def _ffn_kernel(slot_ref, x0_ref, x1_ref, w1_hbm, w3_hbm, w2_hbm, o0_ref, o1_ref, dst_ref, wstage, w1b, w3b, w2b, wsem,
                *, layer):
    e = pl.program_id(0)
    i = pl.program_id(1)

    def weight_copies(expert):
        return [pltpu.make_async_copy(w_hbm.at[layer, expert], wstage.at[k], wsem.at[k])
                for k, w_hbm in enumerate((w1_hbm, w3_hbm, w2_hbm))]

    @pl.when(i == 0)
    def _():
        @pl.when(e == 0)
        def _():
            for cp in weight_copies(0):
                cp.start()
        for cp in weight_copies(e):
            cp.wait()
        w1b[...] = wstage[0].astype(BF16)
        w3b[...] = wstage[1].astype(BF16)
        w2b[...] = wstage[2].astype(BF16)

    @pl.when(jnp.logical_and(i == 1, e + 1 < N_EXPERTS))
    def _():
        for cp in weight_copies(e + 1):
            cp.start()

    info = slot_ref[...]
    lane = lax.broadcasted_iota(jnp.int32, info.shape, 1)
    gate = jnp.sum(jnp.where(lane == e, pltpu.bitcast(info, F32), 0.0), axis=1, keepdims=True)
    info_t = info.T
    word = lax.broadcasted_iota(jnp.int32, info_t.shape, 0)
    dst_ref[0] = jnp.sum(jnp.where(word == N_EXPERTS + e, info_t, 0), axis=0, keepdims=True)

    xs = _unpack_rows([x0_ref[...], x1_ref[...]])
    hg = jnp.dot(xs, w1b[...], preferred_element_type=F32)
    hu = jnp.dot(xs, w3b[...], preferred_element_type=F32)
    hid = (hg * _sigmoid(hg) * hu).astype(BF16)
    out = jnp.dot(hid, w2b[...], preferred_element_type=F32) * gate
    for ref, words in zip((o0_ref, o1_ref), _pack_rows(out)):
        ref[...] = words


def _expert_ffn(xs, slot_rows, w1, w3, w2, layer):
    m = xs[0].shape[0]
    rows = FFN_ROWS
    nt = m // (N_EXPERTS * rows)
    assert SC_PIECES == 2 and nt >= 2
    piece = pl.BlockSpec((rows, SC_ROW), lambda e, i: (e * nt + i, 0))
    any_spec = pl.BlockSpec(memory_space=pl.ANY)
    *outs, dst = pl.pallas_call(
        functools.partial(_ffn_kernel, layer=layer),
        grid=(N_EXPERTS, nt),
        in_specs=[pl.BlockSpec((rows, LANES), lambda e, i: (e * nt + i, 0))] + [piece] * SC_PIECES + [any_spec] * 3,
        out_specs=[piece] * SC_PIECES + [pl.BlockSpec((1, 1, rows), lambda e, i: (e * nt + i, 0, 0))],
        out_shape=[jax.ShapeDtypeStruct((m, SC_ROW), jnp.int32)] * SC_PIECES
        + [jax.ShapeDtypeStruct((m // rows, 1, rows), jnp.int32)],
        scratch_shapes=[pltpu.VMEM((3, D_MODEL, EXPERT_FF), F32), pltpu.VMEM((D_MODEL, EXPERT_FF), BF16),
                        pltpu.VMEM((D_MODEL, EXPERT_FF), BF16), pltpu.VMEM((EXPERT_FF, D_MODEL), BF16),
                        pltpu.SemaphoreType.DMA((3,))],
        compiler_params=_cparams(2),
        name="expert_ffn",
    )(slot_rows, *xs, w1, w3, w2)
    return outs, dst.reshape(-1)


def _combine_kernel(tsub_ref, x_ref, cs_ref, ce_ref, r0_hbm, r1_hbm, o_ref, rbuf, obuf, rows16, sems, osem,
                    *, ntile, total):
    pieces_hbm = (r0_hbm, r1_hbm)
    win = COMBINE_WINDOW
    sub = COMBINE_SUB
    subwin = COMBINE_SUBWIN
    per = COMBINE_TOKENS // sub
    i = pl.program_id(0)
    slot = lax.rem(i, COMBINE_BUFFERS)

    def window_start(t):
        return pl.multiple_of((tsub_ref[t * per] // BF16_TILE_ROWS) * BF16_TILE_ROWS, BF16_TILE_ROWS)

    def copies(t, b):
        s = window_start(t)
        return [pltpu.make_async_copy(pieces_hbm[c].at[pl.ds(s, win)], rbuf.at[b, c], sems.at[b, c])
                for c in range(SC_PIECES)]

    ahead = COMBINE_BUFFERS - 1

    @pl.when(i == 0)
    def _():
        for t in range(min(ahead, ntile)):
            for cp in copies(t, t):
                cp.start()

    @pl.when(i + ahead < ntile)
    def _():
        for cp in copies(i + ahead, lax.rem(i + ahead, COMBINE_BUFFERS)):
            cp.start()

    for cp in copies(i, slot):
        cp.wait()

    first = [_row_to_col(cs_ref[0][:, g * sub:(g + 1) * sub]) for g in range(per)]
    last = [_row_to_col(ce_ref[0][:, g * sub:(g + 1) * sub]) for g in range(per)]

    def zero_unwritten(words, base):
        written = (base + lax.broadcasted_iota(jnp.int32, (win, 1), 0)) < total
        return jnp.where(written, words, 0)

    def owner_matrix(g, base, width):
        r = base + lax.broadcasted_iota(jnp.int32, (1, width), 1)
        return jnp.logical_and(first[g] <= r, r < last[g]).astype(BF16)

    s0 = window_start(i)
    tail = s0 + win > total

    @pl.when(tail)
    def _():
        rows16[...] = _unpack_rows([zero_unwritten(rbuf[slot, c], s0) for c in range(SC_PIECES)])

    @pl.when(jnp.logical_not(tail))
    def _():
        rows16[...] = _unpack_rows([rbuf[slot, c] for c in range(SC_PIECES)])

    offsets = []
    fits = None
    for g in range(per):
        off = (tsub_ref[i * per + g] // BF16_TILE_ROWS) * BF16_TILE_ROWS - s0
        ok = jnp.logical_and(tsub_ref[i * per + g + 1] - s0 <= off + subwin, off + subwin <= win)
        fits = ok if fits is None else jnp.logical_and(fits, ok)
        offsets.append(off)

    @pl.when(fits)
    def _():
        for g in range(per):
            tokens = slice(g * sub, (g + 1) * sub)
            off = pl.multiple_of(offsets[g], BF16_TILE_ROWS)
            q = owner_matrix(g, s0 + off, subwin)
            o_ref[tokens, :] = x_ref[tokens, :] + jnp.dot(q, rows16[pl.ds(off, subwin), :],
                                                         preferred_element_type=F32)

    @pl.when(jnp.logical_not(fits))
    def _():
        def everyone(base):
            return jnp.concatenate([owner_matrix(g, base, win) for g in range(per)], axis=0)

        y = x_ref[...] + jnp.dot(everyone(s0), rows16[...], preferred_element_type=F32)
        n_extra = jnp.maximum(tsub_ref[(i + 1) * per] - (s0 + win) + win - 1, 0) // win

        def extra(k, acc):
            base = pl.multiple_of(s0 + (k + 1) * win, BF16_TILE_ROWS)
            cps = [pltpu.make_async_copy(pieces_hbm[c].at[pl.ds(base, win)], obuf.at[c], osem.at[c])
                   for c in range(SC_PIECES)]
            for cp in cps:
                cp.start()
            for cp in cps:
                cp.wait()
            rows = _unpack_rows([zero_unwritten(obuf[c], base) for c in range(SC_PIECES)])
            return acc + jnp.dot(everyone(base), rows, preferred_element_type=F32)

        o_ref[...] = lax.fori_loop(0, n_extra, extra, y)


def _combine(xmid, cs, ce, pieces, total):
    n = xmid.shape[0]
    tt = COMBINE_TOKENS
    win = COMBINE_WINDOW
    ntile = n // tt
    tsub = jnp.concatenate([cs[::COMBINE_SUB], jnp.full((1,), total, jnp.int32)])
    cs3 = cs.reshape(ntile, 1, tt)
    ce3 = ce.reshape(ntile, 1, tt)
    any_spec = pl.BlockSpec(memory_space=pl.ANY)
    tok = pl.BlockSpec((1, 1, tt), lambda i, ts: (i, 0, 0))
    grid_spec = pltpu.PrefetchScalarGridSpec(
        num_scalar_prefetch=1,
        grid=(ntile,),
        in_specs=[pl.BlockSpec((tt, D_MODEL), lambda i, ts: (i, 0)), tok, tok] + [any_spec] * SC_PIECES,
        out_specs=pl.BlockSpec((tt, D_MODEL), lambda i, ts: (i, 0)),
        scratch_shapes=[pltpu.VMEM((COMBINE_BUFFERS, SC_PIECES, win, SC_ROW), jnp.int32),
                        pltpu.VMEM((SC_PIECES, win, SC_ROW), jnp.int32), pltpu.VMEM((win, D_MODEL), BF16),
                        pltpu.SemaphoreType.DMA((COMBINE_BUFFERS, SC_PIECES)), pltpu.SemaphoreType.DMA((SC_PIECES,))],
    )
    return pl.pallas_call(
        functools.partial(_combine_kernel, ntile=ntile, total=total),
        grid_spec=grid_spec,
        out_shape=jax.ShapeDtypeStruct((n, D_MODEL), F32),
        compiler_params=_cparams(1),
        name="combine",
    )(tsub, xmid, cs3, ce3, *pieces)


def _t5_bucket(rel):
    half = REL_BUCKETS // 2
    max_exact = half // 2
    base = np.where(rel > 0, half, 0)
    n = np.abs(rel)
    large = max_exact + (np.log(np.maximum(n, 1) / max_exact) / math.log(REL_MAX_DIST / max_exact)
                         * (half - max_exact)).astype(np.int32)
    large = np.minimum(large, half - 1)
    return (base + np.where(n < max_exact, n, large)).astype(np.int32)


def _head_perm():
    nq = ATTN_HEADS // 2
    cols = []
    for j in range(nq):
        for half in range(2):
            h = j + nq * half
            cols.extend(range(h * ATTN_HEAD_DIM, (h + 1) * ATTN_HEAD_DIM))
    return np.asarray(cols, np.int32)


def _attn_bias_tables(rel_bias):
    q_pos = np.arange(BLOCK)[:, None]
    k_off = np.arange(3 * BLOCK)[None, :] - BLOCK
    rel = k_off - q_pos
    in_window = np.abs(rel) <= WINDOW
    onehot = jnp.asarray(_t5_bucket(rel)[:, :, None] == np.arange(REL_BUCKETS)[None, None, :], F32)
    bias = jnp.einsum("qkb,bh->hqk", onehot, rel_bias.astype(F32), precision=lax.Precision.HIGHEST)
    col = np.arange(3 * BLOCK)[None, :]
    tables = []
    for valid in (col >= BLOCK, np.ones_like(col, bool), col < 2 * BLOCK):
        t = jnp.where(jnp.asarray(in_window & valid)[None], bias, NEG)
        nq = ATTN_HEADS // 2
        rows = [jnp.concatenate([t[j], t[j + nq]], axis=1) for j in range(nq)]
        tables.append(jnp.concatenate(rows, axis=0))
    return jnp.stack(tables)


def _retention_tables(decay_logit, norm_g):
    cr = RET_CHUNK
    lg = jax.nn.log_sigmoid(decay_logit.astype(F32))
    lgf, lgb = lg[0][:, None, None], lg[1][:, None, None]
    pos = np.arange(cr, dtype=np.float32)
    dist = pos[:, None] - pos[None, :]
    scale = RET_DIM ** -0.5
    dmask = jnp.where(jnp.asarray(dist >= 0)[None],
                      jnp.exp(lgf * np.maximum(dist, 0.0)[None]),
                      jnp.exp(lgb * np.maximum(-dist, 0.0)[None])) * scale
    col = lambda v: jnp.broadcast_to(v[:, :, None], (RET_HEADS, cr, RET_DIM))
    rowf = col(jnp.exp(lg[0][:, None] * pos[None]))
    rowb = col(jnp.exp(lg[1][:, None] * (cr - 1.0 - pos)[None]))
    wkf = col(jnp.exp(lg[0][:, None] * (cr - pos)[None]) * scale)
    wkb = col(jnp.exp(lg[1][:, None] * (pos + 1.0)[None]) * scale)
    dec = jnp.concatenate([jnp.exp(lg[0] * cr), jnp.exp(lg[1] * cr)])
    return dec, dmask, rowf, rowb, wkf, wkb, norm_g.astype(F32)


def _layer(x2, b, s, p):
    qa, ka, va, qr, kr, vr, gr, ga, gt = _in_proj(x2, p["g1"], p["w_in"], p["qg"], p["kg"], p["bdq"], p["bdk"])
    attn = _attention(qa, ka, va, p["bias3"], p["sink"], b, s)
    retn = _retention(qr, kr, vr, gr, p["retn"], b, s)
    xmid, afft, *h2 = _merge(attn, retn, ga, gt, x2, p["wba"], p["wbr"], p["wo"], p["g2"], p["wr"])
    n = b * s
    cap = max(1, EC_CAPACITY_FACTOR * n // N_EXPERTS)
    total = N_EXPERTS * cap
    idx, slots, cs, ce = _select(afft, cap)
    slot_rows = _slot_rows(afft, slots, idx)
    xs = [_sc_gather(piece, idx) for piece in h2]
    outs, dst = _expert_ffn(xs, slot_rows, p["w1"], p["w3"], p["w2"], p["layer"])
    by_token = [_sc_scatter(o, dst, total + COMBINE_WINDOW) for o in outs]
    return _combine(xmid, cs, ce, by_token, total)


def kernel(x_prompt, x_sample, norm_mix_g, w_in, q_norm_g, k_norm_g, attn_sink, rel_bias, retn_decay_logit, retn_norm_g, w_branch_attn, w_branch_retn, w_out, norm_ffn_g, w_router, w_exp_gate, w_exp_up, w_exp_down):
    depth = w_in.shape[0]
    perm = _head_perm()
    bias3 = _attn_bias_tables(rel_bias)
    bdq = jnp.asarray(np.kron(np.eye(ATTN_HEADS), np.ones((ATTN_HEAD_DIM, ATTN_HEAD_DIM))), BF16)
    bdk = jnp.asarray(np.kron(np.eye(ATTN_KV_HEADS), np.ones((ATTN_HEAD_DIM, ATTN_HEAD_DIM))), BF16)
    layers = []
    for l in range(depth):
        w = w_in[l]
        w = jnp.concatenate([w[:, :ATTN_WIDTH][:, perm], w[:, ATTN_WIDTH:]], axis=1).astype(BF16)
        wr = jnp.pad(w_router[l], ((0, 0), (0, LANES - N_EXPERTS))).astype(BF16)
        layers.append(dict(
            g1=norm_mix_g[l].astype(F32)[None], w_in=w,
            qg=(jnp.tile(q_norm_g[l].astype(F32), ATTN_HEADS) * (ATTN_HEAD_DIM ** -0.5))[None],
            kg=jnp.tile(k_norm_g[l].astype(F32), ATTN_KV_HEADS)[None],
            bdq=bdq, bdk=bdk, bias3=bias3, sink=attn_sink[l].astype(F32),
            retn=_retention_tables(retn_decay_logit[l], retn_norm_g[l]),
            wba=w_branch_attn[l][perm, :].astype(BF16), wbr=w_branch_retn[l].astype(BF16),
            wo=w_out[l].astype(BF16), g2=norm_ffn_g[l].astype(F32)[None], wr=wr,
            w1=w_exp_gate, w3=w_exp_up, w2=w_exp_down, layer=l))

    def trunk(x):
        b, s, d = x.shape
        x2 = x.reshape(b * s, d)
        for p in layers:
            x2 = _layer(x2, b, s, p)
        return x2.reshape(b, s, d)

    return (trunk(x_prompt), trunk(x_sample))
```

```python
import functools
import math

import numpy as np
import jax
import jax.numpy as jnp
from jax import lax
from jax.experimental import pallas as pl
from jax.experimental.pallas import tpu as pltpu
from jax.experimental.pallas import tpu_sc as plsc

D_MODEL = 1024
ATTN_HEADS = 8
ATTN_KV_HEADS = 2
ATTN_HEAD_DIM = 64
WINDOW = 128
BLOCK = 128
REL_BUCKETS = 32
REL_MAX_DIST = 128
RET_HEADS = 4
RET_DIM = 128
N_EXPERTS = 16
EC_CAPACITY_FACTOR = 2
EXPERT_FF = 1024
EPS = 1e-6

ATTN_WIDTH = ATTN_HEADS * ATTN_HEAD_DIM
KV_WIDTH = ATTN_KV_HEADS * ATTN_HEAD_DIM
RET_WIDTH = RET_HEADS * RET_DIM
IN_SPLITS = (ATTN_WIDTH, KV_WIDTH, KV_WIDTH, RET_WIDTH, RET_WIDTH, RET_WIDTH, RET_WIDTH, D_MODEL, D_MODEL)
IN_OFFSETS = tuple(int(o) for o in np.cumsum((0,) + IN_SPLITS))

LANES = 128
BF16_TILE_ROWS = 16
VMEM_LIMIT_BYTES = 56 * 1024 * 1024

TOKEN_TILE = 1024
MERGE_ROWS = 256
IN_PROJ_TILE = 1024
ATTN_QUERIES = 2048
RET_CHUNK = 256
RET_STEP = 2048
FFN_ROWS = 1024
SELECT_BLOCKS = 128
SELECT_SLOTS = 2048
SC_WINDOW = 128
SC_ROW = 256
PACKED_WIDTH = D_MODEL // 2
SC_PIECES = PACKED_WIDTH // SC_ROW
COMBINE_TOKENS = 512
COMBINE_WINDOW = 1280
COMBINE_SUB = 128
COMBINE_SUBWIN = 384
COMBINE_BUFFERS = 3

F32 = jnp.float32
BF16 = jnp.bfloat16
NEG = -1e30


def _cparams(n_axes, vmem=VMEM_LIMIT_BYTES):
    return pltpu.CompilerParams(dimension_semantics=("arbitrary",) * n_axes, vmem_limit_bytes=vmem)


def _sigmoid(x):
    return 0.5 * jnp.tanh(0.5 * x) + 0.5


HIGH_HALF = -65536


def _pack_rows(x):
    bits = pltpu.bitcast(x.astype(BF16).astype(F32), jnp.int32)
    words = lax.shift_right_logical(bits[:, :PACKED_WIDTH], 16) | (bits[:, PACKED_WIDTH:] & HIGH_HALF)
    return [words[:, c * SC_ROW:(c + 1) * SC_ROW] for c in range(SC_PIECES)]


def _unpack_rows(pieces):
    low = [pltpu.bitcast(lax.shift_left(w, 16), F32) for w in pieces]
    high = [pltpu.bitcast(w & HIGH_HALF, F32) for w in pieces]
    return jnp.concatenate(low + high, axis=1).astype(BF16)


def _in_proj_kernel(x_ref, g_ref, w_ref, qg_ref, kg_ref, bdq_ref, bdk_ref,
                    qa_ref, ka_ref, va_ref, qr_ref, kr_ref, vr_ref, gr_ref, ga_ref, gt_ref):
    x = x_ref[...]
    ms = jnp.mean(x * x, axis=-1, keepdims=True)
    h = (x * lax.rsqrt(ms + EPS) * g_ref[...]).astype(BF16)

    def mm(k):
        return jnp.dot(h, w_ref[:, IN_OFFSETS[k]:IN_OFFSETS[k + 1]], preferred_element_type=F32)

    def head_norm(t, bd_ref, gain_ref):
        ss = jnp.dot((t * t).astype(BF16), bd_ref[...], preferred_element_type=F32)
        return t * lax.rsqrt(ss * (1.0 / ATTN_HEAD_DIM) + EPS) * gain_ref[...]

    q_raw = mm(0)
    k_raw = mm(1)
    for k, ref in ((2, va_ref), (3, qr_ref), (4, kr_ref), (5, vr_ref)):
        ref[...] = mm(k).astype(BF16)
    g = mm(6)
    gr_ref[...] = (g * _sigmoid(g)).astype(BF16)
    ga_ref[...] = _sigmoid(mm(7)).astype(BF16)
    gt_ref[...] = _sigmoid(mm(8)).astype(BF16)
    qa_ref[...] = head_norm(q_raw, bdq_ref, qg_ref).astype(BF16)
    ka_ref[...] = head_norm(k_raw, bdk_ref, kg_ref).astype(BF16)


def _in_proj(x2, g, w, qg, kg, bdq, bdk):
    n = x2.shape[0]
    tm = IN_PROJ_TILE
    full = lambda a: pl.BlockSpec(a.shape, lambda i: (0,) * a.ndim, pipeline_mode=pl.Buffered(1))
    widths = IN_SPLITS
    return pl.pallas_call(
        _in_proj_kernel,
        grid=(n // tm,),
        in_specs=[pl.BlockSpec((tm, D_MODEL), lambda i: (i, 0)), full(g), full(w), full(qg), full(kg),
                  full(bdq), full(bdk)],
        out_specs=[pl.BlockSpec((tm, wd), lambda i: (i, 0)) for wd in widths],
        out_shape=[jax.ShapeDtypeStruct((n, wd), BF16) for wd in widths],
        compiler_params=_cparams(1),
        name="in_proj",
    )(x2, g, w, qg, kg, bdq, bdk)


def _attn_kernel(sink_ref, q_ref, kp_ref, kc_ref, kn_ref, vp_ref, vc_ref, vn_ref, bias_ref, o_ref, *, nsteps):
    nq = ATTN_HEADS // 2
    ni = pl.program_id(1)
    k = jnp.concatenate([kp_ref[...], kc_ref[...], kn_ref[...]], axis=0)
    v = jnp.concatenate([vp_ref[...], vc_ref[...], vn_ref[...]], axis=0)
    low = lax.broadcasted_iota(jnp.int32, k.shape, 1) < ATTN_HEAD_DIM
    zero = jnp.zeros_like(k)
    k_lo, k_hi = jnp.where(low, k, zero), jnp.where(low, zero, k)
    v_lo, v_hi = jnp.where(low, v, zero), jnp.where(low, zero, v)
    nk = 3 * BLOCK
    low_o = lax.broadcasted_iota(jnp.int32, (BLOCK, LANES), 1) < ATTN_HEAD_DIM
    key_low = lax.broadcasted_iota(jnp.int32, (2 * nk, LANES), 0) < nk
    lane_low = lax.broadcasted_iota(jnp.int32, (2 * nk, LANES), 1) < ATTN_HEAD_DIM
    ones_bd = (key_low == lane_low).astype(BF16)
    nsub = ATTN_QUERIES // BLOCK
    for sb in range(nsub):
        rows = slice(sb * BLOCK, (sb + 1) * BLOCK)
        keys = slice(sb * BLOCK, sb * BLOCK + nk)
        q = q_ref[rows, :]
        qs = jnp.concatenate([q[:, j * LANES:(j + 1) * LANES] for j in range(nq)], axis=0)
        kbd = jnp.concatenate([k_lo[keys], k_hi[keys]], axis=0)
        vbd = jnp.concatenate([v_lo[keys], v_hi[keys]], axis=0)
        s = lax.dot_general(qs, kbd, (((1,), (1,)), ((), ())), preferred_element_type=F32)
        if sb == 0:
            table = jnp.where(ni == 0, 0, 1)
        elif sb == nsub - 1:
            table = jnp.where(ni == nsteps - 1, 2, 1)
        else:
            table = 1
        s = s + bias_ref[table]
        probs, sink_terms = [], []
        for j in range(nq):
            row_p, row_sink = [], []
            for half in range(2):
                sj = s[j * BLOCK:(j + 1) * BLOCK, half * nk:(half + 1) * nk]
                sk = sink_ref[j + nq * half]
                m = jnp.maximum(jnp.max(sj, axis=-1, keepdims=True), sk)
                row_p.append(jnp.exp(sj - m).astype(BF16))
                row_sink.append(jnp.exp(sk - m))
            probs.append(jnp.concatenate(row_p, axis=1))
            sink_terms.append(jnp.where(low_o, row_sink[0], row_sink[1]))
        pm = jnp.concatenate(probs, axis=0)
        od = jnp.dot(pm, jnp.concatenate([vbd, ones_bd], axis=1), preferred_element_type=F32)
        o = od[:, :LANES] / (od[:, LANES:] + jnp.concatenate(sink_terms, axis=0))
        for j in range(nq):
            o_ref[rows, j * LANES:(j + 1) * LANES] = o[j * BLOCK:(j + 1) * BLOCK].astype(BF16)


def _attention(qa, ka, va, bias3, sink, b, s):
    tq = ATTN_QUERIES
    per = tq // BLOCK
    nb = s // BLOCK
    nsteps = s // tq
    assert s % tq == 0 and nb >= 2
    n = b * s
    main = lambda wd: pl.BlockSpec((tq, wd), lambda bi, ni: (bi * nsteps + ni, 0))
    prev = pl.BlockSpec((BLOCK, KV_WIDTH), lambda bi, ni: (bi * nb + jnp.maximum(ni * per - 1, 0), 0))
    nxt = pl.BlockSpec((BLOCK, KV_WIDTH), lambda bi, ni: (bi * nb + jnp.minimum(ni * per + per, nb - 1), 0))
    return pl.pallas_call(
        functools.partial(_attn_kernel, nsteps=nsteps),
        grid=(b, nsteps),
        in_specs=[pl.BlockSpec(memory_space=pltpu.SMEM), main(ATTN_WIDTH),
                  prev, main(KV_WIDTH), nxt, prev, main(KV_WIDTH), nxt,
                  pl.BlockSpec(bias3.shape, lambda bi, ni: (0, 0, 0))],
        out_specs=main(ATTN_WIDTH),
        out_shape=jax.ShapeDtypeStruct((n, ATTN_WIDTH), BF16),
        compiler_params=_cparams(2),
        name="attn",
    )(sink, qa, ka, ka, ka, va, va, va, bias3)


def _retn_kernel(dec_ref, q_ref, k_ref, v_ref, g_ref, dmask_ref, rowf_ref, rowb_ref, wkf_ref, wkb_ref, ng_ref,
                 o_ref, tstore, uf, tb, *, nsteps):
    p = pl.program_id(1)
    n = pl.program_id(2)
    cr = RET_CHUNK
    per = RET_STEP // cr
    tn = (((0,), (0,)), ((), ()))
    nt = (((1,), (1,)), ((), ()))
    hs = lambda h: slice(h * RET_DIM, (h + 1) * RET_DIM)

    @pl.when(p == 0)
    def _():
        @pl.when(n == 0)
        def _():
            tb[...] = jnp.zeros_like(tb)
        first_chunk = (nsteps - 1 - n) * per
        for sub in reversed(range(per)):
            rows = slice(sub * cr, (sub + 1) * cr)
            for h in range(RET_HEADS):
                tstore[first_chunk + sub, h] = tb[h].astype(BF16)
                kw = (k_ref[rows, hs(h)].astype(F32) * wkb_ref[h]).astype(BF16)
                upd = lax.dot_general(kw, v_ref[rows, hs(h)], tn, preferred_element_type=F32)
                tb[h] = dec_ref[RET_HEADS + h] * tb[h] + upd

    @pl.when(p == 1)
    def _():
        @pl.when(n == 0)
        def _():
            uf[...] = jnp.zeros_like(uf)
        heads = range(RET_HEADS)
        for sub in range(per):
            rows = slice(sub * cr, (sub + 1) * cr)
            qs = [q_ref[rows, hs(h)] for h in heads]
            ks = [k_ref[rows, hs(h)] for h in heads]
            vs = [v_ref[rows, hs(h)] for h in heads]
            scores = [lax.dot_general(qs[h], ks[h], nt, preferred_element_type=F32) for h in heads]
            states = [jnp.concatenate([uf[h].astype(BF16), tstore[n * per + sub, h]], axis=1) for h in heads]
            cross = [jnp.dot(qs[h], states[h], preferred_element_type=F32) for h in heads]
            update = [lax.dot_general((ks[h].astype(F32) * wkf_ref[h]).astype(BF16), vs[h], tn,
                                      preferred_element_type=F32) for h in heads]
            intra = [jnp.dot((scores[h] * dmask_ref[h]).astype(BF16), vs[h], preferred_element_type=F32)
                     for h in heads]
            for h in heads:
                o = intra[h] + cross[h][:, :RET_DIM] * rowf_ref[h] + cross[h][:, RET_DIM:] * rowb_ref[h]
                mu = jnp.mean(o, axis=-1, keepdims=True)
                d = o - mu
                var = jnp.mean(d * d, axis=-1, keepdims=True)
                on = d * lax.rsqrt(var + EPS) * ng_ref[h:h + 1, :]
                gate = g_ref[rows, hs(h)].astype(F32)
                o_ref[rows, hs(h)] = (gate * on).astype(BF16)
                uf[h] = dec_ref[h] * uf[h] + update[h]


def _retention(qr, kr, vr, gr, tables, b, s):
    dec, dmask, rowf, rowb, wkf, wkb, ng = tables
    rs = RET_STEP
    nsteps = s // rs
    nc = s // RET_CHUNK
    assert s % rs == 0
    n = b * s
    full = lambda a: pl.BlockSpec(a.shape, lambda bi, pi, ni: (0,) * a.ndim)
    fwd_spec = pl.BlockSpec((rs, RET_WIDTH), lambda bi, pi, ni: (bi * nsteps + ni * pi, 0))
    kv_spec = pl.BlockSpec((rs, RET_WIDTH),
                           lambda bi, pi, ni: (bi * nsteps + ni * pi + (1 - pi) * (nsteps - 1 - ni), 0))
    return pl.pallas_call(
        functools.partial(_retn_kernel, nsteps=nsteps),
        grid=(b, 2, nsteps),
        in_specs=[pl.BlockSpec(memory_space=pltpu.SMEM), fwd_spec, kv_spec, kv_spec, fwd_spec,
                  full(dmask), full(rowf), full(rowb), full(wkf), full(wkb), full(ng)],
        out_specs=fwd_spec,
        out_shape=jax.ShapeDtypeStruct((n, RET_WIDTH), BF16),
        scratch_shapes=[pltpu.VMEM((nc, RET_HEADS, RET_DIM, RET_DIM), BF16),
                        pltpu.VMEM((RET_HEADS, RET_DIM, RET_DIM), F32),
                        pltpu.VMEM((RET_HEADS, RET_DIM, RET_DIM), F32)],
        compiler_params=_cparams(3),
        name="retention",
    )(dec, qr, kr, vr, gr, dmask, rowf, rowb, wkf, wkb, ng)


def _merge_kernel(attn_ref, retn_ref, ga_ref, gr_ref, x_ref, wba_ref, wbr_ref, wo_ref, g2_ref, wr_ref,
                  xmid_ref, afft_ref, *h2_refs):
    tiles = [slice(t * MERGE_ROWS, (t + 1) * MERGE_ROWS) for t in range(TOKEN_TILE // MERGE_ROWS)]
    branch = [(jnp.dot(attn_ref[rows, :], wba_ref[...], preferred_element_type=F32),
               jnp.dot(retn_ref[rows, :], wbr_ref[...], preferred_element_type=F32)) for rows in tiles]
    resid = []
    for rows, (a, r) in zip(tiles, branch):
        merged = ga_ref[rows, :].astype(F32) * a + gr_ref[rows, :].astype(F32) * r
        xn = x_ref[rows, :] + jnp.dot(merged.astype(BF16), wo_ref[...], preferred_element_type=F32)
        xmid_ref[rows, :] = xn
        resid.append(xn)
    scores = []
    for rows, xn in zip(tiles, resid):
        ms = jnp.mean(xn * xn, axis=-1, keepdims=True)
        h2 = xn * lax.rsqrt(ms + EPS) * g2_ref[...]
        for ref, words in zip(h2_refs, _pack_rows(h2)):
            ref[rows, :] = words
        scores.append(jnp.dot(h2.astype(BF16), wr_ref[...], preferred_element_type=F32))
    for rows, logits in zip(tiles, scores):
        real = lax.broadcasted_iota(jnp.int32, logits.shape, 1) < N_EXPERTS
        logits = jnp.where(real, logits, -jnp.inf)
        m = jnp.max(logits, axis=-1, keepdims=True)
        ex = jnp.exp(logits - m)
        aff = ex / jnp.sum(ex, axis=-1, keepdims=True)
        afft_ref[:, rows] = aff.T[:N_EXPERTS, :]


def _merge(attn, retn, ga, gr, x2, wba, wbr, wo, g2, wr):
    n = x2.shape[0]
    tm = TOKEN_TILE
    full = lambda a: pl.BlockSpec(a.shape, lambda i: (0,) * a.ndim)
    row = lambda wd: pl.BlockSpec((tm, wd), lambda i: (i, 0))
    return pl.pallas_call(
        _merge_kernel,
        grid=(n // tm,),
        in_specs=[row(ATTN_WIDTH), row(RET_WIDTH), row(D_MODEL), row(D_MODEL), row(D_MODEL),
                  full(wba), full(wbr), full(wo), full(g2), full(wr)],
        out_specs=[row(D_MODEL), pl.BlockSpec((N_EXPERTS, tm), lambda i: (0, i))] + [row(SC_ROW)] * SC_PIECES,
        out_shape=[jax.ShapeDtypeStruct((n, D_MODEL), F32), jax.ShapeDtypeStruct((N_EXPERTS, n), F32)]
        + [jax.ShapeDtypeStruct((n, SC_ROW), jnp.int32)] * SC_PIECES,
        compiler_params=_cparams(1),
        name="merge",
    )(attn, retn, ga, gr, x2, wba, wbr, wo, g2, wr)


def _select_kernel(aff_ref, u_ref, ls_ref, idx_ref, slot_ref, cs_ref, ce_ref,
                   thr, selbuf, cnt, csr, rank, digits, offi, *, cap, tb):
    s = pl.program_id(0)
    nch = cap // SELECT_SLOTS
    ps = jnp.where(s < N_EXPERTS, 0, 1)
    later = jnp.maximum(s - N_EXPERTS, 0)
    e = jnp.where(s < N_EXPERTS, s, later // nch)
    j = jnp.where(s < N_EXPERTS, 0, later % nch)
    nblk = SELECT_BLOCKS
    pc = SELECT_SLOTS

    def cumsum(vals):
        inb = jnp.dot(vals.astype(BF16), u_ref[...], preferred_element_type=F32)
        tot = jnp.broadcast_to(inb[:, tb - 1:tb], (nblk, LANES))
        off = jnp.dot(ls_ref[...], tot, preferred_element_type=F32, precision=lax.Precision.HIGHEST)
        return inb, off[:, 0:1], tot[:, 0:1]

    @pl.when(jnp.logical_and(ps == 0, jnp.logical_and(e == 0, j == 0)))
    def _():
        def bit_step(t, curs):
            bit = jnp.left_shift(jnp.int32(1), 30 - t)
            out = []
            for x in range(N_EXPERTS):
                cand = curs[x] | bit
                n_ge = jnp.sum((pltpu.bitcast(aff_ref[x], jnp.int32) >= cand).astype(jnp.int32), keepdims=True)
                out.append(jnp.where(n_ge >= cap, cand, curs[x]))
            return tuple(out)

        found = lax.fori_loop(0, 31, bit_step, tuple(jnp.zeros((1, 1), jnp.int32) for _ in range(N_EXPERTS)))
        for x in range(N_EXPERTS):
            thr[x] = jnp.broadcast_to(found[x], thr.shape[1:])

    @pl.when(jnp.logical_and(ps == 0, j == 0))
    def _():
        bits = pltpu.bitcast(aff_ref[e], jnp.int32)
        limit = thr[e][0:1, 0:1]
        gt = bits > limit
        eq = bits == limit
        need = (cap - jnp.sum(gt.astype(jnp.int32), keepdims=True)).astype(F32)
        eqf = eq.astype(F32)
        eq_in, eq_off, _ = cumsum(eqf)
        eq_rank = eq_in + eq_off - eqf
        sel = jnp.logical_or(gt, jnp.logical_and(eq, eq_rank < need)).astype(F32)
        selbuf[e] = sel.astype(BF16)

        @pl.when(e == 0)
        def _():
            cnt[...] = sel

        @pl.when(e > 0)
        def _():
            cnt[...] = cnt[...] + sel

    @pl.when(jnp.logical_and(ps == 1, j == 0))
    def _():
        @pl.when(e == 0)
        def _():
            c = cnt[...]
            c_in, c_off, _ = cumsum(c)
            start = c_in + c_off - c
            csr[...] = start
            cs_ref[...] = start.astype(jnp.int32)
            ce_ref[...] = (start + c).astype(jnp.int32)
            rank[...] = jnp.zeros_like(rank)

        sel = selbuf[e].astype(F32)
        s_in, s_off, s_tot = cumsum(sel)
        count_t = (s_in + s_off).T
        high = jnp.floor(count_t * (1.0 / 256.0))
        digits[:, :nblk] = high.astype(BF16)
        digits[:, nblk:] = (count_t - 256.0 * high).astype(BF16)
        slot_ref[0] = (csr[...] + rank[...]).astype(jnp.int32)
        rank[...] = rank[...] + sel
        offi[...] = jnp.broadcast_to(s_off + s_tot, (nblk, LANES))

    @pl.when(ps == 1)
    def _():
        slot = (j * pc + lax.broadcasted_iota(jnp.int32, (1, pc), 1)).astype(F32)
        blk = jnp.sum((offi[:, 0:1] <= slot).astype(jnp.int32), axis=0, keepdims=True)
        owner = jnp.where(lax.broadcasted_iota(jnp.int32, (nblk, pc), 0) == blk, 1.0, 0.0)
        weights = jnp.concatenate([256.0 * owner, owner], axis=0).astype(BF16)
        counts = jnp.dot(digits[...], weights, preferred_element_type=F32)
        inb = jnp.sum((counts <= slot + 0.5).astype(jnp.int32), axis=0, keepdims=True)
        idx_ref[0] = blk * tb + inb


def _select(afft, cap):
    n = afft.shape[1]
    nblk = SELECT_BLOCKS
    tb = n // nblk
    pc = SELECT_SLOTS
    assert n % nblk == 0 and tb % LANES == 0 and cap % pc == 0 and cap < 65536
    nch = cap // pc
    aff3 = afft.reshape(N_EXPERTS, nblk, tb)
    upper = jnp.asarray(np.triu(np.ones((tb, tb), np.float32)), BF16)
    lstrict = jnp.asarray(np.tril(np.ones((nblk, nblk), np.float32), -1))
    full = lambda a: pl.BlockSpec(a.shape, lambda s: (0,) * a.ndim)
    idx_spec = pl.BlockSpec((1, 1, pc), lambda s: (jnp.maximum(s - N_EXPERTS, 0), 0, 0))
    slot_spec = pl.BlockSpec((1, nblk, tb), lambda s: (jnp.maximum(s - N_EXPERTS, 0) // nch, 0, 0))
    tok_spec = pl.BlockSpec((nblk, tb), lambda s: (0, 0))
    idx, slots, cs, ce = pl.pallas_call(
        functools.partial(_select_kernel, cap=cap, tb=tb),
        grid=(N_EXPERTS + N_EXPERTS * nch,),
        in_specs=[full(aff3), full(upper), full(lstrict)],
        out_specs=[idx_spec, slot_spec, tok_spec, tok_spec],
        out_shape=[jax.ShapeDtypeStruct((N_EXPERTS * nch, 1, pc), jnp.int32),
                   jax.ShapeDtypeStruct((N_EXPERTS, nblk, tb), jnp.int32),
                   jax.ShapeDtypeStruct((nblk, tb), jnp.int32), jax.ShapeDtypeStruct((nblk, tb), jnp.int32)],
        scratch_shapes=[pltpu.VMEM((N_EXPERTS, 8, LANES), jnp.int32), pltpu.VMEM((N_EXPERTS, nblk, tb), BF16),
                        pltpu.VMEM((nblk, tb), F32), pltpu.VMEM((nblk, tb), F32), pltpu.VMEM((nblk, tb), F32),
                        pltpu.VMEM((tb, 2 * nblk), BF16), pltpu.VMEM((nblk, LANES), F32)],
        compiler_params=_cparams(1),
        name="select",
    )(aff3, upper, lstrict)
    return idx.reshape(-1), slots.reshape(N_EXPERTS, n), cs.reshape(-1), ce.reshape(-1)


def _slot_rows(afft, slots, idx, cap):
    n = afft.shape[1]
    per_expert = n // LANES
    table = jnp.concatenate([lax.bitcast_convert_type(afft, jnp.int32), slots], axis=0).reshape(-1, LANES)
    expert = jnp.arange(N_EXPERTS * cap, dtype=jnp.int32) // cap
    gate_row = expert * per_expert + idx // LANES
    return _sc_gather(table, jnp.concatenate([gate_row, gate_row + N_EXPERTS * per_expert]))


def _sc_mesh():
    return plsc.VectorSubcoreMesh(core_axis_name="c", subcore_axis_name="s")


def _sc_scatter(rows, idx, m_out):
    m, d = rows.shape
    assert m % SC_WINDOW == 0

    @functools.partial(pl.kernel, out_type=jax.ShapeDtypeStruct((m_out, d), rows.dtype), mesh=_sc_mesh(),
                       name="sc_scatter")
    def scatter(x_hbm, i_hbm, o_hbm):
        def body(x_vmem, i_vmem):
            pltpu.sync_copy(x_vmem, o_hbm.at[i_vmem.at[0]])

        pltpu.emit_pipeline(
            body,
            grid=(m // SC_WINDOW,),
            in_specs=[pl.BlockSpec((SC_WINDOW, d), lambda i: (i, 0)),
                      pl.BlockSpec((1, SC_WINDOW), lambda i: (0, i))],
            out_specs=[],
            core_axis_name=("c", "s"),
            dimension_semantics=(pltpu.PARALLEL,),
        )(x_hbm, i_hbm)

    return scatter(rows, idx.reshape(1, m))


def _sc_gather(table, idx):
    m = idx.shape[0]
    d = table.shape[1]
    assert m % SC_WINDOW == 0

    @functools.partial(pl.kernel, out_type=jax.ShapeDtypeStruct((m, d), table.dtype), mesh=_sc_mesh(),
                       name="sc_gather")
    def gather(x_hbm, i_hbm, o_hbm):
        def body(i_vmem, o_vmem):
            pltpu.sync_copy(x_hbm.at[i_vmem.at[0]], o_vmem)

        pltpu.emit_pipeline(
            body,
            grid=(m // SC_WINDOW,),
            in_specs=[pl.BlockSpec((1, SC_WINDOW), lambda i: (0, i))],
            out_specs=[pl.BlockSpec((SC_WINDOW, d), lambda i: (i, 0))],
            core_axis_name=("c", "s"),
            dimension_semantics=(pltpu.PARALLEL,),
        )(i_hbm, o_hbm)

    return gather(table, idx.reshape(1, m))


def _row_to_col(row):
    n = row.shape[1]
    eye = lax.broadcasted_iota(jnp.int32, (n, n), 0) == lax.broadcasted_iota(jnp.int32, (n, n), 1)
    return jnp.sum(jnp.where(eye, row, jnp.zeros_like(row)), axis=1, keepdims=True)


def _ffn_kernel(idx_ref, gate_ref, slot_ref, x0_ref, x1_ref, w1_hbm, w3_hbm, w2_hbm, o0_ref, o1_ref, dst_ref,
                wstage, w1b, w3b, w2b, wsem, *, layer):
    e = pl.program_id(0)
    i = pl.program_id(1)

    def weight_copies(expert):
        return [pltpu.make_async_copy(w_hbm.at[layer, expert], wstage.at[k], wsem.at[k])
                for k, w_hbm in enumerate((w1_hbm, w3_hbm, w2_hbm))]

    @pl.when(i == 0)
    def _():
        @pl.when(e == 0)
        def _():
            for cp in weight_copies(0):
                cp.start()
        for cp in weight_copies(e):
            cp.wait()
        w1b[...] = wstage[0].astype(BF16)
        w3b[...] = wstage[1].astype(BF16)
        w2b[...] = wstage[2].astype(BF16)

    @pl.when(jnp.logical_and(i == 1, e + 1 < N_EXPERTS))
    def _():
        for cp in weight_copies(e + 1):
            cp.start()

    token = _row_to_col(idx_ref[0])
    mine = lax.broadcasted_iota(jnp.int32, (FFN_ROWS, LANES), 1) == token % LANES
    gate = jnp.sum(jnp.where(mine, pltpu.bitcast(gate_ref[...], F32), 0.0), axis=1, keepdims=True)
    dst_ref[0] = jnp.sum(jnp.where(mine, slot_ref[...], 0).T, axis=0, keepdims=True)

    xs = _unpack_rows([x0_ref[...], x1_ref[...]])
    hg = jnp.dot(xs, w1b[...], preferred_element_type=F32)
    hu = jnp.dot(xs, w3b[...], preferred_element_type=F32)
    hid = (hg * _sigmoid(hg) * hu).astype(BF16)
    out = jnp.dot(hid, w2b[...], preferred_element_type=F32) * gate
    for ref, words in zip((o0_ref, o1_ref), _pack_rows(out)):
        ref[...] = words


def _expert_ffn(xs, idx, slot_rows, w1, w3, w2, layer):
    m = xs[0].shape[0]
    rows = FFN_ROWS
    nt = m // (N_EXPERTS * rows)
    assert SC_PIECES == 2 and nt >= 2
    piece = pl.BlockSpec((rows, SC_ROW), lambda e, i: (e * nt + i, 0))
    nsteps = N_EXPERTS * nt
    gate_rows = pl.BlockSpec((rows, LANES), lambda e, i: (e * nt + i, 0))
    dest_rows = pl.BlockSpec((rows, LANES), lambda e, i: (nsteps + e * nt + i, 0))
    per_step = pl.BlockSpec((1, 1, rows), lambda e, i: (e * nt + i, 0, 0))
    any_spec = pl.BlockSpec(memory_space=pl.ANY)
    *outs, dst = pl.pallas_call(
        functools.partial(_ffn_kernel, layer=layer),
        grid=(N_EXPERTS, nt),
        in_specs=[per_step, gate_rows, dest_rows] + [piece] * SC_PIECES + [any_spec] * 3,
        out_specs=[piece] * SC_PIECES + [per_step],
        out_shape=[jax.ShapeDtypeStruct((m, SC_ROW), jnp.int32)] * SC_PIECES
        + [jax.ShapeDtypeStruct((m // rows, 1, rows), jnp.int32)],
        scratch_shapes=[pltpu.VMEM((3, D_MODEL, EXPERT_FF), F32), pltpu.VMEM((D_MODEL, EXPERT_FF), BF16),
                        pltpu.VMEM((D_MODEL, EXPERT_FF), BF16), pltpu.VMEM((EXPERT_FF, D_MODEL), BF16),
                        pltpu.SemaphoreType.DMA((3,))],
        compiler_params=_cparams(2),
        name="expert_ffn",
    )(idx.reshape(nsteps, 1, rows), slot_rows, slot_rows, *xs, w1, w3, w2)
    return outs, dst.reshape(-1)


def _combine_kernel(tsub_ref, x_ref, cs_ref, ce_ref, r0_hbm, r1_hbm, o_ref, rbuf, obuf, rows16, sems, osem,
                    *, ntile, total):
    pieces_hbm = (r0_hbm, r1_hbm)
    win = COMBINE_WINDOW
    sub = COMBINE_SUB
    subwin = COMBINE_SUBWIN
    per = COMBINE_TOKENS // sub
    i = pl.program_id(0)
    slot = lax.rem(i, COMBINE_BUFFERS)

    def window_start(t):
        return pl.multiple_of((tsub_ref[t * per] // BF16_TILE_ROWS) * BF16_TILE_ROWS, BF16_TILE_ROWS)

    def copies(t, b):
        s = window_start(t)
        return [pltpu.make_async_copy(pieces_hbm[c].at[pl.ds(s, win)], rbuf.at[b, c], sems.at[b, c])
                for c in range(SC_PIECES)]

    ahead = COMBINE_BUFFERS - 1

    @pl.when(i == 0)
    def _():
        for t in range(min(ahead, ntile)):
            for cp in copies(t, t):
                cp.start()

    @pl.when(i + ahead < ntile)
    def _():
        for cp in copies(i + ahead, lax.rem(i + ahead, COMBINE_BUFFERS)):
            cp.start()

    for cp in copies(i, slot):
        cp.wait()

    first = [_row_to_col(cs_ref[0][:, g * sub:(g + 1) * sub]) for g in range(per)]
    last = [_row_to_col(ce_ref[0][:, g * sub:(g + 1) * sub]) for g in range(per)]

    def zero_unwritten(words, base):
        written = (base + lax.broadcasted_iota(jnp.int32, (win, 1), 0)) < total
        return jnp.where(written, words, 0)

    def owner_matrix(g, base, width):
        r = base + lax.broadcasted_iota(jnp.int32, (1, width), 1)
        return jnp.logical_and(first[g] <= r, r < last[g]).astype(BF16)

    s0 = window_start(i)
    tail = s0 + win > total

    @pl.when(tail)
    def _():
        rows16[...] = _unpack_rows([zero_unwritten(rbuf[slot, c], s0) for c in range(SC_PIECES)])

    @pl.when(jnp.logical_not(tail))
    def _():
        rows16[...] = _unpack_rows([rbuf[slot, c] for c in range(SC_PIECES)])

    offsets = []
    fits = None
    for g in range(per):
        off = (tsub_ref[i * per + g] // BF16_TILE_ROWS) * BF16_TILE_ROWS - s0
        ok = jnp.logical_and(tsub_ref[i * per + g + 1] - s0 <= off + subwin, off + subwin <= win)
        fits = ok if fits is None else jnp.logical_and(fits, ok)
        offsets.append(off)

    @pl.when(fits)
    def _():
        for g in range(per):
            tokens = slice(g * sub, (g + 1) * sub)
            off = pl.multiple_of(offsets[g], BF16_TILE_ROWS)
            q = owner_matrix(g, s0 + off, subwin)
            o_ref[tokens, :] = x_ref[tokens, :] + jnp.dot(q, rows16[pl.ds(off, subwin), :],
                                                         preferred_element_type=F32)

    @pl.when(jnp.logical_not(fits))
    def _():
        def everyone(base):
            return jnp.concatenate([owner_matrix(g, base, win) for g in range(per)], axis=0)

        y = x_ref[...] + jnp.dot(everyone(s0), rows16[...], preferred_element_type=F32)
        n_extra = jnp.maximum(tsub_ref[(i + 1) * per] - (s0 + win) + win - 1, 0) // win

        def extra(k, acc):
            base = pl.multiple_of(s0 + (k + 1) * win, BF16_TILE_ROWS)
            cps = [pltpu.make_async_copy(pieces_hbm[c].at[pl.ds(base, win)], obuf.at[c], osem.at[c])
                   for c in range(SC_PIECES)]
            for cp in cps:
                cp.start()
            for cp in cps:
                cp.wait()
            rows = _unpack_rows([zero_unwritten(obuf[c], base) for c in range(SC_PIECES)])
            return acc + jnp.dot(everyone(base), rows, preferred_element_type=F32)

        o_ref[...] = lax.fori_loop(0, n_extra, extra, y)


def _combine(xmid, cs, ce, pieces, total):
    n = xmid.shape[0]
    tt = COMBINE_TOKENS
    win = COMBINE_WINDOW
    ntile = n // tt
    tsub = jnp.concatenate([cs[::COMBINE_SUB], jnp.full((1,), total, jnp.int32)])
    cs3 = cs.reshape(ntile, 1, tt)
    ce3 = ce.reshape(ntile, 1, tt)
    any_spec = pl.BlockSpec(memory_space=pl.ANY)
    tok = pl.BlockSpec((1, 1, tt), lambda i, ts: (i, 0, 0))
    grid_spec = pltpu.PrefetchScalarGridSpec(
        num_scalar_prefetch=1,
        grid=(ntile,),
        in_specs=[pl.BlockSpec((tt, D_MODEL), lambda i, ts: (i, 0)), tok, tok] + [any_spec] * SC_PIECES,
        out_specs=pl.BlockSpec((tt, D_MODEL), lambda i, ts: (i, 0)),
        scratch_shapes=[pltpu.VMEM((COMBINE_BUFFERS, SC_PIECES, win, SC_ROW), jnp.int32),
                        pltpu.VMEM((SC_PIECES, win, SC_ROW), jnp.int32), pltpu.VMEM((win, D_MODEL), BF16),
                        pltpu.SemaphoreType.DMA((COMBINE_BUFFERS, SC_PIECES)), pltpu.SemaphoreType.DMA((SC_PIECES,))],
    )
    return pl.pallas_call(
        functools.partial(_combine_kernel, ntile=ntile, total=total),
        grid_spec=grid_spec,
        out_shape=jax.ShapeDtypeStruct((n, D_MODEL), F32),
        compiler_params=_cparams(1),
        name="combine",
    )(tsub, xmid, cs3, ce3, *pieces)


def _t5_bucket(rel):
    half = REL_BUCKETS // 2
    max_exact = half // 2
    base = np.where(rel > 0, half, 0)
    n = np.abs(rel)
    large = max_exact + (np.log(np.maximum(n, 1) / max_exact) / math.log(REL_MAX_DIST / max_exact)
                         * (half - max_exact)).astype(np.int32)
    large = np.minimum(large, half - 1)
    return (base + np.where(n < max_exact, n, large)).astype(np.int32)


def _head_perm():
    nq = ATTN_HEADS // 2
    cols = []
    for j in range(nq):
        for half in range(2):
            h = j + nq * half
            cols.extend(range(h * ATTN_HEAD_DIM, (h + 1) * ATTN_HEAD_DIM))
    return np.asarray(cols, np.int32)


def _attn_bias_tables(rel_bias):
    q_pos = np.arange(BLOCK)[:, None]
    k_off = np.arange(3 * BLOCK)[None, :] - BLOCK
    rel = k_off - q_pos
    in_window = np.abs(rel) <= WINDOW
    onehot = jnp.asarray(_t5_bucket(rel)[:, :, None] == np.arange(REL_BUCKETS)[None, None, :], F32)
    bias = jnp.einsum("qkb,bh->hqk", onehot, rel_bias.astype(F32), precision=lax.Precision.HIGHEST)
    col = np.arange(3 * BLOCK)[None, :]
    tables = []
    for valid in (col >= BLOCK, np.ones_like(col, bool), col < 2 * BLOCK):
        t = jnp.where(jnp.asarray(in_window & valid)[None], bias, NEG)
        nq = ATTN_HEADS // 2
        rows = [jnp.concatenate([t[j], t[j + nq]], axis=1) for j in range(nq)]
        tables.append(jnp.concatenate(rows, axis=0))
    return jnp.stack(tables)


def _retention_tables(decay_logit, norm_g):
    cr = RET_CHUNK
    lg = jax.nn.log_sigmoid(decay_logit.astype(F32))
    lgf, lgb = lg[0][:, None, None], lg[1][:, None, None]
    pos = np.arange(cr, dtype=np.float32)
    dist = pos[:, None] - pos[None, :]
    scale = RET_DIM ** -0.5
    dmask = jnp.where(jnp.asarray(dist >= 0)[None],
                      jnp.exp(lgf * np.maximum(dist, 0.0)[None]),
                      jnp.exp(lgb * np.maximum(-dist, 0.0)[None])) * scale
    col = lambda v: jnp.broadcast_to(v[:, :, None], (RET_HEADS, cr, RET_DIM))
    rowf = col(jnp.exp(lg[0][:, None] * pos[None]))
    rowb = col(jnp.exp(lg[1][:, None] * (cr - 1.0 - pos)[None]))
    wkf = col(jnp.exp(lg[0][:, None] * (cr - pos)[None]) * scale)
    wkb = col(jnp.exp(lg[1][:, None] * (pos + 1.0)[None]) * scale)
    dec = jnp.concatenate([jnp.exp(lg[0] * cr), jnp.exp(lg[1] * cr)])
    return dec, dmask, rowf, rowb, wkf, wkb, norm_g.astype(F32)


def _layer(x2, b, s, p):
    qa, ka, va, qr, kr, vr, gr, ga, gt = _in_proj(x2, p["g1"], p["w_in"], p["qg"], p["kg"], p["bdq"], p["bdk"])
    attn = _attention(qa, ka, va, p["bias3"], p["sink"], b, s)
    retn = _retention(qr, kr, vr, gr, p["retn"], b, s)
    xmid, afft, *h2 = _merge(attn, retn, ga, gt, x2, p["wba"], p["wbr"], p["wo"], p["g2"], p["wr"])
    n = b * s
    cap = max(1, EC_CAPACITY_FACTOR * n // N_EXPERTS)
    total = N_EXPERTS * cap
    idx, slots, cs, ce = _select(afft, cap)
    slot_rows = _slot_rows(afft, slots, idx, cap)
    xs = [_sc_gather(piece, idx) for piece in h2]
    outs, dst = _expert_ffn(xs, idx, slot_rows, p["w1"], p["w3"], p["w2"], p["layer"])
    by_token = [_sc_scatter(o, dst, total + COMBINE_WINDOW) for o in outs]
    return _combine(xmid, cs, ce, by_token, total)


def kernel(x_prompt, x_sample, norm_mix_g, w_in, q_norm_g, k_norm_g, attn_sink, rel_bias, retn_decay_logit, retn_norm_g, w_branch_attn, w_branch_retn, w_out, norm_ffn_g, w_router, w_exp_gate, w_exp_up, w_exp_down):
    depth = w_in.shape[0]
    perm = _head_perm()
    bias3 = _attn_bias_tables(rel_bias)
    bdq = jnp.asarray(np.kron(np.eye(ATTN_HEADS), np.ones((ATTN_HEAD_DIM, ATTN_HEAD_DIM))), BF16)
    bdk = jnp.asarray(np.kron(np.eye(ATTN_KV_HEADS), np.ones((ATTN_HEAD_DIM, ATTN_HEAD_DIM))), BF16)
    layers = []
    for l in range(depth):
        w = w_in[l]
        w = jnp.concatenate([w[:, :ATTN_WIDTH][:, perm], w[:, ATTN_WIDTH:]], axis=1).astype(BF16)
        wr = jnp.pad(w_router[l], ((0, 0), (0, LANES - N_EXPERTS))).astype(BF16)
        layers.append(dict(
            g1=norm_mix_g[l].astype(F32)[None], w_in=w,
            qg=(jnp.tile(q_norm_g[l].astype(F32), ATTN_HEADS) * (ATTN_HEAD_DIM ** -0.5))[None],
            kg=jnp.tile(k_norm_g[l].astype(F32), ATTN_KV_HEADS)[None],
            bdq=bdq, bdk=bdk, bias3=bias3, sink=attn_sink[l].astype(F32),
            retn=_retention_tables(retn_decay_logit[l], retn_norm_g[l]),
            wba=w_branch_attn[l][perm, :].astype(BF16), wbr=w_branch_retn[l].astype(BF16),
            wo=w_out[l].astype(BF16), g2=norm_ffn_g[l].astype(F32)[None], wr=wr,
            w1=w_exp_gate, w3=w_exp_up, w2=w_exp_down, layer=l))

    def trunk(x):
        b, s, d = x.shape
        x2 = x.reshape(b * s, d)
        for p in layers:
            x2 = _layer(x2, b, s, p)
        return x2.reshape(b, s, d)

    return (trunk(x_prompt), trunk(x_sample))
```

```python
import functools
import math

import numpy as np
import jax
import jax.numpy as jnp
from jax import lax
from jax.experimental import pallas as pl
from jax.experimental.pallas import tpu as pltpu
from jax.experimental.pallas import tpu_sc as plsc

D_MODEL = 1024
ATTN_HEADS = 8
ATTN_KV_HEADS = 2
ATTN_HEAD_DIM = 64
WINDOW = 128
BLOCK = 128
REL_BUCKETS = 32
REL_MAX_DIST = 128
RET_HEADS = 4
RET_DIM = 128
N_EXPERTS = 16
EC_CAPACITY_FACTOR = 2
EXPERT_FF = 1024
EPS = 1e-6

ATTN_WIDTH = ATTN_HEADS * ATTN_HEAD_DIM
KV_WIDTH = ATTN_KV_HEADS * ATTN_HEAD_DIM
RET_WIDTH = RET_HEADS * RET_DIM
IN_SPLITS = (ATTN_WIDTH, KV_WIDTH, KV_WIDTH, RET_WIDTH, RET_WIDTH, RET_WIDTH, RET_WIDTH, D_MODEL, D_MODEL)
IN_OFFSETS = tuple(int(o) for o in np.cumsum((0,) + IN_SPLITS))

LANES = 128
BF16_TILE_ROWS = 16
VMEM_LIMIT_BYTES = 56 * 1024 * 1024

TOKEN_TILE = 1024
MERGE_ROWS = 256
IN_PROJ_TILE = 1024
ATTN_QUERIES = 2048
RET_CHUNK = 256
RET_STEP = 2048
FFN_ROWS = 1024
SELECT_BLOCKS = 128
SELECT_SLOTS = 2048
SC_WINDOW = 128
SC_ROW = 256
PACKED_WIDTH = D_MODEL // 2
SC_PIECES = PACKED_WIDTH // SC_ROW
COMBINE_TOKENS = 512
COMBINE_WINDOW = 1280
COMBINE_SUB = 128
COMBINE_SUBWIN = 384
COMBINE_BUFFERS = 3

F32 = jnp.float32
BF16 = jnp.bfloat16
NEG = -1e30


def _cparams(n_axes, vmem=VMEM_LIMIT_BYTES):
    return pltpu.CompilerParams(dimension_semantics=("arbitrary",) * n_axes, vmem_limit_bytes=vmem)


def _sigmoid(x):
    return 0.5 * jnp.tanh(0.5 * x) + 0.5


HIGH_HALF = -65536


def _pack_rows(x):
    bits = pltpu.bitcast(x.astype(BF16).astype(F32), jnp.int32)
    words = lax.shift_right_logical(bits[:, :PACKED_WIDTH], 16) | (bits[:, PACKED_WIDTH:] & HIGH_HALF)
    return [words[:, c * SC_ROW:(c + 1) * SC_ROW] for c in range(SC_PIECES)]


def _unpack_rows(pieces):
    low = [pltpu.bitcast(lax.shift_left(w, 16), F32) for w in pieces]
    high = [pltpu.bitcast(w & HIGH_HALF, F32) for w in pieces]
    return jnp.concatenate(low + high, axis=1).astype(BF16)


def _in_proj_kernel(x_ref, g_ref, w_ref, qg_ref, kg_ref, bdq_ref, bdk_ref,
                    qa_ref, ka_ref, va_ref, qr_ref, kr_ref, vr_ref, gr_ref, ga_ref, gt_ref):
    x = x_ref[...]
    ms = jnp.mean(x * x, axis=-1, keepdims=True)
    h = (x * lax.rsqrt(ms + EPS) * g_ref[...]).astype(BF16)

    def mm(k):
        return jnp.dot(h, w_ref[:, IN_OFFSETS[k]:IN_OFFSETS[k + 1]], preferred_element_type=F32)

    def head_norm(t, bd_ref, gain_ref):
        ss = jnp.dot((t * t).astype(BF16), bd_ref[...], preferred_element_type=F32)
        return t * lax.rsqrt(ss * (1.0 / ATTN_HEAD_DIM) + EPS) * gain_ref[...]

    q_raw = mm(0)
    k_raw = mm(1)
    for k, ref in ((2, va_ref), (3, qr_ref), (4, kr_ref), (5, vr_ref)):
        ref[...] = mm(k).astype(BF16)
    g = mm(6)
    gr_ref[...] = (g * _sigmoid(g)).astype(BF16)
    ga_ref[...] = _sigmoid(mm(7)).astype(BF16)
    gt_ref[...] = _sigmoid(mm(8)).astype(BF16)
    qa_ref[...] = head_norm(q_raw, bdq_ref, qg_ref).astype(BF16)
    ka_ref[...] = head_norm(k_raw, bdk_ref, kg_ref).astype(BF16)


def _in_proj(x2, g, w, qg, kg, bdq, bdk):
    n = x2.shape[0]
    tm = IN_PROJ_TILE
    full = lambda a: pl.BlockSpec(a.shape, lambda i: (0,) * a.ndim, pipeline_mode=pl.Buffered(1))
    widths = IN_SPLITS
    return pl.pallas_call(
        _in_proj_kernel,
        grid=(n // tm,),
        in_specs=[pl.BlockSpec((tm, D_MODEL), lambda i: (i, 0)), full(g), full(w), full(qg), full(kg),
                  full(bdq), full(bdk)],
        out_specs=[pl.BlockSpec((tm, wd), lambda i: (i, 0)) for wd in widths],
        out_shape=[jax.ShapeDtypeStruct((n, wd), BF16) for wd in widths],
        compiler_params=_cparams(1),
        name="in_proj",
    )(x2, g, w, qg, kg, bdq, bdk)


def _attn_kernel(sink_ref, q_ref, kp_ref, kc_ref, kn_ref, vp_ref, vc_ref, vn_ref, bias_ref, o_ref, *, nsteps):
    nq = ATTN_HEADS // 2
    ni = pl.program_id(1)
    k = jnp.concatenate([kp_ref[...], kc_ref[...], kn_ref[...]], axis=0)
    v = jnp.concatenate([vp_ref[...], vc_ref[...], vn_ref[...]], axis=0)
    low = lax.broadcasted_iota(jnp.int32, k.shape, 1) < ATTN_HEAD_DIM
    zero = jnp.zeros_like(k)
    k_lo, k_hi = jnp.where(low, k, zero), jnp.where(low, zero, k)
    v_lo, v_hi = jnp.where(low, v, zero), jnp.where(low, zero, v)
    nk = 3 * BLOCK
    low_o = lax.broadcasted_iota(jnp.int32, (BLOCK, LANES), 1) < ATTN_HEAD_DIM
    key_low = lax.broadcasted_iota(jnp.int32, (2 * nk, LANES), 0) < nk
    lane_low = lax.broadcasted_iota(jnp.int32, (2 * nk, LANES), 1) < ATTN_HEAD_DIM
    ones_bd = (key_low == lane_low).astype(BF16)
    nsub = ATTN_QUERIES // BLOCK
    for sb in range(nsub):
        rows = slice(sb * BLOCK, (sb + 1) * BLOCK)
        keys = slice(sb * BLOCK, sb * BLOCK + nk)
        q = q_ref[rows, :]
        qs = jnp.concatenate([q[:, j * LANES:(j + 1) * LANES] for j in range(nq)], axis=0)
        kbd = jnp.concatenate([k_lo[keys], k_hi[keys]], axis=0)
        vbd = jnp.concatenate([v_lo[keys], v_hi[keys]], axis=0)
        s = lax.dot_general(qs, kbd, (((1,), (1,)), ((), ())), preferred_element_type=F32)
        if sb == 0:
            table = jnp.where(ni == 0, 0, 1)
        elif sb == nsub - 1:
            table = jnp.where(ni == nsteps - 1, 2, 1)
        else:
            table = 1
        s = s + bias_ref[table]
        probs, sink_terms = [], []
        for j in range(nq):
            row_p, row_sink = [], []
            for half in range(2):
                sj = s[j * BLOCK:(j + 1) * BLOCK, half * nk:(half + 1) * nk]
                sk = sink_ref[j + nq * half]
                m = jnp.maximum(jnp.max(sj, axis=-1, keepdims=True), sk)
                row_p.append(jnp.exp(sj - m).astype(BF16))
                row_sink.append(jnp.exp(sk - m))
            probs.append(jnp.concatenate(row_p, axis=1))
            sink_terms.append(jnp.where(low_o, row_sink[0], row_sink[1]))
        pm = jnp.concatenate(probs, axis=0)
        od = jnp.dot(pm, jnp.concatenate([vbd, ones_bd], axis=1), preferred_element_type=F32)
        o = od[:, :LANES] / (od[:, LANES:] + jnp.concatenate(sink_terms, axis=0))
        for j in range(nq):
            o_ref[rows, j * LANES:(j + 1) * LANES] = o[j * BLOCK:(j + 1) * BLOCK].astype(BF16)


def _attention(qa, ka, va, bias3, sink, b, s):
    tq = ATTN_QUERIES
    per = tq // BLOCK
    nb = s // BLOCK
    nsteps = s // tq
    assert s % tq == 0 and nb >= 2
    n = b * s
    main = lambda wd: pl.BlockSpec((tq, wd), lambda bi, ni: (bi * nsteps + ni, 0))
    prev = pl.BlockSpec((BLOCK, KV_WIDTH), lambda bi, ni: (bi * nb + jnp.maximum(ni * per - 1, 0), 0))
    nxt = pl.BlockSpec((BLOCK, KV_WIDTH), lambda bi, ni: (bi * nb + jnp.minimum(ni * per + per, nb - 1), 0))
    return pl.pallas_call(
        functools.partial(_attn_kernel, nsteps=nsteps),
        grid=(b, nsteps),
        in_specs=[pl.BlockSpec(memory_space=pltpu.SMEM), main(ATTN_WIDTH),
                  prev, main(KV_WIDTH), nxt, prev, main(KV_WIDTH), nxt,
                  pl.BlockSpec(bias3.shape, lambda bi, ni: (0, 0, 0))],
        out_specs=main(ATTN_WIDTH),
        out_shape=jax.ShapeDtypeStruct((n, ATTN_WIDTH), BF16),
        compiler_params=_cparams(2),
        name="attn",
    )(sink, qa, ka, ka, ka, va, va, va, bias3)


def _retn_kernel(dec_ref, q_ref, k_ref, v_ref, g_ref, dmask_ref, rowf_ref, rowb_ref, wkf_ref, wkb_ref, ng_ref,
                 o_ref, tstore, uf, tb, *, nsteps):
    p = pl.program_id(1)
    n = pl.program_id(2)
    cr = RET_CHUNK
    per = RET_STEP // cr
    tn = (((0,), (0,)), ((), ()))
    nt = (((1,), (1,)), ((), ()))
    hs = lambda h: slice(h * RET_DIM, (h + 1) * RET_DIM)

    @pl.when(p == 0)
    def _():
        @pl.when(n == 0)
        def _():
            tb[...] = jnp.zeros_like(tb)
        first_chunk = (nsteps - 1 - n) * per
        for sub in reversed(range(per)):
            rows = slice(sub * cr, (sub + 1) * cr)
            for h in range(RET_HEADS):
                tstore[first_chunk + sub, h] = tb[h].astype(BF16)
                kw = (k_ref[rows, hs(h)].astype(F32) * wkb_ref[h]).astype(BF16)
                upd = lax.dot_general(kw, v_ref[rows, hs(h)], tn, preferred_element_type=F32)
                tb[h] = dec_ref[RET_HEADS + h] * tb[h] + upd

    @pl.when(p == 1)
    def _():
        @pl.when(n == 0)
        def _():
            uf[...] = jnp.zeros_like(uf)
        heads = range(RET_HEADS)
        for sub in range(per):
            rows = slice(sub * cr, (sub + 1) * cr)
            qs = [q_ref[rows, hs(h)] for h in heads]
            ks = [k_ref[rows, hs(h)] for h in heads]
            vs = [v_ref[rows, hs(h)] for h in heads]
            scores = [lax.dot_general(qs[h], ks[h], nt, preferred_element_type=F32) for h in heads]
            states = [jnp.concatenate([uf[h].astype(BF16), tstore[n * per + sub, h]], axis=1) for h in heads]
            cross = [jnp.dot(qs[h], states[h], preferred_element_type=F32) for h in heads]
            update = [lax.dot_general((ks[h].astype(F32) * wkf_ref[h]).astype(BF16), vs[h], tn,
                                      preferred_element_type=F32) for h in heads]
            intra = [jnp.dot((scores[h] * dmask_ref[h]).astype(BF16), vs[h], preferred_element_type=F32)
                     for h in heads]
            for h in heads:
                o = intra[h] + cross[h][:, :RET_DIM] * rowf_ref[h] + cross[h][:, RET_DIM:] * rowb_ref[h]
                mu = jnp.mean(o, axis=-1, keepdims=True)
                d = o - mu
                var = jnp.mean(d * d, axis=-1, keepdims=True)
                on = d * lax.rsqrt(var + EPS) * ng_ref[h:h + 1, :]
                gate = g_ref[rows, hs(h)].astype(F32)
                o_ref[rows, hs(h)] = (gate * on).astype(BF16)
                uf[h] = dec_ref[h] * uf[h] + update[h]


def _retention(qr, kr, vr, gr, tables, b, s):
    dec, dmask, rowf, rowb, wkf, wkb, ng = tables
    rs = RET_STEP
    nsteps = s // rs
    nc = s // RET_CHUNK
    assert s % rs == 0
    n = b * s
    full = lambda a: pl.BlockSpec(a.shape, lambda bi, pi, ni: (0,) * a.ndim)
    fwd_spec = pl.BlockSpec((rs, RET_WIDTH), lambda bi, pi, ni: (bi * nsteps + ni * pi, 0))
    kv_spec = pl.BlockSpec((rs, RET_WIDTH),
                           lambda bi, pi, ni: (bi * nsteps + ni * pi + (1 - pi) * (nsteps - 1 - ni), 0))
    return pl.pallas_call(
        functools.partial(_retn_kernel, nsteps=nsteps),
        grid=(b, 2, nsteps),
        in_specs=[pl.BlockSpec(memory_space=pltpu.SMEM), fwd_spec, kv_spec, kv_spec, fwd_spec,
                  full(dmask), full(rowf), full(rowb), full(wkf), full(wkb), full(ng)],
        out_specs=fwd_spec,
        out_shape=jax.ShapeDtypeStruct((n, RET_WIDTH), BF16),
        scratch_shapes=[pltpu.VMEM((nc, RET_HEADS, RET_DIM, RET_DIM), BF16),
                        pltpu.VMEM((RET_HEADS, RET_DIM, RET_DIM), F32),
                        pltpu.VMEM((RET_HEADS, RET_DIM, RET_DIM), F32)],
        compiler_params=_cparams(3),
        name="retention",
    )(dec, qr, kr, vr, gr, dmask, rowf, rowb, wkf, wkb, ng)


def _merge_kernel(attn_ref, retn_ref, ga_ref, gr_ref, x_ref, wba_ref, wbr_ref, wo_ref, g2_ref, wr_ref,
                  xmid_ref, afft_ref, *h2_refs):
    tiles = [slice(t * MERGE_ROWS, (t + 1) * MERGE_ROWS) for t in range(TOKEN_TILE // MERGE_ROWS)]
    branch = [(jnp.dot(attn_ref[rows, :], wba_ref[...], preferred_element_type=F32),
               jnp.dot(retn_ref[rows, :], wbr_ref[...], preferred_element_type=F32)) for rows in tiles]
    resid = []
    for rows, (a, r) in zip(tiles, branch):
        merged = ga_ref[rows, :].astype(F32) * a + gr_ref[rows, :].astype(F32) * r
        xn = x_ref[rows, :] + jnp.dot(merged.astype(BF16), wo_ref[...], preferred_element_type=F32)
        xmid_ref[rows, :] = xn
        resid.append(xn)
    scores = []
    for rows, xn in zip(tiles, resid):
        ms = jnp.mean(xn * xn, axis=-1, keepdims=True)
        h2 = xn * lax.rsqrt(ms + EPS) * g2_ref[...]
        for ref, words in zip(h2_refs, _pack_rows(h2)):
            ref[rows, :] = words
        scores.append(jnp.dot(h2.astype(BF16), wr_ref[...], preferred_element_type=F32))
    for rows, logits in zip(tiles, scores):
        real = lax.broadcasted_iota(jnp.int32, logits.shape, 1) < N_EXPERTS
        logits = jnp.where(real, logits, -jnp.inf)
        m = jnp.max(logits, axis=-1, keepdims=True)
        ex = jnp.exp(logits - m)
        aff = ex / jnp.sum(ex, axis=-1, keepdims=True)
        afft_ref[:, rows] = aff.T[:N_EXPERTS, :]


def _merge(attn, retn, ga, gr, x2, wba, wbr, wo, g2, wr):
    n = x2.shape[0]
    tm = TOKEN_TILE
    full = lambda a: pl.BlockSpec(a.shape, lambda i: (0,) * a.ndim)
    row = lambda wd: pl.BlockSpec((tm, wd), lambda i: (i, 0))
    return pl.pallas_call(
        _merge_kernel,
        grid=(n // tm,),
        in_specs=[row(ATTN_WIDTH), row(RET_WIDTH), row(D_MODEL), row(D_MODEL), row(D_MODEL),
                  full(wba), full(wbr), full(wo), full(g2), full(wr)],
        out_specs=[row(D_MODEL), pl.BlockSpec((N_EXPERTS, tm), lambda i: (0, i))] + [row(SC_ROW)] * SC_PIECES,
        out_shape=[jax.ShapeDtypeStruct((n, D_MODEL), F32), jax.ShapeDtypeStruct((N_EXPERTS, n), F32)]
        + [jax.ShapeDtypeStruct((n, SC_ROW), jnp.int32)] * SC_PIECES,
        compiler_params=_cparams(1),
        name="merge",
    )(attn, retn, ga, gr, x2, wba, wbr, wo, g2, wr)


def _select_kernel(aff_ref, u_ref, ls_ref, idx_ref, slot_ref, cs_ref, ce_ref,
                   thr, selbuf, cnt, csr, rank, digits, offi, *, cap, tb):
    s = pl.program_id(0)
    nch = cap // SELECT_SLOTS
    ps = jnp.where(s < N_EXPERTS, 0, 1)
    later = jnp.maximum(s - N_EXPERTS, 0)
    e = jnp.where(s < N_EXPERTS, s, later // nch)
    j = jnp.where(s < N_EXPERTS, 0, later % nch)
    nblk = SELECT_BLOCKS
    pc = SELECT_SLOTS

    def cumsum(vals):
        inb = jnp.dot(vals.astype(BF16), u_ref[...], preferred_element_type=F32)
        tot = jnp.broadcast_to(inb[:, tb - 1:tb], (nblk, LANES))
        off = jnp.dot(ls_ref[...], tot, preferred_element_type=F32, precision=lax.Precision.HIGHEST)
        return inb, off[:, 0:1], tot[:, 0:1]

    @pl.when(jnp.logical_and(ps == 0, jnp.logical_and(e == 0, j == 0)))
    def _():
        def bit_step(t, curs):
            bit = jnp.left_shift(jnp.int32(1), 30 - t)
            out = []
            for x in range(N_EXPERTS):
                cand = curs[x] | bit
                n_ge = jnp.sum((pltpu.bitcast(aff_ref[x], jnp.int32) >= cand).astype(jnp.int32), keepdims=True)
                out.append(jnp.where(n_ge >= cap, cand, curs[x]))
            return tuple(out)

        found = lax.fori_loop(0, 31, bit_step, tuple(jnp.zeros((1, 1), jnp.int32) for _ in range(N_EXPERTS)))
        for x in range(N_EXPERTS):
            thr[x] = jnp.broadcast_to(found[x], thr.shape[1:])

    @pl.when(jnp.logical_and(ps == 0, j == 0))
    def _():
        bits = pltpu.bitcast(aff_ref[e], jnp.int32)
        limit = thr[e][0:1, 0:1]
        gt = bits > limit
        eq = bits == limit
        need = (cap - jnp.sum(gt.astype(jnp.int32), keepdims=True)).astype(F32)
        eqf = eq.astype(F32)
        eq_in, eq_off, _ = cumsum(eqf)
        eq_rank = eq_in + eq_off - eqf
        sel = jnp.logical_or(gt, jnp.logical_and(eq, eq_rank < need)).astype(F32)
        selbuf[e] = sel.astype(BF16)

        @pl.when(e == 0)
        def _():
            cnt[...] = sel

        @pl.when(e > 0)
        def _():
            cnt[...] = cnt[...] + sel

    @pl.when(jnp.logical_and(ps == 1, j == 0))
    def _():
        @pl.when(e == 0)
        def _():
            c = cnt[...]
            c_in, c_off, _ = cumsum(c)
            start = c_in + c_off - c
            csr[...] = start
            cs_ref[...] = start.astype(jnp.int32)
            ce_ref[...] = (start + c).astype(jnp.int32)
            rank[...] = jnp.zeros_like(rank)

        sel = selbuf[e].astype(F32)
        s_in, s_off, s_tot = cumsum(sel)
        count_t = (s_in + s_off).T
        high = jnp.floor(count_t * (1.0 / 256.0))
        digits[:, :nblk] = high.astype(BF16)
        digits[:, nblk:] = (count_t - 256.0 * high).astype(BF16)
        slot_ref[0] = (csr[...] + rank[...]).astype(jnp.int32)
        rank[...] = rank[...] + sel
        offi[...] = jnp.broadcast_to(s_off + s_tot, (nblk, LANES))

    @pl.when(ps == 1)
    def _():
        slot = (j * pc + lax.broadcasted_iota(jnp.int32, (1, pc), 1)).astype(F32)
        blk = jnp.sum((offi[:, 0:1] <= slot).astype(jnp.int32), axis=0, keepdims=True)
        owner = jnp.where(lax.broadcasted_iota(jnp.int32, (nblk, pc), 0) == blk, 1.0, 0.0)
        weights = jnp.concatenate([256.0 * owner, owner], axis=0).astype(BF16)
        counts = jnp.dot(digits[...], weights, preferred_element_type=F32)
        inb = jnp.sum((counts <= slot + 0.5).astype(jnp.int32), axis=0, keepdims=True)
        idx_ref[0] = blk * tb + inb


def _select(afft, cap):
    n = afft.shape[1]
    nblk = SELECT_BLOCKS
    tb = n // nblk
    pc = SELECT_SLOTS
    assert n % nblk == 0 and tb % LANES == 0 and cap % pc == 0 and cap < 65536
    nch = cap // pc
    aff3 = afft.reshape(N_EXPERTS, nblk, tb)
    upper = jnp.asarray(np.triu(np.ones((tb, tb), np.float32)), BF16)
    lstrict = jnp.asarray(np.tril(np.ones((nblk, nblk), np.float32), -1))
    full = lambda a: pl.BlockSpec(a.shape, lambda s: (0,) * a.ndim)
    idx_spec = pl.BlockSpec((1, 1, pc), lambda s: (jnp.maximum(s - N_EXPERTS, 0), 0, 0))
    slot_spec = pl.BlockSpec((1, nblk, tb), lambda s: (jnp.maximum(s - N_EXPERTS, 0) // nch, 0, 0))
    tok_spec = pl.BlockSpec((nblk, tb), lambda s: (0, 0))
    idx, slots, cs, ce = pl.pallas_call(
        functools.partial(_select_kernel, cap=cap, tb=tb),
        grid=(N_EXPERTS + N_EXPERTS * nch,),
        in_specs=[full(aff3), full(upper), full(lstrict)],
        out_specs=[idx_spec, slot_spec, tok_spec, tok_spec],
        out_shape=[jax.ShapeDtypeStruct((N_EXPERTS * nch, 1, pc), jnp.int32),
                   jax.ShapeDtypeStruct((N_EXPERTS, nblk, tb), jnp.int32),
                   jax.ShapeDtypeStruct((nblk, tb), jnp.int32), jax.ShapeDtypeStruct((nblk, tb), jnp.int32)],
        scratch_shapes=[pltpu.VMEM((N_EXPERTS, 8, LANES), jnp.int32), pltpu.VMEM((N_EXPERTS, nblk, tb), BF16),
                        pltpu.VMEM((nblk, tb), F32), pltpu.VMEM((nblk, tb), F32), pltpu.VMEM((nblk, tb), F32),
                        pltpu.VMEM((tb, 2 * nblk), BF16), pltpu.VMEM((nblk, LANES), F32)],
        compiler_params=_cparams(1),
        name="select",
    )(aff3, upper, lstrict)
    return idx.reshape(-1), slots.reshape(N_EXPERTS, n), cs.reshape(-1), ce.reshape(-1)


def _slot_rows(afft, slots, idx):
    n = afft.shape[1]
    table = jnp.concatenate([lax.bitcast_convert_type(afft, jnp.int32), slots,
                             jnp.zeros((LANES - 2 * N_EXPERTS, n), jnp.int32)], axis=0).T
    return _sc_gather(table, idx)


def _sc_mesh():
    return plsc.VectorSubcoreMesh(core_axis_name="c", subcore_axis_name="s")


def _sc_scatter(rows, idx, m_out):
    m, d = rows.shape
    assert m % SC_WINDOW == 0

    @functools.partial(pl.kernel, out_type=jax.ShapeDtypeStruct((m_out, d), rows.dtype), mesh=_sc_mesh(),
                       name="sc_scatter")
    def scatter(x_hbm, i_hbm, o_hbm):
        def body(x_vmem, i_vmem):
            pltpu.sync_copy(x_vmem, o_hbm.at[i_vmem.at[0]])

        pltpu.emit_pipeline(
            body,
            grid=(m // SC_WINDOW,),
            in_specs=[pl.BlockSpec((SC_WINDOW, d), lambda i: (i, 0)),
                      pl.BlockSpec((1, SC_WINDOW), lambda i: (0, i))],
            out_specs=[],
            core_axis_name=("c", "s"),
            dimension_semantics=(pltpu.PARALLEL,),
        )(x_hbm, i_hbm)

    return scatter(rows, idx.reshape(1, m))


def _sc_gather(table, idx):
    m = idx.shape[0]
    d = table.shape[1]
    assert m % SC_WINDOW == 0

    @functools.partial(pl.kernel, out_type=jax.ShapeDtypeStruct((m, d), table.dtype), mesh=_sc_mesh(),
                       name="sc_gather")
    def gather(x_hbm, i_hbm, o_hbm):
        def body(i_vmem, o_vmem):
            pltpu.sync_copy(x_hbm.at[i_vmem.at[0]], o_vmem)

        pltpu.emit_pipeline(
            body,
            grid=(m // SC_WINDOW,),
            in_specs=[pl.BlockSpec((1, SC_WINDOW), lambda i: (0, i))],
            out_specs=[pl.BlockSpec((SC_WINDOW, d), lambda i: (i, 0))],
            core_axis_name=("c", "s"),
            dimension_semantics=(pltpu.PARALLEL,),
        )(i_hbm, o_hbm)

    return gather(table, idx.reshape(1, m))


def _row_to_col(row):
    n = row.shape[1]
    eye = lax.broadcasted_iota(jnp.int32, (n, n), 0) == lax.broadcasted_iota(jnp.int32, (n, n), 1)
    return jnp.sum(jnp.where(eye, row, jnp.zeros_like(row)), axis=1, keepdims=True)


def _ffn_kernel(slot_ref, x0_ref, x1_ref, w1_hbm, w3_hbm, w2_hbm, o0_ref, o1_ref, dst_ref, wstage, w1b, w3b, w2b, wsem,
                *, layer):
    e = pl.program_id(0)
    i = pl.program_id(1)

    def weight_copies(expert):
        return [pltpu.make_async_copy(w_hbm.at[layer, expert], wstage.at[k], wsem.at[k])
                for k, w_hbm in enumerate((w1_hbm, w3_hbm, w2_hbm))]

    @pl.when(i == 0)
    def _():
        @pl.when(e == 0)
        def _():
            for cp in weight_copies(0):
                cp.start()
        for cp in weight_copies(e):
            cp.wait()
        w1b[...] = wstage[0].astype(BF16)
        w3b[...] = wstage[1].astype(BF16)
        w2b[...] = wstage[2].astype(BF16)

    @pl.when(jnp.logical_and(i == 1, e + 1 < N_EXPERTS))
    def _():
        for cp in weight_copies(e + 1):
            cp.start()

    info = slot_ref[...]
    lane = lax.broadcasted_iota(jnp.int32, info.shape, 1)
    gate = jnp.sum(jnp.where(lane == e, pltpu.bitcast(info, F32), 0.0), axis=1, keepdims=True)
    info_t = info.T
    word = lax.broadcasted_iota(jnp.int32, info_t.shape, 0)
    dst_ref[0] = jnp.sum(jnp.where(word == N_EXPERTS + e, info_t, 0), axis=0, keepdims=True)

    xs = _unpack_rows([x0_ref[...], x1_ref[...]])
    hg = jnp.dot(xs, w1b[...], preferred_element_type=F32)
    hu = jnp.dot(xs, w3b[...], preferred_element_type=F32)
    hid = (hg * _sigmoid(hg) * hu).astype(BF16)
    out = jnp.dot(hid, w2b[...], preferred_element_type=F32) * gate
    for ref, words in zip((o0_ref, o1_ref), _pack_rows(out)):
        ref[...] = words


def _expert_ffn(xs, slot_rows, w1, w3, w2, layer):
    m = xs[0].shape[0]
    rows = FFN_ROWS
    nt = m // (N_EXPERTS * rows)
    assert SC_PIECES == 2 and nt >= 2
    piece = pl.BlockSpec((rows, SC_ROW), lambda e, i: (e * nt + i, 0))
    any_spec = pl.BlockSpec(memory_space=pl.ANY)
    *outs, dst = pl.pallas_call(
        functools.partial(_ffn_kernel, layer=layer),
        grid=(N_EXPERTS, nt),
        in_specs=[pl.BlockSpec((rows, LANES), lambda e, i: (e * nt + i, 0))] + [piece] * SC_PIECES + [any_spec] * 3,
        out_specs=[piece] * SC_PIECES + [pl.BlockSpec((1, 1, rows), lambda e, i: (e * nt + i, 0, 0))],
        out_shape=[jax.ShapeDtypeStruct((m, SC_ROW), jnp.int32)] * SC_PIECES
        + [jax.ShapeDtypeStruct((m // rows, 1, rows), jnp.int32)],
        scratch_shapes=[pltpu.VMEM((3, D_MODEL, EXPERT_FF), F32), pltpu.VMEM((D_MODEL, EXPERT_FF), BF16),
                        pltpu.VMEM((D_MODEL, EXPERT_FF), BF16), pltpu.VMEM((EXPERT_FF, D_MODEL), BF16),
                        pltpu.SemaphoreType.DMA((3,))],
        compiler_params=_cparams(2),
        name="expert_ffn",
    )(slot_rows, *xs, w1, w3, w2)
    return outs, dst.reshape(-1)


def _combine_kernel(tsub_ref, x_ref, cs_ref, ce_ref, r0_hbm, r1_hbm, o_ref, rbuf, obuf, rows16, sems, osem,
                    *, ntile, total):
    _combine_tile(pl.program_id(0), tsub_ref, x_ref, cs_ref, ce_ref, r0_hbm, r1_hbm, o_ref, rbuf, obuf, rows16, sems,
                  osem, ntile=ntile, total=total)


def _combine_then_in_proj_kernel(tsub_ref, x_ref, cs_ref, ce_ref, r0_hbm, r1_hbm, g_ref, w_ref, qg_ref, kg_ref,
                                 bdq_ref, bdk_ref, y_ref, qa_ref, ka_ref, va_ref, qr_ref, kr_ref, vr_ref, gr_ref,
                                 ga_ref, gt_ref, rbuf, obuf, rows16, sems, osem, *, ntile, total):
    _combine_tile(pl.program_id(0), tsub_ref, x_ref, cs_ref, ce_ref, r0_hbm, r1_hbm, y_ref, rbuf, obuf, rows16, sems,
                  osem, ntile=ntile, total=total)
    _in_proj_kernel(y_ref, g_ref, w_ref, qg_ref, kg_ref, bdq_ref, bdk_ref,
                    qa_ref, ka_ref, va_ref, qr_ref, kr_ref, vr_ref, gr_ref, ga_ref, gt_ref)


def _combine_tile(i, tsub_ref, x_ref, cs_ref, ce_ref, r0_hbm, r1_hbm, o_ref, rbuf, obuf, rows16, sems, osem,
                  *, ntile, total):
    pieces_hbm = (r0_hbm, r1_hbm)
    win = COMBINE_WINDOW
    sub = COMBINE_SUB
    subwin = COMBINE_SUBWIN
    per = COMBINE_TOKENS // sub
    slot = lax.rem(i, COMBINE_BUFFERS)

    def window_start(t):
        return pl.multiple_of((tsub_ref[t * per] // BF16_TILE_ROWS) * BF16_TILE_ROWS, BF16_TILE_ROWS)

    def copies(t, b):
        s = window_start(t)
        return [pltpu.make_async_copy(pieces_hbm[c].at[pl.ds(s, win)], rbuf.at[b, c], sems.at[b, c])
                for c in range(SC_PIECES)]

    ahead = COMBINE_BUFFERS - 1

    @pl.when(i == 0)
    def _():
        for t in range(min(ahead, ntile)):
            for cp in copies(t, t):
                cp.start()

    @pl.when(i + ahead < ntile)
    def _():
        for cp in copies(i + ahead, lax.rem(i + ahead, COMBINE_BUFFERS)):
            cp.start()

    for cp in copies(i, slot):
        cp.wait()

    first = [_row_to_col(cs_ref[0][:, g * sub:(g + 1) * sub]) for g in range(per)]
    last = [_row_to_col(ce_ref[0][:, g * sub:(g + 1) * sub]) for g in range(per)]

    def zero_unwritten(words, base):
        written = (base + lax.broadcasted_iota(jnp.int32, (win, 1), 0)) < total
        return jnp.where(written, words, 0)

    def owner_matrix(g, base, width):
        r = base + lax.broadcasted_iota(jnp.int32, (1, width), 1)
        return jnp.logical_and(first[g] <= r, r < last[g]).astype(BF16)

    s0 = window_start(i)
    tail = s0 + win > total

    @pl.when(tail)
    def _():
        rows16[...] = _unpack_rows([zero_unwritten(rbuf[slot, c], s0) for c in range(SC_PIECES)])

    @pl.when(jnp.logical_not(tail))
    def _():
        rows16[...] = _unpack_rows([rbuf[slot, c] for c in range(SC_PIECES)])

    offsets = []
    fits = None
    for g in range(per):
        off = (tsub_ref[i * per + g] // BF16_TILE_ROWS) * BF16_TILE_ROWS - s0
        ok = jnp.logical_and(tsub_ref[i * per + g + 1] - s0 <= off + subwin, off + subwin <= win)
        fits = ok if fits is None else jnp.logical_and(fits, ok)
        offsets.append(off)

    @pl.when(fits)
    def _():
        for g in range(per):
            tokens = slice(g * sub, (g + 1) * sub)
            off = pl.multiple_of(offsets[g], BF16_TILE_ROWS)
            q = owner_matrix(g, s0 + off, subwin)
            o_ref[tokens, :] = x_ref[tokens, :] + jnp.dot(q, rows16[pl.ds(off, subwin), :],
                                                         preferred_element_type=F32)

    @pl.when(jnp.logical_not(fits))
    def _():
        def everyone(base):
            return jnp.concatenate([owner_matrix(g, base, win) for g in range(per)], axis=0)

        y = x_ref[...] + jnp.dot(everyone(s0), rows16[...], preferred_element_type=F32)
        n_extra = jnp.maximum(tsub_ref[(i + 1) * per] - (s0 + win) + win - 1, 0) // win

        def extra(k, acc):
            base = pl.multiple_of(s0 + (k + 1) * win, BF16_TILE_ROWS)
            cps = [pltpu.make_async_copy(pieces_hbm[c].at[pl.ds(base, win)], obuf.at[c], osem.at[c])
                   for c in range(SC_PIECES)]
            for cp in cps:
                cp.start()
            for cp in cps:
                cp.wait()
            rows = _unpack_rows([zero_unwritten(obuf[c], base) for c in range(SC_PIECES)])
            return acc + jnp.dot(everyone(base), rows, preferred_element_type=F32)

        o_ref[...] = lax.fori_loop(0, n_extra, extra, y)


def _combine(pending, in_proj_params=None):
    xmid, cs, ce, pieces, total = pending
    n = xmid.shape[0]
    tt = COMBINE_TOKENS
    win = COMBINE_WINDOW
    ntile = n // tt
    tsub = jnp.concatenate([cs[::COMBINE_SUB], jnp.full((1,), total, jnp.int32)])
    cs3 = cs.reshape(ntile, 1, tt)
    ce3 = ce.reshape(ntile, 1, tt)
    any_spec = pl.BlockSpec(memory_space=pl.ANY)
    tok = pl.BlockSpec((1, 1, tt), lambda i, ts: (i, 0, 0))
    row = lambda wd: pl.BlockSpec((tt, wd), lambda i, ts: (i, 0))
    in_specs = [row(D_MODEL), tok, tok] + [any_spec] * SC_PIECES
    out_specs = [row(D_MODEL)]
    out_shape = [jax.ShapeDtypeStruct((n, D_MODEL), F32)]
    operands = [tsub, xmid, cs3, ce3, *pieces]
    body, name = _combine_kernel, "combine"
    if in_proj_params is not None:
        full = lambda a: pl.BlockSpec(a.shape, lambda i, ts: (0,) * a.ndim, pipeline_mode=pl.Buffered(1))
        in_specs += [full(a) for a in in_proj_params]
        out_specs += [row(wd) for wd in IN_SPLITS]
        out_shape += [jax.ShapeDtypeStruct((n, wd), BF16) for wd in IN_SPLITS]
        operands += list(in_proj_params)
        body, name = _combine_then_in_proj_kernel, "combine_in_proj"
    grid_spec = pltpu.PrefetchScalarGridSpec(
        num_scalar_prefetch=1,
        grid=(ntile,),
        in_specs=in_specs,
        out_specs=out_specs,
        scratch_shapes=[pltpu.VMEM((COMBINE_BUFFERS, SC_PIECES, win, SC_ROW), jnp.int32),
                        pltpu.VMEM((SC_PIECES, win, SC_ROW), jnp.int32), pltpu.VMEM((win, D_MODEL), BF16),
                        pltpu.SemaphoreType.DMA((COMBINE_BUFFERS, SC_PIECES)), pltpu.SemaphoreType.DMA((SC_PIECES,))],
    )
    y, *proj = pl.pallas_call(
        functools.partial(body, ntile=ntile, total=total),
        grid_spec=grid_spec,
        out_shape=out_shape,
        compiler_params=_cparams(1),
        name=name,
    )(*operands)
    return (y, proj) if in_proj_params is not None else y


def _t5_bucket(rel):
    half = REL_BUCKETS // 2
    max_exact = half // 2
    base = np.where(rel > 0, half, 0)
    n = np.abs(rel)
    large = max_exact + (np.log(np.maximum(n, 1) / max_exact) / math.log(REL_MAX_DIST / max_exact)
                         * (half - max_exact)).astype(np.int32)
    large = np.minimum(large, half - 1)
    return (base + np.where(n < max_exact, n, large)).astype(np.int32)


def _head_perm():
    nq = ATTN_HEADS // 2
    cols = []
    for j in range(nq):
        for half in range(2):
            h = j + nq * half
            cols.extend(range(h * ATTN_HEAD_DIM, (h + 1) * ATTN_HEAD_DIM))
    return np.asarray(cols, np.int32)


def _attn_bias_tables(rel_bias):
    q_pos = np.arange(BLOCK)[:, None]
    k_off = np.arange(3 * BLOCK)[None, :] - BLOCK
    rel = k_off - q_pos
    in_window = np.abs(rel) <= WINDOW
    onehot = jnp.asarray(_t5_bucket(rel)[:, :, None] == np.arange(REL_BUCKETS)[None, None, :], F32)
    bias = jnp.einsum("qkb,bh->hqk", onehot, rel_bias.astype(F32), precision=lax.Precision.HIGHEST)
    col = np.arange(3 * BLOCK)[None, :]
    tables = []
    for valid in (col >= BLOCK, np.ones_like(col, bool), col < 2 * BLOCK):
        t = jnp.where(jnp.asarray(in_window & valid)[None], bias, NEG)
        nq = ATTN_HEADS // 2
        rows = [jnp.concatenate([t[j], t[j + nq]], axis=1) for j in range(nq)]
        tables.append(jnp.concatenate(rows, axis=0))
    return jnp.stack(tables)


def _retention_tables(decay_logit, norm_g):
    cr = RET_CHUNK
    lg = jax.nn.log_sigmoid(decay_logit.astype(F32))
    lgf, lgb = lg[0][:, None, None], lg[1][:, None, None]
    pos = np.arange(cr, dtype=np.float32)
    dist = pos[:, None] - pos[None, :]
    scale = RET_DIM ** -0.5
    dmask = jnp.where(jnp.asarray(dist >= 0)[None],
                      jnp.exp(lgf * np.maximum(dist, 0.0)[None]),
                      jnp.exp(lgb * np.maximum(-dist, 0.0)[None])) * scale
    col = lambda v: jnp.broadcast_to(v[:, :, None], (RET_HEADS, cr, RET_DIM))
    rowf = col(jnp.exp(lg[0][:, None] * pos[None]))
    rowb = col(jnp.exp(lg[1][:, None] * (cr - 1.0 - pos)[None]))
    wkf = col(jnp.exp(lg[0][:, None] * (cr - pos)[None]) * scale)
    wkb = col(jnp.exp(lg[1][:, None] * (pos + 1.0)[None]) * scale)
    dec = jnp.concatenate([jnp.exp(lg[0] * cr), jnp.exp(lg[1] * cr)])
    return dec, dmask, rowf, rowb, wkf, wkb, norm_g.astype(F32)


def _layer(x2, pending, b, s, p):
    in_proj_params = (p["g1"], p["w_in"], p["qg"], p["kg"], p["bdq"], p["bdk"])
    if pending is None:
        qa, ka, va, qr, kr, vr, gr, ga, gt = _in_proj(x2, *in_proj_params)
    else:
        x2, (qa, ka, va, qr, kr, vr, gr, ga, gt) = _combine(pending, in_proj_params)
    attn = _attention(qa, ka, va, p["bias3"], p["sink"], b, s)
    retn = _retention(qr, kr, vr, gr, p["retn"], b, s)
    xmid, afft, *h2 = _merge(attn, retn, ga, gt, x2, p["wba"], p["wbr"], p["wo"], p["g2"], p["wr"])
    n = b * s
    cap = max(1, EC_CAPACITY_FACTOR * n // N_EXPERTS)
    total = N_EXPERTS * cap
    idx, slots, cs, ce = _select(afft, cap)
    slot_rows = _slot_rows(afft, slots, idx)
    xs = [_sc_gather(piece, idx) for piece in h2]
    outs, dst = _expert_ffn(xs, slot_rows, p["w1"], p["w3"], p["w2"], p["layer"])
    by_token = [_sc_scatter(o, dst, total + COMBINE_WINDOW) for o in outs]
    return xmid, cs, ce, by_token, total


def kernel(x_prompt, x_sample, norm_mix_g, w_in, q_norm_g, k_norm_g, attn_sink, rel_bias, retn_decay_logit, retn_norm_g, w_branch_attn, w_branch_retn, w_out, norm_ffn_g, w_router, w_exp_gate, w_exp_up, w_exp_down):
    depth = w_in.shape[0]
    perm = _head_perm()
    bias3 = _attn_bias_tables(rel_bias)
    bdq = jnp.asarray(np.kron(np.eye(ATTN_HEADS), np.ones((ATTN_HEAD_DIM, ATTN_HEAD_DIM))), BF16)
    bdk = jnp.asarray(np.kron(np.eye(ATTN_KV_HEADS), np.ones((ATTN_HEAD_DIM, ATTN_HEAD_DIM))), BF16)
    layers = []
    for l in range(depth):
        w = w_in[l]
        w = jnp.concatenate([w[:, :ATTN_WIDTH][:, perm], w[:, ATTN_WIDTH:]], axis=1).astype(BF16)
        wr = jnp.pad(w_router[l], ((0, 0), (0, LANES - N_EXPERTS))).astype(BF16)
        layers.append(dict(
            g1=norm_mix_g[l].astype(F32)[None], w_in=w,
            qg=(jnp.tile(q_norm_g[l].astype(F32), ATTN_HEADS) * (ATTN_HEAD_DIM ** -0.5))[None],
            kg=jnp.tile(k_norm_g[l].astype(F32), ATTN_KV_HEADS)[None],
            bdq=bdq, bdk=bdk, bias3=bias3, sink=attn_sink[l].astype(F32),
            retn=_retention_tables(retn_decay_logit[l], retn_norm_g[l]),
            wba=w_branch_attn[l][perm, :].astype(BF16), wbr=w_branch_retn[l].astype(BF16),
            wo=w_out[l].astype(BF16), g2=norm_ffn_g[l].astype(F32)[None], wr=wr,
            w1=w_exp_gate, w3=w_exp_up, w2=w_exp_down, layer=l))

    def trunk(x):
        b, s, d = x.shape
        x2, pending = x.reshape(b * s, d), None
        for p in layers:
            pending = _layer(x2, pending, b, s, p)
        return _combine(pending).reshape(b, s, d)

    return (trunk(x_prompt), trunk(x_sample))
```

```python
import functools
import math

import numpy as np
import jax
import jax.numpy as jnp
from jax import lax
from jax.experimental import pallas as pl
from jax.experimental.pallas import tpu as pltpu
from jax.experimental.pallas import tpu_sc as plsc

D_MODEL = 1024
ATTN_HEADS = 8
ATTN_KV_HEADS = 2
ATTN_HEAD_DIM = 64
WINDOW = 128
BLOCK = 128
REL_BUCKETS = 32
REL_MAX_DIST = 128
RET_HEADS = 4
RET_DIM = 128
N_EXPERTS = 16
EC_CAPACITY_FACTOR = 2
EXPERT_FF = 1024
EPS = 1e-6

ATTN_WIDTH = ATTN_HEADS * ATTN_HEAD_DIM
KV_WIDTH = ATTN_KV_HEADS * ATTN_HEAD_DIM
RET_WIDTH = RET_HEADS * RET_DIM
IN_SPLITS = (ATTN_WIDTH, KV_WIDTH, KV_WIDTH, RET_WIDTH, RET_WIDTH, RET_WIDTH, RET_WIDTH, D_MODEL, D_MODEL)
IN_OFFSETS = tuple(int(o) for o in np.cumsum((0,) + IN_SPLITS))

LANES = 128
BF16_TILE_ROWS = 16
VMEM_LIMIT_BYTES = 56 * 1024 * 1024

TOKEN_TILE = 1024
MERGE_ROWS = 256
MERGE_INPUT_BUFFERS = 3
IN_PROJ_TILE = 1024
ATTN_QUERIES = 2048
RET_CHUNK = 256
RET_STEP = 2048
FFN_ROWS = 1024
SELECT_BLOCKS = 128
SELECT_SLOTS = 2048
SC_WINDOW = 128
SC_ROW = 256
PACKED_WIDTH = D_MODEL // 2
SC_PIECES = PACKED_WIDTH // SC_ROW
COMBINE_TOKENS = 512
COMBINE_WINDOW = 1280
COMBINE_SUB = 128
COMBINE_SUBWIN = 384
COMBINE_BUFFERS = 3

F32 = jnp.float32
BF16 = jnp.bfloat16
NEG = -1e30


def _cparams(n_axes, vmem=VMEM_LIMIT_BYTES):
    return pltpu.CompilerParams(dimension_semantics=("arbitrary",) * n_axes, vmem_limit_bytes=vmem)


def _sigmoid(x):
    return 0.5 * jnp.tanh(0.5 * x) + 0.5


HIGH_HALF = -65536


def _pack_rows(x):
    bits = pltpu.bitcast(x.astype(BF16).astype(F32), jnp.int32)
    words = lax.shift_right_logical(bits[:, :PACKED_WIDTH], 16) | (bits[:, PACKED_WIDTH:] & HIGH_HALF)
    return [words[:, c * SC_ROW:(c + 1) * SC_ROW] for c in range(SC_PIECES)]


def _unpack_rows(pieces):
    low = [pltpu.bitcast(lax.shift_left(w, 16), F32) for w in pieces]
    high = [pltpu.bitcast(w & HIGH_HALF, F32) for w in pieces]
    return jnp.concatenate(low + high, axis=1).astype(BF16)


def _in_proj_kernel(x_ref, g_ref, w_ref, qg_ref, kg_ref, bdq_ref, bdk_ref,
                    qa_ref, ka_ref, va_ref, qr_ref, kr_ref, vr_ref, gr_ref, ga_ref, gt_ref):
    x = x_ref[...]
    ms = jnp.mean(x * x, axis=-1, keepdims=True)
    h = (x * lax.rsqrt(ms + EPS) * g_ref[...]).astype(BF16)

    def mm(k):
        return jnp.dot(h, w_ref[:, IN_OFFSETS[k]:IN_OFFSETS[k + 1]], preferred_element_type=F32)

    def head_norm(t, bd_ref, gain_ref):
        ss = jnp.dot((t * t).astype(BF16), bd_ref[...], preferred_element_type=F32)
        return t * lax.rsqrt(ss * (1.0 / ATTN_HEAD_DIM) + EPS) * gain_ref[...]

    q_raw = mm(0)
    k_raw = mm(1)
    for k, ref in ((2, va_ref), (3, qr_ref), (4, kr_ref), (5, vr_ref)):
        ref[...] = mm(k).astype(BF16)
    g = mm(6)
    gr_ref[...] = (g * _sigmoid(g)).astype(BF16)
    ga_ref[...] = _sigmoid(mm(7)).astype(BF16)
    gt_ref[...] = _sigmoid(mm(8)).astype(BF16)
    qa_ref[...] = head_norm(q_raw, bdq_ref, qg_ref).astype(BF16)
    ka_ref[...] = head_norm(k_raw, bdk_ref, kg_ref).astype(BF16)


def _in_proj(x2, g, w, qg, kg, bdq, bdk):
    n = x2.shape[0]
    tm = IN_PROJ_TILE
    full = lambda a: pl.BlockSpec(a.shape, lambda i: (0,) * a.ndim, pipeline_mode=pl.Buffered(1))
    widths = IN_SPLITS
    return pl.pallas_call(
        _in_proj_kernel,
        grid=(n // tm,),
        in_specs=[pl.BlockSpec((tm, D_MODEL), lambda i: (i, 0)), full(g), full(w), full(qg), full(kg),
                  full(bdq), full(bdk)],
        out_specs=[pl.BlockSpec((tm, wd), lambda i: (i, 0)) for wd in widths],
        out_shape=[jax.ShapeDtypeStruct((n, wd), BF16) for wd in widths],
        compiler_params=_cparams(1),
        name="in_proj",
    )(x2, g, w, qg, kg, bdq, bdk)


def _attn_kernel(sink_ref, q_ref, kp_ref, kc_ref, kn_ref, vp_ref, vc_ref, vn_ref, bias_ref, o_ref, *, nsteps):
    nq = ATTN_HEADS // 2
    ni = pl.program_id(1)
    k = jnp.concatenate([kp_ref[...], kc_ref[...], kn_ref[...]], axis=0)
    v = jnp.concatenate([vp_ref[...], vc_ref[...], vn_ref[...]], axis=0)
    low = lax.broadcasted_iota(jnp.int32, k.shape, 1) < ATTN_HEAD_DIM
    zero = jnp.zeros_like(k)
    k_lo, k_hi = jnp.where(low, k, zero), jnp.where(low, zero, k)
    v_lo, v_hi = jnp.where(low, v, zero), jnp.where(low, zero, v)
    nk = 3 * BLOCK
    low_o = lax.broadcasted_iota(jnp.int32, (BLOCK, LANES), 1) < ATTN_HEAD_DIM
    key_low = lax.broadcasted_iota(jnp.int32, (2 * nk, LANES), 0) < nk
    lane_low = lax.broadcasted_iota(jnp.int32, (2 * nk, LANES), 1) < ATTN_HEAD_DIM
    ones_bd = (key_low == lane_low).astype(BF16)
    nsub = ATTN_QUERIES // BLOCK
    for sb in range(nsub):
        rows = slice(sb * BLOCK, (sb + 1) * BLOCK)
        keys = slice(sb * BLOCK, sb * BLOCK + nk)
        q = q_ref[rows, :]
        qs = jnp.concatenate([q[:, j * LANES:(j + 1) * LANES] for j in range(nq)], axis=0)
        kbd = jnp.concatenate([k_lo[keys], k_hi[keys]], axis=0)
        vbd = jnp.concatenate([v_lo[keys], v_hi[keys]], axis=0)
        s = lax.dot_general(qs, kbd, (((1,), (1,)), ((), ())), preferred_element_type=F32)
        if sb == 0:
            table = jnp.where(ni == 0, 0, 1)
        elif sb == nsub - 1:
            table = jnp.where(ni == nsteps - 1, 2, 1)
        else:
            table = 1
        s = s + bias_ref[table]
        probs, sink_terms = [], []
        for j in range(nq):
            row_p, row_sink = [], []
            for half in range(2):
                sj = s[j * BLOCK:(j + 1) * BLOCK, half * nk:(half + 1) * nk]
                sk = sink_ref[j + nq * half]
                m = jnp.maximum(jnp.max(sj, axis=-1, keepdims=True), sk)
                row_p.append(jnp.exp(sj - m).astype(BF16))
                row_sink.append(jnp.exp(sk - m))
            probs.append(jnp.concatenate(row_p, axis=1))
            sink_terms.append(jnp.where(low_o, row_sink[0], row_sink[1]))
        pm = jnp.concatenate(probs, axis=0)
        od = jnp.dot(pm, jnp.concatenate([vbd, ones_bd], axis=1), preferred_element_type=F32)
        o = od[:, :LANES] / (od[:, LANES:] + jnp.concatenate(sink_terms, axis=0))
        for j in range(nq):
            o_ref[rows, j * LANES:(j + 1) * LANES] = o[j * BLOCK:(j + 1) * BLOCK].astype(BF16)


def _attention(qa, ka, va, bias3, sink, b, s):
    tq = ATTN_QUERIES
    per = tq // BLOCK
    nb = s // BLOCK
    nsteps = s // tq
    assert s % tq == 0 and nb >= 2
    n = b * s
    main = lambda wd: pl.BlockSpec((tq, wd), lambda bi, ni: (bi * nsteps + ni, 0))
    prev = pl.BlockSpec((BLOCK, KV_WIDTH), lambda bi, ni: (bi * nb + jnp.maximum(ni * per - 1, 0), 0))
    nxt = pl.BlockSpec((BLOCK, KV_WIDTH), lambda bi, ni: (bi * nb + jnp.minimum(ni * per + per, nb - 1), 0))
    return pl.pallas_call(
        functools.partial(_attn_kernel, nsteps=nsteps),
        grid=(b, nsteps),
        in_specs=[pl.BlockSpec(memory_space=pltpu.SMEM), main(ATTN_WIDTH),
                  prev, main(KV_WIDTH), nxt, prev, main(KV_WIDTH), nxt,
                  pl.BlockSpec(bias3.shape, lambda bi, ni: (0, 0, 0))],
        out_specs=main(ATTN_WIDTH),
        out_shape=jax.ShapeDtypeStruct((n, ATTN_WIDTH), BF16),
        compiler_params=_cparams(2),
        name="attn",
    )(sink, qa, ka, ka, ka, va, va, va, bias3)


def _retn_kernel(dec_ref, q_ref, k_ref, v_ref, g_ref, dmask_ref, rowf_ref, rowb_ref, wkf_ref, wkb_ref, ng_ref,
                 o_ref, tstore, uf, tb, *, nsteps):
    p = pl.program_id(1)
    n = pl.program_id(2)
    cr = RET_CHUNK
    per = RET_STEP // cr
    tn = (((0,), (0,)), ((), ()))
    nt = (((1,), (1,)), ((), ()))
    hs = lambda h: slice(h * RET_DIM, (h + 1) * RET_DIM)

    @pl.when(p == 0)
    def _():
        @pl.when(n == 0)
        def _():
            tb[...] = jnp.zeros_like(tb)
        first_chunk = (nsteps - 1 - n) * per
        for sub in reversed(range(per)):
            rows = slice(sub * cr, (sub + 1) * cr)
            for h in range(RET_HEADS):
                tstore[first_chunk + sub, h] = tb[h].astype(BF16)
                kw = (k_ref[rows, hs(h)].astype(F32) * wkb_ref[h]).astype(BF16)
                upd = lax.dot_general(kw, v_ref[rows, hs(h)], tn, preferred_element_type=F32)
                tb[h] = dec_ref[RET_HEADS + h] * tb[h] + upd

    @pl.when(p == 1)
    def _():
        @pl.when(n == 0)
        def _():
            uf[...] = jnp.zeros_like(uf)
        heads = range(RET_HEADS)
        for sub in range(per):
            rows = slice(sub * cr, (sub + 1) * cr)
            qs = [q_ref[rows, hs(h)] for h in heads]
            ks = [k_ref[rows, hs(h)] for h in heads]
            vs = [v_ref[rows, hs(h)] for h in heads]
            scores = [lax.dot_general(qs[h], ks[h], nt, preferred_element_type=F32) for h in heads]
            states = [jnp.concatenate([uf[h].astype(BF16), tstore[n * per + sub, h]], axis=1) for h in heads]
            cross = [jnp.dot(qs[h], states[h], preferred_element_type=F32) for h in heads]
            update = [lax.dot_general((ks[h].astype(F32) * wkf_ref[h]).astype(BF16), vs[h], tn,
                                      preferred_element_type=F32) for h in heads]
            intra = [jnp.dot((scores[h] * dmask_ref[h]).astype(BF16), vs[h], preferred_element_type=F32)
                     for h in heads]
            for h in heads:
                o = intra[h] + cross[h][:, :RET_DIM] * rowf_ref[h] + cross[h][:, RET_DIM:] * rowb_ref[h]
                mu = jnp.mean(o, axis=-1, keepdims=True)
                d = o - mu
                var = jnp.mean(d * d, axis=-1, keepdims=True)
                on = d * lax.rsqrt(var + EPS) * ng_ref[h:h + 1, :]
                gate = g_ref[rows, hs(h)].astype(F32)
                o_ref[rows, hs(h)] = (gate * on).astype(BF16)
                uf[h] = dec_ref[h] * uf[h] + update[h]


def _retention(qr, kr, vr, gr, tables, b, s):
    dec, dmask, rowf, rowb, wkf, wkb, ng = tables
    rs = RET_STEP
    nsteps = s // rs
    nc = s // RET_CHUNK
    assert s % rs == 0
    n = b * s
    full = lambda a: pl.BlockSpec(a.shape, lambda bi, pi, ni: (0,) * a.ndim)
    fwd_spec = pl.BlockSpec((rs, RET_WIDTH), lambda bi, pi, ni: (bi * nsteps + ni * pi, 0))
    kv_spec = pl.BlockSpec((rs, RET_WIDTH),
                           lambda bi, pi, ni: (bi * nsteps + ni * pi + (1 - pi) * (nsteps - 1 - ni), 0))
    return pl.pallas_call(
        functools.partial(_retn_kernel, nsteps=nsteps),
        grid=(b, 2, nsteps),
        in_specs=[pl.BlockSpec(memory_space=pltpu.SMEM), fwd_spec, kv_spec, kv_spec, fwd_spec,
                  full(dmask), full(rowf), full(rowb), full(wkf), full(wkb), full(ng)],
        out_specs=fwd_spec,
        out_shape=jax.ShapeDtypeStruct((n, RET_WIDTH), BF16),
        scratch_shapes=[pltpu.VMEM((nc, RET_HEADS, RET_DIM, RET_DIM), BF16),
                        pltpu.VMEM((RET_HEADS, RET_DIM, RET_DIM), F32),
                        pltpu.VMEM((RET_HEADS, RET_DIM, RET_DIM), F32)],
        compiler_params=_cparams(3),
        name="retention",
    )(dec, qr, kr, vr, gr, dmask, rowf, rowb, wkf, wkb, ng)


def _merge_kernel(attn_ref, retn_ref, ga_ref, gr_ref, x_ref, wba_ref, wbr_ref, wo_ref, g2_ref, wr_ref,
                  xmid_ref, afft_ref, *h2_refs):
    tiles = [slice(t * MERGE_ROWS, (t + 1) * MERGE_ROWS) for t in range(TOKEN_TILE // MERGE_ROWS)]
    branch = [(jnp.dot(attn_ref[rows, :], wba_ref[...], preferred_element_type=F32),
               jnp.dot(retn_ref[rows, :], wbr_ref[...], preferred_element_type=F32)) for rows in tiles]
    resid = []
    for rows, (a, r) in zip(tiles, branch):
        merged = ga_ref[rows, :].astype(F32) * a + gr_ref[rows, :].astype(F32) * r
        xn = x_ref[rows, :] + jnp.dot(merged.astype(BF16), wo_ref[...], preferred_element_type=F32)
        xmid_ref[rows, :] = xn
        resid.append(xn)
    scores = []
    for rows, xn in zip(tiles, resid):
        ms = jnp.mean(xn * xn, axis=-1, keepdims=True)
        h2 = xn * lax.rsqrt(ms + EPS) * g2_ref[...]
        for ref, words in zip(h2_refs, _pack_rows(h2)):
            ref[rows, :] = words
        scores.append(jnp.dot(h2.astype(BF16), wr_ref[...], preferred_element_type=F32))
    for rows, logits in zip(tiles, scores):
        real = lax.broadcasted_iota(jnp.int32, logits.shape, 1) < N_EXPERTS
        logits = jnp.where(real, logits, -jnp.inf)
        m = jnp.max(logits, axis=-1, keepdims=True)
        ex = jnp.exp(logits - m)
        aff = ex / jnp.sum(ex, axis=-1, keepdims=True)
        afft_ref[:, rows] = aff.T[:N_EXPERTS, :]


def _merge(attn, retn, ga, gr, x2, wba, wbr, wo, g2, wr):
    n = x2.shape[0]
    tm = TOKEN_TILE
    row = lambda wd: pl.BlockSpec((tm, wd), lambda i: (i, 0))
    deep = lambda wd: pl.BlockSpec((tm, wd), lambda i: (i, 0), pipeline_mode=pl.Buffered(MERGE_INPUT_BUFFERS))
    tile_in = [deep(ATTN_WIDTH), deep(RET_WIDTH), deep(D_MODEL), deep(D_MODEL), deep(D_MODEL)]
    tile_out = [row(D_MODEL), pl.BlockSpec((N_EXPERTS, tm), lambda i: (0, i))] + [row(SC_ROW)] * SC_PIECES

    def merge_call(attn_hbm, retn_hbm, ga_hbm, gr_hbm, x_hbm, wba_ref, wbr_ref, wo_ref, g2_ref, wr_ref, *out_hbm):
        def tile(attn_ref, retn_ref, ga_ref, gr_ref, x_ref, *out_refs):
            _merge_kernel(attn_ref, retn_ref, ga_ref, gr_ref, x_ref, wba_ref, wbr_ref, wo_ref, g2_ref, wr_ref,
                          *out_refs)

        pltpu.emit_pipeline(tile, grid=(n // tm,), in_specs=tile_in, out_specs=tile_out)(
            attn_hbm, retn_hbm, ga_hbm, gr_hbm, x_hbm, *out_hbm)

    any_spec = pl.BlockSpec(memory_space=pl.ANY)
    vmem_spec = pl.BlockSpec(memory_space=pltpu.VMEM)
    return pl.pallas_call(
        merge_call,
        in_specs=[any_spec] * 5 + [vmem_spec] * 5,
        out_specs=[any_spec] * (2 + SC_PIECES),
        out_shape=[jax.ShapeDtypeStruct((n, D_MODEL), F32), jax.ShapeDtypeStruct((N_EXPERTS, n), F32)]
        + [jax.ShapeDtypeStruct((n, SC_ROW), jnp.int32)] * SC_PIECES,
        compiler_params=_cparams(0),
        name="merge",
    )(attn, retn, ga, gr, x2, wba, wbr, wo, g2, wr)


def _select_kernel(aff_ref, u_ref, ls_ref, idx_ref, slot_ref, cs_ref, ce_ref,
                   thr, selbuf, cnt, csr, rank, digits, offi, *, cap, tb):
    s = pl.program_id(0)
    nch = cap // SELECT_SLOTS
    ps = jnp.where(s < N_EXPERTS, 0, 1)
    later = jnp.maximum(s - N_EXPERTS, 0)
    e = jnp.where(s < N_EXPERTS, s, later // nch)
    j = jnp.where(s < N_EXPERTS, 0, later % nch)
    nblk = SELECT_BLOCKS
    pc = SELECT_SLOTS

    def cumsum(vals):
        inb = jnp.dot(vals.astype(BF16), u_ref[...], preferred_element_type=F32)
        tot = jnp.broadcast_to(inb[:, tb - 1:tb], (nblk, LANES))
        off = jnp.dot(ls_ref[...], tot, preferred_element_type=F32, precision=lax.Precision.HIGHEST)
        return inb, off[:, 0:1], tot[:, 0:1]

    @pl.when(jnp.logical_and(ps == 0, jnp.logical_and(e == 0, j == 0)))
    def _():
        def bit_step(t, curs):
            bit = jnp.left_shift(jnp.int32(1), 30 - t)
            out = []
            for x in range(N_EXPERTS):
                cand = curs[x] | bit
                n_ge = jnp.sum((pltpu.bitcast(aff_ref[x], jnp.int32) >= cand).astype(jnp.int32), keepdims=True)
                out.append(jnp.where(n_ge >= cap, cand, curs[x]))
            return tuple(out)

        found = lax.fori_loop(0, 31, bit_step, tuple(jnp.zeros((1, 1), jnp.int32) for _ in range(N_EXPERTS)))
        for x in range(N_EXPERTS):
            thr[x] = jnp.broadcast_to(found[x], thr.shape[1:])

    @pl.when(jnp.logical_and(ps == 0, j == 0))
    def _():
        bits = pltpu.bitcast(aff_ref[e], jnp.int32)
        limit = thr[e][0:1, 0:1]
        gt = bits > limit
        eq = bits == limit
        need = (cap - jnp.sum(gt.astype(jnp.int32), keepdims=True)).astype(F32)
        eqf = eq.astype(F32)
        eq_in, eq_off, _ = cumsum(eqf)
        eq_rank = eq_in + eq_off - eqf
        sel = jnp.logical_or(gt, jnp.logical_and(eq, eq_rank < need)).astype(F32)
        selbuf[e] = sel.astype(BF16)

        @pl.when(e == 0)
        def _():
            cnt[...] = sel

        @pl.when(e > 0)
        def _():
            cnt[...] = cnt[...] + sel

    @pl.when(jnp.logical_and(ps == 1, j == 0))
    def _():
        @pl.when(e == 0)
        def _():
            c = cnt[...]
            c_in, c_off, _ = cumsum(c)
            start = c_in + c_off - c
            csr[...] = start
            cs_ref[...] = start.astype(jnp.int32)
            ce_ref[...] = (start + c).astype(jnp.int32)
            rank[...] = jnp.zeros_like(rank)

        sel = selbuf[e].astype(F32)
        s_in, s_off, s_tot = cumsum(sel)
        count_t = (s_in + s_off).T
        high = jnp.floor(count_t * (1.0 / 256.0))
        digits[:, :nblk] = high.astype(BF16)
        digits[:, nblk:] = (count_t - 256.0 * high).astype(BF16)
        slot_ref[0] = (csr[...] + rank[...]).astype(jnp.int32)
        rank[...] = rank[...] + sel
        offi[...] = jnp.broadcast_to(s_off + s_tot, (nblk, LANES))

    @pl.when(ps == 1)
    def _():
        slot = (j * pc + lax.broadcasted_iota(jnp.int32, (1, pc), 1)).astype(F32)
        blk = jnp.sum((offi[:, 0:1] <= slot).astype(jnp.int32), axis=0, keepdims=True)
        owner = jnp.where(lax.broadcasted_iota(jnp.int32, (nblk, pc), 0) == blk, 1.0, 0.0)
        weights = jnp.concatenate([256.0 * owner, owner], axis=0).astype(BF16)
        counts = jnp.dot(digits[...], weights, preferred_element_type=F32)
        inb = jnp.sum((counts <= slot + 0.5).astype(jnp.int32), axis=0, keepdims=True)
        idx_ref[0] = blk * tb + inb


def _select(afft, cap):
    n = afft.shape[1]
    nblk = SELECT_BLOCKS
    tb = n // nblk
    pc = SELECT_SLOTS
    assert n % nblk == 0 and tb % LANES == 0 and cap % pc == 0 and cap < 65536
    nch = cap // pc
    aff3 = afft.reshape(N_EXPERTS, nblk, tb)
    upper = jnp.asarray(np.triu(np.ones((tb, tb), np.float32)), BF16)
    lstrict = jnp.asarray(np.tril(np.ones((nblk, nblk), np.float32), -1))
    full = lambda a: pl.BlockSpec(a.shape, lambda s: (0,) * a.ndim)
    idx_spec = pl.BlockSpec((1, 1, pc), lambda s: (jnp.maximum(s - N_EXPERTS, 0), 0, 0))
    slot_spec = pl.BlockSpec((1, nblk, tb), lambda s: (jnp.maximum(s - N_EXPERTS, 0) // nch, 0, 0))
    tok_spec = pl.BlockSpec((nblk, tb), lambda s: (0, 0))
    idx, slots, cs, ce = pl.pallas_call(
        functools.partial(_select_kernel, cap=cap, tb=tb),
        grid=(N_EXPERTS + N_EXPERTS * nch,),
        in_specs=[full(aff3), full(upper), full(lstrict)],
        out_specs=[idx_spec, slot_spec, tok_spec, tok_spec],
        out_shape=[jax.ShapeDtypeStruct((N_EXPERTS * nch, 1, pc), jnp.int32),
                   jax.ShapeDtypeStruct((N_EXPERTS, nblk, tb), jnp.int32),
                   jax.ShapeDtypeStruct((nblk, tb), jnp.int32), jax.ShapeDtypeStruct((nblk, tb), jnp.int32)],
        scratch_shapes=[pltpu.VMEM((N_EXPERTS, 8, LANES), jnp.int32), pltpu.VMEM((N_EXPERTS, nblk, tb), BF16),
                        pltpu.VMEM((nblk, tb), F32), pltpu.VMEM((nblk, tb), F32), pltpu.VMEM((nblk, tb), F32),
                        pltpu.VMEM((tb, 2 * nblk), BF16), pltpu.VMEM((nblk, LANES), F32)],
        compiler_params=_cparams(1),
        name="select",
    )(aff3, upper, lstrict)
    return idx.reshape(-1), slots.reshape(N_EXPERTS, n), cs.reshape(-1), ce.reshape(-1)


def _slot_rows(afft, slots, idx):
    n = afft.shape[1]
    table = jnp.concatenate([lax.bitcast_convert_type(afft, jnp.int32), slots,
                             jnp.zeros((LANES - 2 * N_EXPERTS, n), jnp.int32)], axis=0).T
    return _sc_gather(table, idx)


def _sc_mesh():
    return plsc.VectorSubcoreMesh(core_axis_name="c", subcore_axis_name="s")


def _sc_scatter(rows, idx, m_out):
    m, d = rows.shape
    assert m % SC_WINDOW == 0

    @functools.partial(pl.kernel, out_type=jax.ShapeDtypeStruct((m_out, d), rows.dtype), mesh=_sc_mesh(),
                       name="sc_scatter")
    def scatter(x_hbm, i_hbm, o_hbm):
        def body(x_vmem, i_vmem):
            pltpu.sync_copy(x_vmem, o_hbm.at[i_vmem.at[0]])

        pltpu.emit_pipeline(
            body,
            grid=(m // SC_WINDOW,),
            in_specs=[pl.BlockSpec((SC_WINDOW, d), lambda i: (i, 0)),
                      pl.BlockSpec((1, SC_WINDOW), lambda i: (0, i))],
            out_specs=[],
            core_axis_name=("c", "s"),
            dimension_semantics=(pltpu.PARALLEL,),
        )(x_hbm, i_hbm)

    return scatter(rows, idx.reshape(1, m))


def _sc_gather(table, idx):
    m = idx.shape[0]
    d = table.shape[1]
    assert m % SC_WINDOW == 0

    @functools.partial(pl.kernel, out_type=jax.ShapeDtypeStruct((m, d), table.dtype), mesh=_sc_mesh(),
                       name="sc_gather")
    def gather(x_hbm, i_hbm, o_hbm):
        def body(i_vmem, o_vmem):
            pltpu.sync_copy(x_hbm.at[i_vmem.at[0]], o_vmem)

        pltpu.emit_pipeline(
            body,
            grid=(m // SC_WINDOW,),
            in_specs=[pl.BlockSpec((1, SC_WINDOW), lambda i: (0, i))],
            out_specs=[pl.BlockSpec((SC_WINDOW, d), lambda i: (i, 0))],
            core_axis_name=("c", "s"),
            dimension_semantics=(pltpu.PARALLEL,),
        )(i_hbm, o_hbm)

    return gather(table, idx.reshape(1, m))


def _row_to_col(row):
    n = row.shape[1]
    eye = lax.broadcasted_iota(jnp.int32, (n, n), 0) == lax.broadcasted_iota(jnp.int32, (n, n), 1)
    return jnp.sum(jnp.where(eye, row, jnp.zeros_like(row)), axis=1, keepdims=True)


def _ffn_kernel(slot_ref, x0_ref, x1_ref, w1_hbm, w3_hbm, w2_hbm, o0_ref, o1_ref, dst_ref, wstage, w1b, w3b, w2b, wsem,
                *, layer):
    e = pl.program_id(0)
    i = pl.program_id(1)

    def weight_copies(expert):
        return [pltpu.make_async_copy(w_hbm.at[layer, expert], wstage.at[k], wsem.at[k])
                for k, w_hbm in enumerate((w1_hbm, w3_hbm, w2_hbm))]

    @pl.when(i == 0)
    def _():
        @pl.when(e == 0)
        def _():
            for cp in weight_copies(0):
                cp.start()
        for cp in weight_copies(e):
            cp.wait()
        w1b[...] = wstage[0].astype(BF16)
        w3b[...] = wstage[1].astype(BF16)
        w2b[...] = wstage[2].astype(BF16)

    @pl.when(jnp.logical_and(i == 1, e + 1 < N_EXPERTS))
    def _():
        for cp in weight_copies(e + 1):
            cp.start()

    info = slot_ref[...]
    lane = lax.broadcasted_iota(jnp.int32, info.shape, 1)
    gate = jnp.sum(jnp.where(lane == e, pltpu.bitcast(info, F32), 0.0), axis=1, keepdims=True)
    info_t = info.T
    word = lax.broadcasted_iota(jnp.int32, info_t.shape, 0)
    dst_ref[0] = jnp.sum(jnp.where(word == N_EXPERTS + e, info_t, 0), axis=0, keepdims=True)

    xs = _unpack_rows([x0_ref[...], x1_ref[...]])
    hg = jnp.dot(xs, w1b[...], preferred_element_type=F32)
    hu = jnp.dot(xs, w3b[...], preferred_element_type=F32)
    hid = (hg * _sigmoid(hg) * hu).astype(BF16)
    out = jnp.dot(hid, w2b[...], preferred_element_type=F32) * gate
    for ref, words in zip((o0_ref, o1_ref), _pack_rows(out)):
        ref[...] = words


def _expert_ffn(xs, slot_rows, w1, w3, w2, layer):
    m = xs[0].shape[0]
    rows = FFN_ROWS
    nt = m // (N_EXPERTS * rows)
    assert SC_PIECES == 2 and nt >= 2
    piece = pl.BlockSpec((rows, SC_ROW), lambda e, i: (e * nt + i, 0))
    any_spec = pl.BlockSpec(memory_space=pl.ANY)
    *outs, dst = pl.pallas_call(
        functools.partial(_ffn_kernel, layer=layer),
        grid=(N_EXPERTS, nt),
        in_specs=[pl.BlockSpec((rows, LANES), lambda e, i: (e * nt + i, 0))] + [piece] * SC_PIECES + [any_spec] * 3,
        out_specs=[piece] * SC_PIECES + [pl.BlockSpec((1, 1, rows), lambda e, i: (e * nt + i, 0, 0))],
        out_shape=[jax.ShapeDtypeStruct((m, SC_ROW), jnp.int32)] * SC_PIECES
        + [jax.ShapeDtypeStruct((m // rows, 1, rows), jnp.int32)],
        scratch_shapes=[pltpu.VMEM((3, D_MODEL, EXPERT_FF), F32), pltpu.VMEM((D_MODEL, EXPERT_FF), BF16),
                        pltpu.VMEM((D_MODEL, EXPERT_FF), BF16), pltpu.VMEM((EXPERT_FF, D_MODEL), BF16),
                        pltpu.SemaphoreType.DMA((3,))],
        compiler_params=_cparams(2),
        name="expert_ffn",
    )(slot_rows, *xs, w1, w3, w2)
    return outs, dst.reshape(-1)


def _combine_kernel(tsub_ref, x_ref, cs_ref, ce_ref, r0_hbm, r1_hbm, o_ref, rbuf, obuf, rows16, sems, osem,
                    *, ntile, total):
    _combine_tile(pl.program_id(0), tsub_ref, x_ref, cs_ref, ce_ref, r0_hbm, r1_hbm, o_ref, rbuf, obuf, rows16, sems,
                  osem, ntile=ntile, total=total)


def _combine_then_in_proj_kernel(tsub_ref, x_ref, cs_ref, ce_ref, r0_hbm, r1_hbm, g_ref, w_ref, qg_ref, kg_ref,
                                 bdq_ref, bdk_ref, y_ref, qa_ref, ka_ref, va_ref, qr_ref, kr_ref, vr_ref, gr_ref,
                                 ga_ref, gt_ref, rbuf, obuf, rows16, sems, osem, *, ntile, total):
    _combine_tile(pl.program_id(0), tsub_ref, x_ref, cs_ref, ce_ref, r0_hbm, r1_hbm, y_ref, rbuf, obuf, rows16, sems,
                  osem, ntile=ntile, total=total)
    _in_proj_kernel(y_ref, g_ref, w_ref, qg_ref, kg_ref, bdq_ref, bdk_ref,
                    qa_ref, ka_ref, va_ref, qr_ref, kr_ref, vr_ref, gr_ref, ga_ref, gt_ref)


def _combine_tile(i, tsub_ref, x_ref, cs_ref, ce_ref, r0_hbm, r1_hbm, o_ref, rbuf, obuf, rows16, sems, osem,
                  *, ntile, total):
    pieces_hbm = (r0_hbm, r1_hbm)
    win = COMBINE_WINDOW
    sub = COMBINE_SUB
    subwin = COMBINE_SUBWIN
    per = COMBINE_TOKENS // sub
    slot = lax.rem(i, COMBINE_BUFFERS)

    def window_start(t):
        return pl.multiple_of((tsub_ref[t * per] // BF16_TILE_ROWS) * BF16_TILE_ROWS, BF16_TILE_ROWS)

    def copies(t, b):
        s = window_start(t)
        return [pltpu.make_async_copy(pieces_hbm[c].at[pl.ds(s, win)], rbuf.at[b, c], sems.at[b, c])
                for c in range(SC_PIECES)]

    ahead = COMBINE_BUFFERS - 1

    @pl.when(i == 0)
    def _():
        for t in range(min(ahead, ntile)):
            for cp in copies(t, t):
                cp.start()

    @pl.when(i + ahead < ntile)
    def _():
        for cp in copies(i + ahead, lax.rem(i + ahead, COMBINE_BUFFERS)):
            cp.start()

    for cp in copies(i, slot):
        cp.wait()

    first = [_row_to_col(cs_ref[0][:, g * sub:(g + 1) * sub]) for g in range(per)]
    last = [_row_to_col(ce_ref[0][:, g * sub:(g + 1) * sub]) for g in range(per)]

    def zero_unwritten(words, base):
        written = (base + lax.broadcasted_iota(jnp.int32, (win, 1), 0)) < total
        return jnp.where(written, words, 0)

    def owner_matrix(g, base, width):
        r = base + lax.broadcasted_iota(jnp.int32, (1, width), 1)
        return jnp.logical_and(first[g] <= r, r < last[g]).astype(BF16)

    s0 = window_start(i)
    tail = s0 + win > total

    @pl.when(tail)
    def _():
        rows16[...] = _unpack_rows([zero_unwritten(rbuf[slot, c], s0) for c in range(SC_PIECES)])

    @pl.when(jnp.logical_not(tail))
    def _():
        rows16[...] = _unpack_rows([rbuf[slot, c] for c in range(SC_PIECES)])

    offsets = []
    fits = None
    for g in range(per):
        off = (tsub_ref[i * per + g] // BF16_TILE_ROWS) * BF16_TILE_ROWS - s0
        ok = jnp.logical_and(tsub_ref[i * per + g + 1] - s0 <= off + subwin, off + subwin <= win)
        fits = ok if fits is None else jnp.logical_and(fits, ok)
        offsets.append(off)

    @pl.when(fits)
    def _():
        for g in range(per):
            tokens = slice(g * sub, (g + 1) * sub)
            off = pl.multiple_of(offsets[g], BF16_TILE_ROWS)
            q = owner_matrix(g, s0 + off, subwin)
            o_ref[tokens, :] = x_ref[tokens, :] + jnp.dot(q, rows16[pl.ds(off, subwin), :],
                                                         preferred_element_type=F32)

    @pl.when(jnp.logical_not(fits))
    def _():
        def everyone(base):
            return jnp.concatenate([owner_matrix(g, base, win) for g in range(per)], axis=0)

        y = x_ref[...] + jnp.dot(everyone(s0), rows16[...], preferred_element_type=F32)
        n_extra = jnp.maximum(tsub_ref[(i + 1) * per] - (s0 + win) + win - 1, 0) // win

        def extra(k, acc):
            base = pl.multiple_of(s0 + (k + 1) * win, BF16_TILE_ROWS)
            cps = [pltpu.make_async_copy(pieces_hbm[c].at[pl.ds(base, win)], obuf.at[c], osem.at[c])
                   for c in range(SC_PIECES)]
            for cp in cps:
                cp.start()
            for cp in cps:
                cp.wait()
            rows = _unpack_rows([zero_unwritten(obuf[c], base) for c in range(SC_PIECES)])
            return acc + jnp.dot(everyone(base), rows, preferred_element_type=F32)

        o_ref[...] = lax.fori_loop(0, n_extra, extra, y)


def _combine(pending, in_proj_params=None):
    xmid, cs, ce, pieces, total = pending
    n = xmid.shape[0]
    tt = COMBINE_TOKENS
    win = COMBINE_WINDOW
    ntile = n // tt
    tsub = jnp.concatenate([cs[::COMBINE_SUB], jnp.full((1,), total, jnp.int32)])
    cs3 = cs.reshape(ntile, 1, tt)
    ce3 = ce.reshape(ntile, 1, tt)
    any_spec = pl.BlockSpec(memory_space=pl.ANY)
    tok = pl.BlockSpec((1, 1, tt), lambda i, ts: (i, 0, 0))
    row = lambda wd: pl.BlockSpec((tt, wd), lambda i, ts: (i, 0))
    in_specs = [row(D_MODEL), tok, tok] + [any_spec] * SC_PIECES
    out_specs = [row(D_MODEL)]
    out_shape = [jax.ShapeDtypeStruct((n, D_MODEL), F32)]
    operands = [tsub, xmid, cs3, ce3, *pieces]
    body, name = _combine_kernel, "combine"
    if in_proj_params is not None:
        full = lambda a: pl.BlockSpec(a.shape, lambda i, ts: (0,) * a.ndim, pipeline_mode=pl.Buffered(1))
        in_specs += [full(a) for a in in_proj_params]
        out_specs += [row(wd) for wd in IN_SPLITS]
        out_shape += [jax.ShapeDtypeStruct((n, wd), BF16) for wd in IN_SPLITS]
        operands += list(in_proj_params)
        body, name = _combine_then_in_proj_kernel, "combine_in_proj"
    grid_spec = pltpu.PrefetchScalarGridSpec(
        num_scalar_prefetch=1,
        grid=(ntile,),
        in_specs=in_specs,
        out_specs=out_specs,
        scratch_shapes=[pltpu.VMEM((COMBINE_BUFFERS, SC_PIECES, win, SC_ROW), jnp.int32),
                        pltpu.VMEM((SC_PIECES, win, SC_ROW), jnp.int32), pltpu.VMEM((win, D_MODEL), BF16),
                        pltpu.SemaphoreType.DMA((COMBINE_BUFFERS, SC_PIECES)), pltpu.SemaphoreType.DMA((SC_PIECES,))],
    )
    y, *proj = pl.pallas_call(
        functools.partial(body, ntile=ntile, total=total),
        grid_spec=grid_spec,
        out_shape=out_shape,
        compiler_params=_cparams(1),
        name=name,
    )(*operands)
    return (y, proj) if in_proj_params is not None else y


def _t5_bucket(rel):
    half = REL_BUCKETS // 2
    max_exact = half // 2
    base = np.where(rel > 0, half, 0)
    n = np.abs(rel)
    large = max_exact + (np.log(np.maximum(n, 1) / max_exact) / math.log(REL_MAX_DIST / max_exact)
                         * (half - max_exact)).astype(np.int32)
    large = np.minimum(large, half - 1)
    return (base + np.where(n < max_exact, n, large)).astype(np.int32)


def _head_perm():
    nq = ATTN_HEADS // 2
    cols = []
    for j in range(nq):
        for half in range(2):
            h = j + nq * half
            cols.extend(range(h * ATTN_HEAD_DIM, (h + 1) * ATTN_HEAD_DIM))
    return np.asarray(cols, np.int32)


def _attn_bias_tables(rel_bias):
    q_pos = np.arange(BLOCK)[:, None]
    k_off = np.arange(3 * BLOCK)[None, :] - BLOCK
    rel = k_off - q_pos
    in_window = np.abs(rel) <= WINDOW
    onehot = jnp.asarray(_t5_bucket(rel)[:, :, None] == np.arange(REL_BUCKETS)[None, None, :], F32)
    bias = jnp.einsum("qkb,bh->hqk", onehot, rel_bias.astype(F32), precision=lax.Precision.HIGHEST)
    col = np.arange(3 * BLOCK)[None, :]
    tables = []
    for valid in (col >= BLOCK, np.ones_like(col, bool), col < 2 * BLOCK):
        t = jnp.where(jnp.asarray(in_window & valid)[None], bias, NEG)
        nq = ATTN_HEADS // 2
        rows = [jnp.concatenate([t[j], t[j + nq]], axis=1) for j in range(nq)]
        tables.append(jnp.concatenate(rows, axis=0))
    return jnp.stack(tables)


def _retention_tables(decay_logit, norm_g):
    cr = RET_CHUNK
    lg = jax.nn.log_sigmoid(decay_logit.astype(F32))
    lgf, lgb = lg[0][:, None, None], lg[1][:, None, None]
    pos = np.arange(cr, dtype=np.float32)
    dist = pos[:, None] - pos[None, :]
    scale = RET_DIM ** -0.5
    dmask = jnp.where(jnp.asarray(dist >= 0)[None],
                      jnp.exp(lgf * np.maximum(dist, 0.0)[None]),
                      jnp.exp(lgb * np.maximum(-dist, 0.0)[None])) * scale
    col = lambda v: jnp.broadcast_to(v[:, :, None], (RET_HEADS, cr, RET_DIM))
    rowf = col(jnp.exp(lg[0][:, None] * pos[None]))
    rowb = col(jnp.exp(lg[1][:, None] * (cr - 1.0 - pos)[None]))
    wkf = col(jnp.exp(lg[0][:, None] * (cr - pos)[None]) * scale)
    wkb = col(jnp.exp(lg[1][:, None] * (pos + 1.0)[None]) * scale)
    dec = jnp.concatenate([jnp.exp(lg[0] * cr), jnp.exp(lg[1] * cr)])
    return dec, dmask, rowf, rowb, wkf, wkb, norm_g.astype(F32)


def _layer(x2, pending, b, s, p):
    in_proj_params = (p["g1"], p["w_in"], p["qg"], p["kg"], p["bdq"], p["bdk"])
    if pending is None:
        qa, ka, va, qr, kr, vr, gr, ga, gt = _in_proj(x2, *in_proj_params)
    else:
        x2, (qa, ka, va, qr, kr, vr, gr, ga, gt) = _combine(pending, in_proj_params)
    attn = _attention(qa, ka, va, p["bias3"], p["sink"], b, s)
    retn = _retention(qr, kr, vr, gr, p["retn"], b, s)
    xmid, afft, *h2 = _merge(attn, retn, ga, gt, x2, p["wba"], p["wbr"], p["wo"], p["g2"], p["wr"])
    n = b * s
    cap = max(1, EC_CAPACITY_FACTOR * n // N_EXPERTS)
    total = N_EXPERTS * cap
    idx, slots, cs, ce = _select(afft, cap)
    slot_rows = _slot_rows(afft, slots, idx)
    xs = [_sc_gather(piece, idx) for piece in h2]
    outs, dst = _expert_ffn(xs, slot_rows, p["w1"], p["w3"], p["w2"], p["layer"])
    by_token = [_sc_scatter(o, dst, total + COMBINE_WINDOW) for o in outs]
    return xmid, cs, ce, by_token, total


def kernel(x_prompt, x_sample, norm_mix_g, w_in, q_norm_g, k_norm_g, attn_sink, rel_bias, retn_decay_logit, retn_norm_g, w_branch_attn, w_branch_retn, w_out, norm_ffn_g, w_router, w_exp_gate, w_exp_up, w_exp_down):
    depth = w_in.shape[0]
    perm = _head_perm()
    bias3 = _attn_bias_tables(rel_bias)
    bdq = jnp.asarray(np.kron(np.eye(ATTN_HEADS), np.ones((ATTN_HEAD_DIM, ATTN_HEAD_DIM))), BF16)
    bdk = jnp.asarray(np.kron(np.eye(ATTN_KV_HEADS), np.ones((ATTN_HEAD_DIM, ATTN_HEAD_DIM))), BF16)
    layers = []
    for l in range(depth):
        w = w_in[l]
        w = jnp.concatenate([w[:, :ATTN_WIDTH][:, perm], w[:, ATTN_WIDTH:]], axis=1).astype(BF16)
        wr = jnp.pad(w_router[l], ((0, 0), (0, LANES - N_EXPERTS))).astype(BF16)
        layers.append(dict(
            g1=norm_mix_g[l].astype(F32)[None], w_in=w,
            qg=(jnp.tile(q_norm_g[l].astype(F32), ATTN_HEADS) * (ATTN_HEAD_DIM ** -0.5))[None],
            kg=jnp.tile(k_norm_g[l].astype(F32), ATTN_KV_HEADS)[None],
            bdq=bdq, bdk=bdk, bias3=bias3, sink=attn_sink[l].astype(F32),
            retn=_retention_tables(retn_decay_logit[l], retn_norm_g[l]),
            wba=w_branch_attn[l][perm, :].astype(BF16), wbr=w_branch_retn[l].astype(BF16),
            wo=w_out[l].astype(BF16), g2=norm_ffn_g[l].astype(F32)[None], wr=wr,
            w1=w_exp_gate, w3=w_exp_up, w2=w_exp_down, layer=l))

    def trunk(x):
        b, s, d = x.shape
        x2, pending = x.reshape(b * s, d), None
        for p in layers:
            pending = _layer(x2, pending, b, s, p)
        return _combine(pending).reshape(b, s, d)

    return (trunk(x_prompt), trunk(x_sample))
```

```python
import functools
import math

import numpy as np
import jax
import jax.numpy as jnp
from jax import lax
from jax.experimental import pallas as pl
from jax.experimental.pallas import tpu as pltpu
from jax.experimental.pallas import tpu_sc as plsc

D_MODEL = 1024
ATTN_HEADS = 8
ATTN_KV_HEADS = 2
ATTN_HEAD_DIM = 64
WINDOW = 128
BLOCK = 128
REL_BUCKETS = 32
REL_MAX_DIST = 128
RET_HEADS = 4
RET_DIM = 128
N_EXPERTS = 16
EC_CAPACITY_FACTOR = 2
EXPERT_FF = 1024
EPS = 1e-6

ATTN_WIDTH = ATTN_HEADS * ATTN_HEAD_DIM
KV_WIDTH = ATTN_KV_HEADS * ATTN_HEAD_DIM
RET_WIDTH = RET_HEADS * RET_DIM
IN_SPLITS = (ATTN_WIDTH, KV_WIDTH, KV_WIDTH, RET_WIDTH, RET_WIDTH, RET_WIDTH, RET_WIDTH, D_MODEL, D_MODEL)
IN_OFFSETS = tuple(int(o) for o in np.cumsum((0,) + IN_SPLITS))

LANES = 128
BF16_TILE_ROWS = 16
VMEM_LIMIT_BYTES = 56 * 1024 * 1024

TOKEN_TILE = 1024
MERGE_ROWS = 256
IN_PROJ_TILE = 1024
ATTN_QUERIES = 2048
RET_CHUNK = 256
RET_STEP = 2048
FFN_ROWS = 1024
SELECT_BLOCKS = 128
SELECT_SLOTS = 2048
SC_WINDOW = 128
SC_ROW = 256
PACKED_WIDTH = D_MODEL // 2
SC_PIECES = PACKED_WIDTH // SC_ROW
COMBINE_TOKENS = 512
COMBINE_WINDOW = 1280
COMBINE_SUB = 128
COMBINE_SUBWIN = 384
COMBINE_BUFFERS = 3

F32 = jnp.float32
BF16 = jnp.bfloat16
NEG = -1e30


def _cparams(n_axes, vmem=VMEM_LIMIT_BYTES):
    return pltpu.CompilerParams(dimension_semantics=("arbitrary",) * n_axes, vmem_limit_bytes=vmem)


def _sigmoid(x):
    return 0.5 * jnp.tanh(0.5 * x) + 0.5


HIGH_HALF = -65536


def _pack_rows(x):
    bits = pltpu.bitcast(x.astype(BF16).astype(F32), jnp.int32)
    words = lax.shift_right_logical(bits[:, :PACKED_WIDTH], 16) | (bits[:, PACKED_WIDTH:] & HIGH_HALF)
    return [words[:, c * SC_ROW:(c + 1) * SC_ROW] for c in range(SC_PIECES)]


def _unpack_rows(pieces):
    low = [pltpu.bitcast(lax.shift_left(w, 16), F32) for w in pieces]
    high = [pltpu.bitcast(w & HIGH_HALF, F32) for w in pieces]
    return jnp.concatenate(low + high, axis=1).astype(BF16)


def _in_proj_kernel(x_ref, g_ref, w_ref, qg_ref, kg_ref, bdq_ref, bdk_ref,
                    qa_ref, ka_ref, va_ref, qr_ref, kr_ref, vr_ref, gr_ref, ga_ref, gt_ref):
    x = x_ref[...]
    ms = jnp.mean(x * x, axis=-1, keepdims=True)
    h = (x * lax.rsqrt(ms + EPS) * g_ref[...]).astype(BF16)

    def mm(k):
        return jnp.dot(h, w_ref[:, IN_OFFSETS[k]:IN_OFFSETS[k + 1]], preferred_element_type=F32)

    def head_norm(t, bd_ref, gain_ref):
        ss = jnp.dot((t * t).astype(BF16), bd_ref[...], preferred_element_type=F32)
        return t * lax.rsqrt(ss * (1.0 / ATTN_HEAD_DIM) + EPS) * gain_ref[...]

    q_raw = mm(0)
    k_raw = mm(1)
    for k, ref in ((2, va_ref), (3, qr_ref), (4, kr_ref), (5, vr_ref)):
        ref[...] = mm(k).astype(BF16)
    g = mm(6)
    gr_ref[...] = (g * _sigmoid(g)).astype(BF16)
    ga_ref[...] = _sigmoid(mm(7)).astype(BF16)
    gt_ref[...] = _sigmoid(mm(8)).astype(BF16)
    qa_ref[...] = head_norm(q_raw, bdq_ref, qg_ref).astype(BF16)
    ka_ref[...] = head_norm(k_raw, bdk_ref, kg_ref).astype(BF16)


def _in_proj(x2, g, w, qg, kg, bdq, bdk):
    n = x2.shape[0]
    tm = IN_PROJ_TILE
    full = lambda a: pl.BlockSpec(a.shape, lambda i: (0,) * a.ndim, pipeline_mode=pl.Buffered(1))
    widths = IN_SPLITS
    return pl.pallas_call(
        _in_proj_kernel,
        grid=(n // tm,),
        in_specs=[pl.BlockSpec((tm, D_MODEL), lambda i: (i, 0)), full(g), full(w), full(qg), full(kg),
                  full(bdq), full(bdk)],
        out_specs=[pl.BlockSpec((tm, wd), lambda i: (i, 0)) for wd in widths],
        out_shape=[jax.ShapeDtypeStruct((n, wd), BF16) for wd in widths],
        compiler_params=_cparams(1),
        name="in_proj",
    )(x2, g, w, qg, kg, bdq, bdk)


def _attn_kernel(sink_ref, q_ref, kp_ref, kc_ref, kn_ref, vp_ref, vc_ref, vn_ref, bias_ref, o_ref, *, nsteps):
    nq = ATTN_HEADS // 2
    ni = pl.program_id(1)
    k = jnp.concatenate([kp_ref[...], kc_ref[...], kn_ref[...]], axis=0)
    v = jnp.concatenate([vp_ref[...], vc_ref[...], vn_ref[...]], axis=0)
    low = lax.broadcasted_iota(jnp.int32, k.shape, 1) < ATTN_HEAD_DIM
    zero = jnp.zeros_like(k)
    k_lo, k_hi = jnp.where(low, k, zero), jnp.where(low, zero, k)
    v_lo, v_hi = jnp.where(low, v, zero), jnp.where(low, zero, v)
    nk = 3 * BLOCK
    low_o = lax.broadcasted_iota(jnp.int32, (BLOCK, LANES), 1) < ATTN_HEAD_DIM
    key_low = lax.broadcasted_iota(jnp.int32, (2 * nk, LANES), 0) < nk
    lane_low = lax.broadcasted_iota(jnp.int32, (2 * nk, LANES), 1) < ATTN_HEAD_DIM
    ones_bd = (key_low == lane_low).astype(BF16)
    nsub = ATTN_QUERIES // BLOCK
    for sb in range(nsub):
        rows = slice(sb * BLOCK, (sb + 1) * BLOCK)
        keys = slice(sb * BLOCK, sb * BLOCK + nk)
        q = q_ref[rows, :]
        qs = jnp.concatenate([q[:, j * LANES:(j + 1) * LANES] for j in range(nq)], axis=0)
        kbd = jnp.concatenate([k_lo[keys], k_hi[keys]], axis=0)
        vbd = jnp.concatenate([v_lo[keys], v_hi[keys]], axis=0)
        s = lax.dot_general(qs, kbd, (((1,), (1,)), ((), ())), preferred_element_type=F32)
        if sb == 0:
            table = jnp.where(ni == 0, 0, 1)
        elif sb == nsub - 1:
            table = jnp.where(ni == nsteps - 1, 2, 1)
        else:
            table = 1
        s = s + bias_ref[table]
        probs, sink_terms = [], []
        for j in range(nq):
            row_p, row_sink = [], []
            for half in range(2):
                sj = s[j * BLOCK:(j + 1) * BLOCK, half * nk:(half + 1) * nk]
                sk = sink_ref[j + nq * half]
                m = jnp.maximum(jnp.max(sj, axis=-1, keepdims=True), sk)
                row_p.append(jnp.exp(sj - m).astype(BF16))
                row_sink.append(jnp.exp(sk - m))
            probs.append(jnp.concatenate(row_p, axis=1))
            sink_terms.append(jnp.where(low_o, row_sink[0], row_sink[1]))
        pm = jnp.concatenate(probs, axis=0)
        od = jnp.dot(pm, jnp.concatenate([vbd, ones_bd], axis=1), preferred_element_type=F32)
        o = od[:, :LANES] / (od[:, LANES:] + jnp.concatenate(sink_terms, axis=0))
        for j in range(nq):
            o_ref[rows, j * LANES:(j + 1) * LANES] = o[j * BLOCK:(j + 1) * BLOCK].astype(BF16)


def _attention(qa, ka, va, bias3, sink, b, s):
    tq = ATTN_QUERIES
    per = tq // BLOCK
    nb = s // BLOCK
    nsteps = s // tq
    assert s % tq == 0 and nb >= 2
    n = b * s
    main = lambda wd: pl.BlockSpec((tq, wd), lambda bi, ni: (bi * nsteps + ni, 0))
    prev = pl.BlockSpec((BLOCK, KV_WIDTH), lambda bi, ni: (bi * nb + jnp.maximum(ni * per - 1, 0), 0))
    nxt = pl.BlockSpec((BLOCK, KV_WIDTH), lambda bi, ni: (bi * nb + jnp.minimum(ni * per + per, nb - 1), 0))
    return pl.pallas_call(
        functools.partial(_attn_kernel, nsteps=nsteps),
        grid=(b, nsteps),
        in_specs=[pl.BlockSpec(memory_space=pltpu.SMEM), main(ATTN_WIDTH),
                  prev, main(KV_WIDTH), nxt, prev, main(KV_WIDTH), nxt,
                  pl.BlockSpec(bias3.shape, lambda bi, ni: (0, 0, 0))],
        out_specs=main(ATTN_WIDTH),
        out_shape=jax.ShapeDtypeStruct((n, ATTN_WIDTH), BF16),
        compiler_params=_cparams(2),
        name="attn",
    )(sink, qa, ka, ka, ka, va, va, va, bias3)


def _retn_kernel(dec_ref, q_ref, k_ref, v_ref, g_ref, dmask_ref, rowf_ref, rowb_ref, wkf_ref, wkb_ref, ng_ref,
                 o_ref, tstore, uf, tb, *, nsteps):
    p = pl.program_id(1)
    n = pl.program_id(2)
    cr = RET_CHUNK
    per = RET_STEP // cr
    tn = (((0,), (0,)), ((), ()))
    nt = (((1,), (1,)), ((), ()))
    hs = lambda h: slice(h * RET_DIM, (h + 1) * RET_DIM)

    @pl.when(p == 0)
    def _():
        @pl.when(n == 0)
        def _():
            tb[...] = jnp.zeros_like(tb)
        first_chunk = (nsteps - 1 - n) * per
        for sub in reversed(range(per)):
            rows = slice(sub * cr, (sub + 1) * cr)
            for h in range(RET_HEADS):
                tstore[first_chunk + sub, h] = tb[h].astype(BF16)
                kw = (k_ref[rows, hs(h)].astype(F32) * wkb_ref[h]).astype(BF16)
                upd = lax.dot_general(kw, v_ref[rows, hs(h)], tn, preferred_element_type=F32)
                tb[h] = dec_ref[RET_HEADS + h] * tb[h] + upd

    @pl.when(p == 1)
    def _():
        @pl.when(n == 0)
        def _():
            uf[...] = jnp.zeros_like(uf)
        heads = range(RET_HEADS)
        for sub in range(per):
            rows = slice(sub * cr, (sub + 1) * cr)
            qs = [q_ref[rows, hs(h)] for h in heads]
            ks = [k_ref[rows, hs(h)] for h in heads]
            vs = [v_ref[rows, hs(h)] for h in heads]
            scores = [lax.dot_general(qs[h], ks[h], nt, preferred_element_type=F32) for h in heads]
            states = [jnp.concatenate([uf[h].astype(BF16), tstore[n * per + sub, h]], axis=1) for h in heads]
            cross = [jnp.dot(qs[h], states[h], preferred_element_type=F32) for h in heads]
            update = [lax.dot_general((ks[h].astype(F32) * wkf_ref[h]).astype(BF16), vs[h], tn,
                                      preferred_element_type=F32) for h in heads]
            intra = [jnp.dot((scores[h] * dmask_ref[h]).astype(BF16), vs[h], preferred_element_type=F32)
                     for h in heads]
            for h in heads:
                o = intra[h] + cross[h][:, :RET_DIM] * rowf_ref[h] + cross[h][:, RET_DIM:] * rowb_ref[h]
                mu = jnp.mean(o, axis=-1, keepdims=True)
                d = o - mu
                var = jnp.mean(d * d, axis=-1, keepdims=True)
                on = d * lax.rsqrt(var + EPS) * ng_ref[h:h + 1, :]
                gate = g_ref[rows, hs(h)].astype(F32)
                o_ref[rows, hs(h)] = (gate * on).astype(BF16)
                uf[h] = dec_ref[h] * uf[h] + update[h]


def _retention(qr, kr, vr, gr, tables, b, s):
    dec, dmask, rowf, rowb, wkf, wkb, ng = tables
    rs = RET_STEP
    nsteps = s // rs
    nc = s // RET_CHUNK
    assert s % rs == 0
    n = b * s
    full = lambda a: pl.BlockSpec(a.shape, lambda bi, pi, ni: (0,) * a.ndim)
    fwd_spec = pl.BlockSpec((rs, RET_WIDTH), lambda bi, pi, ni: (bi * nsteps + ni * pi, 0))
    kv_spec = pl.BlockSpec((rs, RET_WIDTH),
                           lambda bi, pi, ni: (bi * nsteps + ni * pi + (1 - pi) * (nsteps - 1 - ni), 0))
    return pl.pallas_call(
        functools.partial(_retn_kernel, nsteps=nsteps),
        grid=(b, 2, nsteps),
        in_specs=[pl.BlockSpec(memory_space=pltpu.SMEM), fwd_spec, kv_spec, kv_spec, fwd_spec,
                  full(dmask), full(rowf), full(rowb), full(wkf), full(wkb), full(ng)],
        out_specs=fwd_spec,
        out_shape=jax.ShapeDtypeStruct((n, RET_WIDTH), BF16),
        scratch_shapes=[pltpu.VMEM((nc, RET_HEADS, RET_DIM, RET_DIM), BF16),
                        pltpu.VMEM((RET_HEADS, RET_DIM, RET_DIM), F32),
                        pltpu.VMEM((RET_HEADS, RET_DIM, RET_DIM), F32)],
        compiler_params=_cparams(3),
        name="retention",
    )(dec, qr, kr, vr, gr, dmask, rowf, rowb, wkf, wkb, ng)


def _merge_kernel(attn_ref, retn_ref, ga_ref, gr_ref, x_ref, wba_ref, wbr_ref, wo_ref, g2_ref, wr_ref,
                  xmid_ref, afft_ref, *h2_refs):
    tiles = [slice(t * MERGE_ROWS, (t + 1) * MERGE_ROWS) for t in range(TOKEN_TILE // MERGE_ROWS)]
    branch = [(jnp.dot(attn_ref[rows, :], wba_ref[...], preferred_element_type=F32),
               jnp.dot(retn_ref[rows, :], wbr_ref[...], preferred_element_type=F32)) for rows in tiles]
    resid = []
    for rows, (a, r) in zip(tiles, branch):
        merged = ga_ref[rows, :].astype(F32) * a + gr_ref[rows, :].astype(F32) * r
        xn = x_ref[rows, :] + jnp.dot(merged.astype(BF16), wo_ref[...], preferred_element_type=F32)
        xmid_ref[rows, :] = xn
        resid.append(xn)
    scores = []
    for rows, xn in zip(tiles, resid):
        ms = jnp.mean(xn * xn, axis=-1, keepdims=True)
        h2 = xn * lax.rsqrt(ms + EPS) * g2_ref[...]
        for ref, words in zip(h2_refs, _pack_rows(h2)):
            ref[rows, :] = words
        scores.append(jnp.dot(h2.astype(BF16), wr_ref[...], preferred_element_type=F32))
    for rows, logits in zip(tiles, scores):
        real = lax.broadcasted_iota(jnp.int32, logits.shape, 1) < N_EXPERTS
        logits = jnp.where(real, logits, -jnp.inf)
        m = jnp.max(logits, axis=-1, keepdims=True)
        ex = jnp.exp(logits - m)
        aff = ex / jnp.sum(ex, axis=-1, keepdims=True)
        afft_ref[:, rows] = aff.T[:N_EXPERTS, :]


def _merge(attn, retn, ga, gr, x2, wba, wbr, wo, g2, wr):
    n = x2.shape[0]
    tm = TOKEN_TILE
    full = lambda a: pl.BlockSpec(a.shape, lambda i: (0,) * a.ndim)
    row = lambda wd: pl.BlockSpec((tm, wd), lambda i: (i, 0))
    return pl.pallas_call(
        _merge_kernel,
        grid=(n // tm,),
        in_specs=[row(ATTN_WIDTH), row(RET_WIDTH), row(D_MODEL), row(D_MODEL), row(D_MODEL),
                  full(wba), full(wbr), full(wo), full(g2), full(wr)],
        out_specs=[row(D_MODEL), pl.BlockSpec((N_EXPERTS, tm), lambda i: (0, i))] + [row(SC_ROW)] * SC_PIECES,
        out_shape=[jax.ShapeDtypeStruct((n, D_MODEL), F32), jax.ShapeDtypeStruct((N_EXPERTS, n), F32)]
        + [jax.ShapeDtypeStruct((n, SC_ROW), jnp.int32)] * SC_PIECES,
        compiler_params=_cparams(1),
        name="merge",
    )(attn, retn, ga, gr, x2, wba, wbr, wo, g2, wr)


def _select_kernel(aff_ref, u_ref, ls_ref, idx_ref, slot_ref, cs_ref, ce_ref,
                   thr, selbuf, cnt, csr, rank, digits, offi, *, cap, tb):
    s = pl.program_id(0)
    nch = cap // SELECT_SLOTS
    ps = jnp.where(s < N_EXPERTS, 0, 1)
    later = jnp.maximum(s - N_EXPERTS, 0)
    e = jnp.where(s < N_EXPERTS, s, later // nch)
    j = jnp.where(s < N_EXPERTS, 0, later % nch)
    nblk = SELECT_BLOCKS
    pc = SELECT_SLOTS

    def cumsum(vals):
        inb = jnp.dot(vals.astype(BF16), u_ref[...], preferred_element_type=F32)
        tot = jnp.broadcast_to(inb[:, tb - 1:tb], (nblk, LANES))
        off = jnp.dot(ls_ref[...], tot, preferred_element_type=F32, precision=lax.Precision.HIGHEST)
        return inb, off[:, 0:1], tot[:, 0:1]

    @pl.when(jnp.logical_and(ps == 0, jnp.logical_and(e == 0, j == 0)))
    def _():
        def bit_step(t, curs):
            bit = jnp.left_shift(jnp.int32(1), 30 - t)
            out = []
            for x in range(N_EXPERTS):
                cand = curs[x] | bit
                n_ge = jnp.sum((pltpu.bitcast(aff_ref[x], jnp.int32) >= cand).astype(jnp.int32), keepdims=True)
                out.append(jnp.where(n_ge >= cap, cand, curs[x]))
            return tuple(out)

        found = lax.fori_loop(0, 31, bit_step, tuple(jnp.zeros((1, 1), jnp.int32) for _ in range(N_EXPERTS)))
        for x in range(N_EXPERTS):
            thr[x] = jnp.broadcast_to(found[x], thr.shape[1:])

    @pl.when(jnp.logical_and(ps == 0, j == 0))
    def _():
        bits = pltpu.bitcast(aff_ref[e], jnp.int32)
        limit = thr[e][0:1, 0:1]
        gt = bits > limit
        eq = bits == limit
        need = (cap - jnp.sum(gt.astype(jnp.int32), keepdims=True)).astype(F32)
        eqf = eq.astype(F32)
        eq_in, eq_off, _ = cumsum(eqf)
        eq_rank = eq_in + eq_off - eqf
        sel = jnp.logical_or(gt, jnp.logical_and(eq, eq_rank < need)).astype(F32)
        selbuf[e] = sel.astype(BF16)

        @pl.when(e == 0)
        def _():
            cnt[...] = sel

        @pl.when(e > 0)
        def _():
            cnt[...] = cnt[...] + sel

    @pl.when(jnp.logical_and(ps == 1, j == 0))
    def _():
        @pl.when(e == 0)
        def _():
            c = cnt[...]
            c_in, c_off, _ = cumsum(c)
            start = c_in + c_off - c
            csr[...] = start
            cs_ref[...] = start.astype(jnp.int32)
            ce_ref[...] = (start + c).astype(jnp.int32)
            rank[...] = jnp.zeros_like(rank)

        sel = selbuf[e].astype(F32)
        s_in, s_off, s_tot = cumsum(sel)
        count_t = (s_in + s_off).T
        high = jnp.floor(count_t * (1.0 / 256.0))
        digits[:, :nblk] = high.astype(BF16)
        digits[:, nblk:] = (count_t - 256.0 * high).astype(BF16)
        slot_ref[0] = (csr[...] + rank[...]).astype(jnp.int32)
        rank[...] = rank[...] + sel
        offi[...] = jnp.broadcast_to(s_off + s_tot, (nblk, LANES))

    @pl.when(ps == 1)
    def _():
        slot = (j * pc + lax.broadcasted_iota(jnp.int32, (1, pc), 1)).astype(F32)
        blk = jnp.sum((offi[:, 0:1] <= slot).astype(jnp.int32), axis=0, keepdims=True)
        owner = jnp.where(lax.broadcasted_iota(jnp.int32, (nblk, pc), 0) == blk, 1.0, 0.0)
        weights = jnp.concatenate([256.0 * owner, owner], axis=0).astype(BF16)
        counts = jnp.dot(digits[...], weights, preferred_element_type=F32)
        inb = jnp.sum((counts <= slot + 0.5).astype(jnp.int32), axis=0, keepdims=True)
        idx_ref[0] = blk * tb + inb


def _select(afft, cap):
    n = afft.shape[1]
    nblk = SELECT_BLOCKS
    tb = n // nblk
    pc = SELECT_SLOTS
    assert n % nblk == 0 and tb % LANES == 0 and cap % pc == 0 and cap < 65536
    nch = cap // pc
    aff3 = afft.reshape(N_EXPERTS, nblk, tb)
    upper = jnp.asarray(np.triu(np.ones((tb, tb), np.float32)), BF16)
    lstrict = jnp.asarray(np.tril(np.ones((nblk, nblk), np.float32), -1))
    full = lambda a: pl.BlockSpec(a.shape, lambda s: (0,) * a.ndim)
    idx_spec = pl.BlockSpec((1, 1, pc), lambda s: (jnp.maximum(s - N_EXPERTS, 0), 0, 0))
    slot_spec = pl.BlockSpec((1, nblk, tb), lambda s: (jnp.maximum(s - N_EXPERTS, 0) // nch, 0, 0))
    tok_spec = pl.BlockSpec((nblk, tb), lambda s: (0, 0))
    idx, slots, cs, ce = pl.pallas_call(
        functools.partial(_select_kernel, cap=cap, tb=tb),
        grid=(N_EXPERTS + N_EXPERTS * nch,),
        in_specs=[full(aff3), full(upper), full(lstrict)],
        out_specs=[idx_spec, slot_spec, tok_spec, tok_spec],
        out_shape=[jax.ShapeDtypeStruct((N_EXPERTS * nch, 1, pc), jnp.int32),
                   jax.ShapeDtypeStruct((N_EXPERTS, nblk, tb), jnp.int32),
                   jax.ShapeDtypeStruct((nblk, tb), jnp.int32), jax.ShapeDtypeStruct((nblk, tb), jnp.int32)],
        scratch_shapes=[pltpu.VMEM((N_EXPERTS, 8, LANES), jnp.int32), pltpu.VMEM((N_EXPERTS, nblk, tb), BF16),
                        pltpu.VMEM((nblk, tb), F32), pltpu.VMEM((nblk, tb), F32), pltpu.VMEM((nblk, tb), F32),
                        pltpu.VMEM((tb, 2 * nblk), BF16), pltpu.VMEM((nblk, LANES), F32)],
        compiler_params=_cparams(1),
        name="select",
    )(aff3, upper, lstrict)
    return idx.reshape(-1), slots.reshape(N_EXPERTS, n), cs.reshape(-1), ce.reshape(-1)


def _slot_rows(afft, slots, idx):
    n = afft.shape[1]
    table = jnp.concatenate([lax.bitcast_convert_type(afft, jnp.int32), slots,
                             jnp.zeros((LANES - 2 * N_EXPERTS, n), jnp.int32)], axis=0).T
    return _sc_gather(table, idx)


def _sc_mesh():
    return plsc.VectorSubcoreMesh(core_axis_name="c", subcore_axis_name="s")


def _sc_scatter(rows, idx, m_out):
    m, d = rows.shape
    assert m % SC_WINDOW == 0

    @functools.partial(pl.kernel, out_type=jax.ShapeDtypeStruct((m_out, d), rows.dtype), mesh=_sc_mesh(),
                       name="sc_scatter")
    def scatter(x_hbm, i_hbm, o_hbm):
        def body(x_vmem, i_vmem):
            pltpu.sync_copy(x_vmem, o_hbm.at[i_vmem.at[0]])

        pltpu.emit_pipeline(
            body,
            grid=(m // SC_WINDOW,),
            in_specs=[pl.BlockSpec((SC_WINDOW, d), lambda i: (i, 0)),
                      pl.BlockSpec((1, SC_WINDOW), lambda i: (0, i))],
            out_specs=[],
            core_axis_name=("c", "s"),
            dimension_semantics=(pltpu.PARALLEL,),
        )(x_hbm, i_hbm)

    return scatter(rows, idx.reshape(1, m))


def _sc_gather(table, idx):
    m = idx.shape[0]
    d = table.shape[1]
    assert m % SC_WINDOW == 0

    @functools.partial(pl.kernel, out_type=jax.ShapeDtypeStruct((m, d), table.dtype), mesh=_sc_mesh(),
                       name="sc_gather")
    def gather(x_hbm, i_hbm, o_hbm):
        def body(i_vmem, o_vmem):
            pltpu.sync_copy(x_hbm.at[i_vmem.at[0]], o_vmem)

        pltpu.emit_pipeline(
            body,
            grid=(m // SC_WINDOW,),
            in_specs=[pl.BlockSpec((1, SC_WINDOW), lambda i: (0, i))],
            out_specs=[pl.BlockSpec((SC_WINDOW, d), lambda i: (i, 0))],
            core_axis_name=("c", "s"),
            dimension_semantics=(pltpu.PARALLEL,),
        )(i_hbm, o_hbm)

    return gather(table, idx.reshape(1, m))


def _row_to_col(row):
    n = row.shape[1]
    eye = lax.broadcasted_iota(jnp.int32, (n, n), 0) == lax.broadcasted_iota(jnp.int32, (n, n), 1)
    return jnp.sum(jnp.where(eye, row, jnp.zeros_like(row)), axis=1, keepdims=True)


def _ffn_kernel(slot_ref, x0_ref, x1_ref, w1_hbm, w3_hbm, w2_hbm, o0_ref, o1_ref, dst_ref, wstage, w1b, w3b, w2b, wsem,
                *, layer):
    e = pl.program_id(0)
    i = pl.program_id(1)

    def weight_copies(expert):
        return [pltpu.make_async_copy(w_hbm.at[layer, expert], wstage.at[k], wsem.at[k])
                for k, w_hbm in enumerate((w1_hbm, w3_hbm, w2_hbm))]

    @pl.when(i == 0)
    def _():
        @pl.when(e == 0)
        def _():
            for cp in weight_copies(0):
                cp.start()
        for cp in weight_copies(e):
            cp.wait()
        w1b[...] = wstage[0].astype(BF16)
        w3b[...] = wstage[1].astype(BF16)
        w2b[...] = wstage[2].astype(BF16)

    @pl.when(jnp.logical_and(i == 1, e + 1 < N_EXPERTS))
    def _():
        for cp in weight_copies(e + 1):
            cp.start()

    info = slot_ref[...]
    lane = lax.broadcasted_iota(jnp.int32, info.shape, 1)
    gate = jnp.sum(jnp.where(lane == e, pltpu.bitcast(info, F32), 0.0), axis=1, keepdims=True)
    info_t = info.T
    word = lax.broadcasted_iota(jnp.int32, info_t.shape, 0)
    dst_ref[0] = jnp.sum(jnp.where(word == N_EXPERTS + e, info_t, 0), axis=0, keepdims=True)

    xs = _unpack_rows([x0_ref[...], x1_ref[...]])
    hg = jnp.dot(xs, w1b[...], preferred_element_type=F32)
    hu = jnp.dot(xs, w3b[...], preferred_element_type=F32)
    hid = (hg * _sigmoid(hg) * hu).astype(BF16)
    out = jnp.dot(hid, w2b[...], preferred_element_type=F32) * gate
    for ref, words in zip((o0_ref, o1_ref), _pack_rows(out)):
        ref[...] = words


def _expert_ffn(xs, slot_rows, w1, w3, w2, layer):
    m = xs[0].shape[0]
    rows = FFN_ROWS
    nt = m // (N_EXPERTS * rows)
    assert SC_PIECES == 2 and nt >= 2
    piece = pl.BlockSpec((rows, SC_ROW), lambda e, i: (e * nt + i, 0))
    any_spec = pl.BlockSpec(memory_space=pl.ANY)
    *outs, dst = pl.pallas_call(
        functools.partial(_ffn_kernel, layer=layer),
        grid=(N_EXPERTS, nt),
        in_specs=[pl.BlockSpec((rows, LANES), lambda e, i: (e * nt + i, 0))] + [piece] * SC_PIECES + [any_spec] * 3,
        out_specs=[piece] * SC_PIECES + [pl.BlockSpec((1, 1, rows), lambda e, i: (e * nt + i, 0, 0))],
        out_shape=[jax.ShapeDtypeStruct((m, SC_ROW), jnp.int32)] * SC_PIECES
        + [jax.ShapeDtypeStruct((m // rows, 1, rows), jnp.int32)],
        scratch_shapes=[pltpu.VMEM((3, D_MODEL, EXPERT_FF), F32), pltpu.VMEM((D_MODEL, EXPERT_FF), BF16),
                        pltpu.VMEM((D_MODEL, EXPERT_FF), BF16), pltpu.VMEM((EXPERT_FF, D_MODEL), BF16),
                        pltpu.SemaphoreType.DMA((3,))],
        compiler_params=_cparams(2),
        name="expert_ffn",
    )(slot_rows, *xs, w1, w3, w2)
    return outs, dst.reshape(-1)


def _combine_kernel(tsub_ref, x_ref, cs_ref, ce_ref, r0_hbm, r1_hbm, o_ref, rbuf, obuf, rows16, sems, osem,
                    *, ntile, total):
    _combine_tile(pl.program_id(0), tsub_ref, x_ref, cs_ref, ce_ref, r0_hbm, r1_hbm, o_ref, rbuf, obuf, rows16, sems,
                  osem, ntile=ntile, total=total)


def _combine_then_in_proj_kernel(tsub_ref, x_ref, cs_ref, ce_ref, r0_hbm, r1_hbm, g_ref, w_ref, qg_ref, kg_ref,
                                 bdq_ref, bdk_ref, y_ref, qa_ref, ka_ref, va_ref, qr_ref, kr_ref, vr_ref, gr_ref,
                                 ga_ref, gt_ref, rbuf, obuf, rows16, sems, osem, *, ntile, total):
    _combine_tile(pl.program_id(0), tsub_ref, x_ref, cs_ref, ce_ref, r0_hbm, r1_hbm, y_ref, rbuf, obuf, rows16, sems,
                  osem, ntile=ntile, total=total)
    _in_proj_kernel(y_ref, g_ref, w_ref, qg_ref, kg_ref, bdq_ref, bdk_ref,
                    qa_ref, ka_ref, va_ref, qr_ref, kr_ref, vr_ref, gr_ref, ga_ref, gt_ref)


def _combine_tile(i, tsub_ref, x_ref, cs_ref, ce_ref, r0_hbm, r1_hbm, o_ref, rbuf, obuf, rows16, sems, osem,
                  *, ntile, total):
    pieces_hbm = (r0_hbm, r1_hbm)
    win = COMBINE_WINDOW
    sub = COMBINE_SUB
    subwin = COMBINE_SUBWIN
    per = COMBINE_TOKENS // sub
    slot = lax.rem(i, COMBINE_BUFFERS)

    def window_start(t):
        return pl.multiple_of((tsub_ref[t * per] // BF16_TILE_ROWS) * BF16_TILE_ROWS, BF16_TILE_ROWS)

    def copies(t, b):
        s = window_start(t)
        return [pltpu.make_async_copy(pieces_hbm[c].at[pl.ds(s, win)], rbuf.at[b, c], sems.at[b, c])
                for c in range(SC_PIECES)]

    ahead = COMBINE_BUFFERS - 1

    @pl.when(i == 0)
    def _():
        for t in range(min(ahead, ntile)):
            for cp in copies(t, t):
                cp.start(priority=1)

    @pl.when(i + ahead < ntile)
    def _():
        for cp in copies(i + ahead, lax.rem(i + ahead, COMBINE_BUFFERS)):
            cp.start(priority=1)

    for cp in copies(i, slot):
        cp.wait()

    first = [_row_to_col(cs_ref[0][:, g * sub:(g + 1) * sub]) for g in range(per)]
    last = [_row_to_col(ce_ref[0][:, g * sub:(g + 1) * sub]) for g in range(per)]

    def zero_unwritten(words, base):
        written = (base + lax.broadcasted_iota(jnp.int32, (win, 1), 0)) < total
        return jnp.where(written, words, 0)

    def owner_matrix(g, base, width):
        r = base + lax.broadcasted_iota(jnp.int32, (1, width), 1)
        return jnp.logical_and(first[g] <= r, r < last[g]).astype(BF16)

    s0 = window_start(i)
    tail = s0 + win > total

    @pl.when(tail)
    def _():
        rows16[...] = _unpack_rows([zero_unwritten(rbuf[slot, c], s0) for c in range(SC_PIECES)])

    @pl.when(jnp.logical_not(tail))
    def _():
        rows16[...] = _unpack_rows([rbuf[slot, c] for c in range(SC_PIECES)])

    offsets = []
    fits = None
    for g in range(per):
        off = (tsub_ref[i * per + g] // BF16_TILE_ROWS) * BF16_TILE_ROWS - s0
        ok = jnp.logical_and(tsub_ref[i * per + g + 1] - s0 <= off + subwin, off + subwin <= win)
        fits = ok if fits is None else jnp.logical_and(fits, ok)
        offsets.append(off)

    @pl.when(fits)
    def _():
        for g in range(per):
            tokens = slice(g * sub, (g + 1) * sub)
            off = pl.multiple_of(offsets[g], BF16_TILE_ROWS)
            q = owner_matrix(g, s0 + off, subwin)
            o_ref[tokens, :] = x_ref[tokens, :] + jnp.dot(q, rows16[pl.ds(off, subwin), :],
                                                         preferred_element_type=F32)

    @pl.when(jnp.logical_not(fits))
    def _():
        def everyone(base):
            return jnp.concatenate([owner_matrix(g, base, win) for g in range(per)], axis=0)

        y = x_ref[...] + jnp.dot(everyone(s0), rows16[...], preferred_element_type=F32)
        n_extra = jnp.maximum(tsub_ref[(i + 1) * per] - (s0 + win) + win - 1, 0) // win

        def extra(k, acc):
            base = pl.multiple_of(s0 + (k + 1) * win, BF16_TILE_ROWS)
            cps = [pltpu.make_async_copy(pieces_hbm[c].at[pl.ds(base, win)], obuf.at[c], osem.at[c])
                   for c in range(SC_PIECES)]
            for cp in cps:
                cp.start()
            for cp in cps:
                cp.wait()
            rows = _unpack_rows([zero_unwritten(obuf[c], base) for c in range(SC_PIECES)])
            return acc + jnp.dot(everyone(base), rows, preferred_element_type=F32)

        o_ref[...] = lax.fori_loop(0, n_extra, extra, y)


def _combine(pending, in_proj_params=None):
    xmid, cs, ce, pieces, total = pending
    n = xmid.shape[0]
    tt = COMBINE_TOKENS
    win = COMBINE_WINDOW
    ntile = n // tt
    tsub = jnp.concatenate([cs[::COMBINE_SUB], jnp.full((1,), total, jnp.int32)])
    cs3 = cs.reshape(ntile, 1, tt)
    ce3 = ce.reshape(ntile, 1, tt)
    any_spec = pl.BlockSpec(memory_space=pl.ANY)
    tok = pl.BlockSpec((1, 1, tt), lambda i, ts: (i, 0, 0))
    row = lambda wd: pl.BlockSpec((tt, wd), lambda i, ts: (i, 0))
    in_specs = [row(D_MODEL), tok, tok] + [any_spec] * SC_PIECES
    out_specs = [row(D_MODEL)]
    out_shape = [jax.ShapeDtypeStruct((n, D_MODEL), F32)]
    operands = [tsub, xmid, cs3, ce3, *pieces]
    body, name = _combine_kernel, "combine"
    if in_proj_params is not None:
        full = lambda a: pl.BlockSpec(a.shape, lambda i, ts: (0,) * a.ndim, pipeline_mode=pl.Buffered(1))
        in_specs += [full(a) for a in in_proj_params]
        out_specs += [row(wd) for wd in IN_SPLITS]
        out_shape += [jax.ShapeDtypeStruct((n, wd), BF16) for wd in IN_SPLITS]
        operands += list(in_proj_params)
        body, name = _combine_then_in_proj_kernel, "combine_in_proj"
    grid_spec = pltpu.PrefetchScalarGridSpec(
        num_scalar_prefetch=1,
        grid=(ntile,),
        in_specs=in_specs,
        out_specs=out_specs,
        scratch_shapes=[pltpu.VMEM((COMBINE_BUFFERS, SC_PIECES, win, SC_ROW), jnp.int32),
                        pltpu.VMEM((SC_PIECES, win, SC_ROW), jnp.int32), pltpu.VMEM((win, D_MODEL), BF16),
                        pltpu.SemaphoreType.DMA((COMBINE_BUFFERS, SC_PIECES)), pltpu.SemaphoreType.DMA((SC_PIECES,))],
    )
    y, *proj = pl.pallas_call(
        functools.partial(body, ntile=ntile, total=total),
        grid_spec=grid_spec,
        out_shape=out_shape,
        compiler_params=_cparams(1),
        name=name,
    )(*operands)
    return (y, proj) if in_proj_params is not None else y


def _t5_bucket(rel):
    half = REL_BUCKETS // 2
    max_exact = half // 2
    base = np.where(rel > 0, half, 0)
    n = np.abs(rel)
    large = max_exact + (np.log(np.maximum(n, 1) / max_exact) / math.log(REL_MAX_DIST / max_exact)
                         * (half - max_exact)).astype(np.int32)
    large = np.minimum(large, half - 1)
    return (base + np.where(n < max_exact, n, large)).astype(np.int32)


def _head_perm():
    nq = ATTN_HEADS // 2
    cols = []
    for j in range(nq):
        for half in range(2):
            h = j + nq * half
            cols.extend(range(h * ATTN_HEAD_DIM, (h + 1) * ATTN_HEAD_DIM))
    return np.asarray(cols, np.int32)


def _attn_bias_tables(rel_bias):
    q_pos = np.arange(BLOCK)[:, None]
    k_off = np.arange(3 * BLOCK)[None, :] - BLOCK
    rel = k_off - q_pos
    in_window = np.abs(rel) <= WINDOW
    onehot = jnp.asarray(_t5_bucket(rel)[:, :, None] == np.arange(REL_BUCKETS)[None, None, :], F32)
    bias = jnp.einsum("qkb,bh->hqk", onehot, rel_bias.astype(F32), precision=lax.Precision.HIGHEST)
    col = np.arange(3 * BLOCK)[None, :]
    tables = []
    for valid in (col >= BLOCK, np.ones_like(col, bool), col < 2 * BLOCK):
        t = jnp.where(jnp.asarray(in_window & valid)[None], bias, NEG)
        nq = ATTN_HEADS // 2
        rows = [jnp.concatenate([t[j], t[j + nq]], axis=1) for j in range(nq)]
        tables.append(jnp.concatenate(rows, axis=0))
    return jnp.stack(tables)


def _retention_tables(decay_logit, norm_g):
    cr = RET_CHUNK
    lg = jax.nn.log_sigmoid(decay_logit.astype(F32))
    lgf, lgb = lg[0][:, None, None], lg[1][:, None, None]
    pos = np.arange(cr, dtype=np.float32)
    dist = pos[:, None] - pos[None, :]
    scale = RET_DIM ** -0.5
    dmask = jnp.where(jnp.asarray(dist >= 0)[None],
                      jnp.exp(lgf * np.maximum(dist, 0.0)[None]),
                      jnp.exp(lgb * np.maximum(-dist, 0.0)[None])) * scale
    col = lambda v: jnp.broadcast_to(v[:, :, None], (RET_HEADS, cr, RET_DIM))
    rowf = col(jnp.exp(lg[0][:, None] * pos[None]))
    rowb = col(jnp.exp(lg[1][:, None] * (cr - 1.0 - pos)[None]))
    wkf = col(jnp.exp(lg[0][:, None] * (cr - pos)[None]) * scale)
    wkb = col(jnp.exp(lg[1][:, None] * (pos + 1.0)[None]) * scale)
    dec = jnp.concatenate([jnp.exp(lg[0] * cr), jnp.exp(lg[1] * cr)])
    return dec, dmask, rowf, rowb, wkf, wkb, norm_g.astype(F32)


def _layer(x2, pending, b, s, p):
    in_proj_params = (p["g1"], p["w_in"], p["qg"], p["kg"], p["bdq"], p["bdk"])
    if pending is None:
        qa, ka, va, qr, kr, vr, gr, ga, gt = _in_proj(x2, *in_proj_params)
    else:
        x2, (qa, ka, va, qr, kr, vr, gr, ga, gt) = _combine(pending, in_proj_params)
    attn = _attention(qa, ka, va, p["bias3"], p["sink"], b, s)
    retn = _retention(qr, kr, vr, gr, p["retn"], b, s)
    xmid, afft, *h2 = _merge(attn, retn, ga, gt, x2, p["wba"], p["wbr"], p["wo"], p["g2"], p["wr"])
    n = b * s
    cap = max(1, EC_CAPACITY_FACTOR * n // N_EXPERTS)
    total = N_EXPERTS * cap
    idx, slots, cs, ce = _select(afft, cap)
    slot_rows = _slot_rows(afft, slots, idx)
    xs = [_sc_gather(piece, idx) for piece in h2]
    outs, dst = _expert_ffn(xs, slot_rows, p["w1"], p["w3"], p["w2"], p["layer"])
    by_token = [_sc_scatter(o, dst, total + COMBINE_WINDOW) for o in outs]
    return xmid, cs, ce, by_token, total


def kernel(x_prompt, x_sample, norm_mix_g, w_in, q_norm_g, k_norm_g, attn_sink, rel_bias, retn_decay_logit, retn_norm_g, w_branch_attn, w_branch_retn, w_out, norm_ffn_g, w_router, w_exp_gate, w_exp_up, w_exp_down):
    depth = w_in.shape[0]
    perm = _head_perm()
    bias3 = _attn_bias_tables(rel_bias)
    bdq = jnp.asarray(np.kron(np.eye(ATTN_HEADS), np.ones((ATTN_HEAD_DIM, ATTN_HEAD_DIM))), BF16)
    bdk = jnp.asarray(np.kron(np.eye(ATTN_KV_HEADS), np.ones((ATTN_HEAD_DIM, ATTN_HEAD_DIM))), BF16)
    layers = []
    for l in range(depth):
        w = w_in[l]
        w = jnp.concatenate([w[:, :ATTN_WIDTH][:, perm], w[:, ATTN_WIDTH:]], axis=1).astype(BF16)
        wr = jnp.pad(w_router[l], ((0, 0), (0, LANES - N_EXPERTS))).astype(BF16)
        layers.append(dict(
            g1=norm_mix_g[l].astype(F32)[None], w_in=w,
            qg=(jnp.tile(q_norm_g[l].astype(F32), ATTN_HEADS) * (ATTN_HEAD_DIM ** -0.5))[None],
            kg=jnp.tile(k_norm_g[l].astype(F32), ATTN_KV_HEADS)[None],
            bdq=bdq, bdk=bdk, bias3=bias3, sink=attn_sink[l].astype(F32),
            retn=_retention_tables(retn_decay_logit[l], retn_norm_g[l]),
            wba=w_branch_attn[l][perm, :].astype(BF16), wbr=w_branch_retn[l].astype(BF16),
            wo=w_out[l].astype(BF16), g2=norm_ffn_g[l].astype(F32)[None], wr=wr,
            w1=w_exp_gate, w3=w_exp_up, w2=w_exp_down, layer=l))

    def trunk(x):
        b, s, d = x.shape
        x2, pending = x.reshape(b * s, d), None
        for p in layers:
            pending = _layer(x2, pending, b, s, p)
        return _combine(pending).reshape(b, s, d)

    return (trunk(x_prompt), trunk(x_sample))
```
